```python
import math
import jax, jax.numpy as jnp
from jax import lax
import numpy as np

D_MODEL = 1024
BATCH = 8
SEQ = 8192
DEPTH = 2

HEAD_DIM = 64
BLOCK_Q = 128
NEG = -1e30
EPS = 1e-6
FORCE_BONUS = 1000.0

SB_HEADS = 4
SWA_HEADS = 4
SWA_KV_HEADS = 2
SWA_WINDOW = 128
NSA_HEADS = 4
NSA_KV_HEADS = 2
NSA_CMP_LEN = 32
NSA_CMP_STRIDE = 16
NSA_CMP_HIDDEN = 128
NSA_SEL_LEN = 64
NSA_TOPK = 16
NSA_WINDOW = 512
NSA_QBLOCK = 64
MLA_HEADS = 4
MLA_NOPE = 64
MLA_ROPE = 32
MLA_V = 64
MLA_Q_RANK = 192
MLA_KV_RANK = 128
ROPE_THETA = 10000.0
T5_BUCKETS = 32
T5_MAX_DIST = 1024
N_BIAS_HEADS = SWA_HEADS + NSA_HEADS
MOE_GROUPS = 4
MOE_EXPERTS_PER_GROUP = 8
MOE_EXPERTS = MOE_GROUPS * MOE_EXPERTS_PER_GROUP
MOE_HIDDEN = 256
MOE_TOPK = 2

IN_SPLITS = (SB_HEADS * HEAD_DIM, SB_HEADS * HEAD_DIM, SB_HEADS * HEAD_DIM,
             SWA_HEADS * HEAD_DIM, SWA_KV_HEADS * HEAD_DIM, SWA_KV_HEADS * HEAD_DIM,
             NSA_HEADS * HEAD_DIM,
             NSA_KV_HEADS * HEAD_DIM, NSA_KV_HEADS * HEAD_DIM,
             NSA_KV_HEADS * HEAD_DIM, NSA_KV_HEADS * HEAD_DIM,
             NSA_KV_HEADS * HEAD_DIM, NSA_KV_HEADS * HEAD_DIM,
             3 * NSA_HEADS,
             MLA_Q_RANK, MLA_KV_RANK, MLA_ROPE)
IN_COLS = (3 * SB_HEADS + SWA_HEADS + 2 * SWA_KV_HEADS + NSA_HEADS + 6 * NSA_KV_HEADS) * HEAD_DIM \
    + 3 * NSA_HEADS + MLA_Q_RANK + MLA_KV_RANK + MLA_ROPE
GROUP_WIDTH = 4 * HEAD_DIM
N_GROUPS_MIX = 4
MIX_WIDTH = N_GROUPS_MIX * GROUP_WIDTH

kernel_name = "hymba_style_sb_swa_nsa_mla_hmoe"


def rms_norm(x, g):
    xf = x.astype(jnp.float32)
    y = xf * lax.rsqrt(jnp.mean(xf * xf, axis=-1, keepdims=True) + EPS)
    return (y * g.astype(jnp.float32)).astype(x.dtype)


def t5_bucket(dist):
    n = jnp.maximum(dist, 0)
    max_exact = T5_BUCKETS // 2
    nf = jnp.maximum(n, 1).astype(jnp.float32)
    large = max_exact + (jnp.log(nf / max_exact) / math.log(T5_MAX_DIST / max_exact)
                         * (T5_BUCKETS - max_exact)).astype(jnp.int32)
    large = jnp.minimum(large, T5_BUCKETS - 1)
    return jnp.where(n < max_exact, n, large)


def masked_softmax(logits, mask, sink=None):
    l = jnp.where(mask, logits, NEG)
    m = jnp.max(l, axis=-1, keepdims=True)
    if sink is not None:
        m = jnp.maximum(m, sink)
    p = jnp.where(mask, jnp.exp(l - m), 0.0)
    denom = jnp.sum(p, axis=-1, keepdims=True)
    if sink is not None:
        denom = denom + jnp.exp(sink - m)
    return p / jnp.maximum(denom, 1e-30)


def rope(x, cos, sin):
    half = x.shape[-1] // 2
    xf = x.astype(jnp.float32)
    x1, x2 = xf[..., :half], xf[..., half:]
    c = cos[None, :, None, :]
    s = sin[None, :, None, :]
    return jnp.concatenate([x1 * c - x2 * s, x1 * s + x2 * c], axis=-1).astype(x.dtype)


def stick_breaking_attention(q, k, v):
    B, S, H, d = q.shape
    scale = d ** -0.5
    k_pos = jnp.arange(S)

    def block(i):
        start = i * BLOCK_Q
        qi = lax.dynamic_slice_in_dim(q, start, BLOCK_Q, axis=1)
        q_pos = start + jnp.arange(BLOCK_Q)
        mask = k_pos[None, :] < q_pos[:, None]
        z = jnp.einsum('bqhd,bkhd->bhqk', qi, k).astype(jnp.float32) * scale
        log_keep = jnp.where(mask, jax.nn.log_sigmoid(-z), 0.0)
        after = lax.cumsum(log_keep, axis=3, reverse=True) - log_keep
        a = jnp.where(mask, jnp.exp(jax.nn.log_sigmoid(z) + after), 0.0)
        return jnp.einsum('bhqk,bkhd->bqhd', a.astype(v.dtype), v)

    out = lax.map(block, jnp.arange(S // BLOCK_Q))
    return jnp.moveaxis(out, 0, 1).reshape(B, S, H, d)


def banded_attention(q, k, v, window, bias_tbl, sinks):
    B, S, H, d = q.shape
    Hk = k.shape[2]
    grp = H // Hk
    nprev = -(-(window - 1) // BLOCK_Q)
    pad = nprev * BLOCK_Q
    span = pad + BLOCK_Q
    kp = jnp.pad(k, ((0, 0), (pad, 0), (0, 0), (0, 0)))
    vp = jnp.pad(v, ((0, 0), (pad, 0), (0, 0), (0, 0)))
    qg = q.reshape(B, S, Hk, grp, d)
    tbl = bias_tbl.reshape(Hk, grp, T5_BUCKETS)
    snk = None if sinks is None else sinks.reshape(Hk, grp, 1, 1).astype(jnp.float32)
    scale = d ** -0.5

    def block(i):
        start = i * BLOCK_Q
        qi = lax.dynamic_slice_in_dim(qg, start, BLOCK_Q, axis=1)
        ki = lax.dynamic_slice_in_dim(kp, start, span, axis=1)
        vi = lax.dynamic_slice_in_dim(vp, start, span, axis=1)
        q_pos = start + jnp.arange(BLOCK_Q)
        k_pos = start - pad + jnp.arange(span)
        dist = q_pos[:, None] - k_pos[None, :]
        mask = (dist >= 0) & (dist < window) & (k_pos[None, :] >= 0)
        logits = jnp.einsum('bqhgd,bkhd->bhgqk', qi, ki).astype(jnp.float32) * scale
        logits = logits + tbl[:, :, t5_bucket(dist)].astype(jnp.float32)
        p = masked_softmax(logits, mask, snk)
        return jnp.einsum('bhgqk,bkhd->bqhgd', p.astype(v.dtype), vi)

    out = lax.map(block, jnp.arange(S // BLOCK_Q))
    return jnp.moveaxis(out, 0, 1).reshape(B, S, H, d)


def causal_attention(q, k, v):
    B, S, H, dk = q.shape
    dv = v.shape[-1]
    scale = dk ** -0.5
    k_pos = jnp.arange(S)

    def block(i):
        start = i * BLOCK_Q
        qi = lax.dynamic_slice_in_dim(q, start, BLOCK_Q, axis=1)
        q_pos = start + jnp.arange(BLOCK_Q)
        mask = k_pos[None, :] <= q_pos[:, None]
        logits = jnp.einsum('bqhd,bkhd->bhqk', qi, k).astype(jnp.float32) * scale
        p = masked_softmax(logits, mask)
        return jnp.einsum('bhqk,bkhd->bqhd', p.astype(v.dtype), v)

    out = lax.map(block, jnp.arange(S // BLOCK_Q))
    return jnp.moveaxis(out, 0, 1).reshape(B, S, H, dv)


def nsa_compress(x, pos, w1, w2):
    B, S, Hk, d = x.shape
    n_cat = NSA_CMP_LEN // NSA_CMP_STRIDE
    nb = S // NSA_CMP_STRIDE
    nc = nb - n_cat + 1
    xb = x.reshape(B, nb, NSA_CMP_STRIDE, Hk, d)
    win = jnp.concatenate([xb[:, j:j + nc] for j in range(n_cat)], axis=2)
    win = win + pos[None, None, :, None, :]
    h = jax.nn.gelu(jnp.einsum('bnlhd,ldf->bnhf', win, w1))
    return jnp.einsum('bnhf,fe->bnhe', h, w2)


def nsa_attention(q, k_cmp, v_cmp, k_slc, v_slc, k_win, v_win, gates, bias_tbl):
    B, S, H, d = q.shape
    Hk = k_slc.shape[2]
    grp = H // Hk
    nc = k_cmp.shape[1]
    n_sel = S // NSA_SEL_LEN
    topk = min(NSA_TOPK, n_sel)
    span_sel = topk * NSA_SEL_LEN
    scale = d ** -0.5
    tbl = bias_tbl.reshape(Hk, grp, T5_BUCKETS)
    qg = q.reshape(B, S, Hk, grp, d)
    cmp_start = jnp.arange(nc) * NSA_CMP_STRIDE
    cmp_end = cmp_start + NSA_CMP_LEN - 1
    sel_idx = jnp.arange(n_sel)
    sel_start = sel_idx * NSA_SEL_LEN
    overlap = ((cmp_start[:, None] < sel_start[None, :] + NSA_SEL_LEN)
               & (cmp_end[:, None] >= sel_start[None, :])).astype(jnp.float32)
    ks_blk = k_slc.reshape(B, n_sel, NSA_SEL_LEN, Hk, d).transpose(0, 3, 1, 2, 4)
    vs_blk = v_slc.reshape(B, n_sel, NSA_SEL_LEN, Hk, d).transpose(0, 3, 1, 2, 4)
    b_ix = jnp.arange(B)[:, None, None, None]
    h_ix = jnp.arange(Hk)[None, :, None, None]
    hh = jnp.arange(Hk)[:, None, None, None, None]
    gg = jnp.arange(grp)[None, :, None, None, None]
    tok_off = jnp.arange(NSA_SEL_LEN)

    def block(i):
        start = i * NSA_QBLOCK
        qi = lax.dynamic_slice_in_dim(qg, start, NSA_QBLOCK, axis=1)
        q_pos = start + jnp.arange(NSA_QBLOCK)
        dist_c = q_pos[:, None] - cmp_end[None, :]
        mask_c = dist_c >= 0
        lc = jnp.einsum('bqhgd,bnhd->bhgqn', qi, k_cmp).astype(jnp.float32) * scale
        lc = lc + tbl[:, :, t5_bucket(dist_c)].astype(jnp.float32)
        pc = masked_softmax(lc, mask_c)
        o_c = jnp.einsum('bhgqn,bnhd->bqhgd', pc.astype(v_cmp.dtype), v_cmp)
        imp = jnp.einsum('bhgqn,nj->bhqj', pc, overlap)
        cur = q_pos // NSA_SEL_LEN
        forced = ((sel_idx[None, :] == 0) | (sel_idx[None, :] == cur[:, None])
                  | (sel_idx[None, :] == cur[:, None] - 1))
        valid = sel_start[None, :] <= q_pos[:, None]
        score = jnp.where(valid, imp + jnp.where(forced, FORCE_BONUS, 0.0), NEG)
        _, idx = lax.top_k(score, topk)
        ksel = ks_blk[b_ix, h_ix, idx]
        vsel = vs_blk[b_ix, h_ix, idx]
        key_pos = idx[..., None] * NSA_SEL_LEN + tok_off
        dist_s = q_pos[None, None, :, None, None] - key_pos
        ls = jnp.einsum('bqhgd,bhqkld->bhgqkl', qi, ksel).astype(jnp.float32) * scale
        ls = ls + tbl[hh, gg, t5_bucket(dist_s)[:, :, None]].astype(jnp.float32)
        ls = ls.reshape(B, Hk, grp, NSA_QBLOCK, span_sel)
        mask_s = (dist_s >= 0).reshape(B, Hk, 1, NSA_QBLOCK, span_sel)
        ps = masked_softmax(ls, mask_s)
        o_s = jnp.einsum('bhgqm,bhqmd->bqhgd', ps.astype(v_slc.dtype),
                         vsel.reshape(B, Hk, NSA_QBLOCK, span_sel, d))
        return o_c, o_s

    o_c, o_s = lax.map(block, jnp.arange(S // NSA_QBLOCK))
    o_c = jnp.moveaxis(o_c, 0, 1).reshape(B, S, H, d)
    o_s = jnp.moveaxis(o_s, 0, 1).reshape(B, S, H, d)
    o_w = banded_attention(q, k_win, v_win, NSA_WINDOW, bias_tbl, None)
    return (gates[:, :, 0, :, None] * o_c + gates[:, :, 1, :, None] * o_s
            + gates[:, :, 2, :, None] * o_w)


def mla_attention(c_q, c_kv, k_pe, q_lat_g, w_q_up, kv_lat_g, w_kv_up, q_g, k_g, cos, sin):
    B, S, _ = c_q.shape
    q = (rms_norm(c_q, q_lat_g) @ w_q_up).reshape(B, S, MLA_HEADS, MLA_NOPE + MLA_ROPE)
    kv = (rms_norm(c_kv, kv_lat_g) @ w_kv_up).reshape(B, S, MLA_HEADS, MLA_NOPE + MLA_V)
    k_nope, v = kv[..., :MLA_NOPE], kv[..., MLA_NOPE:]
    k = jnp.concatenate([k_nope, jnp.broadcast_to(k_pe[:, :, None, :], (B, S, MLA_HEADS, MLA_ROPE))], axis=-1)
    q = rms_norm(q, q_g)
    k = rms_norm(k, k_g)
    q = jnp.concatenate([q[..., :MLA_NOPE], rope(q[..., MLA_NOPE:], cos, sin)], axis=-1)
    k = jnp.concatenate([k[..., :MLA_NOPE], rope(k[..., MLA_NOPE:], cos, sin)], axis=-1)
    return causal_attention(q, k, v)


def mixer(h, rel_bias, cos, sin, w_in, swa_q_norm, swa_k_norm, swa_sinks, nsa_q_norm,
          nsa_k_norm, nsa_cmp_pos, nsa_cmp_w1, nsa_cmp_w2, mla_q_lat_norm, mla_w_q_up,
          mla_kv_lat_norm, mla_w_kv_up, mla_q_norm, mla_k_norm, out_norm, w_out):
    B, S, _ = h.shape
    proj = h @ w_in
    parts = []
    off = 0
    for n in IN_SPLITS:
        parts.append(proj[..., off:off + n])
        off += n
    (sb_q, sb_k, sb_v, swa_q, swa_k, swa_v, nsa_q, nsa_kc, nsa_vc, nsa_ks, nsa_vs,
     nsa_kw, nsa_vw, nsa_g, mla_cq, mla_ckv, mla_kpe) = parts

    def heads(t):
        return t.reshape(B, S, -1, HEAD_DIM)

    o_a = stick_breaking_attention(heads(sb_q), heads(sb_k), heads(sb_v))
    o_b = banded_attention(rms_norm(heads(swa_q), swa_q_norm), rms_norm(heads(swa_k), swa_k_norm),
                           heads(swa_v), SWA_WINDOW, rel_bias[:, :SWA_HEADS].T, swa_sinks)
    k_cmp = rms_norm(nsa_compress(heads(nsa_kc), nsa_cmp_pos[0], nsa_cmp_w1[0], nsa_cmp_w2[0]), nsa_k_norm[0])
    v_cmp = nsa_compress(heads(nsa_vc), nsa_cmp_pos[1], nsa_cmp_w1[1], nsa_cmp_w2[1])
    gates = jax.nn.sigmoid(nsa_g.reshape(B, S, 3, NSA_HEADS))
    o_c = nsa_attention(rms_norm(heads(nsa_q), nsa_q_norm), k_cmp, v_cmp,
                        rms_norm(heads(nsa_ks), nsa_k_norm[1]), heads(nsa_vs),
                        rms_norm(heads(nsa_kw), nsa_k_norm[2]), heads(nsa_vw),
                        gates, rel_bias[:, SWA_HEADS:].T)
    o_d = mla_attention(mla_cq, mla_ckv, mla_kpe, mla_q_lat_norm, mla_w_q_up, mla_kv_lat_norm,
                        mla_w_kv_up, mla_q_norm, mla_k_norm, cos, sin)
    o = jnp.stack([o_a.reshape(B, S, GROUP_WIDTH), o_b.reshape(B, S, GROUP_WIDTH),
                   o_c.reshape(B, S, GROUP_WIDTH), o_d.reshape(B, S, GROUP_WIDTH)], axis=2)
    o = rms_norm(o, out_norm.reshape(N_GROUPS_MIX, GROUP_WIDTH)).reshape(B, S, MIX_WIDTH)
    return o @ w_out


def hier_moe(h, w_group, b_group, w_expert, b_expert, w_gate, w_up, w_down):
    B, S, D = h.shape
    t = h.reshape(B * S, D)
    g_logits = (t @ w_group).astype(jnp.float32) + b_group.astype(jnp.float32)
    g_prob = jax.nn.softmax(g_logits, axis=-1)
    g_star = jnp.argmax(g_logits, axis=-1)
    g_w = jnp.max(g_prob, axis=-1, keepdims=True)
    e_logits = ((t @ w_expert).astype(jnp.float32) + b_expert.astype(jnp.float32)
                ).reshape(-1, MOE_GROUPS, MOE_EXPERTS_PER_GROUP)
    e_in = e_logits[jnp.arange(t.shape[0]), g_star]
    e_prob = jax.nn.softmax(e_in, axis=-1)
    top_p, top_i = lax.top_k(e_prob, MOE_TOPK)
    top_w = top_p / jnp.sum(top_p, axis=-1, keepdims=True) * g_w
    expert_id = g_star[:, None] * MOE_EXPERTS_PER_GROUP + top_i
    combine = jnp.sum(jax.nn.one_hot(expert_id, MOE_EXPERTS, dtype=jnp.float32)
                      * top_w[..., None], axis=1).astype(t.dtype)
    y = jnp.zeros_like(t)
    for e in range(MOE_EXPERTS):
        a = jax.nn.silu(t @ w_gate[e]) * (t @ w_up[e])
        y = y + combine[:, e:e + 1] * (a @ w_down[e])
    return y.reshape(B, S, D)


def setup_inputs(seed: int = 0) -> dict:
    key = jax.random.key(seed)
    ks = jax.random.split(key, 28)
    f32 = jnp.float32
    L = DEPTH

    def nrm(k, shape, scale):
        return jax.random.normal(k, shape, f32) * scale

    def gain(k, shape):
        return 1.0 + 0.02 * jax.random.normal(k, shape, f32)

    return {
        "x": nrm(ks[0], (BATCH, SEQ, D_MODEL), 1.0),
        "rel_bias": nrm(ks[1], (T5_BUCKETS, N_BIAS_HEADS), 0.5),
        "attn_norm": gain(ks[2], (L, D_MODEL)),
        "w_in": nrm(ks[3], (L, D_MODEL, IN_COLS), D_MODEL ** -0.5),
        "swa_q_norm": gain(ks[4], (L, HEAD_DIM)),
        "swa_k_norm": gain(ks[5], (L, HEAD_DIM)),
        "swa_sinks": nrm(ks[6], (L, SWA_HEADS), 0.5),
        "nsa_q_norm": gain(ks[7], (L, HEAD_DIM)),
        "nsa_k_norm": gain(ks[8], (L, 3, HEAD_DIM)),
        "nsa_cmp_pos": nrm(ks[9], (L, 2, NSA_CMP_LEN, HEAD_DIM), 0.1),
        "nsa_cmp_w1": nrm(ks[10], (L, 2, NSA_CMP_LEN, HEAD_DIM, NSA_CMP_HIDDEN), (NSA_CMP_LEN * HEAD_DIM) ** -0.5),
        "nsa_cmp_w2": nrm(ks[11], (L, 2, NSA_CMP_HIDDEN, HEAD_DIM), NSA_CMP_HIDDEN ** -0.5),
        "mla_q_lat_norm": gain(ks[12], (L, MLA_Q_RANK)),
        "mla_w_q_up": nrm(ks[13], (L, MLA_Q_RANK, MLA_HEADS * (MLA_NOPE + MLA_ROPE)), MLA_Q_RANK ** -0.5),
        "mla_kv_lat_norm": gain(ks[14], (L, MLA_KV_RANK)),
        "mla_w_kv_up": nrm(ks[15], (L, MLA_KV_RANK, MLA_HEADS * (MLA_NOPE + MLA_V)), MLA_KV_RANK ** -0.5),
        "mla_q_norm": gain(ks[16], (L, MLA_NOPE + MLA_ROPE)),
        "mla_k_norm": gain(ks[17], (L, MLA_NOPE + MLA_ROPE)),
        "out_norm": gain(ks[18], (L, MIX_WIDTH)),
        "w_out": nrm(ks[19], (L, MIX_WIDTH, D_MODEL), MIX_WIDTH ** -0.5),
        "ffn_norm": gain(ks[20], (L, D_MODEL)),
        "moe_w_group": nrm(ks[21], (L, D_MODEL, MOE_GROUPS), D_MODEL ** -0.5),
        "moe_b_group": nrm(ks[22], (L, MOE_GROUPS), 0.01),
        "moe_w_expert": nrm(ks[23], (L, D_MODEL, MOE_EXPERTS), D_MODEL ** -0.5),
        "moe_b_expert": nrm(ks[24], (L, MOE_EXPERTS), 0.01),
        "moe_w_gate": nrm(ks[25], (L, MOE_EXPERTS, D_MODEL, MOE_HIDDEN), D_MODEL ** -0.5),
        "moe_w_up": nrm(ks[26], (L, MOE_EXPERTS, D_MODEL, MOE_HIDDEN), D_MODEL ** -0.5),
        "moe_w_down": nrm(ks[27], (L, MOE_EXPERTS, MOE_HIDDEN, D_MODEL), MOE_HIDDEN ** -0.5),
    }


def reference(x, rel_bias, attn_norm, w_in, swa_q_norm, swa_k_norm, swa_sinks, nsa_q_norm,
              nsa_k_norm, nsa_cmp_pos, nsa_cmp_w1, nsa_cmp_w2, mla_q_lat_norm, mla_w_q_up,
              mla_kv_lat_norm, mla_w_kv_up, mla_q_norm, mla_k_norm, out_norm, w_out, ffn_norm,
              moe_w_group, moe_b_group, moe_w_expert, moe_b_expert, moe_w_gate, moe_w_up,
              moe_w_down):
    S = x.shape[1]
    pos = jnp.arange(S, dtype=jnp.float32)
    inv_freq = ROPE_THETA ** (-jnp.arange(0, MLA_ROPE, 2, dtype=jnp.float32) / MLA_ROPE)
    ang = pos[:, None] * inv_freq[None, :]
    cos, sin = jnp.cos(ang), jnp.sin(ang)
    for l in range(DEPTH):
        h = rms_norm(x, attn_norm[l])
        x = x + mixer(h, rel_bias, cos, sin, w_in[l], swa_q_norm[l], swa_k_norm[l], swa_sinks[l],
                      nsa_q_norm[l], nsa_k_norm[l], nsa_cmp_pos[l], nsa_cmp_w1[l], nsa_cmp_w2[l],
                      mla_q_lat_norm[l], mla_w_q_up[l], mla_kv_lat_norm[l], mla_w_kv_up[l],
                      mla_q_norm[l], mla_k_norm[l], out_norm[l], w_out[l])
        h = rms_norm(x, ffn_norm[l])
        x = x + hier_moe(h, moe_w_group[l], moe_b_group[l], moe_w_expert[l], moe_b_expert[l],
                         moe_w_gate[l], moe_w_up[l], moe_w_down[l])
    return x
```

```python
import functools
import math

import numpy as np
import jax
import jax.numpy as jnp
from jax import lax
from jax.experimental import pallas as pl
from jax.experimental.pallas import tpu as pltpu

F32 = jnp.float32
BF16 = jnp.bfloat16

D_MODEL = 1024
HEAD_DIM = 64
NEG = -1e30
EPS = 1e-6
FORCE_BONUS = 1000.0
SWA_WINDOW = 128
NSA_CMP_LEN = 32
NSA_CMP_STRIDE = 16
NSA_CMP_HIDDEN = 128
NSA_SEL_LEN = 64
NSA_TOPK = 16
NSA_WINDOW = 512
MLA_HEADS = 4
MLA_NOPE = 64
MLA_ROPE = 32
MLA_V = 64
MLA_Q_RANK = 192
MLA_KV_RANK = 128
MLA_QK = MLA_NOPE + MLA_ROPE
ROPE_THETA = 10000.0
T5_BUCKETS = 32
T5_MAX_DIST = 1024
MOE_GROUPS = 4
MOE_EPG = 8
MOE_EXPERTS = MOE_GROUPS * MOE_EPG
MOE_HIDDEN = 256
GROUP_WIDTH = 256
LANES = 128
VMEM_LIMIT = 48 * 1024 * 1024

PB_SBQ, PB_SBK, PB_SBV = 0, 256, 512
PB_SWAQ, PB_SWAK, PB_SWAV = 768, 1024, 1152
PB_NSAQ, PB_NSAKS, PB_NSAVS, PB_NSAKW, PB_NSAVW = 1280, 1536, 1664, 1792, 1920
PB_MLAQ, PB_MLAK, PB_MLAV = 2048, 2560, 3072
PB_WIDTH = 3328
PF_KC, PF_VC, PF_GATE = 0, 128, 256
PF_WIDTH = 384
W_CQ, W_CKV, W_KPE = 2432, 2688, 2816
W_WIDTH = 3328

NT_DIMS = (((1,), (1,)), ((), ()))


def _dot(a, b):
    return jnp.dot(a, b, preferred_element_type=F32)


def _dot_nt(a, b):
    return lax.dot_general(a, b, NT_DIMS, preferred_element_type=F32)


def _split_bf16(x):
    hi = x.astype(BF16)
    lo = (x - hi.astype(F32)).astype(BF16)
    return hi, lo


def _dot_exact_rhs(x, m):
    hi, lo = _split_bf16(x)
    return _dot(hi, m) + _dot(lo, m)


def _block_diag_ones(width, seg):
    idx = np.arange(width) // seg
    return jnp.asarray(idx[:, None] == idx[None, :], BF16)


def _seg_rms(x, seg_ones, count):
    width = x.shape[1]
    ms = _dot_exact_rhs(x * x, seg_ones[:width, :width]) * (1.0 / count)
    return x * lax.rsqrt(ms + EPS)


def _row_rms(x, count):
    return x * lax.rsqrt(jnp.sum(x * x, axis=-1, keepdims=True) * (1.0 / count) + EPS)


def _prep_kernel(x_ref, an_ref, w_ref, g64_ref, glat_ref, wq_ref, wkv_ref, gmla_ref, cos_ref, sin_ref,
                 s64_ref, s128_ref, pb_ref, pf_ref):
    x = x_ref[...]
    s64 = s64_ref[...]
    s128 = s128_ref[...]
    h = _row_rms(x, D_MODEL) * an_ref[...]
    hb = h.astype(BF16)

    def proj(lo, hi):
        return _dot(hb, w_ref[:, lo:hi])

    scale = HEAD_DIM ** -0.5
    pb_ref[:, PB_SBQ:PB_SBQ + 256] = (proj(0, 256) * scale).astype(BF16)
    pb_ref[:, PB_SBK:PB_SBK + 256] = proj(256, 512).astype(BF16)
    pb_ref[:, PB_SBV:PB_SBV + 256] = proj(512, 768).astype(BF16)
    g64 = g64_ref[...]
    pb_ref[:, PB_SWAQ:PB_SWAQ + 256] = (_seg_rms(proj(768, 1024), s64, 64) * g64[0:1, :] * scale).astype(BF16)
    pb_ref[:, PB_SWAK:PB_SWAK + 128] = (_seg_rms(proj(1024, 1152), s64, 64) * g64[1:2, :128]).astype(BF16)
    pb_ref[:, PB_SWAV:PB_SWAV + 128] = proj(1152, 1280).astype(BF16)
    pb_ref[:, PB_NSAQ:PB_NSAQ + 256] = (_seg_rms(proj(1280, 1536), s64, 64) * g64[2:3, :] * scale).astype(BF16)
    pf_ref[:, PF_KC:PF_KC + 128] = proj(1536, 1664)
    pf_ref[:, PF_VC:PF_VC + 128] = proj(1664, 1792)
    pb_ref[:, PB_NSAKS:PB_NSAKS + 128] = (_seg_rms(proj(1792, 1920), s64, 64) * g64[3:4, :128]).astype(BF16)
    pb_ref[:, PB_NSAVS:PB_NSAVS + 128] = proj(1920, 2048).astype(BF16)
    pb_ref[:, PB_NSAKW:PB_NSAKW + 128] = (_seg_rms(proj(2048, 2176), s64, 64) * g64[4:5, :128]).astype(BF16)
    pb_ref[:, PB_NSAVW:PB_NSAVW + 128] = proj(2176, 2304).astype(BF16)
    pf_ref[:, PF_GATE:PF_GATE + 128] = jax.nn.sigmoid(proj(2304, 2432))

    glat = glat_ref[...]
    cq = _row_rms(proj(W_CQ, W_CQ + 256), MLA_Q_RANK) * glat[0:1, :]
    q = _dot(cq.astype(BF16), wq_ref[...])
    ckv = _row_rms(proj(W_CKV, W_CKV + 128), MLA_KV_RANK) * glat[1:2, :128]
    kv = _dot(ckv.astype(BF16), wkv_ref[...])
    k = kv[:, :512] + proj(W_KPE, W_KPE + 512)
    gm = gmla_ref[...]
    q = _seg_rms(q, s128, MLA_QK) * gm[0:1, :]
    k = _seg_rms(k, s128, MLA_QK) * gm[1:2, :]
    cos = cos_ref[...]
    sin = sin_ref[...]
    lane = lax.broadcasted_iota(jnp.int32, (x.shape[0], LANES), 1)
    first_half = lane < MLA_NOPE + MLA_ROPE // 2

    def rope(t):
        partner = jnp.where(first_half, pltpu.roll(t, LANES - MLA_ROPE // 2, 1), pltpu.roll(t, MLA_ROPE // 2, 1))
        return t * cos + partner * sin

    qscale = MLA_QK ** -0.5
    for hd in range(MLA_HEADS):
        sl = slice(hd * LANES, (hd + 1) * LANES)
        pb_ref[:, PB_MLAQ + hd * LANES:PB_MLAQ + (hd + 1) * LANES] = (rope(q[:, sl]) * qscale).astype(BF16)
        pb_ref[:, PB_MLAK + hd * LANES:PB_MLAK + (hd + 1) * LANES] = rope(k[:, sl]).astype(BF16)
    pb_ref[:, PB_MLAV:PB_MLAV + 256] = kv[:, 512:768].astype(BF16)


def _prep(x2, an, w, g64, glat, wq, wkv, gmla, cos_t, sin_t, seq, tm):
    tokens = x2.shape[0]
    n_pos = seq // tm
    const = lambda t: (0, 0)
    return pl.pallas_call(
        _prep_kernel,
        out_shape=(jax.ShapeDtypeStruct((tokens, PB_WIDTH), BF16), jax.ShapeDtypeStruct((tokens, PF_WIDTH), F32)),
        grid=(tokens // tm,),
        in_specs=[
            pl.BlockSpec((tm, D_MODEL), lambda t: (t, 0)),
            pl.BlockSpec((1, D_MODEL), const),
            pl.BlockSpec((D_MODEL, W_WIDTH), const),
            pl.BlockSpec((8, 256), const),
            pl.BlockSpec((8, 256), const),
            pl.BlockSpec((256, 512), const),
            pl.BlockSpec((128, 768), const),
            pl.BlockSpec((8, 512), const),
            pl.BlockSpec((tm, LANES), lambda t: (t % n_pos, 0)),
            pl.BlockSpec((tm, LANES), lambda t: (t % n_pos, 0)),
            pl.BlockSpec((256, 256), const),
            pl.BlockSpec((512, 512), const),
        ],
        out_specs=(pl.BlockSpec((tm, PB_WIDTH), lambda t: (t, 0)), pl.BlockSpec((tm, PF_WIDTH), lambda t: (t, 0))),
        compiler_params=pltpu.CompilerParams(dimension_semantics=("parallel",), vmem_limit_bytes=VMEM_LIMIT),
        name="prep",
    )(x2, an, w, g64, glat, wq, wkv, gmla, cos_t, sin_t, _block_diag_ones(256, HEAD_DIM),
      _block_diag_ones(512, LANES))


SB_TQ, SB_KB, SB_SUB = 256, 512, 128


def _sb_kernel(qi_ref, kj_ref, q_ref, k_ref, v_ref, o_ref, carry_ref, acc_ref):
    p = pl.program_id(1)
    qi = qi_ref[p]
    kj = kj_ref[p]
    q_start = qi * SB_TQ
    first = kj == (q_start + SB_TQ - 1) // SB_KB

    @pl.when(first)
    def _():
        carry_ref[...] = jnp.zeros_like(carry_ref)
        acc_ref[...] = jnp.zeros_like(acc_ref)

    r = lax.broadcasted_iota(jnp.int32, (SB_SUB, SB_SUB), 0)
    c = lax.broadcasted_iota(jnp.int32, (SB_SUB, SB_SUB), 1)
    tri = jnp.where(r >= c, 1.0, 0.0).astype(BF16)
    tri2 = jnp.concatenate([tri, tri], axis=0)
    q_pos = q_start + lax.broadcasted_iota(jnp.int32, (SB_TQ, SB_SUB), 0)
    k_off = lax.broadcasted_iota(jnp.int32, (SB_TQ, SB_SUB), 1)

    for u in reversed(range(SB_KB // SB_SUB)):
        k_start = kj * SB_KB + u * SB_SUB

        @pl.when(k_start < q_start + SB_TQ)
        def _(u=u, k_start=k_start):
            mask = (k_start + k_off) < q_pos
            rows = slice(u * SB_SUB, (u + 1) * SB_SUB)
            for hd in range(4):
                cols = slice(hd * HEAD_DIM, (hd + 1) * HEAD_DIM)
                z = _dot_nt(q_ref[:, cols], k_ref[rows, cols])
                log_keep = jnp.where(mask, -(jnp.maximum(z, 0.0) + jnp.log1p(jnp.exp(-jnp.abs(z)))), 0.0)
                hi, lo = _split_bf16(log_keep)
                suffix = _dot(jnp.concatenate([hi, lo], axis=1), tri2)
                carry = carry_ref[hd]
                a = jnp.where(mask, jnp.exp(z + suffix + carry), 0.0)
                acc_ref[hd] += _dot(a.astype(BF16), v_ref[rows, cols])
                carry_ref[hd] = carry + suffix[:, 0:1]

    @pl.when(kj == 0)
    def _():
        for hd in range(4):
            o_ref[:, hd * HEAD_DIM:(hd + 1) * HEAD_DIM] = acc_ref[hd]


def _sb_attention(pb, batch, seq):
    nq = seq // SB_TQ
    qi, kj = [], []
    for i in range(nq):
        for j in reversed(range((i * SB_TQ + SB_TQ - 1) // SB_KB + 1)):
            qi.append(i)
            kj.append(j)
    qi = jnp.asarray(np.array(qi, np.int32))
    kj = jnp.asarray(np.array(kj, np.int32))
    grid_spec = pltpu.PrefetchScalarGridSpec(
        num_scalar_prefetch=2,
        grid=(batch, int(qi.shape[0])),
        in_specs=[
            pl.BlockSpec((None, SB_TQ, 256), lambda b, p, qi, kj: (b, qi[p], PB_SBQ // 256)),
            pl.BlockSpec((None, SB_KB, 256), lambda b, p, qi, kj: (b, kj[p], PB_SBK // 256)),
            pl.BlockSpec((None, SB_KB, 256), lambda b, p, qi, kj: (b, kj[p], PB_SBV // 256)),
        ],
        out_specs=pl.BlockSpec((None, SB_TQ, 256), lambda b, p, qi, kj: (b, qi[p], 0)),
        scratch_shapes=[pltpu.VMEM((4, SB_TQ, 1), F32), pltpu.VMEM((4, SB_TQ, HEAD_DIM), F32)],
    )
    return pl.pallas_call(
        _sb_kernel,
        out_shape=jax.ShapeDtypeStruct((batch, seq, 256), F32),
        grid_spec=grid_spec,
        compiler_params=pltpu.CompilerParams(dimension_semantics=("parallel", "arbitrary"),
                                             vmem_limit_bytes=VMEM_LIMIT),
        name="stick_breaking",
    )(qi, kj, pb, pb, pb)


def _banded_kernel(sink_ref, q_ref, kp_ref, kc_ref, vp_ref, vc_ref, bias_ref, o_ref, *, tq, window, use_sink):
    i = pl.program_id(1)
    qi = lax.broadcasted_iota(jnp.int32, (tq, tq), 0)
    ki = lax.broadcasted_iota(jnp.int32, (tq, tq), 1)
    dist_prev = qi + tq - ki
    dist_cur = qi - ki
    mask_prev = (dist_prev < window) & (i > 0)
    mask_cur = (dist_cur >= 0) & (dist_cur < window)
    for hd in range(4):
        cols = slice(hd * HEAD_DIM, (hd + 1) * HEAD_DIM)
        kcols = slice((hd // 2) * HEAD_DIM, (hd // 2 + 1) * HEAD_DIM)
        q = q_ref[:, cols]
        s_prev = jnp.where(mask_prev, _dot_nt(q, kp_ref[:, kcols]) + bias_ref[hd, :, :tq], NEG)
        s_cur = jnp.where(mask_cur, _dot_nt(q, kc_ref[:, kcols]) + bias_ref[hd, :, tq:], NEG)
        m = jnp.maximum(jnp.max(s_prev, axis=-1, keepdims=True), jnp.max(s_cur, axis=-1, keepdims=True))
        if use_sink:
            sink = sink_ref[hd]
            m = jnp.maximum(m, sink)
        p_prev = jnp.where(mask_prev, jnp.exp(s_prev - m), 0.0)
        p_cur = jnp.where(mask_cur, jnp.exp(s_cur - m), 0.0)
        denom = jnp.sum(p_prev, axis=-1, keepdims=True) + jnp.sum(p_cur, axis=-1, keepdims=True)
        if use_sink:
            denom = denom + jnp.exp(sink - m)
        o = _dot(p_prev.astype(BF16), vp_ref[:, kcols]) + _dot(p_cur.astype(BF16), vc_ref[:, kcols])
        o_ref[:, cols] = o / jnp.maximum(denom, 1e-30)


def _banded_attention(pb, sinks, bias, batch, seq, window, q_col, k_col, v_col, use_sink):
    tq = -(-(window - 1) // 128) * 128
    grid_spec = pltpu.PrefetchScalarGridSpec(
        num_scalar_prefetch=1,
        grid=(batch, seq // tq),
        in_specs=[
            pl.BlockSpec((None, tq, 256), lambda b, i, s: (b, i, q_col // 256)),
            pl.BlockSpec((None, tq, 128), lambda b, i, s: (b, jnp.maximum(i - 1, 0), k_col // 128)),
            pl.BlockSpec((None, tq, 128), lambda b, i, s: (b, i, k_col // 128)),
            pl.BlockSpec((None, tq, 128), lambda b, i, s: (b, jnp.maximum(i - 1, 0), v_col // 128)),
            pl.BlockSpec((None, tq, 128), lambda b, i, s: (b, i, v_col // 128)),
            pl.BlockSpec((4, tq, 2 * tq), lambda b, i, s: (0, 0, 0)),
        ],
        out_specs=pl.BlockSpec((None, tq, 256), lambda b, i, s: (b, i, 0)),
    )
    return pl.pallas_call(
        functools.partial(_banded_kernel, tq=tq, window=window, use_sink=use_sink),
        out_shape=jax.ShapeDtypeStruct((batch, seq, 256), F32),
        grid_spec=grid_spec,
        compiler_params=pltpu.CompilerParams(dimension_semantics=("parallel", "arbitrary"),
                                             vmem_limit_bytes=VMEM_LIMIT),
        name="banded_w%d" % window,
    )(sinks, pb, pb, pb, pb, pb, bias)


def _compress_kernel(rows_ref, pos_ref, w1_ref, w2_ref, g_ref, o_ref):
    win = rows_ref[...] + pos_ref[...]
    hid = jax.nn.gelu(_dot(win.astype(BF16), w1_ref[...]), approximate=True)
    out = _dot(hid.astype(BF16), w2_ref[...])
    normed = _row_rms(out, HEAD_DIM) * g_ref[...]
    o_ref[...] = jnp.where(pl.program_id(0) == 0, normed, out).astype(BF16)


def _compress(rows, pos, w1, w2, gain, tn):
    _, bh, ncp, width = rows.shape
    return pl.pallas_call(
        _compress_kernel,
        out_shape=jax.ShapeDtypeStruct((2, bh, ncp, HEAD_DIM), BF16),
        grid=(2, bh, ncp // tn),
        in_specs=[
            pl.BlockSpec((None, None, tn, width), lambda c, r, n: (c, r, n, 0)),
            pl.BlockSpec((None, 1, width), lambda c, r, n: (c, 0, 0)),
            pl.BlockSpec((None, width, NSA_CMP_HIDDEN), lambda c, r, n: (c, 0, 0)),
            pl.BlockSpec((None, NSA_CMP_HIDDEN, HEAD_DIM), lambda c, r, n: (c, 0, 0)),
            pl.BlockSpec((1, HEAD_DIM), lambda c, r, n: (0, 0)),
        ],
        out_specs=pl.BlockSpec((None, None, tn, HEAD_DIM), lambda c, r, n: (c, r, n, 0)),
        compiler_params=pltpu.CompilerParams(dimension_semantics=("parallel", "parallel", "parallel"),
                                             vmem_limit_bytes=VMEM_LIMIT),
        name="nsa_compress",
    )(rows, pos, w1, w2, gain)


CMP_TQ = 256


def _cmp_kernel(q_ref, kc_ref, vc_ref, bias_ref, ov_ref, o_ref, sel_ref, *, ncp, n_sel_pad, topk):
    i = pl.program_id(2)
    q_pos = i * CMP_TQ + lax.broadcasted_iota(jnp.int32, (CMP_TQ, ncp), 0)
    cmp_end = lax.broadcasted_iota(jnp.int32, (CMP_TQ, ncp), 1) * NSA_CMP_STRIDE + (NSA_CMP_LEN - 1)
    mask = cmp_end <= q_pos
    kc = kc_ref[...]
    vc = vc_ref[...]
    p_sum = jnp.zeros((CMP_TQ, ncp), F32)
    for g in range(2):
        cols = slice(g * HEAD_DIM, (g + 1) * HEAD_DIM)
        s = jnp.where(mask, _dot_nt(q_ref[:, cols], kc) + bias_ref[g], NEG)
        m = jnp.max(s, axis=-1, keepdims=True)
        p = jnp.where(mask, jnp.exp(s - m), 0.0)
        p = p / jnp.maximum(jnp.sum(p, axis=-1, keepdims=True), 1e-30)
        o_ref[:, cols] = _dot(p.astype(BF16), vc)
        p_sum = p_sum + p
    imp = _dot_exact_rhs(p_sum, ov_ref[...])

    row_pos = i * CMP_TQ + lax.broadcasted_iota(jnp.int32, (CMP_TQ, n_sel_pad), 0)
    blk = lax.broadcasted_iota(jnp.int32, (CMP_TQ, n_sel_pad), 1)
    cur = row_pos >> int(math.log2(NSA_SEL_LEN))
    forced = (blk == 0) | (blk == cur) | (blk == cur - 1)
    valid = blk * NSA_SEL_LEN <= row_pos
    score = jnp.where(valid, imp + jnp.where(forced, FORCE_BONUS, 0.0), NEG)
    blk_f = blk.astype(F32)
    taken = jnp.zeros((CMP_TQ, n_sel_pad), F32)
    for _ in range(topk):
        best = jnp.max(score, axis=-1, keepdims=True)
        first = jnp.min(jnp.where(score == best, blk_f, float(n_sel_pad)), axis=-1, keepdims=True)
        hit = blk_f == first
        taken = jnp.where(hit, 1.0, taken)
        score = jnp.where(hit, -jnp.inf, score)
    sel_ref[...] = jnp.where(valid, taken, 0.0).astype(BF16)


def _cmp_attention(pb, kvc, bias_c, overlap, batch, seq):
    ncp = kvc.shape[2]
    n_sel_pad = overlap.shape[1]
    topk = min(NSA_TOPK, seq // NSA_SEL_LEN)
    return pl.pallas_call(
        functools.partial(_cmp_kernel, ncp=ncp, n_sel_pad=n_sel_pad, topk=topk),
        out_shape=(jax.ShapeDtypeStruct((batch, seq, 256), F32),
                   jax.ShapeDtypeStruct((batch, 2, seq, n_sel_pad), BF16)),
        grid=(batch, 2, seq // CMP_TQ),
        in_specs=[
            pl.BlockSpec((None, CMP_TQ, 128), lambda b, h, i: (b, i, PB_NSAQ // 128 + h)),
            pl.BlockSpec((None, None, ncp, HEAD_DIM), lambda b, h, i: (0, b * 2 + h, 0, 0)),
            pl.BlockSpec((None, None, ncp, HEAD_DIM), lambda b, h, i: (1, b * 2 + h, 0, 0)),
            pl.BlockSpec((2, CMP_TQ, ncp), lambda b, h, i: (h, i, 0)),
            pl.BlockSpec((ncp, n_sel_pad), lambda b, h, i: (0, 0)),
        ],
        out_specs=(pl.BlockSpec((None, CMP_TQ, 128), lambda b, h, i: (b, i, h)),
                   pl.BlockSpec((None, None, CMP_TQ, n_sel_pad), lambda b, h, i: (b, h, i, 0))),
        compiler_params=pltpu.CompilerParams(dimension_semantics=("parallel", "parallel", "arbitrary"),
                                             vmem_limit_bytes=VMEM_LIMIT),
        name="nsa_cmp_select",
    )(pb, kvc, kvc, bias_c, overlap)


SEL_T = 512


def _sel_kernel(qi_ref, kj_ref, q_ref, k_ref, v_ref, sel_ref, exp_ref, bias_ref, o_ref, m_ref, l_ref, acc_ref):
    p = pl.program_id(1)
    qi = qi_ref[p]
    kj = kj_ref[p]

    @pl.when(kj == 0)
    def _():
        m_ref[...] = jnp.full_like(m_ref, NEG)
        l_ref[...] = jnp.zeros_like(l_ref)
        acc_ref[...] = jnp.zeros_like(acc_ref)

    q_pos = qi * SEL_T + lax.broadcasted_iota(jnp.int32, (SEL_T, SEL_T), 0)
    k_pos = kj * SEL_T + lax.broadcasted_iota(jnp.int32, (SEL_T, SEL_T), 1)
    causal = k_pos <= q_pos
    for hk in range(2):
        chosen = _dot(sel_ref[hk], exp_ref[...]) > 0.5
        mask = causal & chosen
        kcols = slice(hk * HEAD_DIM, (hk + 1) * HEAD_DIM)
        for g in range(2):
            hd = hk * 2 + g
            cols = slice(hd * HEAD_DIM, (hd + 1) * HEAD_DIM)
            s = jnp.where(mask, _dot_nt(q_ref[:, cols], k_ref[:, kcols]) + bias_ref[hd], NEG)
            m_old = m_ref[hd]
            m_new = jnp.maximum(m_old, jnp.max(s, axis=-1, keepdims=True))
            alpha = jnp.exp(m_old - m_new)
            pr = jnp.where(mask, jnp.exp(s - m_new), 0.0)
            l_ref[hd] = alpha * l_ref[hd] + jnp.sum(pr, axis=-1, keepdims=True)
            acc_ref[hd] = alpha * acc_ref[hd] + _dot(pr.astype(BF16), v_ref[:, kcols])
            m_ref[hd] = m_new

    @pl.when(kj == qi)
    def _():
        for hd in range(4):
            o_ref[:, hd * HEAD_DIM:(hd + 1) * HEAD_DIM] = acc_ref[hd] / jnp.maximum(l_ref[hd], 1e-30)


def _causal_pairs(n):
    qi, kj = [], []
    for i in range(n):
        for j in range(i + 1):
            qi.append(i)
            kj.append(j)
    return jnp.asarray(np.array(qi, np.int32)), jnp.asarray(np.array(kj, np.int32))


def _sel_attention(pb, sel, expand, bias_s, batch, seq):
    n_sel_pad = sel.shape[3]
    qi, kj = _causal_pairs(seq // SEL_T)
    grid_spec = pltpu.PrefetchScalarGridSpec(
        num_scalar_prefetch=2,
        grid=(batch, int(qi.shape[0])),
        in_specs=[
            pl.BlockSpec((None, SEL_T, 256), lambda b, p, qi, kj: (b, qi[p], PB_NSAQ // 256)),
            pl.BlockSpec((None, SEL_T, 128), lambda b, p, qi, kj: (b, kj[p], PB_NSAKS // 128)),
            pl.BlockSpec((None, SEL_T, 128), lambda b, p, qi, kj: (b, kj[p], PB_NSAVS // 128)),
            pl.BlockSpec((None, 2, SEL_T, n_sel_pad), lambda b, p, qi, kj: (b, 0, qi[p], 0)),
            pl.BlockSpec((n_sel_pad, SEL_T), lambda b, p, qi, kj: (0, kj[p])),
            pl.BlockSpec((4, None, SEL_T, SEL_T), lambda b, p, qi, kj: (0, qi[p] - kj[p], 0, 0)),
        ],
        out_specs=pl.BlockSpec((None, SEL_T, 256), lambda b, p, qi, kj: (b, qi[p], 0)),
        scratch_shapes=[pltpu.VMEM((4, SEL_T, 1), F32), pltpu.VMEM((4, SEL_T, 1), F32),
                        pltpu.VMEM((4, SEL_T, HEAD_DIM), F32)],
    )
    return pl.pallas_call(
        _sel_kernel,
        out_shape=jax.ShapeDtypeStruct((batch, seq, 256), F32),
        grid_spec=grid_spec,
        compiler_params=pltpu.CompilerParams(dimension_semantics=("parallel", "arbitrary"),
                                             vmem_limit_bytes=VMEM_LIMIT),
        name="nsa_selected",
    )(qi, kj, pb, pb, pb, sel, expand, bias_s)


MLA_T = 512


def _mla_kernel(qi_ref, kj_ref, q_ref, k_ref, v_ref, o_ref, m_ref, l_ref, acc_ref):
    p = pl.program_id(1)
    qi = qi_ref[p]
    kj = kj_ref[p]

    @pl.when(kj == 0)
    def _():
        m_ref[...] = jnp.full_like(m_ref, NEG)
        l_ref[...] = jnp.zeros_like(l_ref)
        acc_ref[...] = jnp.zeros_like(acc_ref)

    q_pos = qi * MLA_T + lax.broadcasted_iota(jnp.int32, (MLA_T, MLA_T), 0)
    k_pos = kj * MLA_T + lax.broadcasted_iota(jnp.int32, (MLA_T, MLA_T), 1)
    mask = k_pos <= q_pos
    for hd in range(MLA_HEADS):
        cols = slice(hd * LANES, (hd + 1) * LANES)
        vcols = slice(hd * MLA_V, (hd + 1) * MLA_V)
        s = jnp.where(mask, _dot_nt(q_ref[:, cols], k_ref[:, cols]), NEG)
        m_old = m_ref[hd]
        m_new = jnp.maximum(m_old, jnp.max(s, axis=-1, keepdims=True))
        alpha = jnp.exp(m_old - m_new)
        pr = jnp.where(mask, jnp.exp(s - m_new), 0.0)
        l_ref[hd] = alpha * l_ref[hd] + jnp.sum(pr, axis=-1, keepdims=True)
        acc_ref[hd] = alpha * acc_ref[hd] + _dot(pr.astype(BF16), v_ref[:, vcols])
        m_ref[hd] = m_new

    @pl.when(kj == qi)
    def _():
        for hd in range(MLA_HEADS):
            o_ref[:, hd * MLA_V:(hd + 1) * MLA_V] = acc_ref[hd] / jnp.maximum(l_ref[hd], 1e-30)


def _mla_attention(pb, batch, seq):
    qi, kj = _causal_pairs(seq // MLA_T)
    grid_spec = pltpu.PrefetchScalarGridSpec(
        num_scalar_prefetch=2,
        grid=(batch, int(qi.shape[0])),
        in_specs=[
            pl.BlockSpec((None, MLA_T, 512), lambda b, p, qi, kj: (b, qi[p], PB_MLAQ // 512)),
            pl.BlockSpec((None, MLA_T, 512), lambda b, p, qi, kj: (b, kj[p], PB_MLAK // 512)),
            pl.BlockSpec((None, MLA_T, 256), lambda b, p, qi, kj: (b, kj[p], PB_MLAV // 256)),
        ],
        out_specs=pl.BlockSpec((None, MLA_T, 256), lambda b, p, qi, kj: (b, qi[p], 0)),
        scratch_shapes=[pltpu.VMEM((4, MLA_T, 1), F32), pltpu.VMEM((4, MLA_T, 1), F32),
                        pltpu.VMEM((4, MLA_T, MLA_V), F32)],
    )
    return pl.pallas_call(
        _mla_kernel,
        out_shape=jax.ShapeDtypeStruct((batch, seq, 256), F32),
        grid_spec=grid_spec,
        compiler_params=pltpu.CompilerParams(dimension_semantics=("parallel", "arbitrary"),
                                             vmem_limit_bytes=VMEM_LIMIT),
        name="mla_causal",
    )(qi, kj, pb, pb, pb)


def _outproj_kernel(x_ref, oa_ref, ob_ref, oc_ref, os_ref, ow_ref, od_ref, gate_ref, gexp_ref, gn_ref, w_ref,
                    o_ref):
    gates = gate_ref[...]
    g_hi, g_lo = _split_bf16(gates)

    def gate(branch):
        e = gexp_ref[branch]
        return _dot(g_hi, e) + _dot(g_lo, e)

    o_nsa = gate(0) * oc_ref[...] + gate(1) * os_ref[...] + gate(2) * ow_ref[...]
    gn = gn_ref[...]
    y = x_ref[...]
    for grp, o in enumerate((oa_ref[...], ob_ref[...], o_nsa, od_ref[...])):
        cols = slice(grp * GROUP_WIDTH, (grp + 1) * GROUP_WIDTH)
        normed = _row_rms(o, GROUP_WIDTH) * gn[:, cols]
        y = y + _dot(normed.astype(BF16), w_ref[cols, :])
    o_ref[...] = y


def _outproj(x2, oa, ob, oc, osel, ow, od, pf, gexp, gn, w, tm):
    tokens = x2.shape[0]
    row = lambda t: (t, 0)
    o_spec = pl.BlockSpec((tm, GROUP_WIDTH), row)
    return pl.pallas_call(
        _outproj_kernel,
        out_shape=jax.ShapeDtypeStruct((tokens, D_MODEL), F32),
        grid=(tokens // tm,),
        in_specs=[
            pl.BlockSpec((tm, D_MODEL), row), o_spec, o_spec, o_spec, o_spec, o_spec, o_spec,
            pl.BlockSpec((tm, 128), lambda t: (t, PF_GATE // 128)),
            pl.BlockSpec((3, 128, GROUP_WIDTH), lambda t: (0, 0, 0)),
            pl.BlockSpec((1, D_MODEL), lambda t: (0, 0)),
            pl.BlockSpec((D_MODEL, D_MODEL), lambda t: (0, 0)),
        ],
        out_specs=pl.BlockSpec((tm, D_MODEL), row),
        compiler_params=pltpu.CompilerParams(dimension_semantics=("parallel",), vmem_limit_bytes=VMEM_LIMIT),
        name="out_proj",
    )(x2, oa, ob, oc, osel, ow, od, pf, gexp, gn, w)


MOE_TM = 1024


def _moe_kernel(x_ref, fn_ref, wr_hi_ref, wr_lo_ref, br_ref, wg_ref, wu_ref, wd_ref, o_ref, h_ref, comb_ref, acc_ref):
    e = pl.program_id(1)
    lane = lax.broadcasted_iota(jnp.int32, (MOE_TM, LANES), 1)

    @pl.when(e == 0)
    def _():
        h = _row_rms(x_ref[...], D_MODEL) * fn_ref[...]
        h_ref[...] = h.astype(BF16)
        h_hi, h_lo = _split_bf16(h)
        logits = (_dot(h_hi, wr_hi_ref[...]) + _dot(h_lo, wr_hi_ref[...]) + _dot(h_hi, wr_lo_ref[...])
                  + br_ref[...])
        lane_f = lane.astype(F32)
        no_lane = float(LANES)
        is_group = lane < MOE_GROUPS
        g_max = jnp.max(jnp.where(is_group, logits, -jnp.inf), axis=-1, keepdims=True)
        g_star = jnp.min(jnp.where(is_group & (logits == g_max), lane_f, no_lane), axis=-1, keepdims=True)
        g_den = jnp.sum(jnp.where(is_group, jnp.exp(logits - g_max), 0.0), axis=-1, keepdims=True)
        g_w = 1.0 / g_den
        group_of_lane = ((lane - MOE_GROUPS) >> int(math.log2(MOE_EPG))).astype(F32)
        in_group = (lane >= MOE_GROUPS) & (lane < MOE_GROUPS + MOE_EXPERTS) & (group_of_lane == g_star)
        e_l = jnp.where(in_group, logits, -jnp.inf)
        top1 = jnp.max(e_l, axis=-1, keepdims=True)
        i1 = jnp.min(jnp.where(e_l == top1, lane_f, no_lane), axis=-1, keepdims=True)
        e_l2 = jnp.where(lane_f == i1, -jnp.inf, e_l)
        top2 = jnp.max(e_l2, axis=-1, keepdims=True)
        i2 = jnp.min(jnp.where(e_l2 == top2, lane_f, no_lane), axis=-1, keepdims=True)
        r = jnp.exp(top2 - top1)
        w1 = g_w / (1.0 + r)
        w2 = g_w * r / (1.0 + r)
        comb_ref[...] = jnp.where(lane_f == i1, w1, jnp.where(lane_f == i2, w2, 0.0))
        acc_ref[...] = jnp.zeros_like(acc_ref)

    c_e = jnp.sum(jnp.where(lane == e + MOE_GROUPS, comb_ref[...], 0.0), axis=-1, keepdims=True)
    hb = h_ref[...]
    a = jax.nn.silu(_dot(hb, wg_ref[...])) * _dot(hb, wu_ref[...])
    acc_ref[...] += _dot((a * c_e).astype(BF16), wd_ref[...])

    @pl.when(e == MOE_EXPERTS - 1)
    def _():
        o_ref[...] = x_ref[...] + acc_ref[...]


def _moe(x2, fn, wr_hi, wr_lo, br, wg, wu, wd):
    tokens = x2.shape[0]
    const = lambda t, e: (0, 0)
    return pl.pallas_call(
        _moe_kernel,
        out_shape=jax.ShapeDtypeStruct((tokens, D_MODEL), F32),
        grid=(tokens // MOE_TM, MOE_EXPERTS),
        in_specs=[
            pl.BlockSpec((MOE_TM, D_MODEL), lambda t, e: (t, 0)),
            pl.BlockSpec((1, D_MODEL), const),
            pl.BlockSpec((D_MODEL, LANES), const),
            pl.BlockSpec((D_MODEL, LANES), const),
            pl.BlockSpec((1, LANES), const),
            pl.BlockSpec((None, D_MODEL, MOE_HIDDEN), lambda t, e: (e, 0, 0)),
            pl.BlockSpec((None, D_MODEL, MOE_HIDDEN), lambda t, e: (e, 0, 0)),
            pl.BlockSpec((None, MOE_HIDDEN, D_MODEL), lambda t, e: (e, 0, 0)),
        ],
        out_specs=pl.BlockSpec((MOE_TM, D_MODEL), lambda t, e: (t, 0)),
        scratch_shapes=[pltpu.VMEM((MOE_TM, D_MODEL), BF16), pltpu.VMEM((MOE_TM, LANES), F32),
                        pltpu.VMEM((MOE_TM, D_MODEL), F32)],
        compiler_params=pltpu.CompilerParams(dimension_semantics=("parallel", "arbitrary"),
                                             vmem_limit_bytes=VMEM_LIMIT),
        name="hier_moe",
    )(x2, fn, wr_hi, wr_lo, br, wg, wu, wd)


def _t5_bucket(dist):
    n = jnp.maximum(dist, 0)
    max_exact = T5_BUCKETS // 2
    nf = jnp.maximum(n, 1).astype(F32)
    large = max_exact + (jnp.log(nf / max_exact) / math.log(T5_MAX_DIST / max_exact)
                         * (T5_BUCKETS - max_exact)).astype(jnp.int32)
    large = jnp.minimum(large, T5_BUCKETS - 1)
    return jnp.where(n < max_exact, n, large)


def _position_tables(rel_bias, seq):
    by_dist = rel_bias.T[:, _t5_bucket(jnp.arange(seq))]

    def toeplitz(heads, dist):
        return by_dist[heads][:, jnp.clip(dist, 0, seq - 1)]

    swa_h, nsa_h = slice(0, 4), slice(4, 8)
    tq_swa = -(-(SWA_WINDOW - 1) // 128) * 128
    tq_win = -(-(NSA_WINDOW - 1) // 128) * 128
    band = lambda tq: jnp.arange(tq)[:, None] + tq - jnp.arange(2 * tq)[None, :]
    bias_swa = toeplitz(swa_h, band(tq_swa))
    bias_win = toeplitz(nsa_h, band(tq_win))
    ncp = seq // NSA_CMP_STRIDE
    cmp_end = jnp.arange(ncp) * NSA_CMP_STRIDE + NSA_CMP_LEN - 1
    bias_cmp = toeplitz(nsa_h, jnp.arange(seq)[:, None] - cmp_end[None, :])
    nd = seq // SEL_T
    dist_s = (jnp.arange(nd)[:, None, None] * SEL_T + jnp.arange(SEL_T)[None, :, None]
              - jnp.arange(SEL_T)[None, None, :])
    bias_sel = toeplitz(nsa_h, dist_s)

    n_sel_pad = -(-(seq // NSA_SEL_LEN) // LANES) * LANES
    sel_start = np.arange(n_sel_pad) * NSA_SEL_LEN
    c_start = np.arange(ncp) * NSA_CMP_STRIDE
    c_end = c_start + NSA_CMP_LEN - 1
    real = (np.arange(ncp) < ncp - NSA_CMP_LEN // NSA_CMP_STRIDE + 1)[:, None] & (sel_start < seq)[None, :]
    overlap = ((c_start[:, None] < sel_start[None, :] + NSA_SEL_LEN) & (c_end[:, None] >= sel_start[None, :]) & real)
    expand = (np.arange(n_sel_pad)[:, None] == (np.arange(seq) // NSA_SEL_LEN)[None, :])

    pos = jnp.arange(seq, dtype=F32)
    inv_freq = ROPE_THETA ** (-jnp.arange(0, MLA_ROPE, 2, dtype=F32) / MLA_ROPE)
    ang = pos[:, None] * inv_freq[None, :]
    cos, sin = jnp.cos(ang), jnp.sin(ang)
    ones = jnp.ones((seq, MLA_NOPE), F32)
    tail = LANES - MLA_NOPE - MLA_ROPE
    cos_t = jnp.concatenate([ones, cos, cos, jnp.ones((seq, tail), F32)], axis=1)
    sin_t = jnp.concatenate([0 * ones, -sin, sin, jnp.zeros((seq, tail), F32)], axis=1)
    return dict(bias_swa=bias_swa, bias_win=bias_win, bias_cmp=bias_cmp, bias_sel=bias_sel,
                overlap=jnp.asarray(overlap, BF16), expand=jnp.asarray(expand, BF16), cos_t=cos_t, sin_t=sin_t)


def _pad_to(a, shape):
    return jnp.pad(a, [(0, s - d) for d, s in zip(a.shape, shape)])


def _pack_layer(w_in, swa_q_norm, swa_k_norm, nsa_q_norm, nsa_k_norm, mla_q_lat_norm, mla_w_q_up,
                mla_kv_lat_norm, mla_w_kv_up, mla_q_norm, mla_k_norm):
    kpe = w_in[:, 2636:2668]
    kpe_seg = jnp.concatenate([jnp.zeros((D_MODEL, MLA_NOPE), F32), kpe,
                               jnp.zeros((D_MODEL, LANES - MLA_QK), F32)], axis=1)
    w = jnp.concatenate([
        w_in[:, :2304],
        _pad_to(w_in[:, 2304:2316], (D_MODEL, 128)),
        _pad_to(w_in[:, 2316:2508], (D_MODEL, 256)),
        w_in[:, 2508:2636],
        jnp.tile(kpe_seg, (1, MLA_HEADS)),
    ], axis=1).astype(BF16)
    tile4 = lambda g: jnp.tile(g, 4)
    g64 = _pad_to(jnp.stack([tile4(swa_q_norm), tile4(swa_k_norm), tile4(nsa_q_norm),
                             tile4(nsa_k_norm[1]), tile4(nsa_k_norm[2])]), (8, 256))
    glat = _pad_to(jnp.stack([_pad_to(mla_q_lat_norm, (256,)), _pad_to(mla_kv_lat_norm, (256,))]), (8, 256))
    wq = _pad_to(mla_w_q_up.reshape(MLA_Q_RANK, MLA_HEADS, MLA_QK), (256, MLA_HEADS, LANES))
    wq = wq.reshape(256, MLA_HEADS * LANES).astype(BF16)
    wkv = mla_w_kv_up.reshape(MLA_KV_RANK, MLA_HEADS, MLA_NOPE + MLA_V)
    wk = _pad_to(wkv[:, :, :MLA_NOPE], (MLA_KV_RANK, MLA_HEADS, LANES)).reshape(MLA_KV_RANK, MLA_HEADS * LANES)
    wv = wkv[:, :, MLA_NOPE:].reshape(MLA_KV_RANK, MLA_HEADS * MLA_V)
    wkv_p = jnp.concatenate([wk, wv], axis=1).astype(BF16)
    gmla = _pad_to(jnp.stack([jnp.tile(_pad_to(mla_q_norm, (LANES,)), MLA_HEADS),
                              jnp.tile(_pad_to(mla_k_norm, (LANES,)), MLA_HEADS)]), (8, 512))
    return w, g64, glat, wq, wkv_p, gmla


def _gate_expand():
    rows = np.arange(128)[None, :, None]
    cols = np.arange(GROUP_WIDTH)[None, None, :]
    branch = np.arange(3)[:, None, None]
    return jnp.asarray(rows == branch * 4 + cols // HEAD_DIM, BF16)


def _compress_rows(pf3, batch, seq):
    nb = seq // NSA_CMP_STRIDE
    kv = pf3[:, :, :256].reshape(batch, seq, 2, 2, HEAD_DIM).transpose(2, 0, 3, 1, 4)
    blocks = kv.reshape(2, batch * 2, nb, NSA_CMP_STRIDE * HEAD_DIM)
    nxt = jnp.concatenate([blocks[:, :, 1:], jnp.zeros_like(blocks[:, :, :1])], axis=2)
    return jnp.concatenate([blocks, nxt], axis=3)


def kernel(x, rel_bias, attn_norm, w_in, swa_q_norm, swa_k_norm, swa_sinks, nsa_q_norm, nsa_k_norm, nsa_cmp_pos, nsa_cmp_w1, nsa_cmp_w2, mla_q_lat_norm, mla_w_q_up, mla_kv_lat_norm, mla_w_kv_up, mla_q_norm, mla_k_norm, out_norm, w_out, ffn_norm, moe_w_group, moe_b_group, moe_w_expert, moe_b_expert, moe_w_gate, moe_w_up, moe_w_down):
    batch, seq, _ = x.shape
    depth = w_in.shape[0]
    tokens = batch * seq
    tm = 512
    assert seq % 2048 == 0 and tokens % MOE_TM == 0
    tabs = _position_tables(rel_bias, seq)
    gexp = _gate_expand()
    zero_sinks = jnp.zeros((4,), F32)
    x2 = x.reshape(tokens, D_MODEL)
    for l in range(depth):
        w, g64, glat, wq, wkv, gmla = _pack_layer(
            w_in[l], swa_q_norm[l], swa_k_norm[l], nsa_q_norm[l], nsa_k_norm[l], mla_q_lat_norm[l],
            mla_w_q_up[l], mla_kv_lat_norm[l], mla_w_kv_up[l], mla_q_norm[l], mla_k_norm[l])
        pb, pf = _prep(x2, attn_norm[l][None, :], w, g64, glat, wq, wkv, gmla, tabs["cos_t"], tabs["sin_t"],
                       seq, tm)
        pb3 = pb.reshape(batch, seq, PB_WIDTH)
        pf3 = pf.reshape(batch, seq, PF_WIDTH)
        o_a = _sb_attention(pb3, batch, seq)
        o_b = _banded_attention(pb3, swa_sinks[l], tabs["bias_swa"], batch, seq, SWA_WINDOW,
                                PB_SWAQ, PB_SWAK, PB_SWAV, True)
        rows = _compress_rows(pf3, batch, seq)
        kvc = _compress(rows, nsa_cmp_pos[l].reshape(2, 1, -1),
                        nsa_cmp_w1[l].reshape(2, -1, NSA_CMP_HIDDEN).astype(BF16), nsa_cmp_w2[l].astype(BF16),
                        nsa_k_norm[l][0][None, :], 128)
        o_c, sel = _cmp_attention(pb3, kvc, tabs["bias_cmp"], tabs["overlap"], batch, seq)
        o_s = _sel_attention(pb3, sel, tabs["expand"], tabs["bias_sel"], batch, seq)
        o_w = _banded_attention(pb3, zero_sinks, tabs["bias_win"], batch, seq, NSA_WINDOW,
                                PB_NSAQ, PB_NSAKW, PB_NSAVW, False)
        o_d = _mla_attention(pb3, batch, seq)
        flat = lambda o: o.reshape(tokens, GROUP_WIDTH)
        x2 = _outproj(x2, flat(o_a), flat(o_b), flat(o_c), flat(o_s), flat(o_w), flat(o_d), pf, gexp,
                      out_norm[l][None, :], w_out[l].astype(BF16), tm)
        w_router = _pad_to(jnp.concatenate([moe_w_group[l], moe_w_expert[l]], axis=1), (D_MODEL, LANES))
        wr_hi = w_router.astype(BF16)
        wr_lo = (w_router - wr_hi.astype(F32)).astype(BF16)
        b_router = _pad_to(jnp.concatenate([moe_b_group[l], moe_b_expert[l]])[None, :], (1, LANES))
        x2 = _moe(x2, ffn_norm[l][None, :], wr_hi, wr_lo, b_router, moe_w_gate[l].astype(BF16),
                  moe_w_up[l].astype(BF16), moe_w_down[l].astype(BF16))
    return x2.reshape(batch, seq, D_MODEL)
```

```python
import functools
import math

import numpy as np
import jax
import jax.numpy as jnp
from jax import lax
from jax.experimental import pallas as pl
from jax.experimental.pallas import tpu as pltpu

F32 = jnp.float32
BF16 = jnp.bfloat16

D_MODEL = 1024
HEAD_DIM = 64
NEG = -1e30
EPS = 1e-6
FORCE_BONUS = 1000.0
SWA_WINDOW = 128
NSA_CMP_LEN = 32
NSA_CMP_STRIDE = 16
NSA_CMP_HIDDEN = 128
NSA_SEL_LEN = 64
NSA_TOPK = 16
NSA_WINDOW = 512
MLA_HEADS = 4
MLA_NOPE = 64
MLA_ROPE = 32
MLA_V = 64
MLA_Q_RANK = 192
MLA_KV_RANK = 128
MLA_QK = MLA_NOPE + MLA_ROPE
ROPE_THETA = 10000.0
T5_BUCKETS = 32
T5_MAX_DIST = 1024
MOE_GROUPS = 4
MOE_EPG = 8
MOE_EXPERTS = MOE_GROUPS * MOE_EPG
MOE_HIDDEN = 256
GROUP_WIDTH = 256
LANES = 128
VMEM_LIMIT = 48 * 1024 * 1024

PB_SBQ, PB_SBK, PB_SBV = 0, 256, 512
PB_SWAQ, PB_SWAK, PB_SWAV = 768, 1024, 1152
PB_NSAQ, PB_NSAKS, PB_NSAVS, PB_NSAKW, PB_NSAVW = 1280, 1536, 1664, 1792, 1920
PB_MLAQ, PB_MLAK, PB_MLAV = 2048, 2560, 3072
PB_WIDTH = 3328
PF_KC, PF_VC, PF_GATE = 0, 128, 256
PF_WIDTH = 384
W_CQ, W_CKV, W_KPE = 2432, 2688, 2816
W_WIDTH = 3328

NT_DIMS = (((1,), (1,)), ((), ()))


def _dot(a, b):
    return jnp.dot(a, b, preferred_element_type=F32)


def _dot_nt(a, b):
    return lax.dot_general(a, b, NT_DIMS, preferred_element_type=F32)


def _split_bf16(x):
    hi = x.astype(BF16)
    lo = (x - hi.astype(F32)).astype(BF16)
    return hi, lo


def _dot_exact_rhs(x, m):
    hi, lo = _split_bf16(x)
    return _dot(hi, m) + _dot(lo, m)


def _block_diag_ones(width, seg):
    idx = np.arange(width) // seg
    return jnp.asarray(idx[:, None] == idx[None, :], BF16)


def _seg_rms(x, seg_ones, count):
    width = x.shape[1]
    ms = _dot_exact_rhs(x * x, seg_ones[:width, :width]) * (1.0 / count)
    return x * lax.rsqrt(ms + EPS)


def _row_rms(x, count):
    return x * lax.rsqrt(jnp.sum(x * x, axis=-1, keepdims=True) * (1.0 / count) + EPS)


def _prep_kernel(x_ref, an_ref, w_ref, g64_ref, glat_ref, wq_ref, wkv_ref, gmla_ref, cos_ref, sin_ref,
                 s64_ref, s128_ref, pb_ref, pf_ref):
    x = x_ref[...]
    s64 = s64_ref[...]
    s128 = s128_ref[...]
    h = _row_rms(x, D_MODEL) * an_ref[...]
    hb = h.astype(BF16)

    def proj(lo, hi):
        return _dot(hb, w_ref[:, lo:hi])

    scale = HEAD_DIM ** -0.5
    pb_ref[:, PB_SBQ:PB_SBQ + 256] = (proj(0, 256) * scale).astype(BF16)
    pb_ref[:, PB_SBK:PB_SBK + 256] = proj(256, 512).astype(BF16)
    pb_ref[:, PB_SBV:PB_SBV + 256] = proj(512, 768).astype(BF16)
    g64 = g64_ref[...]
    pb_ref[:, PB_SWAQ:PB_SWAQ + 256] = (_seg_rms(proj(768, 1024), s64, 64) * g64[0:1, :] * scale).astype(BF16)
    pb_ref[:, PB_SWAK:PB_SWAK + 128] = (_seg_rms(proj(1024, 1152), s64, 64) * g64[1:2, :128]).astype(BF16)
    pb_ref[:, PB_SWAV:PB_SWAV + 128] = proj(1152, 1280).astype(BF16)
    pb_ref[:, PB_NSAQ:PB_NSAQ + 256] = (_seg_rms(proj(1280, 1536), s64, 64) * g64[2:3, :] * scale).astype(BF16)
    pf_ref[:, PF_KC:PF_KC + 128] = proj(1536, 1664)
    pf_ref[:, PF_VC:PF_VC + 128] = proj(1664, 1792)
    pb_ref[:, PB_NSAKS:PB_NSAKS + 128] = (_seg_rms(proj(1792, 1920), s64, 64) * g64[3:4, :128]).astype(BF16)
    pb_ref[:, PB_NSAVS:PB_NSAVS + 128] = proj(1920, 2048).astype(BF16)
    pb_ref[:, PB_NSAKW:PB_NSAKW + 128] = (_seg_rms(proj(2048, 2176), s64, 64) * g64[4:5, :128]).astype(BF16)
    pb_ref[:, PB_NSAVW:PB_NSAVW + 128] = proj(2176, 2304).astype(BF16)
    pf_ref[:, PF_GATE:PF_GATE + 128] = jax.nn.sigmoid(proj(2304, 2432))

    glat = glat_ref[...]
    cq = _row_rms(proj(W_CQ, W_CQ + 256), MLA_Q_RANK) * glat[0:1, :]
    q = _dot(cq.astype(BF16), wq_ref[...])
    ckv = _row_rms(proj(W_CKV, W_CKV + 128), MLA_KV_RANK) * glat[1:2, :128]
    kv = _dot(ckv.astype(BF16), wkv_ref[...])
    k = kv[:, :512] + proj(W_KPE, W_KPE + 512)
    gm = gmla_ref[...]
    q = _seg_rms(q, s128, MLA_QK) * gm[0:1, :]
    k = _seg_rms(k, s128, MLA_QK) * gm[1:2, :]
    cos = cos_ref[...]
    sin = sin_ref[...]
    lane = lax.broadcasted_iota(jnp.int32, (x.shape[0], LANES), 1)
    first_half = lane < MLA_NOPE + MLA_ROPE // 2

    def rope(t):
        partner = jnp.where(first_half, pltpu.roll(t, LANES - MLA_ROPE // 2, 1), pltpu.roll(t, MLA_ROPE // 2, 1))
        return t * cos + partner * sin

    qscale = MLA_QK ** -0.5
    for hd in range(MLA_HEADS):
        sl = slice(hd * LANES, (hd + 1) * LANES)
        pb_ref[:, PB_MLAQ + hd * LANES:PB_MLAQ + (hd + 1) * LANES] = (rope(q[:, sl]) * qscale).astype(BF16)
        pb_ref[:, PB_MLAK + hd * LANES:PB_MLAK + (hd + 1) * LANES] = rope(k[:, sl]).astype(BF16)
    pb_ref[:, PB_MLAV:PB_MLAV + 256] = kv[:, 512:768].astype(BF16)


def _prep(x2, an, w, g64, glat, wq, wkv, gmla, cos_t, sin_t, seq, tm):
    tokens = x2.shape[0]
    n_pos = seq // tm
    const = lambda t: (0, 0)
    return pl.pallas_call(
        _prep_kernel,
        out_shape=(jax.ShapeDtypeStruct((tokens, PB_WIDTH), BF16), jax.ShapeDtypeStruct((tokens, PF_WIDTH), F32)),
        grid=(tokens // tm,),
        in_specs=[
            pl.BlockSpec((tm, D_MODEL), lambda t: (t, 0)),
            pl.BlockSpec((1, D_MODEL), const),
            pl.BlockSpec((D_MODEL, W_WIDTH), const),
            pl.BlockSpec((8, 256), const),
            pl.BlockSpec((8, 256), const),
            pl.BlockSpec((256, 512), const),
            pl.BlockSpec((128, 768), const),
            pl.BlockSpec((8, 512), const),
            pl.BlockSpec((tm, LANES), lambda t: (t % n_pos, 0)),
            pl.BlockSpec((tm, LANES), lambda t: (t % n_pos, 0)),
            pl.BlockSpec((256, 256), const),
            pl.BlockSpec((512, 512), const),
        ],
        out_specs=(pl.BlockSpec((tm, PB_WIDTH), lambda t: (t, 0)), pl.BlockSpec((tm, PF_WIDTH), lambda t: (t, 0))),
        compiler_params=pltpu.CompilerParams(dimension_semantics=("parallel",), vmem_limit_bytes=VMEM_LIMIT),
        name="prep",
    )(x2, an, w, g64, glat, wq, wkv, gmla, cos_t, sin_t, _block_diag_ones(256, HEAD_DIM),
      _block_diag_ones(512, LANES))


SB_TQ, SB_KB, SB_SUB = 256, 512, 128


SB_UNDERFLOW = 110.0


def _sb_kernel(qi_ref, kj_ref, kmax_ref, q_ref, k_ref, v_ref, o_ref, carry_ref, acc_ref, qn_ref):
    b = pl.program_id(0)
    p = pl.program_id(1)
    qi = qi_ref[p]
    kj = kj_ref[p]
    q_start = qi * SB_TQ
    first = kj == (q_start + SB_TQ - 1) // SB_KB

    @pl.when(first)
    def _():
        carry_ref[...] = jnp.zeros_like(carry_ref)
        acc_ref[...] = jnp.zeros_like(acc_ref)
        for hd in range(4):
            q = q_ref[:, hd * HEAD_DIM:(hd + 1) * HEAD_DIM].astype(F32)
            qn_ref[hd] = jnp.sqrt(jnp.sum(q * q, axis=-1, keepdims=True))

    r = lax.broadcasted_iota(jnp.int32, (SB_SUB, SB_SUB), 0)
    c = lax.broadcasted_iota(jnp.int32, (SB_SUB, SB_SUB), 1)
    tri = jnp.where(r >= c, 1.0, 0.0).astype(BF16)
    tri2 = jnp.concatenate([tri, tri], axis=0)
    rel = (lax.broadcasted_iota(jnp.int32, (SB_TQ, SB_SUB), 1)
           - lax.broadcasted_iota(jnp.int32, (SB_TQ, SB_SUB), 0))

    for hd in range(4):
        cols = slice(hd * HEAD_DIM, (hd + 1) * HEAD_DIM)
        z_bound = qn_ref[hd] * kmax_ref[b * 4 + hd]
        for u in reversed(range(SB_KB // SB_SUB)):
            k_start = kj * SB_KB + u * SB_SUB
            live = jnp.max(carry_ref[hd] + z_bound) > -SB_UNDERFLOW

            @pl.when((k_start < q_start + SB_TQ) & live)
            def _(u=u, k_start=k_start, hd=hd, cols=cols):
                mask = rel < q_start - k_start
                rows = slice(u * SB_SUB, (u + 1) * SB_SUB)
                z = _dot_nt(q_ref[:, cols], k_ref[rows, cols])
                log_keep = jnp.where(mask, -(jnp.maximum(z, 0.0) + jnp.log1p(jnp.exp(-jnp.abs(z)))), 0.0)
                hi, lo = _split_bf16(log_keep)
                suffix = _dot(jnp.concatenate([hi, lo], axis=1), tri2)
                carry = carry_ref[hd]
                a = jnp.where(mask, jnp.exp(z + suffix + carry), 0.0)
                acc_ref[hd] += _dot(a.astype(BF16), v_ref[rows, cols])
                carry_ref[hd] = carry + suffix[:, 0:1]

    @pl.when(kj == 0)
    def _():
        for hd in range(4):
            o_ref[:, hd * HEAD_DIM:(hd + 1) * HEAD_DIM] = acc_ref[hd]


def _sb_attention(pb, batch, seq):
    nq = seq // SB_TQ
    qi, kj = [], []
    for i in range(nq):
        for j in reversed(range((i * SB_TQ + SB_TQ - 1) // SB_KB + 1)):
            qi.append(i)
            kj.append(j)
    qi = jnp.asarray(np.array(qi, np.int32))
    kj = jnp.asarray(np.array(kj, np.int32))
    keys = pb[:, :, PB_SBK:PB_SBK + 256].astype(F32).reshape(batch, seq, 4, HEAD_DIM)
    kmax = jnp.sqrt(jnp.max(jnp.sum(keys * keys, axis=-1), axis=1)).reshape(batch * 4)
    grid_spec = pltpu.PrefetchScalarGridSpec(
        num_scalar_prefetch=2,
        grid=(batch, int(qi.shape[0])),
        in_specs=[
            pl.BlockSpec(memory_space=pltpu.SMEM),
            pl.BlockSpec((None, SB_TQ, 256), lambda b, p, qi, kj: (b, qi[p], PB_SBQ // 256)),
            pl.BlockSpec((None, SB_KB, 256), lambda b, p, qi, kj: (b, kj[p], PB_SBK // 256)),
            pl.BlockSpec((None, SB_KB, 256), lambda b, p, qi, kj: (b, kj[p], PB_SBV // 256)),
        ],
        out_specs=pl.BlockSpec((None, SB_TQ, 256), lambda b, p, qi, kj: (b, qi[p], 0)),
        scratch_shapes=[pltpu.VMEM((4, SB_TQ, 1), F32), pltpu.VMEM((4, SB_TQ, HEAD_DIM), F32),
                        pltpu.VMEM((4, SB_TQ, 1), F32)],
    )
    return pl.pallas_call(
        _sb_kernel,
        out_shape=jax.ShapeDtypeStruct((batch, seq, 256), F32),
        grid_spec=grid_spec,
        compiler_params=pltpu.CompilerParams(dimension_semantics=("parallel", "arbitrary"),
                                             vmem_limit_bytes=VMEM_LIMIT),
        name="stick_breaking",
    )(qi, kj, kmax, pb, pb, pb)


def _banded_kernel(sink_ref, q_ref, kp_ref, kc_ref, vp_ref, vc_ref, bias_ref, o_ref, *, tq, window, use_sink):
    i = pl.program_id(1)
    qi = lax.broadcasted_iota(jnp.int32, (tq, tq), 0)
    ki = lax.broadcasted_iota(jnp.int32, (tq, tq), 1)
    dist_prev = qi + tq - ki
    dist_cur = qi - ki
    mask_prev = (dist_prev < window) & (i > 0)
    mask_cur = (dist_cur >= 0) & (dist_cur < window)
    for hd in range(4):
        cols = slice(hd * HEAD_DIM, (hd + 1) * HEAD_DIM)
        kcols = slice((hd // 2) * HEAD_DIM, (hd // 2 + 1) * HEAD_DIM)
        q = q_ref[:, cols]
        s_prev = jnp.where(mask_prev, _dot_nt(q, kp_ref[:, kcols]) + bias_ref[hd, :, :tq], NEG)
        s_cur = jnp.where(mask_cur, _dot_nt(q, kc_ref[:, kcols]) + bias_ref[hd, :, tq:], NEG)
        m = jnp.maximum(jnp.max(s_prev, axis=-1, keepdims=True), jnp.max(s_cur, axis=-1, keepdims=True))
        if use_sink:
            sink = sink_ref[hd]
            m = jnp.maximum(m, sink)
        p_prev = jnp.where(mask_prev, jnp.exp(s_prev - m), 0.0)
        p_cur = jnp.where(mask_cur, jnp.exp(s_cur - m), 0.0)
        denom = jnp.sum(p_prev, axis=-1, keepdims=True) + jnp.sum(p_cur, axis=-1, keepdims=True)
        if use_sink:
            denom = denom + jnp.exp(sink - m)
        o = _dot(p_prev.astype(BF16), vp_ref[:, kcols]) + _dot(p_cur.astype(BF16), vc_ref[:, kcols])
        o_ref[:, cols] = o / jnp.maximum(denom, 1e-30)


def _banded_attention(pb, sinks, bias, batch, seq, window, q_col, k_col, v_col, use_sink):
    tq = -(-(window - 1) // 128) * 128
    grid_spec = pltpu.PrefetchScalarGridSpec(
        num_scalar_prefetch=1,
        grid=(batch, seq // tq),
        in_specs=[
            pl.BlockSpec((None, tq, 256), lambda b, i, s: (b, i, q_col // 256)),
            pl.BlockSpec((None, tq, 128), lambda b, i, s: (b, jnp.maximum(i - 1, 0), k_col // 128)),
            pl.BlockSpec((None, tq, 128), lambda b, i, s: (b, i, k_col // 128)),
            pl.BlockSpec((None, tq, 128), lambda b, i, s: (b, jnp.maximum(i - 1, 0), v_col // 128)),
            pl.BlockSpec((None, tq, 128), lambda b, i, s: (b, i, v_col // 128)),
            pl.BlockSpec((4, tq, 2 * tq), lambda b, i, s: (0, 0, 0)),
        ],
        out_specs=pl.BlockSpec((None, tq, 256), lambda b, i, s: (b, i, 0)),
    )
    return pl.pallas_call(
        functools.partial(_banded_kernel, tq=tq, window=window, use_sink=use_sink),
        out_shape=jax.ShapeDtypeStruct((batch, seq, 256), F32),
        grid_spec=grid_spec,
        compiler_params=pltpu.CompilerParams(dimension_semantics=("parallel", "arbitrary"),
                                             vmem_limit_bytes=VMEM_LIMIT),
        name="banded_w%d" % window,
    )(sinks, pb, pb, pb, pb, pb, bias)


def _compress_kernel(rows_ref, pos_ref, w1_ref, w2_ref, g_ref, o_ref):
    win = rows_ref[...] + pos_ref[...]
    hid = jax.nn.gelu(_dot(win.astype(BF16), w1_ref[...]), approximate=True)
    out = _dot(hid.astype(BF16), w2_ref[...])
    normed = _row_rms(out, HEAD_DIM) * g_ref[...]
    o_ref[...] = jnp.where(pl.program_id(0) == 0, normed, out).astype(BF16)


def _compress(rows, pos, w1, w2, gain, tn):
    _, bh, ncp, width = rows.shape
    return pl.pallas_call(
        _compress_kernel,
        out_shape=jax.ShapeDtypeStruct((2, bh, ncp, HEAD_DIM), BF16),
        grid=(2, bh, ncp // tn),
        in_specs=[
            pl.BlockSpec((None, None, tn, width), lambda c, r, n: (c, r, n, 0)),
            pl.BlockSpec((None, 1, width), lambda c, r, n: (c, 0, 0)),
            pl.BlockSpec((None, width, NSA_CMP_HIDDEN), lambda c, r, n: (c, 0, 0)),
            pl.BlockSpec((None, NSA_CMP_HIDDEN, HEAD_DIM), lambda c, r, n: (c, 0, 0)),
            pl.BlockSpec((1, HEAD_DIM), lambda c, r, n: (0, 0)),
        ],
        out_specs=pl.BlockSpec((None, None, tn, HEAD_DIM), lambda c, r, n: (c, r, n, 0)),
        compiler_params=pltpu.CompilerParams(dimension_semantics=("parallel", "parallel", "parallel"),
                                             vmem_limit_bytes=VMEM_LIMIT),
        name="nsa_compress",
    )(rows, pos, w1, w2, gain)


CMP_TQ = 256


def _cmp_kernel(q_ref, kc_ref, vc_ref, bias_ref, ov_ref, o_ref, sel_ref, *, ncp, n_sel_pad, topk):
    i = pl.program_id(2)
    q_pos = i * CMP_TQ + lax.broadcasted_iota(jnp.int32, (CMP_TQ, ncp), 0)
    cmp_end = lax.broadcasted_iota(jnp.int32, (CMP_TQ, ncp), 1) * NSA_CMP_STRIDE + (NSA_CMP_LEN - 1)
    mask = cmp_end <= q_pos
    kc = kc_ref[...]
    vc = vc_ref[...]
    p_sum = jnp.zeros((CMP_TQ, ncp), F32)
    for g in range(2):
        cols = slice(g * HEAD_DIM, (g + 1) * HEAD_DIM)
        s = jnp.where(mask, _dot_nt(q_ref[:, cols], kc) + bias_ref[g], NEG)
        m = jnp.max(s, axis=-1, keepdims=True)
        p = jnp.where(mask, jnp.exp(s - m), 0.0)
        p = p / jnp.maximum(jnp.sum(p, axis=-1, keepdims=True), 1e-30)
        o_ref[:, cols] = _dot(p.astype(BF16), vc)
        p_sum = p_sum + p
    imp = _dot_exact_rhs(p_sum, ov_ref[...])

    row_pos = i * CMP_TQ + lax.broadcasted_iota(jnp.int32, (CMP_TQ, n_sel_pad), 0)
    blk = lax.broadcasted_iota(jnp.int32, (CMP_TQ, n_sel_pad), 1)
    cur = row_pos >> int(math.log2(NSA_SEL_LEN))
    forced = (blk == 0) | (blk == cur) | (blk == cur - 1)
    valid = blk * NSA_SEL_LEN <= row_pos
    score = jnp.where(valid, imp + jnp.where(forced, FORCE_BONUS, 0.0), NEG)
    blk_f = blk.astype(F32)
    taken = jnp.zeros((CMP_TQ, n_sel_pad), F32)
    for _ in range(topk):
        best = jnp.max(score, axis=-1, keepdims=True)
        first = jnp.min(jnp.where(score == best, blk_f, float(n_sel_pad)), axis=-1, keepdims=True)
        hit = blk_f == first
        taken = jnp.where(hit, 1.0, taken)
        score = jnp.where(hit, -jnp.inf, score)
    sel_ref[...] = jnp.where(valid, taken, 0.0).astype(BF16)


def _cmp_attention(pb, kvc, bias_c, overlap, batch, seq):
    ncp = kvc.shape[2]
    n_sel_pad = overlap.shape[1]
    topk = min(NSA_TOPK, seq // NSA_SEL_LEN)
    return pl.pallas_call(
        functools.partial(_cmp_kernel, ncp=ncp, n_sel_pad=n_sel_pad, topk=topk),
        out_shape=(jax.ShapeDtypeStruct((batch, seq, 256), F32),
                   jax.ShapeDtypeStruct((batch, 2, seq, n_sel_pad), BF16)),
        grid=(batch, 2, seq // CMP_TQ),
        in_specs=[
            pl.BlockSpec((None, CMP_TQ, 128), lambda b, h, i: (b, i, PB_NSAQ // 128 + h)),
            pl.BlockSpec((None, None, ncp, HEAD_DIM), lambda b, h, i: (0, b * 2 + h, 0, 0)),
            pl.BlockSpec((None, None, ncp, HEAD_DIM), lambda b, h, i: (1, b * 2 + h, 0, 0)),
            pl.BlockSpec((2, CMP_TQ, ncp), lambda b, h, i: (h, i, 0)),
            pl.BlockSpec((ncp, n_sel_pad), lambda b, h, i: (0, 0)),
        ],
        out_specs=(pl.BlockSpec((None, CMP_TQ, 128), lambda b, h, i: (b, i, h)),
                   pl.BlockSpec((None, None, CMP_TQ, n_sel_pad), lambda b, h, i: (b, h, i, 0))),
        compiler_params=pltpu.CompilerParams(dimension_semantics=("parallel", "parallel", "arbitrary"),
                                             vmem_limit_bytes=VMEM_LIMIT),
        name="nsa_cmp_select",
    )(pb, kvc, kvc, bias_c, overlap)


SEL_T = 512


def _sel_kernel(qi_ref, kj_ref, q_ref, k_ref, v_ref, sel_ref, exp_ref, bias_ref, o_ref, m_ref, l_ref, acc_ref):
    p = pl.program_id(1)
    qi = qi_ref[p]
    kj = kj_ref[p]

    @pl.when(kj == 0)
    def _():
        m_ref[...] = jnp.full_like(m_ref, NEG)
        l_ref[...] = jnp.zeros_like(l_ref)
        acc_ref[...] = jnp.zeros_like(acc_ref)

    def step(diagonal):
        for hk in range(2):
            mask = _dot(sel_ref[hk], exp_ref[...]) > 0.5
            if diagonal:
                row = lax.broadcasted_iota(jnp.int32, (SEL_T, SEL_T), 0)
                col = lax.broadcasted_iota(jnp.int32, (SEL_T, SEL_T), 1)
                mask = mask & (col <= row)
            kcols = slice(hk * HEAD_DIM, (hk + 1) * HEAD_DIM)
            for g in range(2):
                hd = hk * 2 + g
                cols = slice(hd * HEAD_DIM, (hd + 1) * HEAD_DIM)
                s = jnp.where(mask, _dot_nt(q_ref[:, cols], k_ref[:, kcols]) + bias_ref[hd], NEG)
                m_old = m_ref[hd]
                m_new = jnp.maximum(m_old, jnp.max(s, axis=-1, keepdims=True))
                alpha = jnp.exp(m_old - m_new)
                pr = jnp.exp(s - m_new)
                l_ref[hd] = alpha * l_ref[hd] + jnp.sum(pr, axis=-1, keepdims=True)
                acc_ref[hd] = alpha * acc_ref[hd] + _dot(pr.astype(BF16), v_ref[:, kcols])
                m_ref[hd] = m_new

    @pl.when(kj < qi)
    def _():
        step(False)

    @pl.when(kj == qi)
    def _():
        step(True)

    @pl.when(kj == qi)
    def _():
        for hd in range(4):
            o_ref[:, hd * HEAD_DIM:(hd + 1) * HEAD_DIM] = acc_ref[hd] / jnp.maximum(l_ref[hd], 1e-30)


def _causal_pairs(n):
    qi, kj = [], []
    for i in range(n):
        for j in range(i + 1):
            qi.append(i)
            kj.append(j)
    return jnp.asarray(np.array(qi, np.int32)), jnp.asarray(np.array(kj, np.int32))


def _sel_attention(pb, sel, expand, bias_s, batch, seq):
    n_sel_pad = sel.shape[3]
    n_delta = bias_s.shape[1]
    qi, kj = _causal_pairs(seq // SEL_T)
    grid_spec = pltpu.PrefetchScalarGridSpec(
        num_scalar_prefetch=2,
        grid=(batch, int(qi.shape[0])),
        in_specs=[
            pl.BlockSpec((None, SEL_T, 256), lambda b, p, qi, kj: (b, qi[p], PB_NSAQ // 256)),
            pl.BlockSpec((None, SEL_T, 128), lambda b, p, qi, kj: (b, kj[p], PB_NSAKS // 128)),
            pl.BlockSpec((None, SEL_T, 128), lambda b, p, qi, kj: (b, kj[p], PB_NSAVS // 128)),
            pl.BlockSpec((None, 2, SEL_T, n_sel_pad), lambda b, p, qi, kj: (b, 0, qi[p], 0)),
            pl.BlockSpec((n_sel_pad, SEL_T), lambda b, p, qi, kj: (0, kj[p])),
            pl.BlockSpec((4, None, SEL_T, SEL_T),
                         lambda b, p, qi, kj: (0, jnp.minimum(qi[p] - kj[p], n_delta - 1), 0, 0)),
        ],
        out_specs=pl.BlockSpec((None, SEL_T, 256), lambda b, p, qi, kj: (b, qi[p], 0)),
        scratch_shapes=[pltpu.VMEM((4, SEL_T, 1), F32), pltpu.VMEM((4, SEL_T, 1), F32),
                        pltpu.VMEM((4, SEL_T, HEAD_DIM), F32)],
    )
    return pl.pallas_call(
        _sel_kernel,
        out_shape=jax.ShapeDtypeStruct((batch, seq, 256), F32),
        grid_spec=grid_spec,
        compiler_params=pltpu.CompilerParams(dimension_semantics=("parallel", "arbitrary"),
                                             vmem_limit_bytes=VMEM_LIMIT),
        name="nsa_selected",
    )(qi, kj, pb, pb, pb, sel, expand, bias_s)


MLA_T = 512


def _mla_kernel(qi_ref, kj_ref, q_ref, k_ref, v_ref, o_ref, m_ref, l_ref, acc_ref):
    p = pl.program_id(1)
    qi = qi_ref[p]
    kj = kj_ref[p]

    @pl.when(kj == 0)
    def _():
        m_ref[...] = jnp.full_like(m_ref, NEG)
        l_ref[...] = jnp.zeros_like(l_ref)
        acc_ref[...] = jnp.zeros_like(acc_ref)

    def step(diagonal):
        for hd in range(MLA_HEADS):
            cols = slice(hd * LANES, (hd + 1) * LANES)
            vcols = slice(hd * MLA_V, (hd + 1) * MLA_V)
            s = _dot_nt(q_ref[:, cols], k_ref[:, cols])
            if diagonal:
                row = lax.broadcasted_iota(jnp.int32, (MLA_T, MLA_T), 0)
                col = lax.broadcasted_iota(jnp.int32, (MLA_T, MLA_T), 1)
                s = jnp.where(col <= row, s, NEG)
            m_old = m_ref[hd]
            m_new = jnp.maximum(m_old, jnp.max(s, axis=-1, keepdims=True))
            alpha = jnp.exp(m_old - m_new)
            pr = jnp.exp(s - m_new)
            l_ref[hd] = alpha * l_ref[hd] + jnp.sum(pr, axis=-1, keepdims=True)
            acc_ref[hd] = alpha * acc_ref[hd] + _dot(pr.astype(BF16), v_ref[:, vcols])
            m_ref[hd] = m_new

    @pl.when(kj < qi)
    def _():
        step(False)

    @pl.when(kj == qi)
    def _():
        step(True)

    @pl.when(kj == qi)
    def _():
        for hd in range(MLA_HEADS):
            o_ref[:, hd * MLA_V:(hd + 1) * MLA_V] = acc_ref[hd] / jnp.maximum(l_ref[hd], 1e-30)


def _mla_attention(pb, batch, seq):
    qi, kj = _causal_pairs(seq // MLA_T)
    grid_spec = pltpu.PrefetchScalarGridSpec(
        num_scalar_prefetch=2,
        grid=(batch, int(qi.shape[0])),
        in_specs=[
            pl.BlockSpec((None, MLA_T, 512), lambda b, p, qi, kj: (b, qi[p], PB_MLAQ // 512)),
            pl.BlockSpec((None, MLA_T, 512), lambda b, p, qi, kj: (b, kj[p], PB_MLAK // 512)),
            pl.BlockSpec((None, MLA_T, 256), lambda b, p, qi, kj: (b, kj[p], PB_MLAV // 256)),
        ],
        out_specs=pl.BlockSpec((None, MLA_T, 256), lambda b, p, qi, kj: (b, qi[p], 0)),
        scratch_shapes=[pltpu.VMEM((4, MLA_T, 1), F32), pltpu.VMEM((4, MLA_T, 1), F32),
                        pltpu.VMEM((4, MLA_T, MLA_V), F32)],
    )
    return pl.pallas_call(
        _mla_kernel,
        out_shape=jax.ShapeDtypeStruct((batch, seq, 256), F32),
        grid_spec=grid_spec,
        compiler_params=pltpu.CompilerParams(dimension_semantics=("parallel", "arbitrary"),
                                             vmem_limit_bytes=VMEM_LIMIT),
        name="mla_causal",
    )(qi, kj, pb, pb, pb)


def _outproj_kernel(x_ref, oa_ref, ob_ref, oc_ref, os_ref, ow_ref, od_ref, gate_ref, gexp_ref, gn_ref, w_ref,
                    o_ref):
    gates = gate_ref[...]
    g_hi, g_lo = _split_bf16(gates)

    def gate(branch):
        e = gexp_ref[branch]
        return _dot(g_hi, e) + _dot(g_lo, e)

    o_nsa = gate(0) * oc_ref[...] + gate(1) * os_ref[...] + gate(2) * ow_ref[...]
    gn = gn_ref[...]
    y = x_ref[...]
    for grp, o in enumerate((oa_ref[...], ob_ref[...], o_nsa, od_ref[...])):
        cols = slice(grp * GROUP_WIDTH, (grp + 1) * GROUP_WIDTH)
        normed = _row_rms(o, GROUP_WIDTH) * gn[:, cols]
        y = y + _dot(normed.astype(BF16), w_ref[cols, :])
    o_ref[...] = y


def _outproj(x2, oa, ob, oc, osel, ow, od, pf, gexp, gn, w, tm):
    tokens = x2.shape[0]
    row = lambda t: (t, 0)
    o_spec = pl.BlockSpec((tm, GROUP_WIDTH), row)
    return pl.pallas_call(
        _outproj_kernel,
        out_shape=jax.ShapeDtypeStruct((tokens, D_MODEL), F32),
        grid=(tokens // tm,),
        in_specs=[
            pl.BlockSpec((tm, D_MODEL), row), o_spec, o_spec, o_spec, o_spec, o_spec, o_spec,
            pl.BlockSpec((tm, 128), lambda t: (t, PF_GATE // 128)),
            pl.BlockSpec((3, 128, GROUP_WIDTH), lambda t: (0, 0, 0)),
            pl.BlockSpec((1, D_MODEL), lambda t: (0, 0)),
            pl.BlockSpec((D_MODEL, D_MODEL), lambda t: (0, 0)),
        ],
        out_specs=pl.BlockSpec((tm, D_MODEL), row),
        compiler_params=pltpu.CompilerParams(dimension_semantics=("parallel",), vmem_limit_bytes=VMEM_LIMIT),
        name="out_proj",
    )(x2, oa, ob, oc, osel, ow, od, pf, gexp, gn, w)


MOE_TM = 1024


def _moe_kernel(x_ref, fn_ref, wr_hi_ref, wr_lo_ref, br_ref, wg_ref, wu_ref, wd_ref, o_ref, h_ref, comb_ref, acc_ref):
    e = pl.program_id(1)
    lane = lax.broadcasted_iota(jnp.int32, (MOE_TM, LANES), 1)

    @pl.when(e == 0)
    def _():
        h = _row_rms(x_ref[...], D_MODEL) * fn_ref[...]
        h_ref[...] = h.astype(BF16)
        h_hi, h_lo = _split_bf16(h)
        logits = (_dot(h_hi, wr_hi_ref[...]) + _dot(h_lo, wr_hi_ref[...]) + _dot(h_hi, wr_lo_ref[...])
                  + br_ref[...])
        lane_f = lane.astype(F32)
        no_lane = float(LANES)
        is_group = lane < MOE_GROUPS
        g_max = jnp.max(jnp.where(is_group, logits, -jnp.inf), axis=-1, keepdims=True)
        g_star = jnp.min(jnp.where(is_group & (logits == g_max), lane_f, no_lane), axis=-1, keepdims=True)
        g_den = jnp.sum(jnp.where(is_group, jnp.exp(logits - g_max), 0.0), axis=-1, keepdims=True)
        g_w = 1.0 / g_den
        group_of_lane = ((lane - MOE_GROUPS) >> int(math.log2(MOE_EPG))).astype(F32)
        in_group = (lane >= MOE_GROUPS) & (lane < MOE_GROUPS + MOE_EXPERTS) & (group_of_lane == g_star)
        e_l = jnp.where(in_group, logits, -jnp.inf)
        top1 = jnp.max(e_l, axis=-1, keepdims=True)
        i1 = jnp.min(jnp.where(e_l == top1, lane_f, no_lane), axis=-1, keepdims=True)
        e_l2 = jnp.where(lane_f == i1, -jnp.inf, e_l)
        top2 = jnp.max(e_l2, axis=-1, keepdims=True)
        i2 = jnp.min(jnp.where(e_l2 == top2, lane_f, no_lane), axis=-1, keepdims=True)
        r = jnp.exp(top2 - top1)
        w1 = g_w / (1.0 + r)
        w2 = g_w * r / (1.0 + r)
        comb_ref[...] = jnp.where(lane_f == i1, w1, jnp.where(lane_f == i2, w2, 0.0))
        acc_ref[...] = jnp.zeros_like(acc_ref)

    c_e = jnp.sum(jnp.where(lane == e + MOE_GROUPS, comb_ref[...], 0.0), axis=-1, keepdims=True)
    hb = h_ref[...]
    a = jax.nn.silu(_dot(hb, wg_ref[...])) * _dot(hb, wu_ref[...])
    acc_ref[...] += _dot((a * c_e).astype(BF16), wd_ref[...])

    @pl.when(e == MOE_EXPERTS - 1)
    def _():
        o_ref[...] = x_ref[...] + acc_ref[...]


def _moe(x2, fn, wr_hi, wr_lo, br, wg, wu, wd):
    tokens = x2.shape[0]
    const = lambda t, e: (0, 0)
    return pl.pallas_call(
        _moe_kernel,
        out_shape=jax.ShapeDtypeStruct((tokens, D_MODEL), F32),
        grid=(tokens // MOE_TM, MOE_EXPERTS),
        in_specs=[
            pl.BlockSpec((MOE_TM, D_MODEL), lambda t, e: (t, 0)),
            pl.BlockSpec((1, D_MODEL), const),
            pl.BlockSpec((D_MODEL, LANES), const),
            pl.BlockSpec((D_MODEL, LANES), const),
            pl.BlockSpec((1, LANES), const),
            pl.BlockSpec((None, D_MODEL, MOE_HIDDEN), lambda t, e: (e, 0, 0)),
            pl.BlockSpec((None, D_MODEL, MOE_HIDDEN), lambda t, e: (e, 0, 0)),
            pl.BlockSpec((None, MOE_HIDDEN, D_MODEL), lambda t, e: (e, 0, 0)),
        ],
        out_specs=pl.BlockSpec((MOE_TM, D_MODEL), lambda t, e: (t, 0)),
        scratch_shapes=[pltpu.VMEM((MOE_TM, D_MODEL), BF16), pltpu.VMEM((MOE_TM, LANES), F32),
                        pltpu.VMEM((MOE_TM, D_MODEL), F32)],
        compiler_params=pltpu.CompilerParams(dimension_semantics=("parallel", "arbitrary"),
                                             vmem_limit_bytes=VMEM_LIMIT),
        name="hier_moe",
    )(x2, fn, wr_hi, wr_lo, br, wg, wu, wd)


def _t5_bucket(dist):
    n = jnp.maximum(dist, 0)
    max_exact = T5_BUCKETS // 2
    nf = jnp.maximum(n, 1).astype(F32)
    large = max_exact + (jnp.log(nf / max_exact) / math.log(T5_MAX_DIST / max_exact)
                         * (T5_BUCKETS - max_exact)).astype(jnp.int32)
    large = jnp.minimum(large, T5_BUCKETS - 1)
    return jnp.where(n < max_exact, n, large)


def _sel_delta_cap():
    max_exact = T5_BUCKETS // 2
    span = T5_BUCKETS - max_exact
    last_bucket_from = max_exact * (T5_MAX_DIST / max_exact) ** ((span - 1) / span)
    cap = 1
    while (cap - 1) * SEL_T + 1 < 1.25 * last_bucket_from:
        cap += 1
    return cap


def _position_tables(rel_bias, seq):
    buckets = _t5_bucket(jnp.arange(seq))
    first = jnp.sum(buckets[None, :] < jnp.arange(T5_BUCKETS)[:, None], axis=1)
    tbl = rel_bias.T

    def toeplitz(heads, dist):
        shape = (tbl[heads].shape[0],) + (1,) * dist.ndim
        out = jnp.broadcast_to(tbl[heads][:, 0].reshape(shape), shape[:1] + dist.shape)
        for b in range(1, T5_BUCKETS):
            out = jnp.where((dist >= first[b])[None], tbl[heads][:, b].reshape(shape), out)
        return out

    swa_h, nsa_h = slice(0, 4), slice(4, 8)
    tq_swa = -(-(SWA_WINDOW - 1) // 128) * 128
    tq_win = -(-(NSA_WINDOW - 1) // 128) * 128
    band = lambda tq: jnp.arange(tq)[:, None] + tq - jnp.arange(2 * tq)[None, :]
    bias_swa = toeplitz(swa_h, band(tq_swa))
    bias_win = toeplitz(nsa_h, band(tq_win))
    ncp = seq // NSA_CMP_STRIDE
    cmp_end = jnp.arange(ncp) * NSA_CMP_STRIDE + NSA_CMP_LEN - 1
    bias_cmp = toeplitz(nsa_h, jnp.arange(seq)[:, None] - cmp_end[None, :])
    nd = min(_sel_delta_cap() + 1, seq // SEL_T)
    dist_s = (jnp.arange(nd)[:, None, None] * SEL_T + jnp.arange(SEL_T)[None, :, None]
              - jnp.arange(SEL_T)[None, None, :])
    bias_sel = toeplitz(nsa_h, dist_s)

    n_sel_pad = -(-(seq // NSA_SEL_LEN) // LANES) * LANES
    sel_start = np.arange(n_sel_pad) * NSA_SEL_LEN
    c_start = np.arange(ncp) * NSA_CMP_STRIDE
    c_end = c_start + NSA_CMP_LEN - 1
    real = (np.arange(ncp) < ncp - NSA_CMP_LEN // NSA_CMP_STRIDE + 1)[:, None] & (sel_start < seq)[None, :]
    overlap = ((c_start[:, None] < sel_start[None, :] + NSA_SEL_LEN) & (c_end[:, None] >= sel_start[None, :]) & real)
    expand = (np.arange(n_sel_pad)[:, None] == (np.arange(seq) // NSA_SEL_LEN)[None, :])

    pos = jnp.arange(seq, dtype=F32)
    inv_freq = ROPE_THETA ** (-jnp.arange(0, MLA_ROPE, 2, dtype=F32) / MLA_ROPE)
    ang = pos[:, None] * inv_freq[None, :]
    cos, sin = jnp.cos(ang), jnp.sin(ang)
    ones = jnp.ones((seq, MLA_NOPE), F32)
    tail = LANES - MLA_NOPE - MLA_ROPE
    cos_t = jnp.concatenate([ones, cos, cos, jnp.ones((seq, tail), F32)], axis=1)
    sin_t = jnp.concatenate([0 * ones, -sin, sin, jnp.zeros((seq, tail), F32)], axis=1)
    return dict(bias_swa=bias_swa, bias_win=bias_win, bias_cmp=bias_cmp, bias_sel=bias_sel,
                overlap=jnp.asarray(overlap, BF16), expand=jnp.asarray(expand, BF16), cos_t=cos_t, sin_t=sin_t)


def _pad_to(a, shape):
    return jnp.pad(a, [(0, s - d) for d, s in zip(a.shape, shape)])


def _pack_layer(w_in, swa_q_norm, swa_k_norm, nsa_q_norm, nsa_k_norm, mla_q_lat_norm, mla_w_q_up,
                mla_kv_lat_norm, mla_w_kv_up, mla_q_norm, mla_k_norm):
    kpe = w_in[:, 2636:2668]
    kpe_seg = jnp.concatenate([jnp.zeros((D_MODEL, MLA_NOPE), F32), kpe,
                               jnp.zeros((D_MODEL, LANES - MLA_QK), F32)], axis=1)
    w = jnp.concatenate([
        w_in[:, :2304],
        _pad_to(w_in[:, 2304:2316], (D_MODEL, 128)),
        _pad_to(w_in[:, 2316:2508], (D_MODEL, 256)),
        w_in[:, 2508:2636],
        jnp.tile(kpe_seg, (1, MLA_HEADS)),
    ], axis=1).astype(BF16)
    tile4 = lambda g: jnp.tile(g, 4)
    g64 = _pad_to(jnp.stack([tile4(swa_q_norm), tile4(swa_k_norm), tile4(nsa_q_norm),
                             tile4(nsa_k_norm[1]), tile4(nsa_k_norm[2])]), (8, 256))
    glat = _pad_to(jnp.stack([_pad_to(mla_q_lat_norm, (256,)), _pad_to(mla_kv_lat_norm, (256,))]), (8, 256))
    wq = _pad_to(mla_w_q_up.reshape(MLA_Q_RANK, MLA_HEADS, MLA_QK), (256, MLA_HEADS, LANES))
    wq = wq.reshape(256, MLA_HEADS * LANES).astype(BF16)
    wkv = mla_w_kv_up.reshape(MLA_KV_RANK, MLA_HEADS, MLA_NOPE + MLA_V)
    wk = _pad_to(wkv[:, :, :MLA_NOPE], (MLA_KV_RANK, MLA_HEADS, LANES)).reshape(MLA_KV_RANK, MLA_HEADS * LANES)
    wv = wkv[:, :, MLA_NOPE:].reshape(MLA_KV_RANK, MLA_HEADS * MLA_V)
    wkv_p = jnp.concatenate([wk, wv], axis=1).astype(BF16)
    gmla = _pad_to(jnp.stack([jnp.tile(_pad_to(mla_q_norm, (LANES,)), MLA_HEADS),
                              jnp.tile(_pad_to(mla_k_norm, (LANES,)), MLA_HEADS)]), (8, 512))
    return w, g64, glat, wq, wkv_p, gmla


def _gate_expand():
    rows = np.arange(128)[None, :, None]
    cols = np.arange(GROUP_WIDTH)[None, None, :]
    branch = np.arange(3)[:, None, None]
    return jnp.asarray(rows == branch * 4 + cols // HEAD_DIM, BF16)


def _compress_rows(pf3, batch, seq):
    nb = seq // NSA_CMP_STRIDE
    kv = pf3[:, :, :256].reshape(batch, seq, 2, 2, HEAD_DIM).transpose(2, 0, 3, 1, 4)
    blocks = kv.reshape(2, batch * 2, nb, NSA_CMP_STRIDE * HEAD_DIM)
    nxt = jnp.concatenate([blocks[:, :, 1:], jnp.zeros_like(blocks[:, :, :1])], axis=2)
    return jnp.concatenate([blocks, nxt], axis=3)


def kernel(x, rel_bias, attn_norm, w_in, swa_q_norm, swa_k_norm, swa_sinks, nsa_q_norm, nsa_k_norm, nsa_cmp_pos, nsa_cmp_w1, nsa_cmp_w2, mla_q_lat_norm, mla_w_q_up, mla_kv_lat_norm, mla_w_kv_up, mla_q_norm, mla_k_norm, out_norm, w_out, ffn_norm, moe_w_group, moe_b_group, moe_w_expert, moe_b_expert, moe_w_gate, moe_w_up, moe_w_down):
    batch, seq, _ = x.shape
    depth = w_in.shape[0]
    tokens = batch * seq
    tm = 512
    assert seq % 2048 == 0 and tokens % MOE_TM == 0
    tabs = _position_tables(rel_bias, seq)
    gexp = _gate_expand()
    zero_sinks = jnp.zeros((4,), F32)
    x2 = x.reshape(tokens, D_MODEL)
    for l in range(depth):
        w, g64, glat, wq, wkv, gmla = _pack_layer(
            w_in[l], swa_q_norm[l], swa_k_norm[l], nsa_q_norm[l], nsa_k_norm[l], mla_q_lat_norm[l],
            mla_w_q_up[l], mla_kv_lat_norm[l], mla_w_kv_up[l], mla_q_norm[l], mla_k_norm[l])
        pb, pf = _prep(x2, attn_norm[l][None, :], w, g64, glat, wq, wkv, gmla, tabs["cos_t"], tabs["sin_t"],
                       seq, tm)
        pb3 = pb.reshape(batch, seq, PB_WIDTH)
        pf3 = pf.reshape(batch, seq, PF_WIDTH)
        o_a = _sb_attention(pb3, batch, seq)
        o_b = _banded_attention(pb3, swa_sinks[l], tabs["bias_swa"], batch, seq, SWA_WINDOW,
                                PB_SWAQ, PB_SWAK, PB_SWAV, True)
        rows = _compress_rows(pf3, batch, seq)
        kvc = _compress(rows, nsa_cmp_pos[l].reshape(2, 1, -1),
                        nsa_cmp_w1[l].reshape(2, -1, NSA_CMP_HIDDEN).astype(BF16), nsa_cmp_w2[l].astype(BF16),
                        nsa_k_norm[l][0][None, :], 128)
        o_c, sel = _cmp_attention(pb3, kvc, tabs["bias_cmp"], tabs["overlap"], batch, seq)
        o_s = _sel_attention(pb3, sel, tabs["expand"], tabs["bias_sel"], batch, seq)
        o_w = _banded_attention(pb3, zero_sinks, tabs["bias_win"], batch, seq, NSA_WINDOW,
                                PB_NSAQ, PB_NSAKW, PB_NSAVW, False)
        o_d = _mla_attention(pb3, batch, seq)
        flat = lambda o: o.reshape(tokens, GROUP_WIDTH)
        x2 = _outproj(x2, flat(o_a), flat(o_b), flat(o_c), flat(o_s), flat(o_w), flat(o_d), pf, gexp,
                      out_norm[l][None, :], w_out[l].astype(BF16), tm)
        w_router = _pad_to(jnp.concatenate([moe_w_group[l], moe_w_expert[l]], axis=1), (D_MODEL, LANES))
        wr_hi = w_router.astype(BF16)
        wr_lo = (w_router - wr_hi.astype(F32)).astype(BF16)
        b_router = _pad_to(jnp.concatenate([moe_b_group[l], moe_b_expert[l]])[None, :], (1, LANES))
        x2 = _moe(x2, ffn_norm[l][None, :], wr_hi, wr_lo, b_router, moe_w_gate[l].astype(BF16),
                  moe_w_up[l].astype(BF16), moe_w_down[l].astype(BF16))
    return x2.reshape(batch, seq, D_MODEL)
```

```python
import functools
import math

import numpy as np
import jax
import jax.numpy as jnp
from jax import lax
from jax.experimental import pallas as pl
from jax.experimental.pallas import tpu as pltpu

F32 = jnp.float32
BF16 = jnp.bfloat16

D_MODEL = 1024
HEAD_DIM = 64
NEG = -1e30
EPS = 1e-6
FORCE_BONUS = 1000.0
SWA_WINDOW = 128
NSA_CMP_LEN = 32
NSA_CMP_STRIDE = 16
NSA_CMP_HIDDEN = 128
NSA_SEL_LEN = 64
NSA_TOPK = 16
NSA_WINDOW = 512
MLA_HEADS = 4
MLA_NOPE = 64
MLA_ROPE = 32
MLA_V = 64
MLA_Q_RANK = 192
MLA_KV_RANK = 128
MLA_QK = MLA_NOPE + MLA_ROPE
ROPE_THETA = 10000.0
T5_BUCKETS = 32
T5_MAX_DIST = 1024
MOE_GROUPS = 4
MOE_EPG = 8
MOE_EXPERTS = MOE_GROUPS * MOE_EPG
MOE_HIDDEN = 256
GROUP_WIDTH = 256
LANES = 128
VMEM_LIMIT = 48 * 1024 * 1024

PB_MLAQ, PB_MLAK, PB_MLAV = 0, 512, 1024
PB_SBQ, PB_SBK, PB_SBV = 1536, 1792, 2048
PB_SWAQ, PB_NSAQ, PB_NSAVS = 2304, 2560, 2816
PB_SWAK, PB_SWAV, PB_NSAKS, PB_NSAKW, PB_NSAVW = 3072, 3200, 3328, 3456, 3584
PB_WIDTH = 3712
PF_KC, PF_VC, PF_GATE = 0, 128, 256
PF_WIDTH = 384
W_SBQ, W_SBK, W_SBV, W_SWAQ, W_SWAK, W_SWAV, W_NSAQ = 0, 256, 512, 768, 1024, 1152, 1280
W_KC, W_VC, W_KS, W_VS, W_KW, W_VW, W_GATE = 1536, 1664, 1792, 1920, 2176, 2304, 2432
W_CQ, W_CKV, W_KPE = 2560, 2816, 2944
W_WIDTH = 3456

NT_DIMS = (((1,), (1,)), ((), ()))


def _dot(a, b):
    return jnp.dot(a, b, preferred_element_type=F32)


def _dot_nt(a, b):
    return lax.dot_general(a, b, NT_DIMS, preferred_element_type=F32)


def _split_bf16(x):
    hi = x.astype(BF16)
    lo = (x - hi.astype(F32)).astype(BF16)
    return hi, lo


def _dot_exact_rhs(x, m):
    hi, lo = _split_bf16(x)
    return _dot(hi, m) + _dot(lo, m)


def _block_diag_ones(width, seg):
    idx = np.arange(width) // seg
    return jnp.asarray(idx[:, None] == idx[None, :], BF16)


def _seg_rms(x, seg_ones, count):
    width = x.shape[1]
    ms = _dot_exact_rhs(x * x, seg_ones[:width, :width]) * (1.0 / count)
    return x * lax.rsqrt(ms + EPS)


def _row_rms(x, count):
    return x * lax.rsqrt(jnp.sum(x * x, axis=-1, keepdims=True) * (1.0 / count) + EPS)


def _prep_kernel(x_ref, an_ref, w_ref, g64_ref, glat_ref, wq_ref, wkv_ref, gmla_ref, cos_ref, sin_ref,
                 s64_ref, s128_ref, pb_ref, pf_ref):
    x = x_ref[...]
    s64 = s64_ref[...]
    s128 = s128_ref[...]
    h = _row_rms(x, D_MODEL) * an_ref[...]
    hb = h.astype(BF16)

    def proj(lo, hi):
        return _dot(hb, w_ref[:, lo:hi])

    def put(col, value):
        pb_ref[:, col:col + value.shape[1]] = value.astype(BF16)

    def ones_tail(width):
        lane = lax.broadcasted_iota(jnp.int32, (1, width), 1)
        return jnp.where((lane & (LANES - 1)) >= HEAD_DIM, 1.0, 0.0)

    scale = HEAD_DIM ** -0.5
    g64 = g64_ref[...]
    put(PB_SBQ, proj(W_SBQ, W_SBQ + 256) * scale)
    put(PB_SBK, proj(W_SBK, W_SBK + 256))
    put(PB_SBV, proj(W_SBV, W_SBV + 256))
    put(PB_SWAQ, _seg_rms(proj(W_SWAQ, W_SWAQ + 256), s64, 64) * g64[0:1, :] * scale)
    put(PB_SWAK, _seg_rms(proj(W_SWAK, W_SWAK + 128), s64, 64) * g64[1:2, :128])
    put(PB_SWAV, proj(W_SWAV, W_SWAV + 128))
    put(PB_NSAQ, _seg_rms(proj(W_NSAQ, W_NSAQ + 256), s64, 64) * g64[2:3, :] * scale)
    pf_ref[:, PF_KC:PF_KC + 128] = proj(W_KC, W_KC + 128)
    pf_ref[:, PF_VC:PF_VC + 128] = proj(W_VC, W_VC + 128)
    put(PB_NSAKS, _seg_rms(proj(W_KS, W_KS + 128), s64, 64) * g64[3:4, :128])
    put(PB_NSAVS, proj(W_VS, W_VS + 256) + ones_tail(256))
    put(PB_NSAKW, _seg_rms(proj(W_KW, W_KW + 128), s64, 64) * g64[4:5, :128])
    put(PB_NSAVW, proj(W_VW, W_VW + 128))
    pf_ref[:, PF_GATE:PF_GATE + 128] = jax.nn.sigmoid(proj(W_GATE, W_GATE + 128))

    glat = glat_ref[...]
    cq = _row_rms(proj(W_CQ, W_CQ + 256), MLA_Q_RANK) * glat[0:1, :]
    q = _dot(cq.astype(BF16), wq_ref[...])
    ckv = _row_rms(proj(W_CKV, W_CKV + 128), MLA_KV_RANK) * glat[1:2, :128]
    kv = _dot(ckv.astype(BF16), wkv_ref[...])
    k = kv[:, :512] + proj(W_KPE, W_KPE + 512)
    gm = gmla_ref[...]
    q = _seg_rms(q, s128, MLA_QK) * gm[0:1, :]
    k = _seg_rms(k, s128, MLA_QK) * gm[1:2, :]
    cos = cos_ref[...]
    sin = sin_ref[...]
    lane = lax.broadcasted_iota(jnp.int32, (x.shape[0], LANES), 1)
    first_half = lane < MLA_NOPE + MLA_ROPE // 2

    def rope(t):
        partner = jnp.where(first_half, pltpu.roll(t, LANES - MLA_ROPE // 2, 1), pltpu.roll(t, MLA_ROPE // 2, 1))
        return t * cos + partner * sin

    qscale = MLA_QK ** -0.5
    for hd in range(MLA_HEADS):
        sl = slice(hd * LANES, (hd + 1) * LANES)
        put(PB_MLAQ + hd * LANES, rope(q[:, sl]) * qscale)
        put(PB_MLAK + hd * LANES, rope(k[:, sl]))
    put(PB_MLAV, kv[:, 512:1024] + ones_tail(512))


def _prep(x2, an, w, g64, glat, wq, wkv, gmla, cos_t, sin_t, seq, tm):
    tokens = x2.shape[0]
    n_pos = seq // tm
    const = lambda t: (0, 0)
    return pl.pallas_call(
        _prep_kernel,
        out_shape=(jax.ShapeDtypeStruct((tokens, PB_WIDTH), BF16), jax.ShapeDtypeStruct((tokens, PF_WIDTH), F32)),
        grid=(tokens // tm,),
        in_specs=[
            pl.BlockSpec((tm, D_MODEL), lambda t: (t, 0)),
            pl.BlockSpec((1, D_MODEL), const),
            pl.BlockSpec((D_MODEL, W_WIDTH), const),
            pl.BlockSpec((8, 256), const),
            pl.BlockSpec((8, 256), const),
            pl.BlockSpec((256, 512), const),
            pl.BlockSpec((128, 1024), const),
            pl.BlockSpec((8, 512), const),
            pl.BlockSpec((tm, LANES), lambda t: (t % n_pos, 0)),
            pl.BlockSpec((tm, LANES), lambda t: (t % n_pos, 0)),
            pl.BlockSpec((256, 256), const),
            pl.BlockSpec((512, 512), const),
        ],
        out_specs=(pl.BlockSpec((tm, PB_WIDTH), lambda t: (t, 0)), pl.BlockSpec((tm, PF_WIDTH), lambda t: (t, 0))),
        compiler_params=pltpu.CompilerParams(dimension_semantics=("parallel",), vmem_limit_bytes=VMEM_LIMIT),
        name="prep",
    )(x2, an, w, g64, glat, wq, wkv, gmla, cos_t, sin_t, _block_diag_ones(256, HEAD_DIM),
      _block_diag_ones(512, LANES))


SB_TQ, SB_KB, SB_SUB = 256, 1024, 128


SB_UNDERFLOW = 110.0


def _sb_kernel(qi_ref, kj_ref, kmax_ref, q_ref, k_ref, v_ref, sums_ref, o_ref, carry_ref, acc_ref, zb_ref,
               dead_ref):
    b = pl.program_id(0)
    p = pl.program_id(1)
    qi = qi_ref[p]
    kj = kj_ref[p]
    q_start = qi * SB_TQ
    first = kj == (q_start + SB_TQ - 1) // SB_KB

    @pl.when(first)
    def _():
        carry_ref[...] = jnp.zeros_like(carry_ref)
        acc_ref[...] = jnp.zeros_like(acc_ref)
        dead_ref[0] = 0
        ones = jnp.ones((HEAD_DIM, SB_SUB), BF16)
        for hd in range(4):
            q = q_ref[:, hd * HEAD_DIM:(hd + 1) * HEAD_DIM].astype(F32)
            zb_ref[hd * SB_TQ:(hd + 1) * SB_TQ, :] = (jnp.sqrt(_dot_exact_rhs(q * q, ones))
                                                      * (kmax_ref[b * 4 + hd] * 1.01))

    rel = (lax.broadcasted_iota(jnp.int32, (SB_TQ, SB_SUB), 1)
           - lax.broadcasted_iota(jnp.int32, (SB_TQ, SB_SUB), 0))

    def sub_tile(u, k_start):
        mask = jnp.tile(rel < q_start - k_start, (4, 1))
        rows = slice(u * SB_SUB, (u + 1) * SB_SUB)
        head = lambda hd: slice(hd * HEAD_DIM, (hd + 1) * HEAD_DIM)
        z = jnp.concatenate([_dot_nt(q_ref[:, head(hd)], k_ref[rows, head(hd)]) for hd in range(4)], axis=0)
        log_keep = jnp.where(mask, -(jnp.maximum(z, 0.0) + jnp.log(1.0 + jnp.exp(-jnp.abs(z)))), 0.0)
        hi, lo = _split_bf16(log_keep)
        sums = _dot(jnp.concatenate([hi, lo], axis=1), sums_ref[...])
        carry = carry_ref[...]
        a = jnp.where(mask, jnp.exp(z + sums[:, :SB_SUB] + carry), 0.0).astype(BF16)
        for hd in range(4):
            acc_ref[hd] += _dot(a[hd * SB_TQ:(hd + 1) * SB_TQ], v_ref[rows, head(hd)])
        carry_ref[...] = carry + sums[:, SB_SUB:]

    for u in reversed(range(SB_KB // SB_SUB)):
        k_start = kj * SB_KB + u * SB_SUB

        @pl.when((k_start < q_start + SB_TQ) & (dead_ref[0] == 0))
        def _(u=u, k_start=k_start):
            live = jnp.max(carry_ref[...] + zb_ref[...]) > -SB_UNDERFLOW

            @pl.when(live)
            def _():
                sub_tile(u, k_start)

            @pl.when(jnp.logical_not(live))
            def _():
                dead_ref[0] = 1

    @pl.when(kj == 0)
    def _():
        for hd in range(4):
            o_ref[:, hd * HEAD_DIM:(hd + 1) * HEAD_DIM] = acc_ref[hd]


def _sb_attention(pb, batch, seq):
    nq = seq // SB_TQ
    qi, kj = [], []
    for i in range(nq):
        for j in reversed(range((i * SB_TQ + SB_TQ - 1) // SB_KB + 1)):
            qi.append(i)
            kj.append(j)
    qi = jnp.asarray(np.array(qi, np.int32))
    kj = jnp.asarray(np.array(kj, np.int32))
    keys = pb[:, :, PB_SBK:PB_SBK + 256].astype(F32).reshape(batch, seq, 4, HEAD_DIM)
    kmax = jnp.sqrt(jnp.max(jnp.sum(keys * keys, axis=-1), axis=1)).reshape(batch * 4)
    j = np.arange(2 * SB_SUB)[:, None] % SB_SUB
    s = np.arange(2 * SB_SUB)[None, :]
    sums = jnp.asarray((s >= SB_SUB) | (j >= s), BF16)
    grid_spec = pltpu.PrefetchScalarGridSpec(
        num_scalar_prefetch=2,
        grid=(batch, int(qi.shape[0])),
        in_specs=[
            pl.BlockSpec(memory_space=pltpu.SMEM),
            pl.BlockSpec((None, SB_TQ, 256), lambda b, p, qi, kj: (b, qi[p], PB_SBQ // 256)),
            pl.BlockSpec((None, SB_KB, 256), lambda b, p, qi, kj: (b, kj[p], PB_SBK // 256)),
            pl.BlockSpec((None, SB_KB, 256), lambda b, p, qi, kj: (b, kj[p], PB_SBV // 256)),
            pl.BlockSpec((2 * SB_SUB, 2 * SB_SUB), lambda b, p, qi, kj: (0, 0)),
        ],
        out_specs=pl.BlockSpec((None, SB_TQ, 256), lambda b, p, qi, kj: (b, qi[p], 0)),
        scratch_shapes=[pltpu.VMEM((4 * SB_TQ, SB_SUB), F32), pltpu.VMEM((4, SB_TQ, HEAD_DIM), F32),
                        pltpu.VMEM((4 * SB_TQ, SB_SUB), F32), pltpu.SMEM((1,), jnp.int32)],
    )
    return pl.pallas_call(
        _sb_kernel,
        out_shape=jax.ShapeDtypeStruct((batch, seq, 256), F32),
        grid_spec=grid_spec,
        compiler_params=pltpu.CompilerParams(dimension_semantics=("parallel", "arbitrary"),
                                             vmem_limit_bytes=VMEM_LIMIT),
        name="stick_breaking",
    )(qi, kj, kmax, pb, pb, pb, sums)


def _banded_kernel(sink_ref, q_ref, kp_ref, kc_ref, vp_ref, vc_ref, bias_ref, o_ref, *, tq, window, use_sink):
    i = pl.program_id(1)
    qi = lax.broadcasted_iota(jnp.int32, (tq, tq), 0)
    ki = lax.broadcasted_iota(jnp.int32, (tq, tq), 1)
    dist_prev = qi + tq - ki
    dist_cur = qi - ki
    mask_prev = (dist_prev < window) & (i > 0)
    mask_cur = (dist_cur >= 0) & (dist_cur < window)
    for hd in range(4):
        cols = slice(hd * HEAD_DIM, (hd + 1) * HEAD_DIM)
        kcols = slice((hd // 2) * HEAD_DIM, (hd // 2 + 1) * HEAD_DIM)
        q = q_ref[:, cols]
        s_prev = jnp.where(mask_prev, _dot_nt(q, kp_ref[:, kcols]) + bias_ref[hd, :, :tq], NEG)
        s_cur = jnp.where(mask_cur, _dot_nt(q, kc_ref[:, kcols]) + bias_ref[hd, :, tq:], NEG)
        m = jnp.maximum(jnp.max(s_prev, axis=-1, keepdims=True), jnp.max(s_cur, axis=-1, keepdims=True))
        if use_sink:
            sink = sink_ref[hd]
            m = jnp.maximum(m, sink)
        p_prev = jnp.where(mask_prev, jnp.exp(s_prev - m), 0.0)
        p_cur = jnp.where(mask_cur, jnp.exp(s_cur - m), 0.0)
        denom = jnp.sum(p_prev, axis=-1, keepdims=True) + jnp.sum(p_cur, axis=-1, keepdims=True)
        if use_sink:
            denom = denom + jnp.exp(sink - m)
        o = _dot(p_prev.astype(BF16), vp_ref[:, kcols]) + _dot(p_cur.astype(BF16), vc_ref[:, kcols])
        o_ref[:, cols] = o / jnp.maximum(denom, 1e-30)


def _banded_attention(pb, sinks, bias, batch, seq, window, q_col, k_col, v_col, use_sink):
    tq = -(-(window - 1) // 128) * 128
    grid_spec = pltpu.PrefetchScalarGridSpec(
        num_scalar_prefetch=1,
        grid=(batch, seq // tq),
        in_specs=[
            pl.BlockSpec((None, tq, 256), lambda b, i, s: (b, i, q_col // 256)),
            pl.BlockSpec((None, tq, 128), lambda b, i, s: (b, jnp.maximum(i - 1, 0), k_col // 128)),
            pl.BlockSpec((None, tq, 128), lambda b, i, s: (b, i, k_col // 128)),
            pl.BlockSpec((None, tq, 128), lambda b, i, s: (b, jnp.maximum(i - 1, 0), v_col // 128)),
            pl.BlockSpec((None, tq, 128), lambda b, i, s: (b, i, v_col // 128)),
            pl.BlockSpec((4, tq, 2 * tq), lambda b, i, s: (0, 0, 0)),
        ],
        out_specs=pl.BlockSpec((None, tq, 256), lambda b, i, s: (b, i, 0)),
    )
    return pl.pallas_call(
        functools.partial(_banded_kernel, tq=tq, window=window, use_sink=use_sink),
        out_shape=jax.ShapeDtypeStruct((batch, seq, 256), F32),
        grid_spec=grid_spec,
        compiler_params=pltpu.CompilerParams(dimension_semantics=("parallel", "arbitrary"),
                                             vmem_limit_bytes=VMEM_LIMIT),
        name="banded_w%d" % window,
    )(sinks, pb, pb, pb, pb, pb, bias)


def _compress_kernel(rows_ref, pos_ref, w1_ref, w2_ref, g_ref, o_ref):
    win = rows_ref[...] + pos_ref[...]
    hid = jax.nn.gelu(_dot(win.astype(BF16), w1_ref[...]), approximate=True)
    out = _dot(hid.astype(BF16), w2_ref[...])
    normed = _row_rms(out, HEAD_DIM) * g_ref[...]
    o_ref[...] = jnp.where(pl.program_id(0) == 0, normed, out).astype(BF16)


def _compress(rows, pos, w1, w2, gain, tn):
    _, bh, ncp, width = rows.shape
    return pl.pallas_call(
        _compress_kernel,
        out_shape=jax.ShapeDtypeStruct((2, bh, ncp, HEAD_DIM), BF16),
        grid=(2, bh, ncp // tn),
        in_specs=[
            pl.BlockSpec((None, None, tn, width), lambda c, r, n: (c, r, n, 0)),
            pl.BlockSpec((None, 1, width), lambda c, r, n: (c, 0, 0)),
            pl.BlockSpec((None, width, NSA_CMP_HIDDEN), lambda c, r, n: (c, 0, 0)),
            pl.BlockSpec((None, NSA_CMP_HIDDEN, HEAD_DIM), lambda c, r, n: (c, 0, 0)),
            pl.BlockSpec((1, HEAD_DIM), lambda c, r, n: (0, 0)),
        ],
        out_specs=pl.BlockSpec((None, None, tn, HEAD_DIM), lambda c, r, n: (c, r, n, 0)),
        compiler_params=pltpu.CompilerParams(dimension_semantics=("parallel", "parallel", "parallel"),
                                             vmem_limit_bytes=VMEM_LIMIT),
        name="nsa_compress",
    )(rows, pos, w1, w2, gain)


CMP_TQ = 256


def _cmp_kernel(q_ref, kc_ref, vc_ref, bias_ref, ov_ref, o_ref, sel_ref, *, ncp, n_sel_pad, topk):
    i = pl.program_id(2)
    q_pos = i * CMP_TQ + lax.broadcasted_iota(jnp.int32, (CMP_TQ, ncp), 0)
    cmp_end = lax.broadcasted_iota(jnp.int32, (CMP_TQ, ncp), 1) * NSA_CMP_STRIDE + (NSA_CMP_LEN - 1)
    mask = cmp_end <= q_pos
    kc = kc_ref[...]
    vc = vc_ref[...]
    p_sum = jnp.zeros((CMP_TQ, ncp), F32)
    for g in range(2):
        cols = slice(g * HEAD_DIM, (g + 1) * HEAD_DIM)
        s = jnp.where(mask, _dot_nt(q_ref[:, cols], kc) + bias_ref[g], NEG)
        m = jnp.max(s, axis=-1, keepdims=True)
        p = jnp.where(mask, jnp.exp(s - m), 0.0)
        p = p / jnp.maximum(jnp.sum(p, axis=-1, keepdims=True), 1e-30)
        o_ref[:, cols] = _dot(p.astype(BF16), vc)
        p_sum = p_sum + p
    imp = _dot_exact_rhs(p_sum, ov_ref[...])

    row_pos = i * CMP_TQ + lax.broadcasted_iota(jnp.int32, (CMP_TQ, n_sel_pad), 0)
    blk = lax.broadcasted_iota(jnp.int32, (CMP_TQ, n_sel_pad), 1)
    cur = row_pos >> int(math.log2(NSA_SEL_LEN))
    forced = (blk == 0) | (blk == cur) | (blk == cur - 1)
    valid = blk * NSA_SEL_LEN <= row_pos
    score = jnp.where(valid, imp + jnp.where(forced, FORCE_BONUS, 0.0), NEG)
    blk_f = blk.astype(F32)
    taken = jnp.zeros((CMP_TQ, n_sel_pad), F32)
    for _ in range(topk):
        best = jnp.max(score, axis=-1, keepdims=True)
        first = jnp.min(jnp.where(score == best, blk_f, float(n_sel_pad)), axis=-1, keepdims=True)
        hit = blk_f == first
        taken = jnp.where(hit, 1.0, taken)
        score = jnp.where(hit, -jnp.inf, score)
    sel_ref[...] = jnp.where(valid, taken, 0.0).astype(BF16)


def _cmp_attention(pb, kvc, bias_c, overlap, batch, seq):
    ncp = kvc.shape[2]
    n_sel_pad = overlap.shape[1]
    topk = min(NSA_TOPK, seq // NSA_SEL_LEN)
    return pl.pallas_call(
        functools.partial(_cmp_kernel, ncp=ncp, n_sel_pad=n_sel_pad, topk=topk),
        out_shape=(jax.ShapeDtypeStruct((batch, seq, 256), F32),
                   jax.ShapeDtypeStruct((batch, 2, seq, n_sel_pad), BF16)),
        grid=(batch, 2, seq // CMP_TQ),
        in_specs=[
            pl.BlockSpec((None, CMP_TQ, 128), lambda b, h, i: (b, i, PB_NSAQ // 128 + h)),
            pl.BlockSpec((None, None, ncp, HEAD_DIM), lambda b, h, i: (0, b * 2 + h, 0, 0)),
            pl.BlockSpec((None, None, ncp, HEAD_DIM), lambda b, h, i: (1, b * 2 + h, 0, 0)),
            pl.BlockSpec((2, CMP_TQ, ncp), lambda b, h, i: (h, i, 0)),
            pl.BlockSpec((ncp, n_sel_pad), lambda b, h, i: (0, 0)),
        ],
        out_specs=(pl.BlockSpec((None, CMP_TQ, 128), lambda b, h, i: (b, i, h)),
                   pl.BlockSpec((None, None, CMP_TQ, n_sel_pad), lambda b, h, i: (b, h, i, 0))),
        compiler_params=pltpu.CompilerParams(dimension_semantics=("parallel", "parallel", "arbitrary"),
                                             vmem_limit_bytes=VMEM_LIMIT),
        name="nsa_cmp_select",
    )(pb, kvc, kvc, bias_c, overlap)


SEL_T = 512


def _flash_init(m_ref, acc_ref):
    m_ref[...] = jnp.full_like(m_ref, NEG)
    acc_ref[...] = jnp.zeros_like(acc_ref)


def _flash_update(s, v_ones, m_ref, acc_ref, hd):
    m_old = m_ref[hd]
    m_new = jnp.maximum(m_old, jnp.max(s, axis=-1, keepdims=True))
    alpha = jnp.exp(m_old - m_new)
    pr = jnp.exp(s - jnp.tile(m_new, (1, s.shape[1] // LANES)))
    acc_ref[hd] = alpha * acc_ref[hd] + _dot(pr.astype(BF16), v_ones)
    m_ref[hd] = m_new


def _flash_finish(acc_ref, o_ref, heads):
    for hd in range(heads):
        acc = acc_ref[hd]
        row_sum = pltpu.roll(acc, HEAD_DIM, 1)
        o_ref[:, hd * HEAD_DIM:(hd + 1) * HEAD_DIM] = (acc / jnp.maximum(row_sum, 1e-30))[:, :HEAD_DIM]


def _sel_kernel(qi_ref, kj_ref, q_ref, k_ref, v_ref, sel_ref, exp_ref, bias_ref, o_ref, m_ref, acc_ref):
    p = pl.program_id(1)
    qi = qi_ref[p]
    kj = kj_ref[p]

    @pl.when(kj == 0)
    def _():
        _flash_init(m_ref, acc_ref)

    def step(diagonal):
        for hk in range(2):
            mask = _dot(sel_ref[hk], exp_ref[...]) > 0.5
            if diagonal:
                row = lax.broadcasted_iota(jnp.int32, (SEL_T, SEL_T), 0)
                col = lax.broadcasted_iota(jnp.int32, (SEL_T, SEL_T), 1)
                mask = mask & (col <= row)
            kcols = slice(hk * HEAD_DIM, (hk + 1) * HEAD_DIM)
            v_ones = v_ref[:, hk * LANES:(hk + 1) * LANES]
            for g in range(2):
                hd = hk * 2 + g
                cols = slice(hd * HEAD_DIM, (hd + 1) * HEAD_DIM)
                s = jnp.where(mask, _dot_nt(q_ref[:, cols], k_ref[:, kcols]) + bias_ref[hd], NEG)
                _flash_update(s, v_ones, m_ref, acc_ref, hd)

    @pl.when(kj < qi)
    def _():
        step(False)

    @pl.when(kj == qi)
    def _():
        step(True)
        _flash_finish(acc_ref, o_ref, 4)


def _causal_pairs(n):
    qi, kj = [], []
    for i in range(n):
        for j in range(i + 1):
            qi.append(i)
            kj.append(j)
    return jnp.asarray(np.array(qi, np.int32)), jnp.asarray(np.array(kj, np.int32))


def _sel_attention(pb, sel, expand, bias_s, batch, seq):
    n_sel_pad = sel.shape[3]
    n_delta = bias_s.shape[1]
    qi, kj = _causal_pairs(seq // SEL_T)
    grid_spec = pltpu.PrefetchScalarGridSpec(
        num_scalar_prefetch=2,
        grid=(batch, int(qi.shape[0])),
        in_specs=[
            pl.BlockSpec((None, SEL_T, 256), lambda b, p, qi, kj: (b, qi[p], PB_NSAQ // 256)),
            pl.BlockSpec((None, SEL_T, 128), lambda b, p, qi, kj: (b, kj[p], PB_NSAKS // 128)),
            pl.BlockSpec((None, SEL_T, 256), lambda b, p, qi, kj: (b, kj[p], PB_NSAVS // 256)),
            pl.BlockSpec((None, 2, SEL_T, n_sel_pad), lambda b, p, qi, kj: (b, 0, qi[p], 0)),
            pl.BlockSpec((n_sel_pad, SEL_T), lambda b, p, qi, kj: (0, kj[p])),
            pl.BlockSpec((4, None, SEL_T, SEL_T),
                         lambda b, p, qi, kj: (0, jnp.minimum(qi[p] - kj[p], n_delta - 1), 0, 0)),
        ],
        out_specs=pl.BlockSpec((None, SEL_T, 256), lambda b, p, qi, kj: (b, qi[p], 0)),
        scratch_shapes=[pltpu.VMEM((4, SEL_T, LANES), F32), pltpu.VMEM((4, SEL_T, LANES), F32)],
    )
    return pl.pallas_call(
        _sel_kernel,
        out_shape=jax.ShapeDtypeStruct((batch, seq, 256), F32),
        grid_spec=grid_spec,
        compiler_params=pltpu.CompilerParams(dimension_semantics=("parallel", "arbitrary"),
                                             vmem_limit_bytes=VMEM_LIMIT),
        name="nsa_selected",
    )(qi, kj, pb, pb, pb, sel, expand, bias_s)


MLA_T = 512


def _mla_kernel(qi_ref, kj_ref, q_ref, k_ref, v_ref, o_ref, m_ref, acc_ref):
    p = pl.program_id(1)
    qi = qi_ref[p]
    kj = kj_ref[p]

    @pl.when(kj == 0)
    def _():
        _flash_init(m_ref, acc_ref)

    def step(diagonal):
        for hd in range(MLA_HEADS):
            cols = slice(hd * LANES, (hd + 1) * LANES)
            s = _dot_nt(q_ref[:, cols], k_ref[:, cols])
            if diagonal:
                row = lax.broadcasted_iota(jnp.int32, (MLA_T, MLA_T), 0)
                col = lax.broadcasted_iota(jnp.int32, (MLA_T, MLA_T), 1)
                s = jnp.where(col <= row, s, NEG)
            _flash_update(s, v_ref[:, cols], m_ref, acc_ref, hd)

    @pl.when(kj < qi)
    def _():
        step(False)

    @pl.when(kj == qi)
    def _():
        step(True)
        _flash_finish(acc_ref, o_ref, MLA_HEADS)


def _mla_attention(pb, batch, seq):
    qi, kj = _causal_pairs(seq // MLA_T)
    grid_spec = pltpu.PrefetchScalarGridSpec(
        num_scalar_prefetch=2,
        grid=(batch, int(qi.shape[0])),
        in_specs=[
            pl.BlockSpec((None, MLA_T, 512), lambda b, p, qi, kj: (b, qi[p], PB_MLAQ // 512)),
            pl.BlockSpec((None, MLA_T, 512), lambda b, p, qi, kj: (b, kj[p], PB_MLAK // 512)),
            pl.BlockSpec((None, MLA_T, 512), lambda b, p, qi, kj: (b, kj[p], PB_MLAV // 512)),
        ],
        out_specs=pl.BlockSpec((None, MLA_T, 256), lambda b, p, qi, kj: (b, qi[p], 0)),
        scratch_shapes=[pltpu.VMEM((4, MLA_T, LANES), F32), pltpu.VMEM((4, MLA_T, LANES), F32)],
    )
    return pl.pallas_call(
        _mla_kernel,
        out_shape=jax.ShapeDtypeStruct((batch, seq, 256), F32),
        grid_spec=grid_spec,
        compiler_params=pltpu.CompilerParams(dimension_semantics=("parallel", "arbitrary"),
                                             vmem_limit_bytes=VMEM_LIMIT),
        name="mla_causal",
    )(qi, kj, pb, pb, pb)


def _outproj_kernel(x_ref, oa_ref, ob_ref, oc_ref, os_ref, ow_ref, od_ref, gate_ref, gexp_ref, gn_ref, w_ref,
                    o_ref):
    gates = gate_ref[...]
    g_hi, g_lo = _split_bf16(gates)

    def gate(branch):
        e = gexp_ref[branch]
        return _dot(g_hi, e) + _dot(g_lo, e)

    o_nsa = gate(0) * oc_ref[...] + gate(1) * os_ref[...] + gate(2) * ow_ref[...]
    gn = gn_ref[...]
    y = x_ref[...]
    for grp, o in enumerate((oa_ref[...], ob_ref[...], o_nsa, od_ref[...])):
        cols = slice(grp * GROUP_WIDTH, (grp + 1) * GROUP_WIDTH)
        normed = _row_rms(o, GROUP_WIDTH) * gn[:, cols]
        y = y + _dot(normed.astype(BF16), w_ref[cols, :])
    o_ref[...] = y


def _outproj(x2, oa, ob, oc, osel, ow, od, pf, gexp, gn, w, tm):
    tokens = x2.shape[0]
    row = lambda t: (t, 0)
    o_spec = pl.BlockSpec((tm, GROUP_WIDTH), row)
    return pl.pallas_call(
        _outproj_kernel,
        out_shape=jax.ShapeDtypeStruct((tokens, D_MODEL), F32),
        grid=(tokens // tm,),
        in_specs=[
            pl.BlockSpec((tm, D_MODEL), row), o_spec, o_spec, o_spec, o_spec, o_spec, o_spec,
            pl.BlockSpec((tm, 128), lambda t: (t, PF_GATE // 128)),
            pl.BlockSpec((3, 128, GROUP_WIDTH), lambda t: (0, 0, 0)),
            pl.BlockSpec((1, D_MODEL), lambda t: (0, 0)),
            pl.BlockSpec((D_MODEL, D_MODEL), lambda t: (0, 0)),
        ],
        out_specs=pl.BlockSpec((tm, D_MODEL), row),
        compiler_params=pltpu.CompilerParams(dimension_semantics=("parallel",), vmem_limit_bytes=VMEM_LIMIT),
        name="out_proj",
    )(x2, oa, ob, oc, osel, ow, od, pf, gexp, gn, w)


MOE_TM = 1024


def _moe_kernel(x_ref, fn_ref, wr_hi_ref, wr_lo_ref, br_ref, wg_ref, wu_ref, wd_ref, o_ref, h_ref, comb_ref, acc_ref):
    e = pl.program_id(1)
    lane = lax.broadcasted_iota(jnp.int32, (MOE_TM, LANES), 1)

    @pl.when(e == 0)
    def _():
        h = _row_rms(x_ref[...], D_MODEL) * fn_ref[...]
        h_ref[...] = h.astype(BF16)
        h_hi, h_lo = _split_bf16(h)
        logits = (_dot(h_hi, wr_hi_ref[...]) + _dot(h_lo, wr_hi_ref[...]) + _dot(h_hi, wr_lo_ref[...])
                  + br_ref[...])
        lane_f = lane.astype(F32)
        no_lane = float(LANES)
        is_group = lane < MOE_GROUPS
        g_max = jnp.max(jnp.where(is_group, logits, -jnp.inf), axis=-1, keepdims=True)
        g_star = jnp.min(jnp.where(is_group & (logits == g_max), lane_f, no_lane), axis=-1, keepdims=True)
        g_den = jnp.sum(jnp.where(is_group, jnp.exp(logits - g_max), 0.0), axis=-1, keepdims=True)
        g_w = 1.0 / g_den
        group_of_lane = ((lane - MOE_GROUPS) >> int(math.log2(MOE_EPG))).astype(F32)
        in_group = (lane >= MOE_GROUPS) & (lane < MOE_GROUPS + MOE_EXPERTS) & (group_of_lane == g_star)
        e_l = jnp.where(in_group, logits, -jnp.inf)
        top1 = jnp.max(e_l, axis=-1, keepdims=True)
        i1 = jnp.min(jnp.where(e_l == top1, lane_f, no_lane), axis=-1, keepdims=True)
        e_l2 = jnp.where(lane_f == i1, -jnp.inf, e_l)
        top2 = jnp.max(e_l2, axis=-1, keepdims=True)
        i2 = jnp.min(jnp.where(e_l2 == top2, lane_f, no_lane), axis=-1, keepdims=True)
        r = jnp.exp(top2 - top1)
        w1 = g_w / (1.0 + r)
        w2 = g_w * r / (1.0 + r)
        comb_ref[...] = jnp.where(lane_f == i1, w1, jnp.where(lane_f == i2, w2, 0.0))
        acc_ref[...] = jnp.zeros_like(acc_ref)

    c_e = jnp.sum(jnp.where(lane == e + MOE_GROUPS, comb_ref[...], 0.0), axis=-1, keepdims=True)
    hb = h_ref[...]
    a = jax.nn.silu(_dot(hb, wg_ref[...])) * _dot(hb, wu_ref[...])
    acc_ref[...] += _dot((a * c_e).astype(BF16), wd_ref[...])

    @pl.when(e == MOE_EXPERTS - 1)
    def _():
        o_ref[...] = x_ref[...] + acc_ref[...]


def _moe(x2, fn, wr_hi, wr_lo, br, wg, wu, wd):
    tokens = x2.shape[0]
    const = lambda t, e: (0, 0)
    return pl.pallas_call(
        _moe_kernel,
        out_shape=jax.ShapeDtypeStruct((tokens, D_MODEL), F32),
        grid=(tokens // MOE_TM, MOE_EXPERTS),
        in_specs=[
            pl.BlockSpec((MOE_TM, D_MODEL), lambda t, e: (t, 0)),
            pl.BlockSpec((1, D_MODEL), const),
            pl.BlockSpec((D_MODEL, LANES), const),
            pl.BlockSpec((D_MODEL, LANES), const),
            pl.BlockSpec((1, LANES), const),
            pl.BlockSpec((None, D_MODEL, MOE_HIDDEN), lambda t, e: (e, 0, 0)),
            pl.BlockSpec((None, D_MODEL, MOE_HIDDEN), lambda t, e: (e, 0, 0)),
            pl.BlockSpec((None, MOE_HIDDEN, D_MODEL), lambda t, e: (e, 0, 0)),
        ],
        out_specs=pl.BlockSpec((MOE_TM, D_MODEL), lambda t, e: (t, 0)),
        scratch_shapes=[pltpu.VMEM((MOE_TM, D_MODEL), BF16), pltpu.VMEM((MOE_TM, LANES), F32),
                        pltpu.VMEM((MOE_TM, D_MODEL), F32)],
        compiler_params=pltpu.CompilerParams(dimension_semantics=("parallel", "arbitrary"),
                                             vmem_limit_bytes=VMEM_LIMIT),
        name="hier_moe",
    )(x2, fn, wr_hi, wr_lo, br, wg, wu, wd)


def _t5_bucket(dist):
    n = jnp.maximum(dist, 0)
    max_exact = T5_BUCKETS // 2
    nf = jnp.maximum(n, 1).astype(F32)
    large = max_exact + (jnp.log(nf / max_exact) / math.log(T5_MAX_DIST / max_exact)
                         * (T5_BUCKETS - max_exact)).astype(jnp.int32)
    large = jnp.minimum(large, T5_BUCKETS - 1)
    return jnp.where(n < max_exact, n, large)


def _sel_delta_cap():
    max_exact = T5_BUCKETS // 2
    span = T5_BUCKETS - max_exact
    last_bucket_from = max_exact * (T5_MAX_DIST / max_exact) ** ((span - 1) / span)
    cap = 1
    while (cap - 1) * SEL_T + 1 < 1.25 * last_bucket_from:
        cap += 1
    return cap


def _position_tables(rel_bias, seq):
    buckets = _t5_bucket(jnp.arange(seq))
    first = jnp.sum(buckets[None, :] < jnp.arange(T5_BUCKETS)[:, None], axis=1)
    tbl = rel_bias.T

    def toeplitz(heads, dist):
        shape = (tbl[heads].shape[0],) + (1,) * dist.ndim
        out = jnp.broadcast_to(tbl[heads][:, 0].reshape(shape), shape[:1] + dist.shape)
        for b in range(1, T5_BUCKETS):
            out = jnp.where((dist >= first[b])[None], tbl[heads][:, b].reshape(shape), out)
        return out

    swa_h, nsa_h = slice(0, 4), slice(4, 8)
    tq_swa = -(-(SWA_WINDOW - 1) // 128) * 128
    tq_win = -(-(NSA_WINDOW - 1) // 128) * 128
    band = lambda tq: jnp.arange(tq)[:, None] + tq - jnp.arange(2 * tq)[None, :]
    bias_swa = toeplitz(swa_h, band(tq_swa))
    bias_win = toeplitz(nsa_h, band(tq_win))
    ncp = seq // NSA_CMP_STRIDE
    cmp_end = jnp.arange(ncp) * NSA_CMP_STRIDE + NSA_CMP_LEN - 1
    bias_cmp = toeplitz(nsa_h, jnp.arange(seq)[:, None] - cmp_end[None, :])
    nd = min(_sel_delta_cap() + 1, seq // SEL_T)
    dist_s = (jnp.arange(nd)[:, None, None] * SEL_T + jnp.arange(SEL_T)[None, :, None]
              - jnp.arange(SEL_T)[None, None, :])
    bias_sel = toeplitz(nsa_h, dist_s)

    n_sel_pad = -(-(seq // NSA_SEL_LEN) // LANES) * LANES
    sel_start = np.arange(n_sel_pad) * NSA_SEL_LEN
    c_start = np.arange(ncp) * NSA_CMP_STRIDE
    c_end = c_start + NSA_CMP_LEN - 1
    real = (np.arange(ncp) < ncp - NSA_CMP_LEN // NSA_CMP_STRIDE + 1)[:, None] & (sel_start < seq)[None, :]
    overlap = ((c_start[:, None] < sel_start[None, :] + NSA_SEL_LEN) & (c_end[:, None] >= sel_start[None, :]) & real)
    expand = (np.arange(n_sel_pad)[:, None] == (np.arange(seq) // NSA_SEL_LEN)[None, :])

    pos = jnp.arange(seq, dtype=F32)
    inv_freq = ROPE_THETA ** (-jnp.arange(0, MLA_ROPE, 2, dtype=F32) / MLA_ROPE)
    ang = pos[:, None] * inv_freq[None, :]
    cos, sin = jnp.cos(ang), jnp.sin(ang)
    ones = jnp.ones((seq, MLA_NOPE), F32)
    tail = LANES - MLA_NOPE - MLA_ROPE
    cos_t = jnp.concatenate([ones, cos, cos, jnp.ones((seq, tail), F32)], axis=1)
    sin_t = jnp.concatenate([0 * ones, -sin, sin, jnp.zeros((seq, tail), F32)], axis=1)
    return dict(bias_swa=bias_swa, bias_win=bias_win, bias_cmp=bias_cmp, bias_sel=bias_sel,
                overlap=jnp.asarray(overlap, BF16), expand=jnp.asarray(expand, BF16), cos_t=cos_t, sin_t=sin_t)


def _pad_to(a, shape):
    return jnp.pad(a, [(0, s - d) for d, s in zip(a.shape, shape)])


def _pack_layer(w_in, swa_q_norm, swa_k_norm, nsa_q_norm, nsa_k_norm, mla_q_lat_norm, mla_w_q_up,
                mla_kv_lat_norm, mla_w_kv_up, mla_q_norm, mla_k_norm):
    kpe = w_in[:, 2636:2668]
    kpe_seg = jnp.concatenate([jnp.zeros((D_MODEL, MLA_NOPE), F32), kpe,
                               jnp.zeros((D_MODEL, LANES - MLA_QK), F32)], axis=1)
    spread = lambda cols: _pad_to(cols.reshape(D_MODEL, -1, HEAD_DIM), (D_MODEL, cols.shape[1] // HEAD_DIM, LANES)
                                  ).reshape(D_MODEL, -1)
    w = jnp.concatenate([
        w_in[:, :1920],
        spread(w_in[:, 1920:2048]),
        w_in[:, 2048:2304],
        _pad_to(w_in[:, 2304:2316], (D_MODEL, 128)),
        _pad_to(w_in[:, 2316:2508], (D_MODEL, 256)),
        w_in[:, 2508:2636],
        jnp.tile(kpe_seg, (1, MLA_HEADS)),
    ], axis=1).astype(BF16)
    tile4 = lambda g: jnp.tile(g, 4)
    g64 = _pad_to(jnp.stack([tile4(swa_q_norm), tile4(swa_k_norm), tile4(nsa_q_norm),
                             tile4(nsa_k_norm[1]), tile4(nsa_k_norm[2])]), (8, 256))
    glat = _pad_to(jnp.stack([_pad_to(mla_q_lat_norm, (256,)), _pad_to(mla_kv_lat_norm, (256,))]), (8, 256))
    wq = _pad_to(mla_w_q_up.reshape(MLA_Q_RANK, MLA_HEADS, MLA_QK), (256, MLA_HEADS, LANES))
    wq = wq.reshape(256, MLA_HEADS * LANES).astype(BF16)
    wkv = mla_w_kv_up.reshape(MLA_KV_RANK, MLA_HEADS, MLA_NOPE + MLA_V)
    wk = _pad_to(wkv[:, :, :MLA_NOPE], (MLA_KV_RANK, MLA_HEADS, LANES)).reshape(MLA_KV_RANK, MLA_HEADS * LANES)
    wv = _pad_to(wkv[:, :, MLA_NOPE:], (MLA_KV_RANK, MLA_HEADS, LANES)).reshape(MLA_KV_RANK, MLA_HEADS * LANES)
    wkv_p = jnp.concatenate([wk, wv], axis=1).astype(BF16)
    gmla = _pad_to(jnp.stack([jnp.tile(_pad_to(mla_q_norm, (LANES,)), MLA_HEADS),
                              jnp.tile(_pad_to(mla_k_norm, (LANES,)), MLA_HEADS)]), (8, 512))
    return w, g64, glat, wq, wkv_p, gmla


def _gate_expand():
    rows = np.arange(128)[None, :, None]
    cols = np.arange(GROUP_WIDTH)[None, None, :]
    branch = np.arange(3)[:, None, None]
    return jnp.asarray(rows == branch * 4 + cols // HEAD_DIM, BF16)


def _compress_rows(pf3, batch, seq):
    nb = seq // NSA_CMP_STRIDE
    kv = pf3[:, :, :256].reshape(batch, seq, 2, 2, HEAD_DIM).transpose(2, 0, 3, 1, 4)
    blocks = kv.reshape(2, batch * 2, nb, NSA_CMP_STRIDE * HEAD_DIM)
    nxt = jnp.concatenate([blocks[:, :, 1:], jnp.zeros_like(blocks[:, :, :1])], axis=2)
    return jnp.concatenate([blocks, nxt], axis=3)


def kernel(x, rel_bias, attn_norm, w_in, swa_q_norm, swa_k_norm, swa_sinks, nsa_q_norm, nsa_k_norm, nsa_cmp_pos, nsa_cmp_w1, nsa_cmp_w2, mla_q_lat_norm, mla_w_q_up, mla_kv_lat_norm, mla_w_kv_up, mla_q_norm, mla_k_norm, out_norm, w_out, ffn_norm, moe_w_group, moe_b_group, moe_w_expert, moe_b_expert, moe_w_gate, moe_w_up, moe_w_down):
    batch, seq, _ = x.shape
    depth = w_in.shape[0]
    tokens = batch * seq
    tm = 512
    assert seq % 2048 == 0 and tokens % MOE_TM == 0
    tabs = _position_tables(rel_bias, seq)
    gexp = _gate_expand()
    zero_sinks = jnp.zeros((4,), F32)
    x2 = x.reshape(tokens, D_MODEL)
    for l in range(depth):
        w, g64, glat, wq, wkv, gmla = _pack_layer(
            w_in[l], swa_q_norm[l], swa_k_norm[l], nsa_q_norm[l], nsa_k_norm[l], mla_q_lat_norm[l],
            mla_w_q_up[l], mla_kv_lat_norm[l], mla_w_kv_up[l], mla_q_norm[l], mla_k_norm[l])
        pb, pf = _prep(x2, attn_norm[l][None, :], w, g64, glat, wq, wkv, gmla, tabs["cos_t"], tabs["sin_t"],
                       seq, tm)
        pb3 = pb.reshape(batch, seq, PB_WIDTH)
        pf3 = pf.reshape(batch, seq, PF_WIDTH)
        o_a = _sb_attention(pb3, batch, seq)
        o_b = _banded_attention(pb3, swa_sinks[l], tabs["bias_swa"], batch, seq, SWA_WINDOW,
                                PB_SWAQ, PB_SWAK, PB_SWAV, True)
        rows = _compress_rows(pf3, batch, seq)
        kvc = _compress(rows, nsa_cmp_pos[l].reshape(2, 1, -1),
                        nsa_cmp_w1[l].reshape(2, -1, NSA_CMP_HIDDEN).astype(BF16), nsa_cmp_w2[l].astype(BF16),
                        nsa_k_norm[l][0][None, :], 128)
        o_c, sel = _cmp_attention(pb3, kvc, tabs["bias_cmp"], tabs["overlap"], batch, seq)
        o_s = _sel_attention(pb3, sel, tabs["expand"], tabs["bias_sel"], batch, seq)
        o_w = _banded_attention(pb3, zero_sinks, tabs["bias_win"], batch, seq, NSA_WINDOW,
                                PB_NSAQ, PB_NSAKW, PB_NSAVW, False)
        o_d = _mla_attention(pb3, batch, seq)
        flat = lambda o: o.reshape(tokens, GROUP_WIDTH)
        x2 = _outproj(x2, flat(o_a), flat(o_b), flat(o_c), flat(o_s), flat(o_w), flat(o_d), pf, gexp,
                      out_norm[l][None, :], w_out[l].astype(BF16), tm)
        w_router = _pad_to(jnp.concatenate([moe_w_group[l], moe_w_expert[l]], axis=1), (D_MODEL, LANES))
        wr_hi = w_router.astype(BF16)
        wr_lo = (w_router - wr_hi.astype(F32)).astype(BF16)
        b_router = _pad_to(jnp.concatenate([moe_b_group[l], moe_b_expert[l]])[None, :], (1, LANES))
        x2 = _moe(x2, ffn_norm[l][None, :], wr_hi, wr_lo, b_router, moe_w_gate[l].astype(BF16),
                  moe_w_up[l].astype(BF16), moe_w_down[l].astype(BF16))
    return x2.reshape(batch, seq, D_MODEL)
```

```python
import functools
import math

import numpy as np
import jax
import jax.numpy as jnp
from jax import lax
from jax.experimental import pallas as pl
from jax.experimental.pallas import tpu as pltpu

F32 = jnp.float32
BF16 = jnp.bfloat16

D_MODEL = 1024
HEAD_DIM = 64
NEG = -1e30
EPS = 1e-6
FORCE_BONUS = 1000.0
SWA_WINDOW = 128
NSA_CMP_LEN = 32
NSA_CMP_STRIDE = 16
NSA_CMP_HIDDEN = 128
NSA_SEL_LEN = 64
NSA_TOPK = 16
NSA_WINDOW = 512
MLA_HEADS = 4
MLA_NOPE = 64
MLA_ROPE = 32
MLA_V = 64
MLA_Q_RANK = 192
MLA_KV_RANK = 128
MLA_QK = MLA_NOPE + MLA_ROPE
ROPE_THETA = 10000.0
T5_BUCKETS = 32
T5_MAX_DIST = 1024
MOE_GROUPS = 4
MOE_EPG = 8
MOE_EXPERTS = MOE_GROUPS * MOE_EPG
MOE_HIDDEN = 256
GROUP_WIDTH = 256
LANES = 128
VMEM_LIMIT = 48 * 1024 * 1024

PB_MLAQ, PB_MLAK, PB_MLAV = 0, 512, 1024
PB_SBQ, PB_SBK, PB_SBV = 1536, 1792, 2048
PB_SWAQ, PB_NSAQ, PB_NSAVS = 2304, 2560, 2816
PB_SWAK, PB_SWAV, PB_NSAKW, PB_NSAVW = 3072, 3200, 3328, 3456
PB_NSAKS = 3584
PB_WIDTH = 4096
PF_KC, PF_VC, PF_GATE = 0, 128, 256
PF_WIDTH = 384
W_SBQ, W_SBK, W_SBV, W_SWAQ, W_SWAK, W_SWAV, W_NSAQ = 0, 256, 512, 768, 1024, 1152, 1280
W_KC, W_VC, W_KS, W_VS, W_KW, W_VW, W_GATE = 1536, 1664, 1792, 1920, 2176, 2304, 2432
W_CQ, W_CKV, W_KPE = 2560, 2816, 2944
W_WIDTH = 3456

NT_DIMS = (((1,), (1,)), ((), ()))


def _dot(a, b):
    return jnp.dot(a, b, preferred_element_type=F32)


def _dot_nt(a, b):
    return lax.dot_general(a, b, NT_DIMS, preferred_element_type=F32)


def _split_bf16(x):
    hi = x.astype(BF16)
    lo = (x - hi.astype(F32)).astype(BF16)
    return hi, lo


def _dot_exact_rhs(x, m):
    hi, lo = _split_bf16(x)
    return _dot(hi, m) + _dot(lo, m)


def _block_diag_ones(width, seg):
    idx = np.arange(width) // seg
    return jnp.asarray(idx[:, None] == idx[None, :], BF16)


def _seg_rms(x, seg_ones, count):
    width = x.shape[1]
    ms = _dot_exact_rhs(x * x, seg_ones[:width, :width]) * (1.0 / count)
    return x * lax.rsqrt(ms + EPS)


def _row_rms(x, count):
    return x * lax.rsqrt(jnp.sum(x * x, axis=-1, keepdims=True) * (1.0 / count) + EPS)


def _prep_kernel(x_ref, an_ref, w_ref, g64_ref, glat_ref, wq_ref, wkv_ref, gmla_ref, cos_ref, sin_ref,
                 s64_ref, s128_ref, pb_ref, pf_ref):
    x = x_ref[...]
    s64 = s64_ref[...]
    s128 = s128_ref[...]
    h = _row_rms(x, D_MODEL) * an_ref[...]
    hb = h.astype(BF16)

    def proj(lo, hi):
        return _dot(hb, w_ref[:, lo:hi])

    def put(col, value):
        pb_ref[:, col:col + value.shape[1]] = value.astype(BF16)

    def ones_tail(width):
        lane = lax.broadcasted_iota(jnp.int32, (1, width), 1)
        return jnp.where((lane & (LANES - 1)) >= HEAD_DIM, 1.0, 0.0)

    scale = HEAD_DIM ** -0.5
    g64 = g64_ref[...]
    put(PB_SBQ, proj(W_SBQ, W_SBQ + 256) * scale)
    put(PB_SBK, proj(W_SBK, W_SBK + 256))
    put(PB_SBV, proj(W_SBV, W_SBV + 256))
    put(PB_SWAQ, _seg_rms(proj(W_SWAQ, W_SWAQ + 256), s64, 64) * g64[0:1, :] * scale)
    put(PB_SWAK, _seg_rms(proj(W_SWAK, W_SWAK + 128), s64, 64) * g64[1:2, :128])
    put(PB_SWAV, proj(W_SWAV, W_SWAV + 128))
    put(PB_NSAQ, _seg_rms(proj(W_NSAQ, W_NSAQ + 256), s64, 64) * g64[2:3, :] * scale)
    pf_ref[:, PF_KC:PF_KC + 128] = proj(W_KC, W_KC + 128)
    pf_ref[:, PF_VC:PF_VC + 128] = proj(W_VC, W_VC + 128)
    ks = _seg_rms(proj(W_KS, W_KS + 128), s64, 64) * g64[3:4, :128]
    ks_swapped = pltpu.roll(ks, HEAD_DIM, 1)
    low = lax.broadcasted_iota(jnp.int32, ks.shape, 1) < HEAD_DIM
    put(PB_NSAKS, jnp.where(low, ks, 0.0))
    put(PB_NSAKS + LANES, jnp.where(low, 0.0, ks_swapped))
    put(PB_NSAKS + 2 * LANES, jnp.where(low, ks_swapped, 0.0))
    put(PB_NSAKS + 3 * LANES, jnp.where(low, 0.0, ks))
    put(PB_NSAVS, proj(W_VS, W_VS + 256) + ones_tail(256))
    put(PB_NSAKW, _seg_rms(proj(W_KW, W_KW + 128), s64, 64) * g64[4:5, :128])
    put(PB_NSAVW, proj(W_VW, W_VW + 128))
    pf_ref[:, PF_GATE:PF_GATE + 128] = jax.nn.sigmoid(proj(W_GATE, W_GATE + 128))

    glat = glat_ref[...]
    cq = _row_rms(proj(W_CQ, W_CQ + 256), MLA_Q_RANK) * glat[0:1, :]
    q = _dot(cq.astype(BF16), wq_ref[...])
    ckv = _row_rms(proj(W_CKV, W_CKV + 128), MLA_KV_RANK) * glat[1:2, :128]
    kv = _dot(ckv.astype(BF16), wkv_ref[...])
    k = kv[:, :512] + proj(W_KPE, W_KPE + 512)
    gm = gmla_ref[...]
    q = _seg_rms(q, s128, MLA_QK) * gm[0:1, :]
    k = _seg_rms(k, s128, MLA_QK) * gm[1:2, :]
    cos = cos_ref[...]
    sin = sin_ref[...]
    lane = lax.broadcasted_iota(jnp.int32, (x.shape[0], LANES), 1)
    first_half = lane < MLA_NOPE + MLA_ROPE // 2

    def rope(t):
        partner = jnp.where(first_half, pltpu.roll(t, LANES - MLA_ROPE // 2, 1), pltpu.roll(t, MLA_ROPE // 2, 1))
        return t * cos + partner * sin

    qscale = MLA_QK ** -0.5
    for hd in range(MLA_HEADS):
        sl = slice(hd * LANES, (hd + 1) * LANES)
        put(PB_MLAQ + hd * LANES, rope(q[:, sl]) * qscale)
        put(PB_MLAK + hd * LANES, rope(k[:, sl]))
    put(PB_MLAV, kv[:, 512:1024] + ones_tail(512))


def _prep(x2, an, w, g64, glat, wq, wkv, gmla, cos_t, sin_t, seq, tm):
    tokens = x2.shape[0]
    n_pos = seq // tm
    const = lambda t: (0, 0)
    return pl.pallas_call(
        _prep_kernel,
        out_shape=(jax.ShapeDtypeStruct((tokens, PB_WIDTH), BF16), jax.ShapeDtypeStruct((tokens, PF_WIDTH), F32)),
        grid=(tokens // tm,),
        in_specs=[
            pl.BlockSpec((tm, D_MODEL), lambda t: (t, 0)),
            pl.BlockSpec((1, D_MODEL), const),
            pl.BlockSpec((D_MODEL, W_WIDTH), const),
            pl.BlockSpec((8, 256), const),
            pl.BlockSpec((8, 256), const),
            pl.BlockSpec((256, 512), const),
            pl.BlockSpec((128, 1024), const),
            pl.BlockSpec((8, 512), const),
            pl.BlockSpec((tm, LANES), lambda t: (t % n_pos, 0)),
            pl.BlockSpec((tm, LANES), lambda t: (t % n_pos, 0)),
            pl.BlockSpec((256, 256), const),
            pl.BlockSpec((512, 512), const),
        ],
        out_specs=(pl.BlockSpec((tm, PB_WIDTH), lambda t: (t, 0)), pl.BlockSpec((tm, PF_WIDTH), lambda t: (t, 0))),
        compiler_params=pltpu.CompilerParams(dimension_semantics=("parallel",), vmem_limit_bytes=VMEM_LIMIT),
        name="prep",
    )(x2, an, w, g64, glat, wq, wkv, gmla, cos_t, sin_t, _block_diag_ones(256, HEAD_DIM),
      _block_diag_ones(512, LANES))


SB_TQ, SB_KB, SB_SUB = 256, 1024, 128


SB_UNDERFLOW = 110.0


def _sb_kernel(qi_ref, kj_ref, kmax_ref, q_ref, k_ref, v_ref, sums_ref, o_ref, carry_ref, acc_ref, zb_ref,
               dead_ref):
    b = pl.program_id(0)
    p = pl.program_id(1)
    qi = qi_ref[p]
    kj = kj_ref[p]
    q_start = qi * SB_TQ
    first = kj == (q_start + SB_TQ - 1) // SB_KB

    @pl.when(first)
    def _():
        carry_ref[...] = jnp.zeros_like(carry_ref)
        acc_ref[...] = jnp.zeros_like(acc_ref)
        dead_ref[0] = 0
        ones = jnp.ones((HEAD_DIM, SB_SUB), BF16)
        for hd in range(4):
            q = q_ref[:, hd * HEAD_DIM:(hd + 1) * HEAD_DIM].astype(F32)
            zb_ref[hd * SB_TQ:(hd + 1) * SB_TQ, :] = (jnp.sqrt(_dot_exact_rhs(q * q, ones))
                                                      * (kmax_ref[b * 4 + hd] * 1.01))

    rel = (lax.broadcasted_iota(jnp.int32, (SB_TQ, SB_SUB), 1)
           - lax.broadcasted_iota(jnp.int32, (SB_TQ, SB_SUB), 0))

    def sub_tile(u, k_start):
        mask = jnp.tile(rel < q_start - k_start, (4, 1))
        rows = slice(u * SB_SUB, (u + 1) * SB_SUB)
        head = lambda hd: slice(hd * HEAD_DIM, (hd + 1) * HEAD_DIM)
        z = jnp.concatenate([_dot_nt(q_ref[:, head(hd)], k_ref[rows, head(hd)]) for hd in range(4)], axis=0)
        log_keep = jnp.where(mask, -(jnp.maximum(z, 0.0) + jnp.log(1.0 + jnp.exp(-jnp.abs(z)))), 0.0)
        hi, lo = _split_bf16(log_keep)
        sums = _dot(jnp.concatenate([hi, lo], axis=1), sums_ref[...])
        carry = carry_ref[...]
        a = jnp.where(mask, jnp.exp(z + sums[:, :SB_SUB] + carry), 0.0).astype(BF16)
        for hd in range(4):
            acc_ref[hd] += _dot(a[hd * SB_TQ:(hd + 1) * SB_TQ], v_ref[rows, head(hd)])
        carry_ref[...] = carry + sums[:, SB_SUB:]

    for u in reversed(range(SB_KB // SB_SUB)):
        k_start = kj * SB_KB + u * SB_SUB

        @pl.when((k_start < q_start + SB_TQ) & (dead_ref[0] == 0))
        def _(u=u, k_start=k_start):
            live = jnp.max(carry_ref[...] + zb_ref[...]) > -SB_UNDERFLOW

            @pl.when(live)
            def _():
                sub_tile(u, k_start)

            @pl.when(jnp.logical_not(live))
            def _():
                dead_ref[0] = 1

    @pl.when(kj == 0)
    def _():
        for hd in range(4):
            o_ref[:, hd * HEAD_DIM:(hd + 1) * HEAD_DIM] = acc_ref[hd]


def _sb_attention(pb, batch, seq):
    nq = seq // SB_TQ
    qi, kj = [], []
    for i in range(nq):
        for j in reversed(range((i * SB_TQ + SB_TQ - 1) // SB_KB + 1)):
            qi.append(i)
            kj.append(j)
    qi = jnp.asarray(np.array(qi, np.int32))
    kj = jnp.asarray(np.array(kj, np.int32))
    keys = pb[:, :, PB_SBK:PB_SBK + 256].astype(F32).reshape(batch, seq, 4, HEAD_DIM)
    kmax = jnp.sqrt(jnp.max(jnp.sum(keys * keys, axis=-1), axis=1)).reshape(batch * 4)
    j = np.arange(2 * SB_SUB)[:, None] % SB_SUB
    s = np.arange(2 * SB_SUB)[None, :]
    sums = jnp.asarray((s >= SB_SUB) | (j >= s), BF16)
    grid_spec = pltpu.PrefetchScalarGridSpec(
        num_scalar_prefetch=2,
        grid=(batch, int(qi.shape[0])),
        in_specs=[
            pl.BlockSpec(memory_space=pltpu.SMEM),
            pl.BlockSpec((None, SB_TQ, 256), lambda b, p, qi, kj: (b, qi[p], PB_SBQ // 256)),
            pl.BlockSpec((None, SB_KB, 256), lambda b, p, qi, kj: (b, kj[p], PB_SBK // 256)),
            pl.BlockSpec((None, SB_KB, 256), lambda b, p, qi, kj: (b, kj[p], PB_SBV // 256)),
            pl.BlockSpec((2 * SB_SUB, 2 * SB_SUB), lambda b, p, qi, kj: (0, 0)),
        ],
        out_specs=pl.BlockSpec((None, SB_TQ, 256), lambda b, p, qi, kj: (b, qi[p], 0)),
        scratch_shapes=[pltpu.VMEM((4 * SB_TQ, SB_SUB), F32), pltpu.VMEM((4, SB_TQ, HEAD_DIM), F32),
                        pltpu.VMEM((4 * SB_TQ, SB_SUB), F32), pltpu.SMEM((1,), jnp.int32)],
    )
    return pl.pallas_call(
        _sb_kernel,
        out_shape=jax.ShapeDtypeStruct((batch, seq, 256), F32),
        grid_spec=grid_spec,
        compiler_params=pltpu.CompilerParams(dimension_semantics=("parallel", "arbitrary"),
                                             vmem_limit_bytes=VMEM_LIMIT),
        name="stick_breaking",
    )(qi, kj, kmax, pb, pb, pb, sums)


def _banded_kernel(sink_ref, q_ref, kp_ref, kc_ref, vp_ref, vc_ref, bias_ref, o_ref, *, tq, window, use_sink):
    i = pl.program_id(1)
    qi = lax.broadcasted_iota(jnp.int32, (tq, tq), 0)
    ki = lax.broadcasted_iota(jnp.int32, (tq, tq), 1)
    dist_prev = qi + tq - ki
    dist_cur = qi - ki
    mask_prev = (dist_prev < window) & (i > 0)
    mask_cur = (dist_cur >= 0) & (dist_cur < window)
    for hd in range(4):
        cols = slice(hd * HEAD_DIM, (hd + 1) * HEAD_DIM)
        kcols = slice((hd // 2) * HEAD_DIM, (hd // 2 + 1) * HEAD_DIM)
        q = q_ref[:, cols]
        s_prev = jnp.where(mask_prev, _dot_nt(q, kp_ref[:, kcols]) + bias_ref[hd, :, :tq], NEG)
        s_cur = jnp.where(mask_cur, _dot_nt(q, kc_ref[:, kcols]) + bias_ref[hd, :, tq:], NEG)
        m = jnp.maximum(jnp.max(s_prev, axis=-1, keepdims=True), jnp.max(s_cur, axis=-1, keepdims=True))
        if use_sink:
            sink = sink_ref[hd]
            m = jnp.maximum(m, sink)
        p_prev = jnp.where(mask_prev, jnp.exp(s_prev - m), 0.0)
        p_cur = jnp.where(mask_cur, jnp.exp(s_cur - m), 0.0)
        denom = jnp.sum(p_prev, axis=-1, keepdims=True) + jnp.sum(p_cur, axis=-1, keepdims=True)
        if use_sink:
            denom = denom + jnp.exp(sink - m)
        o = _dot(p_prev.astype(BF16), vp_ref[:, kcols]) + _dot(p_cur.astype(BF16), vc_ref[:, kcols])
        o_ref[:, cols] = o / jnp.maximum(denom, 1e-30)


def _banded_attention(pb, sinks, bias, batch, seq, window, q_col, k_col, v_col, use_sink):
    tq = -(-(window - 1) // 128) * 128
    grid_spec = pltpu.PrefetchScalarGridSpec(
        num_scalar_prefetch=1,
        grid=(batch, seq // tq),
        in_specs=[
            pl.BlockSpec((None, tq, 256), lambda b, i, s: (b, i, q_col // 256)),
            pl.BlockSpec((None, tq, 128), lambda b, i, s: (b, jnp.maximum(i - 1, 0), k_col // 128)),
            pl.BlockSpec((None, tq, 128), lambda b, i, s: (b, i, k_col // 128)),
            pl.BlockSpec((None, tq, 128), lambda b, i, s: (b, jnp.maximum(i - 1, 0), v_col // 128)),
            pl.BlockSpec((None, tq, 128), lambda b, i, s: (b, i, v_col // 128)),
            pl.BlockSpec((4, tq, 2 * tq), lambda b, i, s: (0, 0, 0)),
        ],
        out_specs=pl.BlockSpec((None, tq, 256), lambda b, i, s: (b, i, 0)),
    )
    return pl.pallas_call(
        functools.partial(_banded_kernel, tq=tq, window=window, use_sink=use_sink),
        out_shape=jax.ShapeDtypeStruct((batch, seq, 256), F32),
        grid_spec=grid_spec,
        compiler_params=pltpu.CompilerParams(dimension_semantics=("parallel", "arbitrary"),
                                             vmem_limit_bytes=VMEM_LIMIT),
        name="banded_w%d" % window,
    )(sinks, pb, pb, pb, pb, pb, bias)


def _compress_kernel(rows_ref, pos_ref, w1_ref, w2_ref, g_ref, o_ref):
    win = rows_ref[...] + pos_ref[...]
    hid = jax.nn.gelu(_dot(win.astype(BF16), w1_ref[...]), approximate=True)
    out = _dot(hid.astype(BF16), w2_ref[...])
    normed = _row_rms(out, HEAD_DIM) * g_ref[...]
    o_ref[...] = jnp.where(pl.program_id(0) == 0, normed, out).astype(BF16)


def _compress(rows, pos, w1, w2, gain, tn):
    _, bh, ncp, width = rows.shape
    return pl.pallas_call(
        _compress_kernel,
        out_shape=jax.ShapeDtypeStruct((2, bh, ncp, HEAD_DIM), BF16),
        grid=(2, bh, ncp // tn),
        in_specs=[
            pl.BlockSpec((None, None, tn, width), lambda c, r, n: (c, r, n, 0)),
            pl.BlockSpec((None, 1, width), lambda c, r, n: (c, 0, 0)),
            pl.BlockSpec((None, width, NSA_CMP_HIDDEN), lambda c, r, n: (c, 0, 0)),
            pl.BlockSpec((None, NSA_CMP_HIDDEN, HEAD_DIM), lambda c, r, n: (c, 0, 0)),
            pl.BlockSpec((1, HEAD_DIM), lambda c, r, n: (0, 0)),
        ],
        out_specs=pl.BlockSpec((None, None, tn, HEAD_DIM), lambda c, r, n: (c, r, n, 0)),
        compiler_params=pltpu.CompilerParams(dimension_semantics=("parallel", "parallel", "parallel"),
                                             vmem_limit_bytes=VMEM_LIMIT),
        name="nsa_compress",
    )(rows, pos, w1, w2, gain)


CMP_TQ = 1024


def _cmp_kernel(q_ref, kc_ref, vc_ref, bias_ref, ov_ref, o_ref, qa_ref, *, ncp, n_sel_pad, topk):
    i = pl.program_id(2)
    q_pos = i * CMP_TQ + lax.broadcasted_iota(jnp.int32, (CMP_TQ, ncp), 0)
    cmp_end = lax.broadcasted_iota(jnp.int32, (CMP_TQ, ncp), 1) * NSA_CMP_STRIDE + (NSA_CMP_LEN - 1)
    mask = cmp_end <= q_pos
    kc = kc_ref[...]
    vc = vc_ref[...]
    p_sum = jnp.zeros((CMP_TQ, ncp), F32)
    for g in range(2):
        cols = slice(g * HEAD_DIM, (g + 1) * HEAD_DIM)
        s = jnp.where(mask, _dot_nt(q_ref[:, cols], kc) + bias_ref[g], NEG)
        m = jnp.max(s, axis=-1, keepdims=True)
        p = jnp.where(mask, jnp.exp(s - m), 0.0)
        p = p / jnp.maximum(jnp.sum(p, axis=-1, keepdims=True), 1e-30)
        o_ref[:, cols] = _dot(p.astype(BF16), vc)
        p_sum = p_sum + p
    imp = _dot_exact_rhs(p_sum, ov_ref[...])

    row_pos = i * CMP_TQ + lax.broadcasted_iota(jnp.int32, (CMP_TQ, n_sel_pad), 0)
    blk = lax.broadcasted_iota(jnp.int32, (CMP_TQ, n_sel_pad), 1)
    cur = row_pos >> int(math.log2(NSA_SEL_LEN))
    forced = (blk == 0) | (blk == cur) | (blk == cur - 1)
    valid = blk * NSA_SEL_LEN <= row_pos
    score = jnp.where(valid, imp + jnp.where(forced, FORCE_BONUS, 0.0), NEG)
    blk_f = blk.astype(F32)
    dropped = jnp.ones((CMP_TQ, n_sel_pad), F32)
    for _ in range(topk):
        best = jnp.max(score, axis=-1, keepdims=True)
        first = jnp.min(jnp.where(score == best, blk_f, float(n_sel_pad)), axis=-1, keepdims=True)
        hit = blk_f == first
        dropped = jnp.where(hit, 0.0, dropped)
        score = jnp.where(hit, -jnp.inf, score)
    dropped = jnp.where(valid, dropped, 1.0).astype(BF16)
    q = q_ref[...]
    low = lax.broadcasted_iota(jnp.int32, q.shape, 1) < HEAD_DIM
    zero = jnp.zeros_like(q)
    width = LANES + n_sel_pad
    qa_ref[:, 0:LANES] = jnp.where(low, q, zero)
    qa_ref[:, LANES:width] = dropped
    qa_ref[:, width:width + LANES] = jnp.where(low, zero, q)
    qa_ref[:, width + LANES:2 * width] = dropped


def _cmp_attention(pb, kvc, bias_c, overlap, batch, seq):
    ncp = kvc.shape[2]
    n_sel_pad = overlap.shape[1]
    topk = min(NSA_TOPK, seq // NSA_SEL_LEN)
    pair = 2 * (LANES + n_sel_pad)
    return pl.pallas_call(
        functools.partial(_cmp_kernel, ncp=ncp, n_sel_pad=n_sel_pad, topk=topk),
        out_shape=(jax.ShapeDtypeStruct((batch, seq, 256), F32),
                   jax.ShapeDtypeStruct((batch, seq, 2 * pair), BF16)),
        grid=(batch, 2, seq // CMP_TQ),
        in_specs=[
            pl.BlockSpec((None, CMP_TQ, 128), lambda b, h, i: (b, i, PB_NSAQ // 128 + h)),
            pl.BlockSpec((None, None, ncp, HEAD_DIM), lambda b, h, i: (0, b * 2 + h, 0, 0)),
            pl.BlockSpec((None, None, ncp, HEAD_DIM), lambda b, h, i: (1, b * 2 + h, 0, 0)),
            pl.BlockSpec((2, CMP_TQ, ncp), lambda b, h, i: (h, i, 0)),
            pl.BlockSpec((ncp, n_sel_pad), lambda b, h, i: (0, 0)),
        ],
        out_specs=(pl.BlockSpec((None, CMP_TQ, 128), lambda b, h, i: (b, i, h)),
                   pl.BlockSpec((None, CMP_TQ, pair), lambda b, h, i: (b, i, h))),
        compiler_params=pltpu.CompilerParams(dimension_semantics=("parallel", "parallel", "arbitrary"),
                                             vmem_limit_bytes=VMEM_LIMIT),
        name="nsa_cmp_select",
    )(pb, kvc, kvc, bias_c, overlap)


SEL_T = 512


def _flash_init(m_ref, acc_ref):
    m_ref[...] = jnp.full_like(m_ref, NEG)
    acc_ref[...] = jnp.zeros_like(acc_ref)


def _flash_update(s, v_ones, m_ref, acc_ref, hd):
    m_old = m_ref[hd]
    m_new = jnp.maximum(m_old, jnp.max(s, axis=-1, keepdims=True))
    alpha = jnp.exp(m_old - m_new)
    pr = jnp.exp(s - jnp.tile(m_new, (1, s.shape[1] // LANES)))
    acc_ref[hd] = alpha * acc_ref[hd] + _dot(pr.astype(BF16), v_ones)
    m_ref[hd] = m_new


def _flash_finish(acc_ref, o_ref, heads):
    for hd in range(heads):
        acc = acc_ref[hd]
        row_sum = pltpu.roll(acc, HEAD_DIM, 1)
        o_ref[:, hd * HEAD_DIM:(hd + 1) * HEAD_DIM] = (acc / jnp.maximum(row_sum, 1e-30))[:, :HEAD_DIM]


def _sel_kernel(qi_ref, kj_ref, q_ref, k_ref, v_ref, pen_ref, bias_ref, o_ref, m_ref, acc_ref):
    p = pl.program_id(1)
    qi = qi_ref[p]
    kj = kj_ref[p]
    width = q_ref.shape[1] // 4

    @pl.when(kj == 0)
    def _():
        _flash_init(m_ref, acc_ref)

    def step(diagonal):
        pen = pen_ref[...]
        for hd in range(4):
            k_pen = jnp.concatenate([k_ref[:, hd * LANES:(hd + 1) * LANES], pen], axis=1)
            s = _dot_nt(q_ref[:, hd * width:(hd + 1) * width], k_pen) + bias_ref[hd]
            if diagonal:
                row = lax.broadcasted_iota(jnp.int32, (SEL_T, SEL_T), 0)
                col = lax.broadcasted_iota(jnp.int32, (SEL_T, SEL_T), 1)
                s = jnp.where(col <= row, s, NEG)
            _flash_update(s, v_ref[:, (hd // 2) * LANES:(hd // 2 + 1) * LANES], m_ref, acc_ref, hd)

    @pl.when(kj < qi)
    def _():
        step(False)

    @pl.when(kj == qi)
    def _():
        step(True)
        _flash_finish(acc_ref, o_ref, 4)


def _causal_pairs(n):
    qi, kj = [], []
    for i in range(n):
        for j in range(i + 1):
            qi.append(i)
            kj.append(j)
    return jnp.asarray(np.array(qi, np.int32)), jnp.asarray(np.array(kj, np.int32))


def _sel_attention(pb, q_aug, penalty, bias_s, batch, seq):
    n_sel_pad = penalty.shape[1]
    n_delta = bias_s.shape[1]
    qi, kj = _causal_pairs(seq // SEL_T)
    grid_spec = pltpu.PrefetchScalarGridSpec(
        num_scalar_prefetch=2,
        grid=(batch, int(qi.shape[0])),
        in_specs=[
            pl.BlockSpec((None, SEL_T, q_aug.shape[2]), lambda b, p, qi, kj: (b, qi[p], 0)),
            pl.BlockSpec((None, SEL_T, 512), lambda b, p, qi, kj: (b, kj[p], PB_NSAKS // 512)),
            pl.BlockSpec((None, SEL_T, 256), lambda b, p, qi, kj: (b, kj[p], PB_NSAVS // 256)),
            pl.BlockSpec((SEL_T, n_sel_pad), lambda b, p, qi, kj: (kj[p], 0)),
            pl.BlockSpec((4, None, SEL_T, SEL_T),
                         lambda b, p, qi, kj: (0, jnp.minimum(qi[p] - kj[p], n_delta - 1), 0, 0)),
        ],
        out_specs=pl.BlockSpec((None, SEL_T, 256), lambda b, p, qi, kj: (b, qi[p], 0)),
        scratch_shapes=[pltpu.VMEM((4, SEL_T, LANES), F32), pltpu.VMEM((4, SEL_T, LANES), F32)],
    )
    return pl.pallas_call(
        _sel_kernel,
        out_shape=jax.ShapeDtypeStruct((batch, seq, 256), F32),
        grid_spec=grid_spec,
        compiler_params=pltpu.CompilerParams(dimension_semantics=("parallel", "arbitrary"),
                                             vmem_limit_bytes=VMEM_LIMIT),
        name="nsa_selected",
    )(qi, kj, q_aug, pb, pb, penalty, bias_s)


MLA_T = 512


def _mla_kernel(qi_ref, kj_ref, q_ref, k_ref, v_ref, o_ref, m_ref, acc_ref):
    p = pl.program_id(1)
    qi = qi_ref[p]
    kj = kj_ref[p]

    @pl.when(kj == 0)
    def _():
        _flash_init(m_ref, acc_ref)

    def step(diagonal):
        for hd in range(MLA_HEADS):
            cols = slice(hd * LANES, (hd + 1) * LANES)
            s = _dot_nt(q_ref[:, cols], k_ref[:, cols])
            if diagonal:
                row = lax.broadcasted_iota(jnp.int32, (MLA_T, MLA_T), 0)
                col = lax.broadcasted_iota(jnp.int32, (MLA_T, MLA_T), 1)
                s = jnp.where(col <= row, s, NEG)
            _flash_update(s, v_ref[:, cols], m_ref, acc_ref, hd)

    @pl.when(kj < qi)
    def _():
        step(False)

    @pl.when(kj == qi)
    def _():
        step(True)
        _flash_finish(acc_ref, o_ref, MLA_HEADS)


def _mla_attention(pb, batch, seq):
    qi, kj = _causal_pairs(seq // MLA_T)
    grid_spec = pltpu.PrefetchScalarGridSpec(
        num_scalar_prefetch=2,
        grid=(batch, int(qi.shape[0])),
        in_specs=[
            pl.BlockSpec((None, MLA_T, 512), lambda b, p, qi, kj: (b, qi[p], PB_MLAQ // 512)),
            pl.BlockSpec((None, MLA_T, 512), lambda b, p, qi, kj: (b, kj[p], PB_MLAK // 512)),
            pl.BlockSpec((None, MLA_T, 512), lambda b, p, qi, kj: (b, kj[p], PB_MLAV // 512)),
        ],
        out_specs=pl.BlockSpec((None, MLA_T, 256), lambda b, p, qi, kj: (b, qi[p], 0)),
        scratch_shapes=[pltpu.VMEM((4, MLA_T, LANES), F32), pltpu.VMEM((4, MLA_T, LANES), F32)],
    )
    return pl.pallas_call(
        _mla_kernel,
        out_shape=jax.ShapeDtypeStruct((batch, seq, 256), F32),
        grid_spec=grid_spec,
        compiler_params=pltpu.CompilerParams(dimension_semantics=("parallel", "arbitrary"),
                                             vmem_limit_bytes=VMEM_LIMIT),
        name="mla_causal",
    )(qi, kj, pb, pb, pb)


def _outproj_kernel(x_ref, oa_ref, ob_ref, oc_ref, os_ref, ow_ref, od_ref, gate_ref, gexp_ref, gn_ref, w_ref,
                    o_ref):
    gates = gate_ref[...]
    g_hi, g_lo = _split_bf16(gates)

    def gate(branch):
        e = gexp_ref[branch]
        return _dot(g_hi, e) + _dot(g_lo, e)

    o_nsa = gate(0) * oc_ref[...] + gate(1) * os_ref[...] + gate(2) * ow_ref[...]
    gn = gn_ref[...]
    y = x_ref[...]
    for grp, o in enumerate((oa_ref[...], ob_ref[...], o_nsa, od_ref[...])):
        cols = slice(grp * GROUP_WIDTH, (grp + 1) * GROUP_WIDTH)
        normed = _row_rms(o, GROUP_WIDTH) * gn[:, cols]
        y = y + _dot(normed.astype(BF16), w_ref[cols, :])
    o_ref[...] = y


def _outproj(x2, oa, ob, oc, osel, ow, od, pf, gexp, gn, w, tm):
    tokens = x2.shape[0]
    row = lambda t: (t, 0)
    o_spec = pl.BlockSpec((tm, GROUP_WIDTH), row)
    return pl.pallas_call(
        _outproj_kernel,
        out_shape=jax.ShapeDtypeStruct((tokens, D_MODEL), F32),
        grid=(tokens // tm,),
        in_specs=[
            pl.BlockSpec((tm, D_MODEL), row), o_spec, o_spec, o_spec, o_spec, o_spec, o_spec,
            pl.BlockSpec((tm, 128), lambda t: (t, PF_GATE // 128)),
            pl.BlockSpec((3, 128, GROUP_WIDTH), lambda t: (0, 0, 0)),
            pl.BlockSpec((1, D_MODEL), lambda t: (0, 0)),
            pl.BlockSpec((D_MODEL, D_MODEL), lambda t: (0, 0)),
        ],
        out_specs=pl.BlockSpec((tm, D_MODEL), row),
        compiler_params=pltpu.CompilerParams(dimension_semantics=("parallel",), vmem_limit_bytes=VMEM_LIMIT),
        name="out_proj",
    )(x2, oa, ob, oc, osel, ow, od, pf, gexp, gn, w)


MOE_TM = 1024


def _moe_kernel(x_ref, fn_ref, wr_hi_ref, wr_lo_ref, br_ref, wg_ref, wu_ref, wd_ref, o_ref, h_ref, comb_ref, acc_ref):
    e = pl.program_id(1)
    lane = lax.broadcasted_iota(jnp.int32, (MOE_TM, LANES), 1)

    @pl.when(e == 0)
    def _():
        h = _row_rms(x_ref[...], D_MODEL) * fn_ref[...]
        h_ref[...] = h.astype(BF16)
        h_hi, h_lo = _split_bf16(h)
        logits = (_dot(h_hi, wr_hi_ref[...]) + _dot(h_lo, wr_hi_ref[...]) + _dot(h_hi, wr_lo_ref[...])
                  + br_ref[...])
        lane_f = lane.astype(F32)
        no_lane = float(LANES)
        is_group = lane < MOE_GROUPS
        g_max = jnp.max(jnp.where(is_group, logits, -jnp.inf), axis=-1, keepdims=True)
        g_star = jnp.min(jnp.where(is_group & (logits == g_max), lane_f, no_lane), axis=-1, keepdims=True)
        g_den = jnp.sum(jnp.where(is_group, jnp.exp(logits - g_max), 0.0), axis=-1, keepdims=True)
        g_w = 1.0 / g_den
        group_of_lane = ((lane - MOE_GROUPS) >> int(math.log2(MOE_EPG))).astype(F32)
        in_group = (lane >= MOE_GROUPS) & (lane < MOE_GROUPS + MOE_EXPERTS) & (group_of_lane == g_star)
        e_l = jnp.where(in_group, logits, -jnp.inf)
        top1 = jnp.max(e_l, axis=-1, keepdims=True)
        i1 = jnp.min(jnp.where(e_l == top1, lane_f, no_lane), axis=-1, keepdims=True)
        e_l2 = jnp.where(lane_f == i1, -jnp.inf, e_l)
        top2 = jnp.max(e_l2, axis=-1, keepdims=True)
        i2 = jnp.min(jnp.where(e_l2 == top2, lane_f, no_lane), axis=-1, keepdims=True)
        r = jnp.exp(top2 - top1)
        w1 = g_w / (1.0 + r)
        w2 = g_w * r / (1.0 + r)
        comb_ref[...] = jnp.where(lane_f == i1, w1, jnp.where(lane_f == i2, w2, 0.0))
        acc_ref[...] = jnp.zeros_like(acc_ref)

    c_e = jnp.sum(jnp.where(lane == e + MOE_GROUPS, comb_ref[...], 0.0), axis=-1, keepdims=True)
    hb = h_ref[...]
    a = jax.nn.silu(_dot(hb, wg_ref[...])) * _dot(hb, wu_ref[...])
    acc_ref[...] += _dot((a * c_e).astype(BF16), wd_ref[...])

    @pl.when(e == MOE_EXPERTS - 1)
    def _():
        o_ref[...] = x_ref[...] + acc_ref[...]


def _moe(x2, fn, wr_hi, wr_lo, br, wg, wu, wd):
    tokens = x2.shape[0]
    const = lambda t, e: (0, 0)
    return pl.pallas_call(
        _moe_kernel,
        out_shape=jax.ShapeDtypeStruct((tokens, D_MODEL), F32),
        grid=(tokens // MOE_TM, MOE_EXPERTS),
        in_specs=[
            pl.BlockSpec((MOE_TM, D_MODEL), lambda t, e: (t, 0)),
            pl.BlockSpec((1, D_MODEL), const),
            pl.BlockSpec((D_MODEL, LANES), const),
            pl.BlockSpec((D_MODEL, LANES), const),
            pl.BlockSpec((1, LANES), const),
            pl.BlockSpec((None, D_MODEL, MOE_HIDDEN), lambda t, e: (e, 0, 0)),
            pl.BlockSpec((None, D_MODEL, MOE_HIDDEN), lambda t, e: (e, 0, 0)),
            pl.BlockSpec((None, MOE_HIDDEN, D_MODEL), lambda t, e: (e, 0, 0)),
        ],
        out_specs=pl.BlockSpec((MOE_TM, D_MODEL), lambda t, e: (t, 0)),
        scratch_shapes=[pltpu.VMEM((MOE_TM, D_MODEL), BF16), pltpu.VMEM((MOE_TM, LANES), F32),
                        pltpu.VMEM((MOE_TM, D_MODEL), F32)],
        compiler_params=pltpu.CompilerParams(dimension_semantics=("parallel", "arbitrary"),
                                             vmem_limit_bytes=VMEM_LIMIT),
        name="hier_moe",
    )(x2, fn, wr_hi, wr_lo, br, wg, wu, wd)


def _t5_bucket(dist):
    n = jnp.maximum(dist, 0)
    max_exact = T5_BUCKETS // 2
    nf = jnp.maximum(n, 1).astype(F32)
    large = max_exact + (jnp.log(nf / max_exact) / math.log(T5_MAX_DIST / max_exact)
                         * (T5_BUCKETS - max_exact)).astype(jnp.int32)
    large = jnp.minimum(large, T5_BUCKETS - 1)
    return jnp.where(n < max_exact, n, large)


def _sel_delta_cap():
    max_exact = T5_BUCKETS // 2
    span = T5_BUCKETS - max_exact
    last_bucket_from = max_exact * (T5_MAX_DIST / max_exact) ** ((span - 1) / span)
    cap = 1
    while (cap - 1) * SEL_T + 1 < 1.25 * last_bucket_from:
        cap += 1
    return cap


def _position_tables(rel_bias, seq):
    buckets = _t5_bucket(jnp.arange(seq))
    first = jnp.sum(buckets[None, :] < jnp.arange(T5_BUCKETS)[:, None], axis=1)
    tbl = rel_bias.T

    def toeplitz(heads, dist):
        shape = (tbl[heads].shape[0],) + (1,) * dist.ndim
        out = jnp.broadcast_to(tbl[heads][:, 0].reshape(shape), shape[:1] + dist.shape)
        for b in range(1, T5_BUCKETS):
            out = jnp.where((dist >= first[b])[None], tbl[heads][:, b].reshape(shape), out)
        return out

    swa_h, nsa_h = slice(0, 4), slice(4, 8)
    tq_swa = -(-(SWA_WINDOW - 1) // 128) * 128
    tq_win = -(-(NSA_WINDOW - 1) // 128) * 128
    band = lambda tq: jnp.arange(tq)[:, None] + tq - jnp.arange(2 * tq)[None, :]
    bias_swa = toeplitz(swa_h, band(tq_swa))
    bias_win = toeplitz(nsa_h, band(tq_win))
    ncp = seq // NSA_CMP_STRIDE
    cmp_end = jnp.arange(ncp) * NSA_CMP_STRIDE + NSA_CMP_LEN - 1
    bias_cmp = toeplitz(nsa_h, jnp.arange(seq)[:, None] - cmp_end[None, :])
    nd = min(_sel_delta_cap() + 1, seq // SEL_T)
    dist_s = (jnp.arange(nd)[:, None, None] * SEL_T + jnp.arange(SEL_T)[None, :, None]
              - jnp.arange(SEL_T)[None, None, :])
    bias_sel = toeplitz(nsa_h, dist_s)

    n_sel_pad = -(-(seq // NSA_SEL_LEN) // LANES) * LANES
    sel_start = np.arange(n_sel_pad) * NSA_SEL_LEN
    c_start = np.arange(ncp) * NSA_CMP_STRIDE
    c_end = c_start + NSA_CMP_LEN - 1
    real = (np.arange(ncp) < ncp - NSA_CMP_LEN // NSA_CMP_STRIDE + 1)[:, None] & (sel_start < seq)[None, :]
    overlap = ((c_start[:, None] < sel_start[None, :] + NSA_SEL_LEN) & (c_end[:, None] >= sel_start[None, :]) & real)
    own_block = (np.arange(seq) // NSA_SEL_LEN)[:, None] == np.arange(n_sel_pad)[None, :]
    penalty = np.where(own_block, -2.0 ** 100, 0.0)

    pos = jnp.arange(seq, dtype=F32)
    inv_freq = ROPE_THETA ** (-jnp.arange(0, MLA_ROPE, 2, dtype=F32) / MLA_ROPE)
    ang = pos[:, None] * inv_freq[None, :]
    cos, sin = jnp.cos(ang), jnp.sin(ang)
    ones = jnp.ones((seq, MLA_NOPE), F32)
    tail = LANES - MLA_NOPE - MLA_ROPE
    cos_t = jnp.concatenate([ones, cos, cos, jnp.ones((seq, tail), F32)], axis=1)
    sin_t = jnp.concatenate([0 * ones, -sin, sin, jnp.zeros((seq, tail), F32)], axis=1)
    return dict(bias_swa=bias_swa, bias_win=bias_win, bias_cmp=bias_cmp, bias_sel=bias_sel,
                overlap=jnp.asarray(overlap, BF16), penalty=jnp.asarray(penalty, BF16), cos_t=cos_t, sin_t=sin_t)


def _pad_to(a, shape):
    return jnp.pad(a, [(0, s - d) for d, s in zip(a.shape, shape)])


def _pack_layer(w_in, swa_q_norm, swa_k_norm, nsa_q_norm, nsa_k_norm, mla_q_lat_norm, mla_w_q_up,
                mla_kv_lat_norm, mla_w_kv_up, mla_q_norm, mla_k_norm):
    kpe = w_in[:, 2636:2668]
    kpe_seg = jnp.concatenate([jnp.zeros((D_MODEL, MLA_NOPE), F32), kpe,
                               jnp.zeros((D_MODEL, LANES - MLA_QK), F32)], axis=1)
    spread = lambda cols: _pad_to(cols.reshape(D_MODEL, -1, HEAD_DIM), (D_MODEL, cols.shape[1] // HEAD_DIM, LANES)
                                  ).reshape(D_MODEL, -1)
    w = jnp.concatenate([
        w_in[:, :1920],
        spread(w_in[:, 1920:2048]),
        w_in[:, 2048:2304],
        _pad_to(w_in[:, 2304:2316], (D_MODEL, 128)),
        _pad_to(w_in[:, 2316:2508], (D_MODEL, 256)),
        w_in[:, 2508:2636],
        jnp.tile(kpe_seg, (1, MLA_HEADS)),
    ], axis=1).astype(BF16)
    tile4 = lambda g: jnp.tile(g, 4)
    g64 = _pad_to(jnp.stack([tile4(swa_q_norm), tile4(swa_k_norm), tile4(nsa_q_norm),
                             tile4(nsa_k_norm[1]), tile4(nsa_k_norm[2])]), (8, 256))
    glat = _pad_to(jnp.stack([_pad_to(mla_q_lat_norm, (256,)), _pad_to(mla_kv_lat_norm, (256,))]), (8, 256))
    wq = _pad_to(mla_w_q_up.reshape(MLA_Q_RANK, MLA_HEADS, MLA_QK), (256, MLA_HEADS, LANES))
    wq = wq.reshape(256, MLA_HEADS * LANES).astype(BF16)
    wkv = mla_w_kv_up.reshape(MLA_KV_RANK, MLA_HEADS, MLA_NOPE + MLA_V)
    wk = _pad_to(wkv[:, :, :MLA_NOPE], (MLA_KV_RANK, MLA_HEADS, LANES)).reshape(MLA_KV_RANK, MLA_HEADS * LANES)
    wv = _pad_to(wkv[:, :, MLA_NOPE:], (MLA_KV_RANK, MLA_HEADS, LANES)).reshape(MLA_KV_RANK, MLA_HEADS * LANES)
    wkv_p = jnp.concatenate([wk, wv], axis=1).astype(BF16)
    gmla = _pad_to(jnp.stack([jnp.tile(_pad_to(mla_q_norm, (LANES,)), MLA_HEADS),
                              jnp.tile(_pad_to(mla_k_norm, (LANES,)), MLA_HEADS)]), (8, 512))
    return w, g64, glat, wq, wkv_p, gmla


def _gate_expand():
    rows = np.arange(128)[None, :, None]
    cols = np.arange(GROUP_WIDTH)[None, None, :]
    branch = np.arange(3)[:, None, None]
    return jnp.asarray(rows == branch * 4 + cols // HEAD_DIM, BF16)


def _compress_rows(pf3, batch, seq):
    nb = seq // NSA_CMP_STRIDE
    kv = pf3[:, :, :256].reshape(batch, seq, 2, 2, HEAD_DIM).transpose(2, 0, 3, 1, 4)
    blocks = kv.reshape(2, batch * 2, nb, NSA_CMP_STRIDE * HEAD_DIM)
    nxt = jnp.concatenate([blocks[:, :, 1:], jnp.zeros_like(blocks[:, :, :1])], axis=2)
    return jnp.concatenate([blocks, nxt], axis=3)


def kernel(x, rel_bias, attn_norm, w_in, swa_q_norm, swa_k_norm, swa_sinks, nsa_q_norm, nsa_k_norm, nsa_cmp_pos, nsa_cmp_w1, nsa_cmp_w2, mla_q_lat_norm, mla_w_q_up, mla_kv_lat_norm, mla_w_kv_up, mla_q_norm, mla_k_norm, out_norm, w_out, ffn_norm, moe_w_group, moe_b_group, moe_w_expert, moe_b_expert, moe_w_gate, moe_w_up, moe_w_down):
    batch, seq, _ = x.shape
    depth = w_in.shape[0]
    tokens = batch * seq
    tm = 512
    assert seq % 2048 == 0 and tokens % MOE_TM == 0
    tabs = _position_tables(rel_bias, seq)
    gexp = _gate_expand()
    zero_sinks = jnp.zeros((4,), F32)
    x2 = x.reshape(tokens, D_MODEL)
    for l in range(depth):
        w, g64, glat, wq, wkv, gmla = _pack_layer(
            w_in[l], swa_q_norm[l], swa_k_norm[l], nsa_q_norm[l], nsa_k_norm[l], mla_q_lat_norm[l],
            mla_w_q_up[l], mla_kv_lat_norm[l], mla_w_kv_up[l], mla_q_norm[l], mla_k_norm[l])
        pb, pf = _prep(x2, attn_norm[l][None, :], w, g64, glat, wq, wkv, gmla, tabs["cos_t"], tabs["sin_t"],
                       seq, tm)
        pb3 = pb.reshape(batch, seq, PB_WIDTH)
        pf3 = pf.reshape(batch, seq, PF_WIDTH)
        o_a = _sb_attention(pb3, batch, seq)
        o_b = _banded_attention(pb3, swa_sinks[l], tabs["bias_swa"], batch, seq, SWA_WINDOW,
                                PB_SWAQ, PB_SWAK, PB_SWAV, True)
        rows = _compress_rows(pf3, batch, seq)
        kvc = _compress(rows, nsa_cmp_pos[l].reshape(2, 1, -1),
                        nsa_cmp_w1[l].reshape(2, -1, NSA_CMP_HIDDEN).astype(BF16), nsa_cmp_w2[l].astype(BF16),
                        nsa_k_norm[l][0][None, :], 128)
        o_c, q_aug = _cmp_attention(pb3, kvc, tabs["bias_cmp"], tabs["overlap"], batch, seq)
        o_s = _sel_attention(pb3, q_aug, tabs["penalty"], tabs["bias_sel"], batch, seq)
        o_w = _banded_attention(pb3, zero_sinks, tabs["bias_win"], batch, seq, NSA_WINDOW,
                                PB_NSAQ, PB_NSAKW, PB_NSAVW, False)
        o_d = _mla_attention(pb3, batch, seq)
        flat = lambda o: o.reshape(tokens, GROUP_WIDTH)
        x2 = _outproj(x2, flat(o_a), flat(o_b), flat(o_c), flat(o_s), flat(o_w), flat(o_d), pf, gexp,
                      out_norm[l][None, :], w_out[l].astype(BF16), tm)
        w_router = _pad_to(jnp.concatenate([moe_w_group[l], moe_w_expert[l]], axis=1), (D_MODEL, LANES))
        wr_hi = w_router.astype(BF16)
        wr_lo = (w_router - wr_hi.astype(F32)).astype(BF16)
        b_router = _pad_to(jnp.concatenate([moe_b_group[l], moe_b_expert[l]])[None, :], (1, LANES))
        x2 = _moe(x2, ffn_norm[l][None, :], wr_hi, wr_lo, b_router, moe_w_gate[l].astype(BF16),
                  moe_w_up[l].astype(BF16), moe_w_down[l].astype(BF16))
    return x2.reshape(batch, seq, D_MODEL)
```

```python
import functools
import math

import numpy as np
import jax
import jax.numpy as jnp
from jax import lax
from jax.experimental import pallas as pl
from jax.experimental.pallas import tpu as pltpu

F32 = jnp.float32
BF16 = jnp.bfloat16

D_MODEL = 1024
HEAD_DIM = 64
NEG = -1e30
EPS = 1e-6
FORCE_BONUS = 1000.0
SWA_WINDOW = 128
NSA_CMP_LEN = 32
NSA_CMP_STRIDE = 16
NSA_CMP_HIDDEN = 128
NSA_SEL_LEN = 64
NSA_TOPK = 16
NSA_WINDOW = 512
MLA_HEADS = 4
MLA_NOPE = 64
MLA_ROPE = 32
MLA_V = 64
MLA_Q_RANK = 192
MLA_KV_RANK = 128
MLA_QK = MLA_NOPE + MLA_ROPE
ROPE_THETA = 10000.0
T5_BUCKETS = 32
T5_MAX_DIST = 1024
MOE_GROUPS = 4
MOE_EPG = 8
MOE_EXPERTS = MOE_GROUPS * MOE_EPG
MOE_HIDDEN = 256
GROUP_WIDTH = 256
LANES = 128
VMEM_LIMIT = 48 * 1024 * 1024

PB_MLAQ, PB_MLAK, PB_MLAV = 0, 512, 1024
PB_SBQ, PB_SBK, PB_SBV = 1536, 1792, 2048
PB_SWAQ, PB_NSAQ, PB_NSAVS = 2304, 2560, 2816
PB_SWAK, PB_SWAV, PB_NSAKW, PB_NSAVW = 3072, 3200, 3328, 3456
PB_NSAKS = 3584
PB_WIDTH = 4096
PF_KC, PF_VC, PF_GATE = 0, 128, 256
PF_WIDTH = 384
W_SBQ, W_SBK, W_SBV, W_SWAQ, W_SWAK, W_SWAV, W_NSAQ = 0, 256, 512, 768, 1024, 1152, 1280
W_KC, W_VC, W_KS, W_VS, W_KW, W_VW, W_GATE = 1536, 1664, 1792, 1920, 2176, 2304, 2432
W_CQ, W_CKV, W_KPE = 2560, 2816, 2944
W_WIDTH = 3456

NT_DIMS = (((1,), (1,)), ((), ()))


def _dot(a, b):
    return jnp.dot(a, b, preferred_element_type=F32)


def _dot_nt(a, b):
    return lax.dot_general(a, b, NT_DIMS, preferred_element_type=F32)


def _split_bf16(x):
    hi = x.astype(BF16)
    lo = (x - hi.astype(F32)).astype(BF16)
    return hi, lo


def _dot_exact_rhs(x, m):
    hi, lo = _split_bf16(x)
    return _dot(hi, m) + _dot(lo, m)


def _block_diag_ones(width, seg):
    idx = np.arange(width) // seg
    return jnp.asarray(idx[:, None] == idx[None, :], BF16)


def _seg_rms(x, seg_ones, count):
    width = x.shape[1]
    ms = _dot_exact_rhs(x * x, seg_ones[:width, :width]) * (1.0 / count)
    return x * lax.rsqrt(ms + EPS)


def _row_rms(x, count):
    return x * lax.rsqrt(jnp.sum(x * x, axis=-1, keepdims=True) * (1.0 / count) + EPS)


def _prep_kernel(x_ref, an_ref, w_ref, g64_ref, glat_ref, wq_ref, wkv_ref, gmla_ref, cos_ref, sin_ref,
                 s64_ref, s128_ref, pb_ref, pf_ref):
    x = x_ref[...]
    s64 = s64_ref[...]
    s128 = s128_ref[...]
    h = _row_rms(x, D_MODEL) * an_ref[...]
    hb = h.astype(BF16)

    def proj(lo, hi):
        return _dot(hb, w_ref[:, lo:hi])

    def put(col, value):
        pb_ref[:, col:col + value.shape[1]] = value.astype(BF16)

    def ones_tail(width):
        lane = lax.broadcasted_iota(jnp.int32, (1, width), 1)
        return jnp.where((lane & (LANES - 1)) >= HEAD_DIM, 1.0, 0.0)

    scale = HEAD_DIM ** -0.5
    g64 = g64_ref[...]
    put(PB_SBQ, proj(W_SBQ, W_SBQ + 256) * scale)
    put(PB_SBK, proj(W_SBK, W_SBK + 256))
    put(PB_SBV, proj(W_SBV, W_SBV + 256))
    put(PB_SWAQ, _seg_rms(proj(W_SWAQ, W_SWAQ + 256), s64, 64) * g64[0:1, :] * scale)
    put(PB_SWAK, _seg_rms(proj(W_SWAK, W_SWAK + 128), s64, 64) * g64[1:2, :128])
    put(PB_SWAV, proj(W_SWAV, W_SWAV + 128))
    put(PB_NSAQ, _seg_rms(proj(W_NSAQ, W_NSAQ + 256), s64, 64) * g64[2:3, :] * scale)
    pf_ref[:, PF_KC:PF_KC + 128] = proj(W_KC, W_KC + 128)
    pf_ref[:, PF_VC:PF_VC + 128] = proj(W_VC, W_VC + 128)
    ks = _seg_rms(proj(W_KS, W_KS + 128), s64, 64) * g64[3:4, :128]
    ks_swapped = pltpu.roll(ks, HEAD_DIM, 1)
    low = lax.broadcasted_iota(jnp.int32, ks.shape, 1) < HEAD_DIM
    put(PB_NSAKS, jnp.where(low, ks, 0.0))
    put(PB_NSAKS + LANES, jnp.where(low, 0.0, ks_swapped))
    put(PB_NSAKS + 2 * LANES, jnp.where(low, ks_swapped, 0.0))
    put(PB_NSAKS + 3 * LANES, jnp.where(low, 0.0, ks))
    put(PB_NSAVS, proj(W_VS, W_VS + 256) + ones_tail(256))
    put(PB_NSAKW, _seg_rms(proj(W_KW, W_KW + 128), s64, 64) * g64[4:5, :128])
    put(PB_NSAVW, proj(W_VW, W_VW + 128))
    pf_ref[:, PF_GATE:PF_GATE + 128] = jax.nn.sigmoid(proj(W_GATE, W_GATE + 128))

    glat = glat_ref[...]
    cq = _row_rms(proj(W_CQ, W_CQ + 256), MLA_Q_RANK) * glat[0:1, :]
    q = _dot(cq.astype(BF16), wq_ref[...])
    ckv = _row_rms(proj(W_CKV, W_CKV + 128), MLA_KV_RANK) * glat[1:2, :128]
    kv = _dot(ckv.astype(BF16), wkv_ref[...])
    k = kv[:, :512] + proj(W_KPE, W_KPE + 512)
    gm = gmla_ref[...]
    q = _seg_rms(q, s128, MLA_QK) * gm[0:1, :]
    k = _seg_rms(k, s128, MLA_QK) * gm[1:2, :]
    cos = cos_ref[...]
    sin = sin_ref[...]
    lane = lax.broadcasted_iota(jnp.int32, (x.shape[0], LANES), 1)
    first_half = lane < MLA_NOPE + MLA_ROPE // 2

    def rope(t):
        partner = jnp.where(first_half, pltpu.roll(t, LANES - MLA_ROPE // 2, 1), pltpu.roll(t, MLA_ROPE // 2, 1))
        return t * cos + partner * sin

    qscale = MLA_QK ** -0.5
    for hd in range(MLA_HEADS):
        sl = slice(hd * LANES, (hd + 1) * LANES)
        put(PB_MLAQ + hd * LANES, rope(q[:, sl]) * qscale)
        put(PB_MLAK + hd * LANES, rope(k[:, sl]))
    put(PB_MLAV, kv[:, 512:1024] + ones_tail(512))


def _prep(x2, an, w, g64, glat, wq, wkv, gmla, cos_t, sin_t, seq, tm):
    tokens = x2.shape[0]
    n_pos = seq // tm
    const = lambda t: (0, 0)
    return pl.pallas_call(
        _prep_kernel,
        out_shape=(jax.ShapeDtypeStruct((tokens, PB_WIDTH), BF16), jax.ShapeDtypeStruct((tokens, PF_WIDTH), F32)),
        grid=(tokens // tm,),
        in_specs=[
            pl.BlockSpec((tm, D_MODEL), lambda t: (t, 0)),
            pl.BlockSpec((1, D_MODEL), const),
            pl.BlockSpec((D_MODEL, W_WIDTH), const),
            pl.BlockSpec((8, 256), const),
            pl.BlockSpec((8, 256), const),
            pl.BlockSpec((256, 512), const),
            pl.BlockSpec((128, 1024), const),
            pl.BlockSpec((8, 512), const),
            pl.BlockSpec((tm, LANES), lambda t: (t % n_pos, 0)),
            pl.BlockSpec((tm, LANES), lambda t: (t % n_pos, 0)),
            pl.BlockSpec((256, 256), const),
            pl.BlockSpec((512, 512), const),
        ],
        out_specs=(pl.BlockSpec((tm, PB_WIDTH), lambda t: (t, 0)), pl.BlockSpec((tm, PF_WIDTH), lambda t: (t, 0))),
        compiler_params=pltpu.CompilerParams(dimension_semantics=("parallel",), vmem_limit_bytes=VMEM_LIMIT),
        name="prep",
    )(x2, an, w, g64, glat, wq, wkv, gmla, cos_t, sin_t, _block_diag_ones(256, HEAD_DIM),
      _block_diag_ones(512, LANES))


SB_TQ, SB_KB, SB_SUB = 256, 1024, 128


SB_UNDERFLOW = 110.0


def _sb_kernel(qi_ref, kj_ref, kmax_ref, q_ref, k_ref, v_ref, sums_ref, o_ref, carry_ref, acc_ref, zb_ref,
               dead_ref):
    b = pl.program_id(0)
    p = pl.program_id(1)
    qi = qi_ref[p]
    kj = kj_ref[p]
    q_start = qi * SB_TQ
    first = kj == (q_start + SB_TQ - 1) // SB_KB

    @pl.when(first)
    def _():
        carry_ref[...] = jnp.zeros_like(carry_ref)
        acc_ref[...] = jnp.zeros_like(acc_ref)
        dead_ref[0] = 0
        ones = jnp.ones((HEAD_DIM, SB_SUB), BF16)
        for hd in range(4):
            q = q_ref[:, hd * HEAD_DIM:(hd + 1) * HEAD_DIM].astype(F32)
            zb_ref[hd * SB_TQ:(hd + 1) * SB_TQ, :] = (jnp.sqrt(_dot_exact_rhs(q * q, ones))
                                                      * (kmax_ref[b * 4 + hd] * 1.01))

    rel = (lax.broadcasted_iota(jnp.int32, (SB_TQ, SB_SUB), 1)
           - lax.broadcasted_iota(jnp.int32, (SB_TQ, SB_SUB), 0))

    def sub_tile(u, k_start):
        mask = jnp.tile(rel < q_start - k_start, (4, 1))
        rows = slice(u * SB_SUB, (u + 1) * SB_SUB)
        head = lambda hd: slice(hd * HEAD_DIM, (hd + 1) * HEAD_DIM)
        z = jnp.concatenate([_dot_nt(q_ref[:, head(hd)], k_ref[rows, head(hd)]) for hd in range(4)], axis=0)
        log_keep = jnp.where(mask, -(jnp.maximum(z, 0.0) + jnp.log(1.0 + jnp.exp(-jnp.abs(z)))), 0.0)
        hi, lo = _split_bf16(log_keep)
        sums = _dot(jnp.concatenate([hi, lo], axis=1), sums_ref[...])
        carry = carry_ref[...]
        a = jnp.where(mask, jnp.exp(z + sums[:, :SB_SUB] + carry), 0.0).astype(BF16)
        for hd in range(4):
            acc_ref[hd] += _dot(a[hd * SB_TQ:(hd + 1) * SB_TQ], v_ref[rows, head(hd)])
        carry_ref[...] = carry + sums[:, SB_SUB:]

    for u in reversed(range(SB_KB // SB_SUB)):
        k_start = kj * SB_KB + u * SB_SUB

        @pl.when((k_start < q_start + SB_TQ) & (dead_ref[0] == 0))
        def _(u=u, k_start=k_start):
            live = jnp.max(carry_ref[...] + zb_ref[...]) > -SB_UNDERFLOW

            @pl.when(live)
            def _():
                sub_tile(u, k_start)

            @pl.when(jnp.logical_not(live))
            def _():
                dead_ref[0] = 1

    @pl.when(kj == 0)
    def _():
        for hd in range(4):
            o_ref[:, hd * HEAD_DIM:(hd + 1) * HEAD_DIM] = acc_ref[hd]


def _sb_attention(pb, batch, seq):
    nq = seq // SB_TQ
    qi, kj = [], []
    for i in range(nq):
        for j in reversed(range((i * SB_TQ + SB_TQ - 1) // SB_KB + 1)):
            qi.append(i)
            kj.append(j)
    qi = jnp.asarray(np.array(qi, np.int32))
    kj = jnp.asarray(np.array(kj, np.int32))
    keys = pb[:, :, PB_SBK:PB_SBK + 256].astype(F32).reshape(batch, seq, 4, HEAD_DIM)
    kmax = jnp.sqrt(jnp.max(jnp.sum(keys * keys, axis=-1), axis=1)).reshape(batch * 4)
    j = np.arange(2 * SB_SUB)[:, None] % SB_SUB
    s = np.arange(2 * SB_SUB)[None, :]
    sums = jnp.asarray((s >= SB_SUB) | (j >= s), BF16)
    grid_spec = pltpu.PrefetchScalarGridSpec(
        num_scalar_prefetch=2,
        grid=(batch, int(qi.shape[0])),
        in_specs=[
            pl.BlockSpec(memory_space=pltpu.SMEM),
            pl.BlockSpec((None, SB_TQ, 256), lambda b, p, qi, kj: (b, qi[p], PB_SBQ // 256)),
            pl.BlockSpec((None, SB_KB, 256), lambda b, p, qi, kj: (b, kj[p], PB_SBK // 256)),
            pl.BlockSpec((None, SB_KB, 256), lambda b, p, qi, kj: (b, kj[p], PB_SBV // 256)),
            pl.BlockSpec((2 * SB_SUB, 2 * SB_SUB), lambda b, p, qi, kj: (0, 0)),
        ],
        out_specs=pl.BlockSpec((None, SB_TQ, 256), lambda b, p, qi, kj: (b, qi[p], 0)),
        scratch_shapes=[pltpu.VMEM((4 * SB_TQ, SB_SUB), F32), pltpu.VMEM((4, SB_TQ, HEAD_DIM), F32),
                        pltpu.VMEM((4 * SB_TQ, SB_SUB), F32), pltpu.SMEM((1,), jnp.int32)],
    )
    return pl.pallas_call(
        _sb_kernel,
        out_shape=jax.ShapeDtypeStruct((batch, seq, 256), F32),
        grid_spec=grid_spec,
        compiler_params=pltpu.CompilerParams(dimension_semantics=("parallel", "arbitrary"),
                                             vmem_limit_bytes=VMEM_LIMIT),
        name="stick_breaking",
    )(qi, kj, kmax, pb, pb, pb, sums)


def _banded_kernel(sink_ref, q_ref, kp_ref, kc_ref, vp_ref, vc_ref, bias_ref, o_ref, *, tq, pad, window, use_sink):
    i = pl.program_id(1)
    dist_prev = (lax.broadcasted_iota(jnp.int32, (tq, pad), 0) + pad
                 - lax.broadcasted_iota(jnp.int32, (tq, pad), 1))
    dist_cur = lax.broadcasted_iota(jnp.int32, (tq, tq), 0) - lax.broadcasted_iota(jnp.int32, (tq, tq), 1)
    mask_prev = (dist_prev < window) & (i > 0)
    mask_cur = (dist_cur >= 0) & (dist_cur < window)
    for hd in range(4):
        cols = slice(hd * HEAD_DIM, (hd + 1) * HEAD_DIM)
        kcols = slice((hd // 2) * HEAD_DIM, (hd // 2 + 1) * HEAD_DIM)
        q = q_ref[:, cols]
        s_prev = jnp.where(mask_prev, _dot_nt(q, kp_ref[:, kcols]) + bias_ref[hd, :, :pad], NEG)
        s_cur = jnp.where(mask_cur, _dot_nt(q, kc_ref[:, kcols]) + bias_ref[hd, :, pad:], NEG)
        m = jnp.maximum(jnp.max(s_prev, axis=-1, keepdims=True), jnp.max(s_cur, axis=-1, keepdims=True))
        if use_sink:
            sink = sink_ref[hd]
            m = jnp.maximum(m, sink)
        p_prev = jnp.where(mask_prev, jnp.exp(s_prev - m), 0.0)
        p_cur = jnp.where(mask_cur, jnp.exp(s_cur - m), 0.0)
        denom = jnp.sum(p_prev, axis=-1, keepdims=True) + jnp.sum(p_cur, axis=-1, keepdims=True)
        if use_sink:
            denom = denom + jnp.exp(sink - m)
        o = _dot(p_prev.astype(BF16), vp_ref[:, kcols]) + _dot(p_cur.astype(BF16), vc_ref[:, kcols])
        o_ref[:, cols] = o / jnp.maximum(denom, 1e-30)


BAND_TQ = 256


def _band_tiles(window):
    pad = -(-(window - 1) // LANES) * LANES
    return pad, max(pad, BAND_TQ)


def _banded_attention(pb, sinks, bias, batch, seq, window, q_col, k_col, v_col, use_sink):
    pad, tq = _band_tiles(window)
    per = tq // pad
    prev = lambda i: jnp.maximum(i * per - 1, 0)
    grid_spec = pltpu.PrefetchScalarGridSpec(
        num_scalar_prefetch=1,
        grid=(batch, seq // tq),
        in_specs=[
            pl.BlockSpec((None, tq, 256), lambda b, i, s: (b, i, q_col // 256)),
            pl.BlockSpec((None, pad, 128), lambda b, i, s: (b, prev(i), k_col // 128)),
            pl.BlockSpec((None, tq, 128), lambda b, i, s: (b, i, k_col // 128)),
            pl.BlockSpec((None, pad, 128), lambda b, i, s: (b, prev(i), v_col // 128)),
            pl.BlockSpec((None, tq, 128), lambda b, i, s: (b, i, v_col // 128)),
            pl.BlockSpec((4, tq, pad + tq), lambda b, i, s: (0, 0, 0)),
        ],
        out_specs=pl.BlockSpec((None, tq, 256), lambda b, i, s: (b, i, 0)),
    )
    return pl.pallas_call(
        functools.partial(_banded_kernel, tq=tq, pad=pad, window=window, use_sink=use_sink),
        out_shape=jax.ShapeDtypeStruct((batch, seq, 256), F32),
        grid_spec=grid_spec,
        compiler_params=pltpu.CompilerParams(dimension_semantics=("parallel", "arbitrary"),
                                             vmem_limit_bytes=VMEM_LIMIT),
        name="banded_w%d" % window,
    )(sinks, pb, pb, pb, pb, pb, bias)


def _compress_kernel(rows_ref, pos_ref, w1_ref, w2_ref, g_ref, o_ref):
    win = rows_ref[...] + pos_ref[...]
    hid = jax.nn.gelu(_dot(win.astype(BF16), w1_ref[...]), approximate=True)
    out = _dot(hid.astype(BF16), w2_ref[...])
    normed = _row_rms(out, HEAD_DIM) * g_ref[...]
    o_ref[...] = jnp.where(pl.program_id(0) == 0, normed, out).astype(BF16)


def _compress(rows, pos, w1, w2, gain, tn):
    _, bh, ncp, width = rows.shape
    return pl.pallas_call(
        _compress_kernel,
        out_shape=jax.ShapeDtypeStruct((2, bh, ncp, HEAD_DIM), BF16),
        grid=(2, bh, ncp // tn),
        in_specs=[
            pl.BlockSpec((None, None, tn, width), lambda c, r, n: (c, r, n, 0)),
            pl.BlockSpec((None, 1, width), lambda c, r, n: (c, 0, 0)),
            pl.BlockSpec((None, width, NSA_CMP_HIDDEN), lambda c, r, n: (c, 0, 0)),
            pl.BlockSpec((None, NSA_CMP_HIDDEN, HEAD_DIM), lambda c, r, n: (c, 0, 0)),
            pl.BlockSpec((1, HEAD_DIM), lambda c, r, n: (0, 0)),
        ],
        out_specs=pl.BlockSpec((None, None, tn, HEAD_DIM), lambda c, r, n: (c, r, n, 0)),
        compiler_params=pltpu.CompilerParams(dimension_semantics=("parallel", "parallel", "parallel"),
                                             vmem_limit_bytes=VMEM_LIMIT),
        name="nsa_compress",
    )(rows, pos, w1, w2, gain)


CMP_TQ = 1024


def _cmp_kernel(q_ref, kc_ref, vc_ref, bias_ref, ov_ref, o_ref, qa_ref, *, ncp, n_sel_pad, topk):
    i = pl.program_id(2)
    q_pos = i * CMP_TQ + lax.broadcasted_iota(jnp.int32, (CMP_TQ, ncp), 0)
    cmp_end = lax.broadcasted_iota(jnp.int32, (CMP_TQ, ncp), 1) * NSA_CMP_STRIDE + (NSA_CMP_LEN - 1)
    mask = cmp_end <= q_pos
    kc = kc_ref[...]
    vc = vc_ref[...]
    p_sum = jnp.zeros((CMP_TQ, ncp), F32)
    for g in range(2):
        cols = slice(g * HEAD_DIM, (g + 1) * HEAD_DIM)
        s = jnp.where(mask, _dot_nt(q_ref[:, cols], kc) + bias_ref[g], NEG)
        m = jnp.max(s, axis=-1, keepdims=True)
        p = jnp.where(mask, jnp.exp(s - m), 0.0)
        p = p / jnp.maximum(jnp.sum(p, axis=-1, keepdims=True), 1e-30)
        o_ref[:, cols] = _dot(p.astype(BF16), vc)
        p_sum = p_sum + p
    imp = _dot_exact_rhs(p_sum, ov_ref[...])

    row_pos = i * CMP_TQ + lax.broadcasted_iota(jnp.int32, (CMP_TQ, n_sel_pad), 0)
    blk = lax.broadcasted_iota(jnp.int32, (CMP_TQ, n_sel_pad), 1)
    cur = row_pos >> int(math.log2(NSA_SEL_LEN))
    forced = (blk == 0) | (blk == cur) | (blk == cur - 1)
    valid = blk * NSA_SEL_LEN <= row_pos
    score = jnp.where(valid, imp + jnp.where(forced, FORCE_BONUS, 0.0), NEG)
    blk_f = blk.astype(F32)
    dropped = jnp.ones((CMP_TQ, n_sel_pad), F32)
    for _ in range(topk):
        best = jnp.max(score, axis=-1, keepdims=True)
        first = jnp.min(jnp.where(score == best, blk_f, float(n_sel_pad)), axis=-1, keepdims=True)
        hit = blk_f == first
        dropped = jnp.where(hit, 0.0, dropped)
        score = jnp.where(hit, -jnp.inf, score)
    dropped = jnp.where(valid, dropped, 1.0).astype(BF16)
    q = q_ref[...]
    low = lax.broadcasted_iota(jnp.int32, q.shape, 1) < HEAD_DIM
    zero = jnp.zeros_like(q)
    width = LANES + n_sel_pad
    qa_ref[:, 0:LANES] = jnp.where(low, q, zero)
    qa_ref[:, LANES:width] = dropped
    qa_ref[:, width:width + LANES] = jnp.where(low, zero, q)
    qa_ref[:, width + LANES:2 * width] = dropped


def _cmp_attention(pb, kvc, bias_c, overlap, batch, seq):
    ncp = kvc.shape[2]
    n_sel_pad = overlap.shape[1]
    topk = min(NSA_TOPK, seq // NSA_SEL_LEN)
    pair = 2 * (LANES + n_sel_pad)
    return pl.pallas_call(
        functools.partial(_cmp_kernel, ncp=ncp, n_sel_pad=n_sel_pad, topk=topk),
        out_shape=(jax.ShapeDtypeStruct((batch, seq, 256), F32),
                   jax.ShapeDtypeStruct((batch, seq, 2 * pair), BF16)),
        grid=(batch, 2, seq // CMP_TQ),
        in_specs=[
            pl.BlockSpec((None, CMP_TQ, 128), lambda b, h, i: (b, i, PB_NSAQ // 128 + h)),
            pl.BlockSpec((None, None, ncp, HEAD_DIM), lambda b, h, i: (0, b * 2 + h, 0, 0)),
            pl.BlockSpec((None, None, ncp, HEAD_DIM), lambda b, h, i: (1, b * 2 + h, 0, 0)),
            pl.BlockSpec((2, CMP_TQ, ncp), lambda b, h, i: (h, i, 0)),
            pl.BlockSpec((ncp, n_sel_pad), lambda b, h, i: (0, 0)),
        ],
        out_specs=(pl.BlockSpec((None, CMP_TQ, 128), lambda b, h, i: (b, i, h)),
                   pl.BlockSpec((None, CMP_TQ, pair), lambda b, h, i: (b, i, h))),
        compiler_params=pltpu.CompilerParams(dimension_semantics=("parallel", "parallel", "arbitrary"),
                                             vmem_limit_bytes=VMEM_LIMIT),
        name="nsa_cmp_select",
    )(pb, kvc, kvc, bias_c, overlap)


SEL_T = 512


def _flash_init(m_ref, acc_ref):
    m_ref[...] = jnp.full_like(m_ref, NEG)
    acc_ref[...] = jnp.zeros_like(acc_ref)


def _flash_update(s, v_ones, m_ref, acc_ref, hd):
    m_old = m_ref[hd]
    m_new = jnp.maximum(m_old, jnp.max(s, axis=-1, keepdims=True))
    alpha = jnp.exp(m_old - m_new)
    pr = jnp.exp(s - jnp.tile(m_new, (1, s.shape[1] // LANES)))
    acc_ref[hd] = alpha * acc_ref[hd] + _dot(pr.astype(BF16), v_ones)
    m_ref[hd] = m_new


def _flash_finish(acc_ref, o_ref, heads):
    for hd in range(heads):
        acc = acc_ref[hd]
        row_sum = pltpu.roll(acc, HEAD_DIM, 1)
        o_ref[:, hd * HEAD_DIM:(hd + 1) * HEAD_DIM] = (acc / jnp.maximum(row_sum, 1e-30))[:, :HEAD_DIM]


def _sel_kernel(qi_ref, kj_ref, q_ref, k_ref, v_ref, pen_ref, bias_ref, o_ref, m_ref, acc_ref):
    p = pl.program_id(1)
    qi = qi_ref[p]
    kj = kj_ref[p]
    width = q_ref.shape[1] // 4

    @pl.when(kj == 0)
    def _():
        _flash_init(m_ref, acc_ref)

    def step(diagonal):
        pen = pen_ref[...]
        for hd in range(4):
            k_pen = jnp.concatenate([k_ref[:, hd * LANES:(hd + 1) * LANES], pen], axis=1)
            s = _dot_nt(q_ref[:, hd * width:(hd + 1) * width], k_pen) + bias_ref[hd]
            if diagonal:
                row = lax.broadcasted_iota(jnp.int32, (SEL_T, SEL_T), 0)
                col = lax.broadcasted_iota(jnp.int32, (SEL_T, SEL_T), 1)
                s = jnp.where(col <= row, s, NEG)
            _flash_update(s, v_ref[:, (hd // 2) * LANES:(hd // 2 + 1) * LANES], m_ref, acc_ref, hd)

    @pl.when(kj < qi)
    def _():
        step(False)

    @pl.when(kj == qi)
    def _():
        step(True)
        _flash_finish(acc_ref, o_ref, 4)


def _causal_pairs(n):
    qi, kj = [], []
    for i in range(n):
        for j in range(i + 1):
            qi.append(i)
            kj.append(j)
    return jnp.asarray(np.array(qi, np.int32)), jnp.asarray(np.array(kj, np.int32))


def _sel_attention(pb, q_aug, penalty, bias_s, batch, seq):
    n_sel_pad = penalty.shape[1]
    n_delta = bias_s.shape[1]
    qi, kj = _causal_pairs(seq // SEL_T)
    grid_spec = pltpu.PrefetchScalarGridSpec(
        num_scalar_prefetch=2,
        grid=(batch, int(qi.shape[0])),
        in_specs=[
            pl.BlockSpec((None, SEL_T, q_aug.shape[2]), lambda b, p, qi, kj: (b, qi[p], 0)),
            pl.BlockSpec((None, SEL_T, 512), lambda b, p, qi, kj: (b, kj[p], PB_NSAKS // 512)),
            pl.BlockSpec((None, SEL_T, 256), lambda b, p, qi, kj: (b, kj[p], PB_NSAVS // 256)),
            pl.BlockSpec((SEL_T, n_sel_pad), lambda b, p, qi, kj: (kj[p], 0)),
            pl.BlockSpec((4, None, SEL_T, SEL_T),
                         lambda b, p, qi, kj: (0, jnp.minimum(qi[p] - kj[p], n_delta - 1), 0, 0)),
        ],
        out_specs=pl.BlockSpec((None, SEL_T, 256), lambda b, p, qi, kj: (b, qi[p], 0)),
        scratch_shapes=[pltpu.VMEM((4, SEL_T, LANES), F32), pltpu.VMEM((4, SEL_T, LANES), F32)],
    )
    return pl.pallas_call(
        _sel_kernel,
        out_shape=jax.ShapeDtypeStruct((batch, seq, 256), F32),
        grid_spec=grid_spec,
        compiler_params=pltpu.CompilerParams(dimension_semantics=("parallel", "arbitrary"),
                                             vmem_limit_bytes=VMEM_LIMIT),
        name="nsa_selected",
    )(qi, kj, q_aug, pb, pb, penalty, bias_s)


MLA_T = 512


def _mla_kernel(qi_ref, kj_ref, q_ref, k_ref, v_ref, o_ref, m_ref, acc_ref):
    p = pl.program_id(1)
    qi = qi_ref[p]
    kj = kj_ref[p]

    @pl.when(kj == 0)
    def _():
        _flash_init(m_ref, acc_ref)

    def step(diagonal):
        for hd in range(MLA_HEADS):
            cols = slice(hd * LANES, (hd + 1) * LANES)
            s = _dot_nt(q_ref[:, cols], k_ref[:, cols])
            if diagonal:
                row = lax.broadcasted_iota(jnp.int32, (MLA_T, MLA_T), 0)
                col = lax.broadcasted_iota(jnp.int32, (MLA_T, MLA_T), 1)
                s = jnp.where(col <= row, s, NEG)
            _flash_update(s, v_ref[:, cols], m_ref, acc_ref, hd)

    @pl.when(kj < qi)
    def _():
        step(False)

    @pl.when(kj == qi)
    def _():
        step(True)
        _flash_finish(acc_ref, o_ref, MLA_HEADS)


def _mla_attention(pb, batch, seq):
    qi, kj = _causal_pairs(seq // MLA_T)
    grid_spec = pltpu.PrefetchScalarGridSpec(
        num_scalar_prefetch=2,
        grid=(batch, int(qi.shape[0])),
        in_specs=[
            pl.BlockSpec((None, MLA_T, 512), lambda b, p, qi, kj: (b, qi[p], PB_MLAQ // 512)),
            pl.BlockSpec((None, MLA_T, 512), lambda b, p, qi, kj: (b, kj[p], PB_MLAK // 512)),
            pl.BlockSpec((None, MLA_T, 512), lambda b, p, qi, kj: (b, kj[p], PB_MLAV // 512)),
        ],
        out_specs=pl.BlockSpec((None, MLA_T, 256), lambda b, p, qi, kj: (b, qi[p], 0)),
        scratch_shapes=[pltpu.VMEM((4, MLA_T, LANES), F32), pltpu.VMEM((4, MLA_T, LANES), F32)],
    )
    return pl.pallas_call(
        _mla_kernel,
        out_shape=jax.ShapeDtypeStruct((batch, seq, 256), F32),
        grid_spec=grid_spec,
        compiler_params=pltpu.CompilerParams(dimension_semantics=("parallel", "arbitrary"),
                                             vmem_limit_bytes=VMEM_LIMIT),
        name="mla_causal",
    )(qi, kj, pb, pb, pb)


def _outproj_kernel(x_ref, oa_ref, ob_ref, oc_ref, os_ref, ow_ref, od_ref, gate_ref, gexp_ref, gn_ref, w_ref,
                    o_ref):
    gates = gate_ref[...]
    g_hi, g_lo = _split_bf16(gates)

    def gate(branch):
        e = gexp_ref[branch]
        return _dot(g_hi, e) + _dot(g_lo, e)

    o_nsa = gate(0) * oc_ref[...] + gate(1) * os_ref[...] + gate(2) * ow_ref[...]
    gn = gn_ref[...]
    y = x_ref[...]
    for grp, o in enumerate((oa_ref[...], ob_ref[...], o_nsa, od_ref[...])):
        cols = slice(grp * GROUP_WIDTH, (grp + 1) * GROUP_WIDTH)
        normed = _row_rms(o, GROUP_WIDTH) * gn[:, cols]
        y = y + _dot(normed.astype(BF16), w_ref[cols, :])
    o_ref[...] = y


def _outproj(x2, oa, ob, oc, osel, ow, od, pf, gexp, gn, w, tm):
    tokens = x2.shape[0]
    row = lambda t: (t, 0)
    o_spec = pl.BlockSpec((tm, GROUP_WIDTH), row)
    return pl.pallas_call(
        _outproj_kernel,
        out_shape=jax.ShapeDtypeStruct((tokens, D_MODEL), F32),
        grid=(tokens // tm,),
        in_specs=[
            pl.BlockSpec((tm, D_MODEL), row), o_spec, o_spec, o_spec, o_spec, o_spec, o_spec,
            pl.BlockSpec((tm, 128), lambda t: (t, PF_GATE // 128)),
            pl.BlockSpec((3, 128, GROUP_WIDTH), lambda t: (0, 0, 0)),
            pl.BlockSpec((1, D_MODEL), lambda t: (0, 0)),
            pl.BlockSpec((D_MODEL, D_MODEL), lambda t: (0, 0)),
        ],
        out_specs=pl.BlockSpec((tm, D_MODEL), row),
        compiler_params=pltpu.CompilerParams(dimension_semantics=("parallel",), vmem_limit_bytes=VMEM_LIMIT),
        name="out_proj",
    )(x2, oa, ob, oc, osel, ow, od, pf, gexp, gn, w)


MOE_TM = 1024
MOE_QUAD = 4
MOE_VMEM_LIMIT = 56 * 1024 * 1024


def _moe_kernel(x_ref, fn_ref, wr_hi_ref, wr_lo_ref, br_ref, cexp_ref, wg_ref, wu_ref, wd_ref, o_ref, h_ref,
                comb_ref):
    lane = lax.broadcasted_iota(jnp.int32, (MOE_TM, LANES), 1)

    @pl.when(pl.program_id(1) == 0)
    def _():
        h = _row_rms(x_ref[...], D_MODEL) * fn_ref[...]
        h_ref[...] = h.astype(BF16)
        h_hi, h_lo = _split_bf16(h)
        logits = (_dot(h_hi, wr_hi_ref[...]) + _dot(h_lo, wr_hi_ref[...]) + _dot(h_hi, wr_lo_ref[...])
                  + br_ref[...])
        lane_f = lane.astype(F32)
        no_lane = float(LANES)
        is_group = lane < MOE_GROUPS
        g_max = jnp.max(jnp.where(is_group, logits, -jnp.inf), axis=-1, keepdims=True)
        g_star = jnp.min(jnp.where(is_group & (logits == g_max), lane_f, no_lane), axis=-1, keepdims=True)
        g_den = jnp.sum(jnp.where(is_group, jnp.exp(logits - g_max), 0.0), axis=-1, keepdims=True)
        g_w = 1.0 / g_den
        group_of_lane = ((lane - MOE_GROUPS) >> int(math.log2(MOE_EPG))).astype(F32)
        in_group = (lane >= MOE_GROUPS) & (lane < MOE_GROUPS + MOE_EXPERTS) & (group_of_lane == g_star)
        e_l = jnp.where(in_group, logits, -jnp.inf)
        top1 = jnp.max(e_l, axis=-1, keepdims=True)
        i1 = jnp.min(jnp.where(e_l == top1, lane_f, no_lane), axis=-1, keepdims=True)
        e_l2 = jnp.where(lane_f == i1, -jnp.inf, e_l)
        top2 = jnp.max(e_l2, axis=-1, keepdims=True)
        i2 = jnp.min(jnp.where(e_l2 == top2, lane_f, no_lane), axis=-1, keepdims=True)
        r = jnp.exp(top2 - top1)
        w1 = g_w / (1.0 + r)
        w2 = g_w * r / (1.0 + r)
        comb = jnp.where(lane_f == i1, w1, jnp.where(lane_f == i2, w2, 0.0))
        c_hi, c_lo = _split_bf16(comb)
        comb_ref[:, :LANES] = c_hi
        comb_ref[:, LANES:] = c_lo
        o_ref[...] = x_ref[...]

    weight = _dot(comb_ref[...], cexp_ref[...])
    hb = h_ref[...]
    half = MOE_QUAD * MOE_HIDDEN // 2
    y = None
    for s in range(2):
        cols = slice(s * half, (s + 1) * half)
        a = jax.nn.silu(_dot(hb, wg_ref[:, cols])) * _dot(hb, wu_ref[:, cols]) * weight[:, cols]
        part = _dot(a.astype(BF16), wd_ref[cols, :])
        y = part if y is None else y + part
    o_ref[...] += y


def _moe(x2, fn, wr_hi, wr_lo, br, wg, wu, wd):
    tokens = x2.shape[0]
    quads = MOE_EXPERTS // MOE_QUAD
    width = MOE_QUAD * MOE_HIDDEN
    const = lambda t, e: (0, 0)
    lane_of_col = MOE_GROUPS + np.arange(quads)[:, None, None] * MOE_QUAD + np.arange(width)[None, None, :] // MOE_HIDDEN
    cexp = jnp.asarray((np.arange(2 * LANES)[None, :, None] % LANES) == lane_of_col, BF16)
    return pl.pallas_call(
        _moe_kernel,
        out_shape=jax.ShapeDtypeStruct((tokens, D_MODEL), F32),
        grid=(tokens // MOE_TM, quads),
        in_specs=[
            pl.BlockSpec((MOE_TM, D_MODEL), lambda t, e: (t, 0)),
            pl.BlockSpec((1, D_MODEL), const),
            pl.BlockSpec((D_MODEL, LANES), const),
            pl.BlockSpec((D_MODEL, LANES), const),
            pl.BlockSpec((1, LANES), const),
            pl.BlockSpec((None, 2 * LANES, width), lambda t, e: (e, 0, 0)),
            pl.BlockSpec((None, D_MODEL, width), lambda t, e: (e, 0, 0)),
            pl.BlockSpec((None, D_MODEL, width), lambda t, e: (e, 0, 0)),
            pl.BlockSpec((None, width, D_MODEL), lambda t, e: (e, 0, 0)),
        ],
        out_specs=pl.BlockSpec((MOE_TM, D_MODEL), lambda t, e: (t, 0)),
        scratch_shapes=[pltpu.VMEM((MOE_TM, D_MODEL), BF16), pltpu.VMEM((MOE_TM, 2 * LANES), BF16)],
        compiler_params=pltpu.CompilerParams(dimension_semantics=("parallel", "arbitrary"),
                                             vmem_limit_bytes=MOE_VMEM_LIMIT),
        name="hier_moe",
    )(x2, fn, wr_hi, wr_lo, br, cexp, wg, wu, wd)


def _t5_bucket(dist):
    n = jnp.maximum(dist, 0)
    max_exact = T5_BUCKETS // 2
    nf = jnp.maximum(n, 1).astype(F32)
    large = max_exact + (jnp.log(nf / max_exact) / math.log(T5_MAX_DIST / max_exact)
                         * (T5_BUCKETS - max_exact)).astype(jnp.int32)
    large = jnp.minimum(large, T5_BUCKETS - 1)
    return jnp.where(n < max_exact, n, large)


def _sel_delta_cap():
    max_exact = T5_BUCKETS // 2
    span = T5_BUCKETS - max_exact
    last_bucket_from = max_exact * (T5_MAX_DIST / max_exact) ** ((span - 1) / span)
    cap = 1
    while (cap - 1) * SEL_T + 1 < 1.25 * last_bucket_from:
        cap += 1
    return cap


def _position_tables(rel_bias, seq):
    buckets = _t5_bucket(jnp.arange(seq))
    first = jnp.sum(buckets[None, :] < jnp.arange(T5_BUCKETS)[:, None], axis=1)
    tbl = rel_bias.T

    def toeplitz(heads, dist):
        shape = (tbl[heads].shape[0],) + (1,) * dist.ndim
        out = jnp.broadcast_to(tbl[heads][:, 0].reshape(shape), shape[:1] + dist.shape)
        for b in range(1, T5_BUCKETS):
            out = jnp.where((dist >= first[b])[None], tbl[heads][:, b].reshape(shape), out)
        return out

    swa_h, nsa_h = slice(0, 4), slice(4, 8)
    def band(window):
        pad, tq = _band_tiles(window)
        return jnp.arange(tq)[:, None] + pad - jnp.arange(pad + tq)[None, :]

    bias_swa = toeplitz(swa_h, band(SWA_WINDOW))
    bias_win = toeplitz(nsa_h, band(NSA_WINDOW))
    ncp = seq // NSA_CMP_STRIDE
    cmp_end = jnp.arange(ncp) * NSA_CMP_STRIDE + NSA_CMP_LEN - 1
    bias_cmp = toeplitz(nsa_h, jnp.arange(seq)[:, None] - cmp_end[None, :])
    nd = min(_sel_delta_cap() + 1, seq // SEL_T)
    dist_s = (jnp.arange(nd)[:, None, None] * SEL_T + jnp.arange(SEL_T)[None, :, None]
              - jnp.arange(SEL_T)[None, None, :])
    bias_sel = toeplitz(nsa_h, dist_s)

    n_sel_pad = -(-(seq // NSA_SEL_LEN) // LANES) * LANES
    sel_start = np.arange(n_sel_pad) * NSA_SEL_LEN
    c_start = np.arange(ncp) * NSA_CMP_STRIDE
    c_end = c_start + NSA_CMP_LEN - 1
    real = (np.arange(ncp) < ncp - NSA_CMP_LEN // NSA_CMP_STRIDE + 1)[:, None] & (sel_start < seq)[None, :]
    overlap = ((c_start[:, None] < sel_start[None, :] + NSA_SEL_LEN) & (c_end[:, None] >= sel_start[None, :]) & real)
    own_block = (np.arange(seq) // NSA_SEL_LEN)[:, None] == np.arange(n_sel_pad)[None, :]
    penalty = np.where(own_block, -2.0 ** 100, 0.0)

    pos = jnp.arange(seq, dtype=F32)
    inv_freq = ROPE_THETA ** (-jnp.arange(0, MLA_ROPE, 2, dtype=F32) / MLA_ROPE)
    ang = pos[:, None] * inv_freq[None, :]
    cos, sin = jnp.cos(ang), jnp.sin(ang)
    ones = jnp.ones((seq, MLA_NOPE), F32)
    tail = LANES - MLA_NOPE - MLA_ROPE
    cos_t = jnp.concatenate([ones, cos, cos, jnp.ones((seq, tail), F32)], axis=1)
    sin_t = jnp.concatenate([0 * ones, -sin, sin, jnp.zeros((seq, tail), F32)], axis=1)
    return dict(bias_swa=bias_swa, bias_win=bias_win, bias_cmp=bias_cmp, bias_sel=bias_sel,
                overlap=jnp.asarray(overlap, BF16), penalty=jnp.asarray(penalty, BF16), cos_t=cos_t, sin_t=sin_t)


def _pad_to(a, shape):
    return jnp.pad(a, [(0, s - d) for d, s in zip(a.shape, shape)])


def _pack_layer(w_in, swa_q_norm, swa_k_norm, nsa_q_norm, nsa_k_norm, mla_q_lat_norm, mla_w_q_up,
                mla_kv_lat_norm, mla_w_kv_up, mla_q_norm, mla_k_norm):
    kpe = w_in[:, 2636:2668]
    kpe_seg = jnp.concatenate([jnp.zeros((D_MODEL, MLA_NOPE), F32), kpe,
                               jnp.zeros((D_MODEL, LANES - MLA_QK), F32)], axis=1)
    spread = lambda cols: _pad_to(cols.reshape(D_MODEL, -1, HEAD_DIM), (D_MODEL, cols.shape[1] // HEAD_DIM, LANES)
                                  ).reshape(D_MODEL, -1)
    w = jnp.concatenate([
        w_in[:, :1920],
        spread(w_in[:, 1920:2048]),
        w_in[:, 2048:2304],
        _pad_to(w_in[:, 2304:2316], (D_MODEL, 128)),
        _pad_to(w_in[:, 2316:2508], (D_MODEL, 256)),
        w_in[:, 2508:2636],
        jnp.tile(kpe_seg, (1, MLA_HEADS)),
    ], axis=1).astype(BF16)
    tile4 = lambda g: jnp.tile(g, 4)
    g64 = _pad_to(jnp.stack([tile4(swa_q_norm), tile4(swa_k_norm), tile4(nsa_q_norm),
                             tile4(nsa_k_norm[1]), tile4(nsa_k_norm[2])]), (8, 256))
    glat = _pad_to(jnp.stack([_pad_to(mla_q_lat_norm, (256,)), _pad_to(mla_kv_lat_norm, (256,))]), (8, 256))
    wq = _pad_to(mla_w_q_up.reshape(MLA_Q_RANK, MLA_HEADS, MLA_QK), (256, MLA_HEADS, LANES))
    wq = wq.reshape(256, MLA_HEADS * LANES).astype(BF16)
    wkv = mla_w_kv_up.reshape(MLA_KV_RANK, MLA_HEADS, MLA_NOPE + MLA_V)
    wk = _pad_to(wkv[:, :, :MLA_NOPE], (MLA_KV_RANK, MLA_HEADS, LANES)).reshape(MLA_KV_RANK, MLA_HEADS * LANES)
    wv = _pad_to(wkv[:, :, MLA_NOPE:], (MLA_KV_RANK, MLA_HEADS, LANES)).reshape(MLA_KV_RANK, MLA_HEADS * LANES)
    wkv_p = jnp.concatenate([wk, wv], axis=1).astype(BF16)
    gmla = _pad_to(jnp.stack([jnp.tile(_pad_to(mla_q_norm, (LANES,)), MLA_HEADS),
                              jnp.tile(_pad_to(mla_k_norm, (LANES,)), MLA_HEADS)]), (8, 512))
    return w, g64, glat, wq, wkv_p, gmla


def _gate_expand():
    rows = np.arange(128)[None, :, None]
    cols = np.arange(GROUP_WIDTH)[None, None, :]
    branch = np.arange(3)[:, None, None]
    return jnp.asarray(rows == branch * 4 + cols // HEAD_DIM, BF16)


def _compress_rows(pf3, batch, seq):
    nb = seq // NSA_CMP_STRIDE
    kv = pf3[:, :, :256].reshape(batch, seq, 2, 2, HEAD_DIM).transpose(2, 0, 3, 1, 4)
    blocks = kv.reshape(2, batch * 2, nb, NSA_CMP_STRIDE * HEAD_DIM)
    nxt = jnp.concatenate([blocks[:, :, 1:], jnp.zeros_like(blocks[:, :, :1])], axis=2)
    return jnp.concatenate([blocks, nxt], axis=3)


def kernel(x, rel_bias, attn_norm, w_in, swa_q_norm, swa_k_norm, swa_sinks, nsa_q_norm, nsa_k_norm, nsa_cmp_pos, nsa_cmp_w1, nsa_cmp_w2, mla_q_lat_norm, mla_w_q_up, mla_kv_lat_norm, mla_w_kv_up, mla_q_norm, mla_k_norm, out_norm, w_out, ffn_norm, moe_w_group, moe_b_group, moe_w_expert, moe_b_expert, moe_w_gate, moe_w_up, moe_w_down):
    batch, seq, _ = x.shape
    depth = w_in.shape[0]
    tokens = batch * seq
    tm = 512
    assert seq % 2048 == 0 and tokens % MOE_TM == 0
    tabs = _position_tables(rel_bias, seq)
    gexp = _gate_expand()
    zero_sinks = jnp.zeros((4,), F32)
    x2 = x.reshape(tokens, D_MODEL)
    for l in range(depth):
        w, g64, glat, wq, wkv, gmla = _pack_layer(
            w_in[l], swa_q_norm[l], swa_k_norm[l], nsa_q_norm[l], nsa_k_norm[l], mla_q_lat_norm[l],
            mla_w_q_up[l], mla_kv_lat_norm[l], mla_w_kv_up[l], mla_q_norm[l], mla_k_norm[l])
        pb, pf = _prep(x2, attn_norm[l][None, :], w, g64, glat, wq, wkv, gmla, tabs["cos_t"], tabs["sin_t"],
                       seq, tm)
        pb3 = pb.reshape(batch, seq, PB_WIDTH)
        pf3 = pf.reshape(batch, seq, PF_WIDTH)
        o_a = _sb_attention(pb3, batch, seq)
        o_b = _banded_attention(pb3, swa_sinks[l], tabs["bias_swa"], batch, seq, SWA_WINDOW,
                                PB_SWAQ, PB_SWAK, PB_SWAV, True)
        rows = _compress_rows(pf3, batch, seq)
        kvc = _compress(rows, nsa_cmp_pos[l].reshape(2, 1, -1),
                        nsa_cmp_w1[l].reshape(2, -1, NSA_CMP_HIDDEN).astype(BF16), nsa_cmp_w2[l].astype(BF16),
                        nsa_k_norm[l][0][None, :], 128)
        o_c, q_aug = _cmp_attention(pb3, kvc, tabs["bias_cmp"], tabs["overlap"], batch, seq)
        o_s = _sel_attention(pb3, q_aug, tabs["penalty"], tabs["bias_sel"], batch, seq)
        o_w = _banded_attention(pb3, zero_sinks, tabs["bias_win"], batch, seq, NSA_WINDOW,
                                PB_NSAQ, PB_NSAKW, PB_NSAVW, False)
        o_d = _mla_attention(pb3, batch, seq)
        flat = lambda o: o.reshape(tokens, GROUP_WIDTH)
        x2 = _outproj(x2, flat(o_a), flat(o_b), flat(o_c), flat(o_s), flat(o_w), flat(o_d), pf, gexp,
                      out_norm[l][None, :], w_out[l].astype(BF16), tm)
        w_router = _pad_to(jnp.concatenate([moe_w_group[l], moe_w_expert[l]], axis=1), (D_MODEL, LANES))
        wr_hi = w_router.astype(BF16)
        wr_lo = (w_router - wr_hi.astype(F32)).astype(BF16)
        b_router = _pad_to(jnp.concatenate([moe_b_group[l], moe_b_expert[l]])[None, :], (1, LANES))
        quads = MOE_EXPERTS // MOE_QUAD
        by_quad = lambda w: w.astype(BF16).reshape(quads, MOE_QUAD, D_MODEL, MOE_HIDDEN).transpose(0, 2, 1, 3
                                                   ).reshape(quads, D_MODEL, MOE_QUAD * MOE_HIDDEN)
        x2 = _moe(x2, ffn_norm[l][None, :], wr_hi, wr_lo, b_router, by_quad(moe_w_gate[l]), by_quad(moe_w_up[l]),
                  moe_w_down[l].astype(BF16).reshape(quads, MOE_QUAD * MOE_HIDDEN, D_MODEL))
    return x2.reshape(batch, seq, D_MODEL)
```

```python
import functools
import math

import numpy as np
import jax
import jax.numpy as jnp
from jax import lax
from jax.experimental import pallas as pl
from jax.experimental.pallas import tpu as pltpu

F32 = jnp.float32
BF16 = jnp.bfloat16

D_MODEL = 1024
HEAD_DIM = 64
NEG = -1e30
EPS = 1e-6
FORCE_BONUS = 1000.0
SWA_WINDOW = 128
NSA_CMP_LEN = 32
NSA_CMP_STRIDE = 16
NSA_CMP_HIDDEN = 128
NSA_SEL_LEN = 64
NSA_TOPK = 16
NSA_WINDOW = 512
MLA_HEADS = 4
MLA_NOPE = 64
MLA_ROPE = 32
MLA_V = 64
MLA_Q_RANK = 192
MLA_KV_RANK = 128
MLA_QK = MLA_NOPE + MLA_ROPE
ROPE_THETA = 10000.0
T5_BUCKETS = 32
T5_MAX_DIST = 1024
MOE_GROUPS = 4
MOE_EPG = 8
MOE_EXPERTS = MOE_GROUPS * MOE_EPG
MOE_HIDDEN = 256
GROUP_WIDTH = 256
LANES = 128
VMEM_LIMIT = 48 * 1024 * 1024

PB_MLAQ, PB_MLAK, PB_MLAV = 0, 512, 1024
PB_SBQ, PB_SBK, PB_SBV = 1536, 1792, 2048
PB_SWAQ, PB_NSAQ, PB_NSAVS = 2304, 2560, 2816
PB_SWAK, PB_SWAV, PB_NSAKW, PB_NSAVW = 3072, 3200, 3328, 3456
PB_NSAKS = 3584
PB_WIDTH = 4096
PF_KC, PF_VC, PF_GATE = 0, 128, 256
PF_WIDTH = 384
W_SBQ, W_SBK, W_SBV, W_SWAQ, W_SWAK, W_SWAV, W_NSAQ = 0, 256, 512, 768, 1024, 1152, 1280
W_KC, W_VC, W_KS, W_VS, W_KW, W_VW, W_GATE = 1536, 1664, 1792, 1920, 2176, 2304, 2432
W_CQ, W_CKV, W_KPE = 2560, 2816, 2944
W_WIDTH = 3456

NT_DIMS = (((1,), (1,)), ((), ()))


def _dot(a, b):
    return jnp.dot(a, b, preferred_element_type=F32)


def _dot_nt(a, b):
    return lax.dot_general(a, b, NT_DIMS, preferred_element_type=F32)


def _split_bf16(x):
    hi = x.astype(BF16)
    lo = (x - hi.astype(F32)).astype(BF16)
    return hi, lo


def _dot_exact_rhs(x, m):
    hi, lo = _split_bf16(x)
    return _dot(hi, m) + _dot(lo, m)


def _block_diag_ones(width, seg):
    idx = np.arange(width) // seg
    return jnp.asarray(idx[:, None] == idx[None, :], BF16)


def _seg_rms(x, seg_ones, count):
    width = x.shape[1]
    ms = _dot_exact_rhs(x * x, seg_ones[:width, :width]) * (1.0 / count)
    return x * lax.rsqrt(ms + EPS)


def _row_rms(x, count):
    return x * lax.rsqrt(jnp.sum(x * x, axis=-1, keepdims=True) * (1.0 / count) + EPS)


def _prep_kernel(x_ref, an_ref, w_ref, g64_ref, glat_ref, wq_ref, wkv_ref, gmla_ref, cos_ref, sin_ref,
                 s64_ref, s128_ref, pb_ref, pf_ref):
    x = x_ref[...]
    s64 = s64_ref[...]
    s128 = s128_ref[...]
    h = _row_rms(x, D_MODEL) * an_ref[...]
    hb = h.astype(BF16)

    def proj(lo, hi):
        return _dot(hb, w_ref[:, lo:hi])

    def put(col, value):
        pb_ref[:, col:col + value.shape[1]] = value.astype(BF16)

    def ones_tail(width):
        lane = lax.broadcasted_iota(jnp.int32, (1, width), 1)
        return jnp.where((lane & (LANES - 1)) >= HEAD_DIM, 1.0, 0.0)

    scale = HEAD_DIM ** -0.5
    g64 = g64_ref[...]
    put(PB_SBQ, proj(W_SBQ, W_SBQ + 256) * scale)
    put(PB_SBK, proj(W_SBK, W_SBK + 256))
    put(PB_SBV, proj(W_SBV, W_SBV + 256))
    put(PB_SWAQ, _seg_rms(proj(W_SWAQ, W_SWAQ + 256), s64, 64) * g64[0:1, :] * scale)
    put(PB_SWAK, _seg_rms(proj(W_SWAK, W_SWAK + 128), s64, 64) * g64[1:2, :128])
    put(PB_SWAV, proj(W_SWAV, W_SWAV + 128))
    put(PB_NSAQ, _seg_rms(proj(W_NSAQ, W_NSAQ + 256), s64, 64) * g64[2:3, :] * scale)
    pf_ref[:, PF_KC:PF_KC + 128] = proj(W_KC, W_KC + 128)
    pf_ref[:, PF_VC:PF_VC + 128] = proj(W_VC, W_VC + 128)
    ks = _seg_rms(proj(W_KS, W_KS + 128), s64, 64) * g64[3:4, :128]
    ks_swapped = pltpu.roll(ks, HEAD_DIM, 1)
    low = lax.broadcasted_iota(jnp.int32, ks.shape, 1) < HEAD_DIM
    put(PB_NSAKS, jnp.where(low, ks, 0.0))
    put(PB_NSAKS + LANES, jnp.where(low, 0.0, ks_swapped))
    put(PB_NSAKS + 2 * LANES, jnp.where(low, ks_swapped, 0.0))
    put(PB_NSAKS + 3 * LANES, jnp.where(low, 0.0, ks))
    put(PB_NSAVS, proj(W_VS, W_VS + 256) + ones_tail(256))
    put(PB_NSAKW, _seg_rms(proj(W_KW, W_KW + 128), s64, 64) * g64[4:5, :128])
    put(PB_NSAVW, proj(W_VW, W_VW + 128))
    pf_ref[:, PF_GATE:PF_GATE + 128] = jax.nn.sigmoid(proj(W_GATE, W_GATE + 128))

    glat = glat_ref[...]
    cq = _row_rms(proj(W_CQ, W_CQ + 256), MLA_Q_RANK) * glat[0:1, :]
    q = _dot(cq.astype(BF16), wq_ref[...])
    ckv = _row_rms(proj(W_CKV, W_CKV + 128), MLA_KV_RANK) * glat[1:2, :128]
    kv = _dot(ckv.astype(BF16), wkv_ref[...])
    k = kv[:, :512] + proj(W_KPE, W_KPE + 512)
    gm = gmla_ref[...]
    q = _seg_rms(q, s128, MLA_QK) * gm[0:1, :]
    k = _seg_rms(k, s128, MLA_QK) * gm[1:2, :]
    cos = cos_ref[...]
    sin = sin_ref[...]
    lane = lax.broadcasted_iota(jnp.int32, (x.shape[0], LANES), 1)
    first_half = lane < MLA_NOPE + MLA_ROPE // 2

    def rope(t):
        partner = jnp.where(first_half, pltpu.roll(t, LANES - MLA_ROPE // 2, 1), pltpu.roll(t, MLA_ROPE // 2, 1))
        return t * cos + partner * sin

    qscale = MLA_QK ** -0.5
    for hd in range(MLA_HEADS):
        sl = slice(hd * LANES, (hd + 1) * LANES)
        put(PB_MLAQ + hd * LANES, rope(q[:, sl]) * qscale)
        put(PB_MLAK + hd * LANES, rope(k[:, sl]))
    put(PB_MLAV, kv[:, 512:1024] + ones_tail(512))


def _prep(x2, an, w, g64, glat, wq, wkv, gmla, cos_t, sin_t, seq, tm):
    tokens = x2.shape[0]
    n_pos = seq // tm
    const = lambda t: (0, 0)
    return pl.pallas_call(
        _prep_kernel,
        out_shape=(jax.ShapeDtypeStruct((tokens, PB_WIDTH), BF16), jax.ShapeDtypeStruct((tokens, PF_WIDTH), F32)),
        grid=(tokens // tm,),
        in_specs=[
            pl.BlockSpec((tm, D_MODEL), lambda t: (t, 0)),
            pl.BlockSpec((1, D_MODEL), const),
            pl.BlockSpec((D_MODEL, W_WIDTH), const),
            pl.BlockSpec((8, 256), const),
            pl.BlockSpec((8, 256), const),
            pl.BlockSpec((256, 512), const),
            pl.BlockSpec((128, 1024), const),
            pl.BlockSpec((8, 512), const),
            pl.BlockSpec((tm, LANES), lambda t: (t % n_pos, 0)),
            pl.BlockSpec((tm, LANES), lambda t: (t % n_pos, 0)),
            pl.BlockSpec((256, 256), const),
            pl.BlockSpec((512, 512), const),
        ],
        out_specs=(pl.BlockSpec((tm, PB_WIDTH), lambda t: (t, 0)), pl.BlockSpec((tm, PF_WIDTH), lambda t: (t, 0))),
        compiler_params=pltpu.CompilerParams(dimension_semantics=("parallel",), vmem_limit_bytes=VMEM_LIMIT),
        name="prep",
    )(x2, an, w, g64, glat, wq, wkv, gmla, cos_t, sin_t, _block_diag_ones(256, HEAD_DIM),
      _block_diag_ones(512, LANES))


SB_TQ, SB_KB, SB_SUB = 256, 1024, 128


SB_UNDERFLOW = 110.0


def _sb_kernel(qi_ref, kj_ref, kmax_ref, q_ref, k_ref, v_ref, sums_ref, o_ref, carry_ref, acc_ref, zb_ref,
               dead_ref):
    b = pl.program_id(0)
    p = pl.program_id(1)
    qi = qi_ref[p]
    kj = kj_ref[p]
    q_start = qi * SB_TQ
    first = kj == (q_start + SB_TQ - 1) // SB_KB

    @pl.when(first)
    def _():
        carry_ref[...] = jnp.zeros_like(carry_ref)
        acc_ref[...] = jnp.zeros_like(acc_ref)
        dead_ref[0] = 0
        ones = jnp.ones((HEAD_DIM, SB_SUB), BF16)
        for hd in range(4):
            q = q_ref[:, hd * HEAD_DIM:(hd + 1) * HEAD_DIM].astype(F32)
            zb_ref[hd * SB_TQ:(hd + 1) * SB_TQ, :] = (jnp.sqrt(_dot_exact_rhs(q * q, ones))
                                                      * (kmax_ref[b * 4 + hd] * 1.01))

    rel = (lax.broadcasted_iota(jnp.int32, (SB_TQ, SB_SUB), 1)
           - lax.broadcasted_iota(jnp.int32, (SB_TQ, SB_SUB), 0))

    def sub_tile(u, k_start):
        mask = jnp.tile(rel < q_start - k_start, (4, 1))
        rows = slice(u * SB_SUB, (u + 1) * SB_SUB)
        head = lambda hd: slice(hd * HEAD_DIM, (hd + 1) * HEAD_DIM)
        z = jnp.concatenate([_dot_nt(q_ref[:, head(hd)], k_ref[rows, head(hd)]) for hd in range(4)], axis=0)
        log_keep = jnp.where(mask, -(jnp.maximum(z, 0.0) + jnp.log(1.0 + jnp.exp(-jnp.abs(z)))), 0.0)
        hi, lo = _split_bf16(log_keep)
        sums = _dot(jnp.concatenate([hi, lo], axis=1), sums_ref[...])
        carry = carry_ref[...]
        a = jnp.where(mask, jnp.exp(z + sums[:, :SB_SUB] + carry), 0.0).astype(BF16)
        for hd in range(4):
            acc_ref[hd] += _dot(a[hd * SB_TQ:(hd + 1) * SB_TQ], v_ref[rows, head(hd)])
        carry_ref[...] = carry + sums[:, SB_SUB:]

    for u in reversed(range(SB_KB // SB_SUB)):
        k_start = kj * SB_KB + u * SB_SUB

        @pl.when((k_start < q_start + SB_TQ) & (dead_ref[0] == 0))
        def _(u=u, k_start=k_start):
            live = jnp.max(carry_ref[...] + zb_ref[...]) > -SB_UNDERFLOW

            @pl.when(live)
            def _():
                sub_tile(u, k_start)

            @pl.when(jnp.logical_not(live))
            def _():
                dead_ref[0] = 1

    @pl.when(kj == 0)
    def _():
        for hd in range(4):
            o_ref[:, hd * HEAD_DIM:(hd + 1) * HEAD_DIM] = acc_ref[hd]


def _sb_attention(pb, batch, seq):
    nq = seq // SB_TQ
    qi, kj = [], []
    for i in range(nq):
        for j in reversed(range((i * SB_TQ + SB_TQ - 1) // SB_KB + 1)):
            qi.append(i)
            kj.append(j)
    qi = jnp.asarray(np.array(qi, np.int32))
    kj = jnp.asarray(np.array(kj, np.int32))
    keys = pb[:, :, PB_SBK:PB_SBK + 256].astype(F32).reshape(batch, seq, 4, HEAD_DIM)
    kmax = jnp.sqrt(jnp.max(jnp.sum(keys * keys, axis=-1), axis=1)).reshape(batch * 4)
    j = np.arange(2 * SB_SUB)[:, None] % SB_SUB
    s = np.arange(2 * SB_SUB)[None, :]
    sums = jnp.asarray((s >= SB_SUB) | (j >= s), BF16)
    grid_spec = pltpu.PrefetchScalarGridSpec(
        num_scalar_prefetch=2,
        grid=(batch, int(qi.shape[0])),
        in_specs=[
            pl.BlockSpec(memory_space=pltpu.SMEM),
            pl.BlockSpec((None, SB_TQ, 256), lambda b, p, qi, kj: (b, qi[p], PB_SBQ // 256)),
            pl.BlockSpec((None, SB_KB, 256), lambda b, p, qi, kj: (b, kj[p], PB_SBK // 256)),
            pl.BlockSpec((None, SB_KB, 256), lambda b, p, qi, kj: (b, kj[p], PB_SBV // 256)),
            pl.BlockSpec((2 * SB_SUB, 2 * SB_SUB), lambda b, p, qi, kj: (0, 0)),
        ],
        out_specs=pl.BlockSpec((None, SB_TQ, 256), lambda b, p, qi, kj: (b, qi[p], 0)),
        scratch_shapes=[pltpu.VMEM((4 * SB_TQ, SB_SUB), F32), pltpu.VMEM((4, SB_TQ, HEAD_DIM), F32),
                        pltpu.VMEM((4 * SB_TQ, SB_SUB), F32), pltpu.SMEM((1,), jnp.int32)],
    )
    return pl.pallas_call(
        _sb_kernel,
        out_shape=jax.ShapeDtypeStruct((batch, seq, 256), F32),
        grid_spec=grid_spec,
        compiler_params=pltpu.CompilerParams(dimension_semantics=("parallel", "arbitrary"),
                                             vmem_limit_bytes=VMEM_LIMIT),
        name="stick_breaking",
    )(qi, kj, kmax, pb, pb, pb, sums)


def _banded_kernel(sink_ref, q_ref, kp_ref, kc_ref, vp_ref, vc_ref, bias_ref, o_ref, *, tq, pad, window, use_sink):
    i = pl.program_id(1)
    dist_prev = (lax.broadcasted_iota(jnp.int32, (tq, pad), 0) + pad
                 - lax.broadcasted_iota(jnp.int32, (tq, pad), 1))
    dist_cur = lax.broadcasted_iota(jnp.int32, (tq, tq), 0) - lax.broadcasted_iota(jnp.int32, (tq, tq), 1)
    mask_prev = (dist_prev < window) & (i > 0)
    mask_cur = (dist_cur >= 0) & (dist_cur < window)
    for hd in range(4):
        cols = slice(hd * HEAD_DIM, (hd + 1) * HEAD_DIM)
        kcols = slice((hd // 2) * HEAD_DIM, (hd // 2 + 1) * HEAD_DIM)
        q = q_ref[:, cols]
        s_prev = jnp.where(mask_prev, _dot_nt(q, kp_ref[:, kcols]) + bias_ref[hd, :, :pad], NEG)
        s_cur = jnp.where(mask_cur, _dot_nt(q, kc_ref[:, kcols]) + bias_ref[hd, :, pad:], NEG)
        m = jnp.maximum(jnp.max(s_prev, axis=-1, keepdims=True), jnp.max(s_cur, axis=-1, keepdims=True))
        if use_sink:
            sink = sink_ref[hd]
            m = jnp.maximum(m, sink)
        p_prev = jnp.where(mask_prev, jnp.exp(s_prev - m), 0.0)
        p_cur = jnp.where(mask_cur, jnp.exp(s_cur - m), 0.0)
        denom = jnp.sum(p_prev, axis=-1, keepdims=True) + jnp.sum(p_cur, axis=-1, keepdims=True)
        if use_sink:
            denom = denom + jnp.exp(sink - m)
        o = _dot(p_prev.astype(BF16), vp_ref[:, kcols]) + _dot(p_cur.astype(BF16), vc_ref[:, kcols])
        o_ref[:, cols] = o / jnp.maximum(denom, 1e-30)


BAND_TQ = 256


def _band_tiles(window):
    pad = -(-(window - 1) // LANES) * LANES
    return pad, max(pad, BAND_TQ)


def _banded_attention(pb, sinks, bias, batch, seq, window, q_col, k_col, v_col, use_sink):
    pad, tq = _band_tiles(window)
    per = tq // pad
    prev = lambda i: jnp.maximum(i * per - 1, 0)
    grid_spec = pltpu.PrefetchScalarGridSpec(
        num_scalar_prefetch=1,
        grid=(batch, seq // tq),
        in_specs=[
            pl.BlockSpec((None, tq, 256), lambda b, i, s: (b, i, q_col // 256)),
            pl.BlockSpec((None, pad, 128), lambda b, i, s: (b, prev(i), k_col // 128)),
            pl.BlockSpec((None, tq, 128), lambda b, i, s: (b, i, k_col // 128)),
            pl.BlockSpec((None, pad, 128), lambda b, i, s: (b, prev(i), v_col // 128)),
            pl.BlockSpec((None, tq, 128), lambda b, i, s: (b, i, v_col // 128)),
            pl.BlockSpec((4, tq, pad + tq), lambda b, i, s: (0, 0, 0)),
        ],
        out_specs=pl.BlockSpec((None, tq, 256), lambda b, i, s: (b, i, 0)),
    )
    return pl.pallas_call(
        functools.partial(_banded_kernel, tq=tq, pad=pad, window=window, use_sink=use_sink),
        out_shape=jax.ShapeDtypeStruct((batch, seq, 256), F32),
        grid_spec=grid_spec,
        compiler_params=pltpu.CompilerParams(dimension_semantics=("parallel", "arbitrary"),
                                             vmem_limit_bytes=VMEM_LIMIT),
        name="banded_w%d" % window,
    )(sinks, pb, pb, pb, pb, pb, bias)


def _compress_kernel(rows_ref, pos_ref, w1_ref, w2_ref, g_ref, o_ref):
    win = rows_ref[...] + pos_ref[...]
    hid = jax.nn.gelu(_dot(win.astype(BF16), w1_ref[...]), approximate=True)
    out = _dot(hid.astype(BF16), w2_ref[...])
    normed = _row_rms(out, HEAD_DIM) * g_ref[...]
    o_ref[...] = jnp.where(pl.program_id(0) == 0, normed, out).astype(BF16)


def _compress(rows, pos, w1, w2, gain, tn):
    _, bh, ncp, width = rows.shape
    return pl.pallas_call(
        _compress_kernel,
        out_shape=jax.ShapeDtypeStruct((2, bh, ncp, HEAD_DIM), BF16),
        grid=(2, bh, ncp // tn),
        in_specs=[
            pl.BlockSpec((None, None, tn, width), lambda c, r, n: (c, r, n, 0)),
            pl.BlockSpec((None, 1, width), lambda c, r, n: (c, 0, 0)),
            pl.BlockSpec((None, width, NSA_CMP_HIDDEN), lambda c, r, n: (c, 0, 0)),
            pl.BlockSpec((None, NSA_CMP_HIDDEN, HEAD_DIM), lambda c, r, n: (c, 0, 0)),
            pl.BlockSpec((1, HEAD_DIM), lambda c, r, n: (0, 0)),
        ],
        out_specs=pl.BlockSpec((None, None, tn, HEAD_DIM), lambda c, r, n: (c, r, n, 0)),
        compiler_params=pltpu.CompilerParams(dimension_semantics=("parallel", "parallel", "parallel"),
                                             vmem_limit_bytes=VMEM_LIMIT),
        name="nsa_compress",
    )(rows, pos, w1, w2, gain)


CMP_TQ = 1024


def _cmp_kernel(q_ref, kc_ref, vc_ref, bias_ref, ov_ref, o_ref, qa_ref, *, ncp, n_sel_pad, topk):
    i = pl.program_id(2)
    q_pos = i * CMP_TQ + lax.broadcasted_iota(jnp.int32, (CMP_TQ, ncp), 0)
    cmp_end = lax.broadcasted_iota(jnp.int32, (CMP_TQ, ncp), 1) * NSA_CMP_STRIDE + (NSA_CMP_LEN - 1)
    mask = cmp_end <= q_pos
    kc = kc_ref[...]
    vc = vc_ref[...]
    p_sum = jnp.zeros((CMP_TQ, ncp), F32)
    for g in range(2):
        cols = slice(g * HEAD_DIM, (g + 1) * HEAD_DIM)
        s = jnp.where(mask, _dot_nt(q_ref[:, cols], kc) + bias_ref[g], NEG)
        m = jnp.max(s, axis=-1, keepdims=True)
        p = jnp.where(mask, jnp.exp(s - m), 0.0)
        p = p / jnp.maximum(jnp.sum(p, axis=-1, keepdims=True), 1e-30)
        o_ref[:, cols] = _dot(p.astype(BF16), vc)
        p_sum = p_sum + p
    imp = _dot_exact_rhs(p_sum, ov_ref[...])

    row_pos = i * CMP_TQ + lax.broadcasted_iota(jnp.int32, (CMP_TQ, n_sel_pad), 0)
    blk = lax.broadcasted_iota(jnp.int32, (CMP_TQ, n_sel_pad), 1)
    cur = row_pos >> int(math.log2(NSA_SEL_LEN))
    forced = (blk == 0) | (blk == cur) | (blk == cur - 1)
    valid = blk * NSA_SEL_LEN <= row_pos
    score = jnp.where(valid, imp + jnp.where(forced, FORCE_BONUS, 0.0), NEG)
    blk_f = blk.astype(F32)
    dropped = jnp.ones((CMP_TQ, n_sel_pad), F32)
    for _ in range(topk):
        best = jnp.max(score, axis=-1, keepdims=True)
        first = jnp.min(jnp.where(score == best, blk_f, float(n_sel_pad)), axis=-1, keepdims=True)
        hit = blk_f == first
        dropped = jnp.where(hit, 0.0, dropped)
        score = jnp.where(hit, -jnp.inf, score)
    dropped = jnp.where(valid, dropped, 1.0).astype(BF16)
    q = q_ref[...]
    low = lax.broadcasted_iota(jnp.int32, q.shape, 1) < HEAD_DIM
    zero = jnp.zeros_like(q)
    width = LANES + n_sel_pad
    qa_ref[:, 0:LANES] = jnp.where(low, q, zero)
    qa_ref[:, LANES:width] = dropped
    qa_ref[:, width:width + LANES] = jnp.where(low, zero, q)
    qa_ref[:, width + LANES:2 * width] = dropped


def _cmp_attention(pb, kvc, bias_c, overlap, batch, seq):
    ncp = kvc.shape[2]
    n_sel_pad = overlap.shape[1]
    topk = min(NSA_TOPK, seq // NSA_SEL_LEN)
    pair = 2 * (LANES + n_sel_pad)
    return pl.pallas_call(
        functools.partial(_cmp_kernel, ncp=ncp, n_sel_pad=n_sel_pad, topk=topk),
        out_shape=(jax.ShapeDtypeStruct((batch, seq, 256), F32),
                   jax.ShapeDtypeStruct((batch, seq, 2 * pair), BF16)),
        grid=(batch, 2, seq // CMP_TQ),
        in_specs=[
            pl.BlockSpec((None, CMP_TQ, 128), lambda b, h, i: (b, i, PB_NSAQ // 128 + h)),
            pl.BlockSpec((None, None, ncp, HEAD_DIM), lambda b, h, i: (0, b * 2 + h, 0, 0)),
            pl.BlockSpec((None, None, ncp, HEAD_DIM), lambda b, h, i: (1, b * 2 + h, 0, 0)),
            pl.BlockSpec((2, CMP_TQ, ncp), lambda b, h, i: (h, i, 0)),
            pl.BlockSpec((ncp, n_sel_pad), lambda b, h, i: (0, 0)),
        ],
        out_specs=(pl.BlockSpec((None, CMP_TQ, 128), lambda b, h, i: (b, i, h)),
                   pl.BlockSpec((None, CMP_TQ, pair), lambda b, h, i: (b, i, h))),
        compiler_params=pltpu.CompilerParams(dimension_semantics=("parallel", "parallel", "arbitrary"),
                                             vmem_limit_bytes=VMEM_LIMIT),
        name="nsa_cmp_select",
    )(pb, kvc, kvc, bias_c, overlap)


SEL_T = 512


def _flash_init(m_ref, acc_ref):
    m_ref[...] = jnp.full_like(m_ref, NEG)
    acc_ref[...] = jnp.zeros_like(acc_ref)


def _flash_update(s, v_ones, m_ref, acc_ref, hd):
    m_old = m_ref[hd]
    m_new = jnp.maximum(m_old, jnp.max(s, axis=-1, keepdims=True))
    alpha = jnp.exp(m_old - m_new)
    pr = jnp.exp(s - jnp.tile(m_new, (1, s.shape[1] // LANES)))
    acc_ref[hd] = alpha * acc_ref[hd] + _dot(pr.astype(BF16), v_ones)
    m_ref[hd] = m_new


def _flash_finish(acc_ref, o_ref, heads):
    for hd in range(heads):
        acc = acc_ref[hd]
        row_sum = pltpu.roll(acc, HEAD_DIM, 1)
        o_ref[:, hd * HEAD_DIM:(hd + 1) * HEAD_DIM] = (acc / jnp.maximum(row_sum, 1e-30))[:, :HEAD_DIM]


def _sel_kernel(qi_ref, kj_ref, q_ref, k_ref, v_ref, pen_ref, bias_ref, o_ref, m_ref, acc_ref):
    p = pl.program_id(1)
    qi = qi_ref[p]
    kj = kj_ref[p]
    width = q_ref.shape[1] // 4

    @pl.when(kj == 0)
    def _():
        _flash_init(m_ref, acc_ref)

    def step(diagonal):
        pen = pen_ref[...]
        for hd in range(4):
            k_pen = jnp.concatenate([k_ref[:, hd * LANES:(hd + 1) * LANES], pen], axis=1)
            s = _dot_nt(q_ref[:, hd * width:(hd + 1) * width], k_pen) + bias_ref[hd]
            if diagonal:
                row = lax.broadcasted_iota(jnp.int32, (SEL_T, SEL_T), 0)
                col = lax.broadcasted_iota(jnp.int32, (SEL_T, SEL_T), 1)
                s = jnp.where(col <= row, s, NEG)
            _flash_update(s, v_ref[:, (hd // 2) * LANES:(hd // 2 + 1) * LANES], m_ref, acc_ref, hd)

    @pl.when(kj < qi)
    def _():
        step(False)

    @pl.when(kj == qi)
    def _():
        step(True)
        _flash_finish(acc_ref, o_ref, 4)


def _causal_pairs(n):
    qi, kj = [], []
    for i in range(n):
        for j in range(i + 1):
            qi.append(i)
            kj.append(j)
    return jnp.asarray(np.array(qi, np.int32)), jnp.asarray(np.array(kj, np.int32))


def _sel_attention(pb, q_aug, penalty, bias_s, batch, seq):
    n_sel_pad = penalty.shape[1]
    n_delta = bias_s.shape[1]
    qi, kj = _causal_pairs(seq // SEL_T)
    grid_spec = pltpu.PrefetchScalarGridSpec(
        num_scalar_prefetch=2,
        grid=(batch, int(qi.shape[0])),
        in_specs=[
            pl.BlockSpec((None, SEL_T, q_aug.shape[2]), lambda b, p, qi, kj: (b, qi[p], 0)),
            pl.BlockSpec((None, SEL_T, 512), lambda b, p, qi, kj: (b, kj[p], PB_NSAKS // 512)),
            pl.BlockSpec((None, SEL_T, 256), lambda b, p, qi, kj: (b, kj[p], PB_NSAVS // 256)),
            pl.BlockSpec((SEL_T, n_sel_pad), lambda b, p, qi, kj: (kj[p], 0)),
            pl.BlockSpec((4, None, SEL_T, SEL_T),
                         lambda b, p, qi, kj: (0, jnp.minimum(qi[p] - kj[p], n_delta - 1), 0, 0)),
        ],
        out_specs=pl.BlockSpec((None, SEL_T, 256), lambda b, p, qi, kj: (b, qi[p], 0)),
        scratch_shapes=[pltpu.VMEM((4, SEL_T, LANES), F32), pltpu.VMEM((4, SEL_T, LANES), F32)],
    )
    return pl.pallas_call(
        _sel_kernel,
        out_shape=jax.ShapeDtypeStruct((batch, seq, 256), F32),
        grid_spec=grid_spec,
        compiler_params=pltpu.CompilerParams(dimension_semantics=("parallel", "arbitrary"),
                                             vmem_limit_bytes=VMEM_LIMIT),
        name="nsa_selected",
    )(qi, kj, q_aug, pb, pb, penalty, bias_s)


MLA_T = 512


def _mla_kernel(qi_ref, kj_ref, q_ref, k_ref, v_ref, o_ref, m_ref, acc_ref):
    p = pl.program_id(1)
    qi = qi_ref[p]
    kj = kj_ref[p]

    @pl.when(kj == 0)
    def _():
        _flash_init(m_ref, acc_ref)

    def step(diagonal):
        for hd in range(MLA_HEADS):
            cols = slice(hd * LANES, (hd + 1) * LANES)
            s = _dot_nt(q_ref[:, cols], k_ref[:, cols])
            if diagonal:
                row = lax.broadcasted_iota(jnp.int32, (MLA_T, MLA_T), 0)
                col = lax.broadcasted_iota(jnp.int32, (MLA_T, MLA_T), 1)
                s = jnp.where(col <= row, s, NEG)
            _flash_update(s, v_ref[:, cols], m_ref, acc_ref, hd)

    @pl.when(kj < qi)
    def _():
        step(False)

    @pl.when(kj == qi)
    def _():
        step(True)
        _flash_finish(acc_ref, o_ref, MLA_HEADS)


def _mla_attention(pb, batch, seq):
    qi, kj = _causal_pairs(seq // MLA_T)
    grid_spec = pltpu.PrefetchScalarGridSpec(
        num_scalar_prefetch=2,
        grid=(batch, int(qi.shape[0])),
        in_specs=[
            pl.BlockSpec((None, MLA_T, 512), lambda b, p, qi, kj: (b, qi[p], PB_MLAQ // 512)),
            pl.BlockSpec((None, MLA_T, 512), lambda b, p, qi, kj: (b, kj[p], PB_MLAK // 512)),
            pl.BlockSpec((None, MLA_T, 512), lambda b, p, qi, kj: (b, kj[p], PB_MLAV // 512)),
        ],
        out_specs=pl.BlockSpec((None, MLA_T, 256), lambda b, p, qi, kj: (b, qi[p], 0)),
        scratch_shapes=[pltpu.VMEM((4, MLA_T, LANES), F32), pltpu.VMEM((4, MLA_T, LANES), F32)],
    )
    return pl.pallas_call(
        _mla_kernel,
        out_shape=jax.ShapeDtypeStruct((batch, seq, 256), F32),
        grid_spec=grid_spec,
        compiler_params=pltpu.CompilerParams(dimension_semantics=("parallel", "arbitrary"),
                                             vmem_limit_bytes=VMEM_LIMIT),
        name="mla_causal",
    )(qi, kj, pb, pb, pb)


def _outproj_kernel(x_ref, oa_ref, ob_ref, oc_ref, os_ref, ow_ref, od_ref, gate_ref, gexp_ref, gn_ref, w_ref,
                    o_ref):
    gates = gate_ref[...]
    g_hi, g_lo = _split_bf16(gates)

    def gate(branch):
        e = gexp_ref[branch]
        return _dot(g_hi, e) + _dot(g_lo, e)

    o_nsa = gate(0) * oc_ref[...] + gate(1) * os_ref[...] + gate(2) * ow_ref[...]
    gn = gn_ref[...]
    y = x_ref[...]
    for grp, o in enumerate((oa_ref[...], ob_ref[...], o_nsa, od_ref[...])):
        cols = slice(grp * GROUP_WIDTH, (grp + 1) * GROUP_WIDTH)
        normed = _row_rms(o, GROUP_WIDTH) * gn[:, cols]
        y = y + _dot(normed.astype(BF16), w_ref[cols, :])
    o_ref[...] = y


def _outproj(x2, oa, ob, oc, osel, ow, od, pf, gexp, gn, w, tm):
    tokens = x2.shape[0]
    row = lambda t: (t, 0)
    o_spec = pl.BlockSpec((tm, GROUP_WIDTH), row)
    return pl.pallas_call(
        _outproj_kernel,
        out_shape=jax.ShapeDtypeStruct((tokens, D_MODEL), F32),
        grid=(tokens // tm,),
        in_specs=[
            pl.BlockSpec((tm, D_MODEL), row), o_spec, o_spec, o_spec, o_spec, o_spec, o_spec,
            pl.BlockSpec((tm, 128), lambda t: (t, PF_GATE // 128)),
            pl.BlockSpec((3, 128, GROUP_WIDTH), lambda t: (0, 0, 0)),
            pl.BlockSpec((1, D_MODEL), lambda t: (0, 0)),
            pl.BlockSpec((D_MODEL, D_MODEL), lambda t: (0, 0)),
        ],
        out_specs=pl.BlockSpec((tm, D_MODEL), row),
        compiler_params=pltpu.CompilerParams(dimension_semantics=("parallel",), vmem_limit_bytes=VMEM_LIMIT),
        name="out_proj",
    )(x2, oa, ob, oc, osel, ow, od, pf, gexp, gn, w)


MOE_TM = 1024
MOE_QUAD = 4
MOE_CHUNK = 256
MOE_SLOTS = MOE_TM + MOE_GROUPS * MOE_CHUNK
MOE_VMEM_LIMIT = 60 * 1024 * 1024


def _moe_kernel(x_ref, fn_ref, wr_hi_ref, wr_lo_ref, br_ref, cexp_ref, wg_ref, wu_ref, wd_ref, o_ref, hs_ref,
                cs_ref, ys_ref, slot_ref, start_ref):
    step = pl.program_id(1)
    lane = lax.broadcasted_iota(jnp.int32, (MOE_TM, LANES), 1)

    @pl.when(step == 0)
    def _():
        h = _row_rms(x_ref[...], D_MODEL) * fn_ref[...]
        h_hi, h_lo = _split_bf16(h)
        logits = (_dot(h_hi, wr_hi_ref[...]) + _dot(h_lo, wr_hi_ref[...]) + _dot(h_hi, wr_lo_ref[...])
                  + br_ref[...])
        lane_f = lane.astype(F32)
        no_lane = float(LANES)
        is_group = lane < MOE_GROUPS
        g_max = jnp.max(jnp.where(is_group, logits, -jnp.inf), axis=-1, keepdims=True)
        g_star = jnp.min(jnp.where(is_group & (logits == g_max), lane_f, no_lane), axis=-1, keepdims=True)
        g_den = jnp.sum(jnp.where(is_group, jnp.exp(logits - g_max), 0.0), axis=-1, keepdims=True)
        g_w = 1.0 / g_den
        group_of_lane = ((lane - MOE_GROUPS) >> int(math.log2(MOE_EPG))).astype(F32)
        in_group = (lane >= MOE_GROUPS) & (lane < MOE_GROUPS + MOE_EXPERTS) & (group_of_lane == g_star)
        e_l = jnp.where(in_group, logits, -jnp.inf)
        top1 = jnp.max(e_l, axis=-1, keepdims=True)
        i1 = jnp.min(jnp.where(e_l == top1, lane_f, no_lane), axis=-1, keepdims=True)
        e_l2 = jnp.where(lane_f == i1, -jnp.inf, e_l)
        top2 = jnp.max(e_l2, axis=-1, keepdims=True)
        i2 = jnp.min(jnp.where(e_l2 == top2, lane_f, no_lane), axis=-1, keepdims=True)
        r = jnp.exp(top2 - top1)
        w1 = g_w / (1.0 + r)
        w2 = g_w * r / (1.0 + r)
        comb = jnp.where(lane_f == i1, w1, jnp.where(lane_f == i2, w2, 0.0))

        onehot = jnp.where(lane_f == g_star, 1.0, 0.0)
        row = lax.broadcasted_iota(jnp.int32, (MOE_TM, LANES), 0)
        incl = onehot
        shift = 1
        while shift < MOE_TM:
            incl = incl + jnp.where(row >= shift, pltpu.roll(incl, shift, 0), 0.0)
            shift *= 2
        counts = incl[MOE_TM - 1:MOE_TM, :]
        padded = jnp.floor((counts + (MOE_CHUNK - 1)) * (1.0 / MOE_CHUNK)) * MOE_CHUNK
        before = (lax.broadcasted_iota(jnp.int32, (LANES, LANES), 0)
                  < lax.broadcasted_iota(jnp.int32, (LANES, LANES), 1))
        starts = _dot(jnp.broadcast_to(padded, (8, LANES)).astype(BF16),
                      jnp.where(before, 1.0, 0.0).astype(BF16))[0:1, :]
        for g in range(MOE_GROUPS + 1):
            start_ref[g] = jnp.sum(jnp.where(lane[0:1, :] == g, starts, 0.0)).astype(jnp.int32)
        slot = jnp.sum(onehot * (starts + incl - onehot), axis=-1, keepdims=True)
        slot_ref[...] = jnp.broadcast_to(slot, (MOE_TM, LANES))
        slot_row = slot_ref[...].T[0:1, :]
        n_slots = hs_ref.shape[0]
        place = jnp.where(lax.broadcasted_iota(jnp.int32, (n_slots, MOE_TM), 0).astype(F32) == slot_row,
                          1.0, 0.0).astype(BF16)
        hs_ref[...] = _dot(place, h.astype(BF16)).astype(BF16)
        c_hi, c_lo = _split_bf16(comb)
        cs_ref[:, :LANES] = _dot(place, c_hi).astype(BF16)
        cs_ref[:, LANES:] = _dot(place, c_lo).astype(BF16)
        ys_ref[...] = jnp.zeros_like(ys_ref)

    group = step // (MOE_EPG // MOE_QUAD)
    first_slot = start_ref[group]
    half = MOE_QUAD * MOE_HIDDEN // 2

    def chunk(c, carry):
        rows = pl.ds(pl.multiple_of(first_slot + c * MOE_CHUNK, MOE_CHUNK), MOE_CHUNK)
        hb = hs_ref[rows, :]
        weight = _dot(cs_ref[rows, :], cexp_ref[...])
        y = None
        for s in range(2):
            cols = slice(s * half, (s + 1) * half)
            a = jax.nn.silu(_dot(hb, wg_ref[:, cols])) * _dot(hb, wu_ref[:, cols]) * weight[:, cols]
            part = _dot(a.astype(BF16), wd_ref[cols, :])
            y = part if y is None else y + part
        ys_ref[rows, :] += y
        return carry

    lax.fori_loop(0, (start_ref[group + 1] - first_slot) // MOE_CHUNK, chunk, 0)

    @pl.when(step == MOE_EXPERTS // MOE_QUAD - 1)
    def _():
        n_slots = hs_ref.shape[0]
        back = jnp.where(lax.broadcasted_iota(jnp.int32, (MOE_TM, n_slots), 1).astype(F32) == slot_ref[:, 0:1],
                         1.0, 0.0).astype(BF16)
        y_hi, y_lo = _split_bf16(ys_ref[...])
        o_ref[...] = x_ref[...] + _dot(back, y_hi) + _dot(back, y_lo)


def _moe(x2, fn, wr_hi, wr_lo, br, wg, wu, wd):
    tokens = x2.shape[0]
    quads = MOE_EXPERTS // MOE_QUAD
    width = MOE_QUAD * MOE_HIDDEN
    const = lambda t, e: (0, 0)
    lane_of_col = MOE_GROUPS + np.arange(quads)[:, None, None] * MOE_QUAD + np.arange(width)[None, None, :] // MOE_HIDDEN
    cexp = jnp.asarray((np.arange(2 * LANES)[None, :, None] % LANES) == lane_of_col, BF16)
    return pl.pallas_call(
        _moe_kernel,
        out_shape=jax.ShapeDtypeStruct((tokens, D_MODEL), F32),
        grid=(tokens // MOE_TM, quads),
        in_specs=[
            pl.BlockSpec((MOE_TM, D_MODEL), lambda t, e: (t, 0)),
            pl.BlockSpec((1, D_MODEL), const),
            pl.BlockSpec((D_MODEL, LANES), const),
            pl.BlockSpec((D_MODEL, LANES), const),
            pl.BlockSpec((1, LANES), const),
            pl.BlockSpec((None, 2 * LANES, width), lambda t, e: (e, 0, 0)),
            pl.BlockSpec((None, D_MODEL, width), lambda t, e: (e, 0, 0)),
            pl.BlockSpec((None, D_MODEL, width), lambda t, e: (e, 0, 0)),
            pl.BlockSpec((None, width, D_MODEL), lambda t, e: (e, 0, 0)),
        ],
        out_specs=pl.BlockSpec((MOE_TM, D_MODEL), lambda t, e: (t, 0)),
        scratch_shapes=[pltpu.VMEM((MOE_SLOTS, D_MODEL), BF16), pltpu.VMEM((MOE_SLOTS, 2 * LANES), BF16),
                        pltpu.VMEM((MOE_SLOTS, D_MODEL), F32), pltpu.VMEM((MOE_TM, LANES), F32),
                        pltpu.SMEM((8,), jnp.int32)],
        compiler_params=pltpu.CompilerParams(dimension_semantics=("parallel", "arbitrary"),
                                             vmem_limit_bytes=MOE_VMEM_LIMIT),
        name="hier_moe",
    )(x2, fn, wr_hi, wr_lo, br, cexp, wg, wu, wd)


def _t5_bucket(dist):
    n = jnp.maximum(dist, 0)
    max_exact = T5_BUCKETS // 2
    nf = jnp.maximum(n, 1).astype(F32)
    large = max_exact + (jnp.log(nf / max_exact) / math.log(T5_MAX_DIST / max_exact)
                         * (T5_BUCKETS - max_exact)).astype(jnp.int32)
    large = jnp.minimum(large, T5_BUCKETS - 1)
    return jnp.where(n < max_exact, n, large)


def _sel_delta_cap():
    max_exact = T5_BUCKETS // 2
    span = T5_BUCKETS - max_exact
    last_bucket_from = max_exact * (T5_MAX_DIST / max_exact) ** ((span - 1) / span)
    cap = 1
    while (cap - 1) * SEL_T + 1 < 1.25 * last_bucket_from:
        cap += 1
    return cap


def _position_tables(rel_bias, seq):
    buckets = _t5_bucket(jnp.arange(seq))
    first = jnp.sum(buckets[None, :] < jnp.arange(T5_BUCKETS)[:, None], axis=1)
    tbl = rel_bias.T

    def toeplitz(heads, dist):
        shape = (tbl[heads].shape[0],) + (1,) * dist.ndim
        out = jnp.broadcast_to(tbl[heads][:, 0].reshape(shape), shape[:1] + dist.shape)
        for b in range(1, T5_BUCKETS):
            out = jnp.where((dist >= first[b])[None], tbl[heads][:, b].reshape(shape), out)
        return out

    swa_h, nsa_h = slice(0, 4), slice(4, 8)
    def band(window):
        pad, tq = _band_tiles(window)
        return jnp.arange(tq)[:, None] + pad - jnp.arange(pad + tq)[None, :]

    bias_swa = toeplitz(swa_h, band(SWA_WINDOW))
    bias_win = toeplitz(nsa_h, band(NSA_WINDOW))
    ncp = seq // NSA_CMP_STRIDE
    cmp_end = jnp.arange(ncp) * NSA_CMP_STRIDE + NSA_CMP_LEN - 1
    bias_cmp = toeplitz(nsa_h, jnp.arange(seq)[:, None] - cmp_end[None, :])
    nd = min(_sel_delta_cap() + 1, seq // SEL_T)
    dist_s = (jnp.arange(nd)[:, None, None] * SEL_T + jnp.arange(SEL_T)[None, :, None]
              - jnp.arange(SEL_T)[None, None, :])
    bias_sel = toeplitz(nsa_h, dist_s)

    n_sel_pad = -(-(seq // NSA_SEL_LEN) // LANES) * LANES
    sel_start = np.arange(n_sel_pad) * NSA_SEL_LEN
    c_start = np.arange(ncp) * NSA_CMP_STRIDE
    c_end = c_start + NSA_CMP_LEN - 1
    real = (np.arange(ncp) < ncp - NSA_CMP_LEN // NSA_CMP_STRIDE + 1)[:, None] & (sel_start < seq)[None, :]
    overlap = ((c_start[:, None] < sel_start[None, :] + NSA_SEL_LEN) & (c_end[:, None] >= sel_start[None, :]) & real)
    own_block = (np.arange(seq) // NSA_SEL_LEN)[:, None] == np.arange(n_sel_pad)[None, :]
    penalty = np.where(own_block, -2.0 ** 100, 0.0)

    pos = jnp.arange(seq, dtype=F32)
    inv_freq = ROPE_THETA ** (-jnp.arange(0, MLA_ROPE, 2, dtype=F32) / MLA_ROPE)
    ang = pos[:, None] * inv_freq[None, :]
    cos, sin = jnp.cos(ang), jnp.sin(ang)
    ones = jnp.ones((seq, MLA_NOPE), F32)
    tail = LANES - MLA_NOPE - MLA_ROPE
    cos_t = jnp.concatenate([ones, cos, cos, jnp.ones((seq, tail), F32)], axis=1)
    sin_t = jnp.concatenate([0 * ones, -sin, sin, jnp.zeros((seq, tail), F32)], axis=1)
    return dict(bias_swa=bias_swa, bias_win=bias_win, bias_cmp=bias_cmp, bias_sel=bias_sel,
                overlap=jnp.asarray(overlap, BF16), penalty=jnp.asarray(penalty, BF16), cos_t=cos_t, sin_t=sin_t)


def _pad_to(a, shape):
    return jnp.pad(a, [(0, s - d) for d, s in zip(a.shape, shape)])


def _pack_layer(w_in, swa_q_norm, swa_k_norm, nsa_q_norm, nsa_k_norm, mla_q_lat_norm, mla_w_q_up,
                mla_kv_lat_norm, mla_w_kv_up, mla_q_norm, mla_k_norm):
    kpe = w_in[:, 2636:2668]
    kpe_seg = jnp.concatenate([jnp.zeros((D_MODEL, MLA_NOPE), F32), kpe,
                               jnp.zeros((D_MODEL, LANES - MLA_QK), F32)], axis=1)
    spread = lambda cols: _pad_to(cols.reshape(D_MODEL, -1, HEAD_DIM), (D_MODEL, cols.shape[1] // HEAD_DIM, LANES)
                                  ).reshape(D_MODEL, -1)
    w = jnp.concatenate([
        w_in[:, :1920],
        spread(w_in[:, 1920:2048]),
        w_in[:, 2048:2304],
        _pad_to(w_in[:, 2304:2316], (D_MODEL, 128)),
        _pad_to(w_in[:, 2316:2508], (D_MODEL, 256)),
        w_in[:, 2508:2636],
        jnp.tile(kpe_seg, (1, MLA_HEADS)),
    ], axis=1).astype(BF16)
    tile4 = lambda g: jnp.tile(g, 4)
    g64 = _pad_to(jnp.stack([tile4(swa_q_norm), tile4(swa_k_norm), tile4(nsa_q_norm),
                             tile4(nsa_k_norm[1]), tile4(nsa_k_norm[2])]), (8, 256))
    glat = _pad_to(jnp.stack([_pad_to(mla_q_lat_norm, (256,)), _pad_to(mla_kv_lat_norm, (256,))]), (8, 256))
    wq = _pad_to(mla_w_q_up.reshape(MLA_Q_RANK, MLA_HEADS, MLA_QK), (256, MLA_HEADS, LANES))
    wq = wq.reshape(256, MLA_HEADS * LANES).astype(BF16)
    wkv = mla_w_kv_up.reshape(MLA_KV_RANK, MLA_HEADS, MLA_NOPE + MLA_V)
    wk = _pad_to(wkv[:, :, :MLA_NOPE], (MLA_KV_RANK, MLA_HEADS, LANES)).reshape(MLA_KV_RANK, MLA_HEADS * LANES)
    wv = _pad_to(wkv[:, :, MLA_NOPE:], (MLA_KV_RANK, MLA_HEADS, LANES)).reshape(MLA_KV_RANK, MLA_HEADS * LANES)
    wkv_p = jnp.concatenate([wk, wv], axis=1).astype(BF16)
    gmla = _pad_to(jnp.stack([jnp.tile(_pad_to(mla_q_norm, (LANES,)), MLA_HEADS),
                              jnp.tile(_pad_to(mla_k_norm, (LANES,)), MLA_HEADS)]), (8, 512))
    return w, g64, glat, wq, wkv_p, gmla


def _gate_expand():
    rows = np.arange(128)[None, :, None]
    cols = np.arange(GROUP_WIDTH)[None, None, :]
    branch = np.arange(3)[:, None, None]
    return jnp.asarray(rows == branch * 4 + cols // HEAD_DIM, BF16)


def _compress_rows(pf3, batch, seq):
    nb = seq // NSA_CMP_STRIDE
    kv = pf3[:, :, :256].reshape(batch, seq, 2, 2, HEAD_DIM).transpose(2, 0, 3, 1, 4)
    blocks = kv.reshape(2, batch * 2, nb, NSA_CMP_STRIDE * HEAD_DIM)
    nxt = jnp.concatenate([blocks[:, :, 1:], jnp.zeros_like(blocks[:, :, :1])], axis=2)
    return jnp.concatenate([blocks, nxt], axis=3)


def kernel(x, rel_bias, attn_norm, w_in, swa_q_norm, swa_k_norm, swa_sinks, nsa_q_norm, nsa_k_norm, nsa_cmp_pos, nsa_cmp_w1, nsa_cmp_w2, mla_q_lat_norm, mla_w_q_up, mla_kv_lat_norm, mla_w_kv_up, mla_q_norm, mla_k_norm, out_norm, w_out, ffn_norm, moe_w_group, moe_b_group, moe_w_expert, moe_b_expert, moe_w_gate, moe_w_up, moe_w_down):
    batch, seq, _ = x.shape
    depth = w_in.shape[0]
    tokens = batch * seq
    tm = 512
    assert seq % 2048 == 0 and tokens % MOE_TM == 0
    tabs = _position_tables(rel_bias, seq)
    gexp = _gate_expand()
    zero_sinks = jnp.zeros((4,), F32)
    x2 = x.reshape(tokens, D_MODEL)
    for l in range(depth):
        w, g64, glat, wq, wkv, gmla = _pack_layer(
            w_in[l], swa_q_norm[l], swa_k_norm[l], nsa_q_norm[l], nsa_k_norm[l], mla_q_lat_norm[l],
            mla_w_q_up[l], mla_kv_lat_norm[l], mla_w_kv_up[l], mla_q_norm[l], mla_k_norm[l])
        pb, pf = _prep(x2, attn_norm[l][None, :], w, g64, glat, wq, wkv, gmla, tabs["cos_t"], tabs["sin_t"],
                       seq, tm)
        pb3 = pb.reshape(batch, seq, PB_WIDTH)
        pf3 = pf.reshape(batch, seq, PF_WIDTH)
        o_a = _sb_attention(pb3, batch, seq)
        o_b = _banded_attention(pb3, swa_sinks[l], tabs["bias_swa"], batch, seq, SWA_WINDOW,
                                PB_SWAQ, PB_SWAK, PB_SWAV, True)
        rows = _compress_rows(pf3, batch, seq)
        kvc = _compress(rows, nsa_cmp_pos[l].reshape(2, 1, -1),
                        nsa_cmp_w1[l].reshape(2, -1, NSA_CMP_HIDDEN).astype(BF16), nsa_cmp_w2[l].astype(BF16),
                        nsa_k_norm[l][0][None, :], 128)
        o_c, q_aug = _cmp_attention(pb3, kvc, tabs["bias_cmp"], tabs["overlap"], batch, seq)
        o_s = _sel_attention(pb3, q_aug, tabs["penalty"], tabs["bias_sel"], batch, seq)
        o_w = _banded_attention(pb3, zero_sinks, tabs["bias_win"], batch, seq, NSA_WINDOW,
                                PB_NSAQ, PB_NSAKW, PB_NSAVW, False)
        o_d = _mla_attention(pb3, batch, seq)
        flat = lambda o: o.reshape(tokens, GROUP_WIDTH)
        x2 = _outproj(x2, flat(o_a), flat(o_b), flat(o_c), flat(o_s), flat(o_w), flat(o_d), pf, gexp,
                      out_norm[l][None, :], w_out[l].astype(BF16), tm)
        w_router = _pad_to(jnp.concatenate([moe_w_group[l], moe_w_expert[l]], axis=1), (D_MODEL, LANES))
        wr_hi = w_router.astype(BF16)
        wr_lo = (w_router - wr_hi.astype(F32)).astype(BF16)
        b_router = _pad_to(jnp.concatenate([moe_b_group[l], moe_b_expert[l]])[None, :], (1, LANES))
        quads = MOE_EXPERTS // MOE_QUAD
        by_quad = lambda w: w.astype(BF16).reshape(quads, MOE_QUAD, D_MODEL, MOE_HIDDEN).transpose(0, 2, 1, 3
                                                   ).reshape(quads, D_MODEL, MOE_QUAD * MOE_HIDDEN)
        x2 = _moe(x2, ffn_norm[l][None, :], wr_hi, wr_lo, b_router, by_quad(moe_w_gate[l]), by_quad(moe_w_up[l]),
                  moe_w_down[l].astype(BF16).reshape(quads, MOE_QUAD * MOE_HIDDEN, D_MODEL))
    return x2.reshape(batch, seq, D_MODEL)
```

```python
import functools
import math

import numpy as np
import jax
import jax.numpy as jnp
from jax import lax
from jax.experimental import pallas as pl
from jax.experimental.pallas import tpu as pltpu

F32 = jnp.float32
BF16 = jnp.bfloat16

D_MODEL = 1024
HEAD_DIM = 64
NEG = -1e30
EPS = 1e-6
FORCE_BONUS = 1000.0
SWA_WINDOW = 128
NSA_CMP_LEN = 32
NSA_CMP_STRIDE = 16
NSA_CMP_HIDDEN = 128
NSA_SEL_LEN = 64
NSA_TOPK = 16
NSA_WINDOW = 512
MLA_HEADS = 4
MLA_NOPE = 64
MLA_ROPE = 32
MLA_V = 64
MLA_Q_RANK = 192
MLA_KV_RANK = 128
MLA_QK = MLA_NOPE + MLA_ROPE
ROPE_THETA = 10000.0
T5_BUCKETS = 32
T5_MAX_DIST = 1024
MOE_GROUPS = 4
MOE_EPG = 8
MOE_EXPERTS = MOE_GROUPS * MOE_EPG
MOE_HIDDEN = 256
GROUP_WIDTH = 256
LANES = 128
VMEM_LIMIT = 48 * 1024 * 1024

PB_MLAQ, PB_MLAK, PB_MLAV = 0, 512, 1024
PB_SBQ, PB_SBK, PB_SBV = 1536, 1792, 2048
PB_SWAQ, PB_NSAQ, PB_NSAVS = 2304, 2560, 2816
PB_SWAK, PB_SWAV, PB_NSAKW, PB_NSAVW = 3072, 3200, 3328, 3456
PB_NSAKS = 3584
PB_WIDTH = 4096
PF_KC, PF_VC, PF_GATE = 0, 128, 256
PF_WIDTH = 384
W_SBQ, W_SBK, W_SBV, W_SWAQ, W_SWAK, W_SWAV, W_NSAQ = 0, 256, 512, 768, 1024, 1152, 1280
W_KC, W_VC, W_KS, W_VS, W_KW, W_VW, W_GATE = 1536, 1664, 1792, 1920, 2176, 2304, 2432
W_CQ, W_CKV, W_KPE = 2560, 2816, 2944
W_WIDTH = 3456

NT_DIMS = (((1,), (1,)), ((), ()))


def _dot(a, b):
    return jnp.dot(a, b, preferred_element_type=F32)


def _dot_nt(a, b):
    return lax.dot_general(a, b, NT_DIMS, preferred_element_type=F32)


def _split_bf16(x):
    hi = x.astype(BF16)
    lo = (x - hi.astype(F32)).astype(BF16)
    return hi, lo


def _dot_exact_rhs(x, m):
    hi, lo = _split_bf16(x)
    return _dot(hi, m) + _dot(lo, m)


def _block_diag_ones(width, seg):
    idx = np.arange(width) // seg
    return jnp.asarray(idx[:, None] == idx[None, :], BF16)


def _seg_rms(x, seg_ones, count):
    width = x.shape[1]
    ms = _dot_exact_rhs(x * x, seg_ones[:width, :width]) * (1.0 / count)
    return x * lax.rsqrt(ms + EPS)


def _row_rms(x, count):
    return x * lax.rsqrt(jnp.sum(x * x, axis=-1, keepdims=True) * (1.0 / count) + EPS)


def _prep_kernel(x_ref, an_ref, w_ref, g64_ref, glat_ref, wq_ref, wkv_ref, gmla_ref, cos_ref, sin_ref,
                 s64_ref, s128_ref, pb_ref, pf_ref):
    x = x_ref[...]
    s64 = s64_ref[...]
    s128 = s128_ref[...]
    h = _row_rms(x, D_MODEL) * an_ref[...]
    hb = h.astype(BF16)

    def proj(lo, hi):
        return _dot(hb, w_ref[:, lo:hi])

    def put(col, value):
        pb_ref[:, col:col + value.shape[1]] = value.astype(BF16)

    def ones_tail(width):
        lane = lax.broadcasted_iota(jnp.int32, (1, width), 1)
        return jnp.where((lane & (LANES - 1)) >= HEAD_DIM, 1.0, 0.0)

    scale = HEAD_DIM ** -0.5
    g64 = g64_ref[...]
    put(PB_SBQ, proj(W_SBQ, W_SBQ + 256) * scale)
    put(PB_SBK, proj(W_SBK, W_SBK + 256))
    put(PB_SBV, proj(W_SBV, W_SBV + 256))
    put(PB_SWAQ, _seg_rms(proj(W_SWAQ, W_SWAQ + 256), s64, 64) * g64[0:1, :] * scale)
    put(PB_SWAK, _seg_rms(proj(W_SWAK, W_SWAK + 128), s64, 64) * g64[1:2, :128])
    put(PB_SWAV, proj(W_SWAV, W_SWAV + 128))
    put(PB_NSAQ, _seg_rms(proj(W_NSAQ, W_NSAQ + 256), s64, 64) * g64[2:3, :] * scale)
    pf_ref[:, PF_KC:PF_KC + 128] = proj(W_KC, W_KC + 128)
    pf_ref[:, PF_VC:PF_VC + 128] = proj(W_VC, W_VC + 128)
    ks = _seg_rms(proj(W_KS, W_KS + 128), s64, 64) * g64[3:4, :128]
    ks_swapped = pltpu.roll(ks, HEAD_DIM, 1)
    low = lax.broadcasted_iota(jnp.int32, ks.shape, 1) < HEAD_DIM
    put(PB_NSAKS, jnp.where(low, ks, 0.0))
    put(PB_NSAKS + LANES, jnp.where(low, 0.0, ks_swapped))
    put(PB_NSAKS + 2 * LANES, jnp.where(low, ks_swapped, 0.0))
    put(PB_NSAKS + 3 * LANES, jnp.where(low, 0.0, ks))
    put(PB_NSAVS, proj(W_VS, W_VS + 256) + ones_tail(256))
    put(PB_NSAKW, _seg_rms(proj(W_KW, W_KW + 128), s64, 64) * g64[4:5, :128])
    put(PB_NSAVW, proj(W_VW, W_VW + 128))
    pf_ref[:, PF_GATE:PF_GATE + 128] = jax.nn.sigmoid(proj(W_GATE, W_GATE + 128))

    glat = glat_ref[...]
    cq = _row_rms(proj(W_CQ, W_CQ + 256), MLA_Q_RANK) * glat[0:1, :]
    q = _dot(cq.astype(BF16), wq_ref[...])
    ckv = _row_rms(proj(W_CKV, W_CKV + 128), MLA_KV_RANK) * glat[1:2, :128]
    kv = _dot(ckv.astype(BF16), wkv_ref[...])
    k = kv[:, :512] + proj(W_KPE, W_KPE + 512)
    gm = gmla_ref[...]
    q = _seg_rms(q, s128, MLA_QK) * gm[0:1, :]
    k = _seg_rms(k, s128, MLA_QK) * gm[1:2, :]
    cos = cos_ref[...]
    sin = sin_ref[...]
    lane = lax.broadcasted_iota(jnp.int32, (x.shape[0], LANES), 1)
    first_half = lane < MLA_NOPE + MLA_ROPE // 2

    def rope(t):
        partner = jnp.where(first_half, pltpu.roll(t, LANES - MLA_ROPE // 2, 1), pltpu.roll(t, MLA_ROPE // 2, 1))
        return t * cos + partner * sin

    qscale = MLA_QK ** -0.5
    for hd in range(MLA_HEADS):
        sl = slice(hd * LANES, (hd + 1) * LANES)
        put(PB_MLAQ + hd * LANES, rope(q[:, sl]) * qscale)
        put(PB_MLAK + hd * LANES, rope(k[:, sl]))
    put(PB_MLAV, kv[:, 512:1024] + ones_tail(512))


def _prep(x2, an, w, g64, glat, wq, wkv, gmla, cos_t, sin_t, seq, tm):
    tokens = x2.shape[0]
    n_pos = seq // tm
    const = lambda t: (0, 0)
    return pl.pallas_call(
        _prep_kernel,
        out_shape=(jax.ShapeDtypeStruct((tokens, PB_WIDTH), BF16), jax.ShapeDtypeStruct((tokens, PF_WIDTH), F32)),
        grid=(tokens // tm,),
        in_specs=[
            pl.BlockSpec((tm, D_MODEL), lambda t: (t, 0)),
            pl.BlockSpec((1, D_MODEL), const),
            pl.BlockSpec((D_MODEL, W_WIDTH), const),
            pl.BlockSpec((8, 256), const),
            pl.BlockSpec((8, 256), const),
            pl.BlockSpec((256, 512), const),
            pl.BlockSpec((128, 1024), const),
            pl.BlockSpec((8, 512), const),
            pl.BlockSpec((tm, LANES), lambda t: (t % n_pos, 0)),
            pl.BlockSpec((tm, LANES), lambda t: (t % n_pos, 0)),
            pl.BlockSpec((256, 256), const),
            pl.BlockSpec((512, 512), const),
        ],
        out_specs=(pl.BlockSpec((tm, PB_WIDTH), lambda t: (t, 0)), pl.BlockSpec((tm, PF_WIDTH), lambda t: (t, 0))),
        compiler_params=pltpu.CompilerParams(dimension_semantics=("parallel",), vmem_limit_bytes=VMEM_LIMIT),
        name="prep",
    )(x2, an, w, g64, glat, wq, wkv, gmla, cos_t, sin_t, _block_diag_ones(256, HEAD_DIM),
      _block_diag_ones(512, LANES))


SB_TQ, SB_KB, SB_SUB = 256, 2048, 128


SB_UNDERFLOW = 110.0


def _sb_kernel(qi_ref, kj_ref, kmax_ref, q_ref, k_ref, v_ref, sums_ref, o_ref, carry_ref, acc_ref, zb_ref,
               dead_ref):
    b = pl.program_id(0)
    p = pl.program_id(1)
    qi = qi_ref[p]
    kj = kj_ref[p]
    q_start = qi * SB_TQ
    first = kj == (q_start + SB_TQ - 1) // SB_KB

    @pl.when(first)
    def _():
        carry_ref[...] = jnp.zeros_like(carry_ref)
        acc_ref[...] = jnp.zeros_like(acc_ref)
        dead_ref[0] = 0
        ones = jnp.ones((HEAD_DIM, SB_SUB), BF16)
        for hd in range(4):
            q = q_ref[:, hd * HEAD_DIM:(hd + 1) * HEAD_DIM].astype(F32)
            zb_ref[hd * SB_TQ:(hd + 1) * SB_TQ, :] = (jnp.sqrt(_dot_exact_rhs(q * q, ones))
                                                      * (kmax_ref[b * 4 + hd] * 1.01))

    rel = (lax.broadcasted_iota(jnp.int32, (SB_TQ, SB_SUB), 1)
           - lax.broadcasted_iota(jnp.int32, (SB_TQ, SB_SUB), 0))

    def sub_tile(u, k_start):
        mask = jnp.tile(rel < q_start - k_start, (4, 1))
        rows = slice(u * SB_SUB, (u + 1) * SB_SUB)
        head = lambda hd: slice(hd * HEAD_DIM, (hd + 1) * HEAD_DIM)
        z = jnp.concatenate([_dot_nt(q_ref[:, head(hd)], k_ref[rows, head(hd)]) for hd in range(4)], axis=0)
        log_keep = jnp.where(mask, -(jnp.maximum(z, 0.0) + jnp.log(1.0 + jnp.exp(-jnp.abs(z)))), 0.0)
        hi, lo = _split_bf16(log_keep)
        sums = _dot(jnp.concatenate([hi, lo], axis=1), sums_ref[...])
        carry = carry_ref[...]
        a = jnp.where(mask, jnp.exp(z + sums[:, :SB_SUB] + carry), 0.0).astype(BF16)
        for hd in range(4):
            acc_ref[hd] += _dot(a[hd * SB_TQ:(hd + 1) * SB_TQ], v_ref[rows, head(hd)])
        carry_ref[...] = carry + sums[:, SB_SUB:]

    for u in reversed(range(SB_KB // SB_SUB)):
        k_start = kj * SB_KB + u * SB_SUB

        @pl.when((k_start < q_start + SB_TQ) & (dead_ref[0] == 0))
        def _(u=u, k_start=k_start):
            live = jnp.max(carry_ref[...] + zb_ref[...]) > -SB_UNDERFLOW

            @pl.when(live)
            def _():
                sub_tile(u, k_start)

            @pl.when(jnp.logical_not(live))
            def _():
                dead_ref[0] = 1

    @pl.when(kj == 0)
    def _():
        for hd in range(4):
            o_ref[:, hd * HEAD_DIM:(hd + 1) * HEAD_DIM] = acc_ref[hd]


def _sb_attention(pb, batch, seq):
    nq = seq // SB_TQ
    qi, kj = [], []
    for i in range(nq):
        for j in reversed(range((i * SB_TQ + SB_TQ - 1) // SB_KB + 1)):
            qi.append(i)
            kj.append(j)
    qi = jnp.asarray(np.array(qi, np.int32))
    kj = jnp.asarray(np.array(kj, np.int32))
    keys = pb[:, :, PB_SBK:PB_SBK + 256].astype(F32).reshape(batch, seq, 4, HEAD_DIM)
    kmax = jnp.sqrt(jnp.max(jnp.sum(keys * keys, axis=-1), axis=1)).reshape(batch * 4)
    j = np.arange(2 * SB_SUB)[:, None] % SB_SUB
    s = np.arange(2 * SB_SUB)[None, :]
    sums = jnp.asarray((s >= SB_SUB) | (j >= s), BF16)
    grid_spec = pltpu.PrefetchScalarGridSpec(
        num_scalar_prefetch=2,
        grid=(batch, int(qi.shape[0])),
        in_specs=[
            pl.BlockSpec(memory_space=pltpu.SMEM),
            pl.BlockSpec((None, SB_TQ, 256), lambda b, p, qi, kj: (b, qi[p], PB_SBQ // 256)),
            pl.BlockSpec((None, SB_KB, 256), lambda b, p, qi, kj: (b, kj[p], PB_SBK // 256)),
            pl.BlockSpec((None, SB_KB, 256), lambda b, p, qi, kj: (b, kj[p], PB_SBV // 256)),
            pl.BlockSpec((2 * SB_SUB, 2 * SB_SUB), lambda b, p, qi, kj: (0, 0)),
        ],
        out_specs=pl.BlockSpec((None, SB_TQ, 256), lambda b, p, qi, kj: (b, qi[p], 0)),
        scratch_shapes=[pltpu.VMEM((4 * SB_TQ, SB_SUB), F32), pltpu.VMEM((4, SB_TQ, HEAD_DIM), F32),
                        pltpu.VMEM((4 * SB_TQ, SB_SUB), F32), pltpu.SMEM((1,), jnp.int32)],
    )
    return pl.pallas_call(
        _sb_kernel,
        out_shape=jax.ShapeDtypeStruct((batch, seq, 256), F32),
        grid_spec=grid_spec,
        compiler_params=pltpu.CompilerParams(dimension_semantics=("parallel", "arbitrary"),
                                             vmem_limit_bytes=VMEM_LIMIT),
        name="stick_breaking",
    )(qi, kj, kmax, pb, pb, pb, sums)


def _banded_kernel(sink_ref, q_ref, kp_ref, kc_ref, vp_ref, vc_ref, bias_ref, o_ref, *, tq, pad, window, use_sink):
    i = pl.program_id(1)
    dist_prev = (lax.broadcasted_iota(jnp.int32, (tq, pad), 0) + pad
                 - lax.broadcasted_iota(jnp.int32, (tq, pad), 1))
    dist_cur = lax.broadcasted_iota(jnp.int32, (tq, tq), 0) - lax.broadcasted_iota(jnp.int32, (tq, tq), 1)
    mask_prev = (dist_prev < window) & (i > 0)
    mask_cur = (dist_cur >= 0) & (dist_cur < window)
    for hd in range(4):
        cols = slice(hd * HEAD_DIM, (hd + 1) * HEAD_DIM)
        kcols = slice((hd // 2) * HEAD_DIM, (hd // 2 + 1) * HEAD_DIM)
        q = q_ref[:, cols]
        s_prev = jnp.where(mask_prev, _dot_nt(q, kp_ref[:, kcols]) + bias_ref[hd, :, :pad], NEG)
        s_cur = jnp.where(mask_cur, _dot_nt(q, kc_ref[:, kcols]) + bias_ref[hd, :, pad:], NEG)
        m = jnp.maximum(jnp.max(s_prev, axis=-1, keepdims=True), jnp.max(s_cur, axis=-1, keepdims=True))
        if use_sink:
            sink = sink_ref[hd]
            m = jnp.maximum(m, sink)
        p_prev = jnp.where(mask_prev, jnp.exp(s_prev - m), 0.0)
        p_cur = jnp.where(mask_cur, jnp.exp(s_cur - m), 0.0)
        denom = jnp.sum(p_prev, axis=-1, keepdims=True) + jnp.sum(p_cur, axis=-1, keepdims=True)
        if use_sink:
            denom = denom + jnp.exp(sink - m)
        o = _dot(p_prev.astype(BF16), vp_ref[:, kcols]) + _dot(p_cur.astype(BF16), vc_ref[:, kcols])
        o_ref[:, cols] = o / jnp.maximum(denom, 1e-30)


BAND_TQ = 256


def _band_tiles(window):
    pad = -(-(window - 1) // LANES) * LANES
    return pad, max(pad, BAND_TQ)


def _banded_attention(pb, sinks, bias, batch, seq, window, q_col, k_col, v_col, use_sink):
    pad, tq = _band_tiles(window)
    per = tq // pad
    prev = lambda i: jnp.maximum(i * per - 1, 0)
    grid_spec = pltpu.PrefetchScalarGridSpec(
        num_scalar_prefetch=1,
        grid=(batch, seq // tq),
        in_specs=[
            pl.BlockSpec((None, tq, 256), lambda b, i, s: (b, i, q_col // 256)),
            pl.BlockSpec((None, pad, 128), lambda b, i, s: (b, prev(i), k_col // 128)),
            pl.BlockSpec((None, tq, 128), lambda b, i, s: (b, i, k_col // 128)),
            pl.BlockSpec((None, pad, 128), lambda b, i, s: (b, prev(i), v_col // 128)),
            pl.BlockSpec((None, tq, 128), lambda b, i, s: (b, i, v_col // 128)),
            pl.BlockSpec((4, tq, pad + tq), lambda b, i, s: (0, 0, 0)),
        ],
        out_specs=pl.BlockSpec((None, tq, 256), lambda b, i, s: (b, i, 0)),
    )
    return pl.pallas_call(
        functools.partial(_banded_kernel, tq=tq, pad=pad, window=window, use_sink=use_sink),
        out_shape=jax.ShapeDtypeStruct((batch, seq, 256), F32),
        grid_spec=grid_spec,
        compiler_params=pltpu.CompilerParams(dimension_semantics=("parallel", "arbitrary"),
                                             vmem_limit_bytes=VMEM_LIMIT),
        name="banded_w%d" % window,
    )(sinks, pb, pb, pb, pb, pb, bias)


def _compress_kernel(rows_ref, pos_ref, w1_ref, w2_ref, g_ref, o_ref):
    win = rows_ref[...] + pos_ref[...]
    hid = jax.nn.gelu(_dot(win.astype(BF16), w1_ref[...]), approximate=True)
    out = _dot(hid.astype(BF16), w2_ref[...])
    normed = _row_rms(out, HEAD_DIM) * g_ref[...]
    o_ref[...] = jnp.where(pl.program_id(0) == 0, normed, out).astype(BF16)


def _compress(rows, pos, w1, w2, gain, tn):
    _, bh, ncp, width = rows.shape
    return pl.pallas_call(
        _compress_kernel,
        out_shape=jax.ShapeDtypeStruct((2, bh, ncp, HEAD_DIM), BF16),
        grid=(2, bh, ncp // tn),
        in_specs=[
            pl.BlockSpec((None, None, tn, width), lambda c, r, n: (c, r, n, 0)),
            pl.BlockSpec((None, 1, width), lambda c, r, n: (c, 0, 0)),
            pl.BlockSpec((None, width, NSA_CMP_HIDDEN), lambda c, r, n: (c, 0, 0)),
            pl.BlockSpec((None, NSA_CMP_HIDDEN, HEAD_DIM), lambda c, r, n: (c, 0, 0)),
            pl.BlockSpec((1, HEAD_DIM), lambda c, r, n: (0, 0)),
        ],
        out_specs=pl.BlockSpec((None, None, tn, HEAD_DIM), lambda c, r, n: (c, r, n, 0)),
        compiler_params=pltpu.CompilerParams(dimension_semantics=("parallel", "parallel", "parallel"),
                                             vmem_limit_bytes=VMEM_LIMIT),
        name="nsa_compress",
    )(rows, pos, w1, w2, gain)


CMP_TQ = 1024


def _cmp_kernel(q_ref, kc_ref, vc_ref, bias_ref, ov_ref, o_ref, qa_ref, *, ncp, n_sel_pad, topk):
    i = pl.program_id(2)
    q_pos = i * CMP_TQ + lax.broadcasted_iota(jnp.int32, (CMP_TQ, ncp), 0)
    cmp_end = lax.broadcasted_iota(jnp.int32, (CMP_TQ, ncp), 1) * NSA_CMP_STRIDE + (NSA_CMP_LEN - 1)
    mask = cmp_end <= q_pos
    kc = kc_ref[...]
    vc = vc_ref[...]
    p_sum = jnp.zeros((CMP_TQ, ncp), F32)
    for g in range(2):
        cols = slice(g * HEAD_DIM, (g + 1) * HEAD_DIM)
        s = jnp.where(mask, _dot_nt(q_ref[:, cols], kc) + bias_ref[g], NEG)
        m = jnp.max(s, axis=-1, keepdims=True)
        p = jnp.where(mask, jnp.exp(s - m), 0.0)
        p = p / jnp.maximum(jnp.sum(p, axis=-1, keepdims=True), 1e-30)
        o_ref[:, cols] = _dot(p.astype(BF16), vc)
        p_sum = p_sum + p
    imp = _dot_exact_rhs(p_sum, ov_ref[...])

    row_pos = i * CMP_TQ + lax.broadcasted_iota(jnp.int32, (CMP_TQ, n_sel_pad), 0)
    blk = lax.broadcasted_iota(jnp.int32, (CMP_TQ, n_sel_pad), 1)
    cur = row_pos >> int(math.log2(NSA_SEL_LEN))
    forced = (blk == 0) | (blk == cur) | (blk == cur - 1)
    valid = blk * NSA_SEL_LEN <= row_pos
    score = jnp.where(valid, imp + jnp.where(forced, FORCE_BONUS, 0.0), NEG)
    blk_f = blk.astype(F32)
    dropped = jnp.ones((CMP_TQ, n_sel_pad), F32)
    for _ in range(topk):
        best = jnp.max(score, axis=-1, keepdims=True)
        first = jnp.min(jnp.where(score == best, blk_f, float(n_sel_pad)), axis=-1, keepdims=True)
        hit = blk_f == first
        dropped = jnp.where(hit, 0.0, dropped)
        score = jnp.where(hit, -jnp.inf, score)
    dropped = jnp.where(valid, dropped, 1.0).astype(BF16)
    q = q_ref[...]
    low = lax.broadcasted_iota(jnp.int32, q.shape, 1) < HEAD_DIM
    zero = jnp.zeros_like(q)
    width = LANES + n_sel_pad
    qa_ref[:, 0:LANES] = jnp.where(low, q, zero)
    qa_ref[:, LANES:width] = dropped
    qa_ref[:, width:width + LANES] = jnp.where(low, zero, q)
    qa_ref[:, width + LANES:2 * width] = dropped


def _cmp_attention(pb, kvc, bias_c, overlap, batch, seq):
    ncp = kvc.shape[2]
    n_sel_pad = overlap.shape[1]
    topk = min(NSA_TOPK, seq // NSA_SEL_LEN)
    pair = 2 * (LANES + n_sel_pad)
    return pl.pallas_call(
        functools.partial(_cmp_kernel, ncp=ncp, n_sel_pad=n_sel_pad, topk=topk),
        out_shape=(jax.ShapeDtypeStruct((batch, seq, 256), F32),
                   jax.ShapeDtypeStruct((batch, seq, 2 * pair), BF16)),
        grid=(batch, 2, seq // CMP_TQ),
        in_specs=[
            pl.BlockSpec((None, CMP_TQ, 128), lambda b, h, i: (b, i, PB_NSAQ // 128 + h)),
            pl.BlockSpec((None, None, ncp, HEAD_DIM), lambda b, h, i: (0, b * 2 + h, 0, 0)),
            pl.BlockSpec((None, None, ncp, HEAD_DIM), lambda b, h, i: (1, b * 2 + h, 0, 0)),
            pl.BlockSpec((2, CMP_TQ, ncp), lambda b, h, i: (h, i, 0)),
            pl.BlockSpec((ncp, n_sel_pad), lambda b, h, i: (0, 0)),
        ],
        out_specs=(pl.BlockSpec((None, CMP_TQ, 128), lambda b, h, i: (b, i, h)),
                   pl.BlockSpec((None, CMP_TQ, pair), lambda b, h, i: (b, i, h))),
        compiler_params=pltpu.CompilerParams(dimension_semantics=("parallel", "parallel", "arbitrary"),
                                             vmem_limit_bytes=VMEM_LIMIT),
        name="nsa_cmp_select",
    )(pb, kvc, kvc, bias_c, overlap)


SEL_T = 512


def _flash_init(m_ref, acc_ref):
    m_ref[...] = jnp.full_like(m_ref, NEG)
    acc_ref[...] = jnp.zeros_like(acc_ref)


def _flash_update(s, v_ones, m_ref, acc_ref, hd):
    m_old = m_ref[hd]
    m_new = jnp.maximum(m_old, jnp.max(s, axis=-1, keepdims=True))
    alpha = jnp.exp(m_old - m_new)
    pr = jnp.exp(s - jnp.tile(m_new, (1, s.shape[1] // LANES)))
    acc_ref[hd] = alpha * acc_ref[hd] + _dot(pr.astype(BF16), v_ones)
    m_ref[hd] = m_new


def _flash_finish(acc_ref, o_ref, heads):
    for hd in range(heads):
        acc = acc_ref[hd]
        row_sum = pltpu.roll(acc, HEAD_DIM, 1)
        o_ref[:, hd * HEAD_DIM:(hd + 1) * HEAD_DIM] = (acc / jnp.maximum(row_sum, 1e-30))[:, :HEAD_DIM]


def _sel_kernel(qi_ref, kj_ref, q_ref, k_ref, v_ref, pen_ref, bias_ref, o_ref, m_ref, acc_ref):
    p = pl.program_id(1)
    qi = qi_ref[p]
    kj = kj_ref[p]
    width = q_ref.shape[1] // 4

    @pl.when(kj == 0)
    def _():
        _flash_init(m_ref, acc_ref)

    def step(diagonal):
        pen = pen_ref[...]

        def scores(hd):
            k_pen = jnp.concatenate([k_ref[:, hd * LANES:(hd + 1) * LANES], pen], axis=1)
            s = _dot_nt(q_ref[:, hd * width:(hd + 1) * width], k_pen) + bias_ref[hd]
            if diagonal:
                row = lax.broadcasted_iota(jnp.int32, (SEL_T, SEL_T), 0)
                col = lax.broadcasted_iota(jnp.int32, (SEL_T, SEL_T), 1)
                s = jnp.where(col <= row, s, NEG)
            return s

        ahead = 3
        pending = [scores(hd) for hd in range(ahead)]
        for hd in range(4):
            if hd + ahead < 4:
                pending.append(scores(hd + ahead))
            _flash_update(pending.pop(0), v_ref[:, (hd // 2) * LANES:(hd // 2 + 1) * LANES], m_ref, acc_ref, hd)

    @pl.when(kj < qi)
    def _():
        step(False)

    @pl.when(kj == qi)
    def _():
        step(True)
        _flash_finish(acc_ref, o_ref, 4)


def _causal_pairs(n):
    qi, kj = [], []
    for i in range(n):
        for j in range(i + 1):
            qi.append(i)
            kj.append(j)
    return jnp.asarray(np.array(qi, np.int32)), jnp.asarray(np.array(kj, np.int32))


def _sel_attention(pb, q_aug, penalty, bias_s, batch, seq):
    n_sel_pad = penalty.shape[1]
    n_delta = bias_s.shape[1]
    qi, kj = _causal_pairs(seq // SEL_T)
    grid_spec = pltpu.PrefetchScalarGridSpec(
        num_scalar_prefetch=2,
        grid=(batch, int(qi.shape[0])),
        in_specs=[
            pl.BlockSpec((None, SEL_T, q_aug.shape[2]), lambda b, p, qi, kj: (b, qi[p], 0)),
            pl.BlockSpec((None, SEL_T, 512), lambda b, p, qi, kj: (b, kj[p], PB_NSAKS // 512)),
            pl.BlockSpec((None, SEL_T, 256), lambda b, p, qi, kj: (b, kj[p], PB_NSAVS // 256)),
            pl.BlockSpec((SEL_T, n_sel_pad), lambda b, p, qi, kj: (kj[p], 0)),
            pl.BlockSpec((4, None, SEL_T, SEL_T),
                         lambda b, p, qi, kj: (0, jnp.minimum(qi[p] - kj[p], n_delta - 1), 0, 0)),
        ],
        out_specs=pl.BlockSpec((None, SEL_T, 256), lambda b, p, qi, kj: (b, qi[p], 0)),
        scratch_shapes=[pltpu.VMEM((4, SEL_T, LANES), F32), pltpu.VMEM((4, SEL_T, LANES), F32)],
    )
    return pl.pallas_call(
        _sel_kernel,
        out_shape=jax.ShapeDtypeStruct((batch, seq, 256), F32),
        grid_spec=grid_spec,
        compiler_params=pltpu.CompilerParams(dimension_semantics=("parallel", "arbitrary"),
                                             vmem_limit_bytes=VMEM_LIMIT),
        name="nsa_selected",
    )(qi, kj, q_aug, pb, pb, penalty, bias_s)


MLA_T = 512


def _mla_kernel(qi_ref, kj_ref, q_ref, k_ref, v_ref, o_ref, m_ref, acc_ref):
    p = pl.program_id(1)
    qi = qi_ref[p]
    kj = kj_ref[p]

    @pl.when(kj == 0)
    def _():
        _flash_init(m_ref, acc_ref)

    def step(diagonal):
        def scores(hd):
            cols = slice(hd * LANES, (hd + 1) * LANES)
            s = _dot_nt(q_ref[:, cols], k_ref[:, cols])
            if diagonal:
                row = lax.broadcasted_iota(jnp.int32, (MLA_T, MLA_T), 0)
                col = lax.broadcasted_iota(jnp.int32, (MLA_T, MLA_T), 1)
                s = jnp.where(col <= row, s, NEG)
            return s

        ahead = 3
        pending = [scores(hd) for hd in range(ahead)]
        for hd in range(MLA_HEADS):
            if hd + ahead < MLA_HEADS:
                pending.append(scores(hd + ahead))
            _flash_update(pending.pop(0), v_ref[:, hd * LANES:(hd + 1) * LANES], m_ref, acc_ref, hd)

    @pl.when(kj < qi)
    def _():
        step(False)

    @pl.when(kj == qi)
    def _():
        step(True)
        _flash_finish(acc_ref, o_ref, MLA_HEADS)


def _mla_attention(pb, batch, seq):
    qi, kj = _causal_pairs(seq // MLA_T)
    grid_spec = pltpu.PrefetchScalarGridSpec(
        num_scalar_prefetch=2,
        grid=(batch, int(qi.shape[0])),
        in_specs=[
            pl.BlockSpec((None, MLA_T, 512), lambda b, p, qi, kj: (b, qi[p], PB_MLAQ // 512)),
            pl.BlockSpec((None, MLA_T, 512), lambda b, p, qi, kj: (b, kj[p], PB_MLAK // 512)),
            pl.BlockSpec((None, MLA_T, 512), lambda b, p, qi, kj: (b, kj[p], PB_MLAV // 512)),
        ],
        out_specs=pl.BlockSpec((None, MLA_T, 256), lambda b, p, qi, kj: (b, qi[p], 0)),
        scratch_shapes=[pltpu.VMEM((4, MLA_T, LANES), F32), pltpu.VMEM((4, MLA_T, LANES), F32)],
    )
    return pl.pallas_call(
        _mla_kernel,
        out_shape=jax.ShapeDtypeStruct((batch, seq, 256), F32),
        grid_spec=grid_spec,
        compiler_params=pltpu.CompilerParams(dimension_semantics=("parallel", "arbitrary"),
                                             vmem_limit_bytes=VMEM_LIMIT),
        name="mla_causal",
    )(qi, kj, pb, pb, pb)


def _outproj_kernel(x_ref, oa_ref, ob_ref, oc_ref, os_ref, ow_ref, od_ref, gate_ref, gexp_ref, gn_ref, w_ref,
                    o_ref):
    gates = gate_ref[...]
    g_hi, g_lo = _split_bf16(gates)

    def gate(branch):
        e = gexp_ref[branch]
        return _dot(g_hi, e) + _dot(g_lo, e)

    o_nsa = gate(0) * oc_ref[...] + gate(1) * os_ref[...] + gate(2) * ow_ref[...]
    gn = gn_ref[...]
    y = x_ref[...]
    for grp, o in enumerate((oa_ref[...], ob_ref[...], o_nsa, od_ref[...])):
        cols = slice(grp * GROUP_WIDTH, (grp + 1) * GROUP_WIDTH)
        normed = _row_rms(o, GROUP_WIDTH) * gn[:, cols]
        y = y + _dot(normed.astype(BF16), w_ref[cols, :])
    o_ref[...] = y


def _outproj(x2, oa, ob, oc, osel, ow, od, pf, gexp, gn, w, tm):
    tokens = x2.shape[0]
    row = lambda t: (t, 0)
    o_spec = pl.BlockSpec((tm, GROUP_WIDTH), row)
    return pl.pallas_call(
        _outproj_kernel,
        out_shape=jax.ShapeDtypeStruct((tokens, D_MODEL), F32),
        grid=(tokens // tm,),
        in_specs=[
            pl.BlockSpec((tm, D_MODEL), row), o_spec, o_spec, o_spec, o_spec, o_spec, o_spec,
            pl.BlockSpec((tm, 128), lambda t: (t, PF_GATE // 128)),
            pl.BlockSpec((3, 128, GROUP_WIDTH), lambda t: (0, 0, 0)),
            pl.BlockSpec((1, D_MODEL), lambda t: (0, 0)),
            pl.BlockSpec((D_MODEL, D_MODEL), lambda t: (0, 0)),
        ],
        out_specs=pl.BlockSpec((tm, D_MODEL), row),
        compiler_params=pltpu.CompilerParams(dimension_semantics=("parallel",), vmem_limit_bytes=VMEM_LIMIT),
        name="out_proj",
    )(x2, oa, ob, oc, osel, ow, od, pf, gexp, gn, w)


MOE_TM = 1024
MOE_QUAD = 4
MOE_CHUNK = 256
MOE_SLOTS = MOE_TM + MOE_GROUPS * MOE_CHUNK
MOE_VMEM_LIMIT = 60 * 1024 * 1024


def _moe_kernel(x_ref, fn_ref, wr_hi_ref, wr_lo_ref, br_ref, cexp_ref, wg_ref, wu_ref, wd_ref, o_ref, hs_ref,
                cs_ref, ys_ref, slot_ref, start_ref):
    step = pl.program_id(1)
    lane = lax.broadcasted_iota(jnp.int32, (MOE_TM, LANES), 1)

    @pl.when(step == 0)
    def _():
        h = _row_rms(x_ref[...], D_MODEL) * fn_ref[...]
        h_hi, h_lo = _split_bf16(h)
        logits = (_dot(h_hi, wr_hi_ref[...]) + _dot(h_lo, wr_hi_ref[...]) + _dot(h_hi, wr_lo_ref[...])
                  + br_ref[...])
        lane_f = lane.astype(F32)
        no_lane = float(LANES)
        is_group = lane < MOE_GROUPS
        g_max = jnp.max(jnp.where(is_group, logits, -jnp.inf), axis=-1, keepdims=True)
        g_star = jnp.min(jnp.where(is_group & (logits == g_max), lane_f, no_lane), axis=-1, keepdims=True)
        g_den = jnp.sum(jnp.where(is_group, jnp.exp(logits - g_max), 0.0), axis=-1, keepdims=True)
        g_w = 1.0 / g_den
        group_of_lane = ((lane - MOE_GROUPS) >> int(math.log2(MOE_EPG))).astype(F32)
        in_group = (lane >= MOE_GROUPS) & (lane < MOE_GROUPS + MOE_EXPERTS) & (group_of_lane == g_star)
        e_l = jnp.where(in_group, logits, -jnp.inf)
        top1 = jnp.max(e_l, axis=-1, keepdims=True)
        i1 = jnp.min(jnp.where(e_l == top1, lane_f, no_lane), axis=-1, keepdims=True)
        e_l2 = jnp.where(lane_f == i1, -jnp.inf, e_l)
        top2 = jnp.max(e_l2, axis=-1, keepdims=True)
        i2 = jnp.min(jnp.where(e_l2 == top2, lane_f, no_lane), axis=-1, keepdims=True)
        r = jnp.exp(top2 - top1)
        w1 = g_w / (1.0 + r)
        w2 = g_w * r / (1.0 + r)
        comb = jnp.where(lane_f == i1, w1, jnp.where(lane_f == i2, w2, 0.0))

        onehot = jnp.where(lane_f == g_star, 1.0, 0.0)
        row = lax.broadcasted_iota(jnp.int32, (MOE_TM, LANES), 0)
        incl = onehot
        shift = 1
        while shift < MOE_TM:
            incl = incl + jnp.where(row >= shift, pltpu.roll(incl, shift, 0), 0.0)
            shift *= 2
        counts = incl[MOE_TM - 1:MOE_TM, :]
        padded = jnp.floor((counts + (MOE_CHUNK - 1)) * (1.0 / MOE_CHUNK)) * MOE_CHUNK
        before = (lax.broadcasted_iota(jnp.int32, (LANES, LANES), 0)
                  < lax.broadcasted_iota(jnp.int32, (LANES, LANES), 1))
        starts = _dot(jnp.broadcast_to(padded, (8, LANES)).astype(BF16),
                      jnp.where(before, 1.0, 0.0).astype(BF16))[0:1, :]
        for g in range(MOE_GROUPS + 1):
            start_ref[g] = jnp.sum(jnp.where(lane[0:1, :] == g, starts, 0.0)).astype(jnp.int32)
        slot = jnp.sum(onehot * (starts + incl - onehot), axis=-1, keepdims=True)
        slot_ref[...] = jnp.broadcast_to(slot, (MOE_TM, LANES))
        slot_row = slot_ref[...].T[0:1, :]
        n_slots = hs_ref.shape[0]
        place = jnp.where(lax.broadcasted_iota(jnp.int32, (n_slots, MOE_TM), 0).astype(F32) == slot_row,
                          1.0, 0.0).astype(BF16)
        hs_ref[...] = _dot(place, h.astype(BF16)).astype(BF16)
        c_hi, c_lo = _split_bf16(comb)
        cs_ref[:, :LANES] = _dot(place, c_hi).astype(BF16)
        cs_ref[:, LANES:] = _dot(place, c_lo).astype(BF16)
        ys_ref[...] = jnp.zeros_like(ys_ref)

    group = step // (MOE_EPG // MOE_QUAD)
    first_slot = start_ref[group]
    half = MOE_QUAD * MOE_HIDDEN // 2

    def chunk(c, carry):
        rows = pl.ds(pl.multiple_of(first_slot + c * MOE_CHUNK, MOE_CHUNK), MOE_CHUNK)
        hb = hs_ref[rows, :]
        weight = _dot(cs_ref[rows, :], cexp_ref[...])
        y = None
        for s in range(2):
            cols = slice(s * half, (s + 1) * half)
            a = jax.nn.silu(_dot(hb, wg_ref[:, cols])) * _dot(hb, wu_ref[:, cols]) * weight[:, cols]
            part = _dot(a.astype(BF16), wd_ref[cols, :])
            y = part if y is None else y + part
        ys_ref[rows, :] += y
        return carry

    lax.fori_loop(0, (start_ref[group + 1] - first_slot) // MOE_CHUNK, chunk, 0)

    @pl.when(step == MOE_EXPERTS // MOE_QUAD - 1)
    def _():
        n_slots = hs_ref.shape[0]
        back = jnp.where(lax.broadcasted_iota(jnp.int32, (MOE_TM, n_slots), 1).astype(F32) == slot_ref[:, 0:1],
                         1.0, 0.0).astype(BF16)
        y_hi, y_lo = _split_bf16(ys_ref[...])
        o_ref[...] = x_ref[...] + _dot(back, y_hi) + _dot(back, y_lo)


def _moe(x2, fn, wr_hi, wr_lo, br, wg, wu, wd):
    tokens = x2.shape[0]
    quads = MOE_EXPERTS // MOE_QUAD
    width = MOE_QUAD * MOE_HIDDEN
    const = lambda t, e: (0, 0)
    lane_of_col = MOE_GROUPS + np.arange(quads)[:, None, None] * MOE_QUAD + np.arange(width)[None, None, :] // MOE_HIDDEN
    cexp = jnp.asarray((np.arange(2 * LANES)[None, :, None] % LANES) == lane_of_col, BF16)
    return pl.pallas_call(
        _moe_kernel,
        out_shape=jax.ShapeDtypeStruct((tokens, D_MODEL), F32),
        grid=(tokens // MOE_TM, quads),
        in_specs=[
            pl.BlockSpec((MOE_TM, D_MODEL), lambda t, e: (t, 0)),
            pl.BlockSpec((1, D_MODEL), const),
            pl.BlockSpec((D_MODEL, LANES), const),
            pl.BlockSpec((D_MODEL, LANES), const),
            pl.BlockSpec((1, LANES), const),
            pl.BlockSpec((None, 2 * LANES, width), lambda t, e: (e, 0, 0)),
            pl.BlockSpec((None, D_MODEL, width), lambda t, e: (e, 0, 0)),
            pl.BlockSpec((None, D_MODEL, width), lambda t, e: (e, 0, 0)),
            pl.BlockSpec((None, width, D_MODEL), lambda t, e: (e, 0, 0)),
        ],
        out_specs=pl.BlockSpec((MOE_TM, D_MODEL), lambda t, e: (t, 0)),
        scratch_shapes=[pltpu.VMEM((MOE_SLOTS, D_MODEL), BF16), pltpu.VMEM((MOE_SLOTS, 2 * LANES), BF16),
                        pltpu.VMEM((MOE_SLOTS, D_MODEL), F32), pltpu.VMEM((MOE_TM, LANES), F32),
                        pltpu.SMEM((8,), jnp.int32)],
        compiler_params=pltpu.CompilerParams(dimension_semantics=("parallel", "arbitrary"),
                                             vmem_limit_bytes=MOE_VMEM_LIMIT),
        name="hier_moe",
    )(x2, fn, wr_hi, wr_lo, br, cexp, wg, wu, wd)


def _t5_bucket(dist):
    n = jnp.maximum(dist, 0)
    max_exact = T5_BUCKETS // 2
    nf = jnp.maximum(n, 1).astype(F32)
    large = max_exact + (jnp.log(nf / max_exact) / math.log(T5_MAX_DIST / max_exact)
                         * (T5_BUCKETS - max_exact)).astype(jnp.int32)
    large = jnp.minimum(large, T5_BUCKETS - 1)
    return jnp.where(n < max_exact, n, large)


def _sel_delta_cap():
    max_exact = T5_BUCKETS // 2
    span = T5_BUCKETS - max_exact
    last_bucket_from = max_exact * (T5_MAX_DIST / max_exact) ** ((span - 1) / span)
    cap = 1
    while (cap - 1) * SEL_T + 1 < 1.25 * last_bucket_from:
        cap += 1
    return cap


def _position_tables(rel_bias, seq):
    buckets = _t5_bucket(jnp.arange(seq))
    first = jnp.sum(buckets[None, :] < jnp.arange(T5_BUCKETS)[:, None], axis=1)
    tbl = rel_bias.T

    def toeplitz(heads, dist):
        shape = (tbl[heads].shape[0],) + (1,) * dist.ndim
        out = jnp.broadcast_to(tbl[heads][:, 0].reshape(shape), shape[:1] + dist.shape)
        for b in range(1, T5_BUCKETS):
            out = jnp.where((dist >= first[b])[None], tbl[heads][:, b].reshape(shape), out)
        return out

    swa_h, nsa_h = slice(0, 4), slice(4, 8)
    def band(window):
        pad, tq = _band_tiles(window)
        return jnp.arange(tq)[:, None] + pad - jnp.arange(pad + tq)[None, :]

    bias_swa = toeplitz(swa_h, band(SWA_WINDOW))
    bias_win = toeplitz(nsa_h, band(NSA_WINDOW))
    ncp = seq // NSA_CMP_STRIDE
    cmp_end = jnp.arange(ncp) * NSA_CMP_STRIDE + NSA_CMP_LEN - 1
    bias_cmp = toeplitz(nsa_h, jnp.arange(seq)[:, None] - cmp_end[None, :])
    nd = min(_sel_delta_cap() + 1, seq // SEL_T)
    dist_s = (jnp.arange(nd)[:, None, None] * SEL_T + jnp.arange(SEL_T)[None, :, None]
              - jnp.arange(SEL_T)[None, None, :])
    bias_sel = toeplitz(nsa_h, dist_s)

    n_sel_pad = -(-(seq // NSA_SEL_LEN) // LANES) * LANES
    sel_start = np.arange(n_sel_pad) * NSA_SEL_LEN
    c_start = np.arange(ncp) * NSA_CMP_STRIDE
    c_end = c_start + NSA_CMP_LEN - 1
    real = (np.arange(ncp) < ncp - NSA_CMP_LEN // NSA_CMP_STRIDE + 1)[:, None] & (sel_start < seq)[None, :]
    overlap = ((c_start[:, None] < sel_start[None, :] + NSA_SEL_LEN) & (c_end[:, None] >= sel_start[None, :]) & real)
    own_block = (np.arange(seq) // NSA_SEL_LEN)[:, None] == np.arange(n_sel_pad)[None, :]
    penalty = np.where(own_block, -2.0 ** 100, 0.0)

    pos = jnp.arange(seq, dtype=F32)
    inv_freq = ROPE_THETA ** (-jnp.arange(0, MLA_ROPE, 2, dtype=F32) / MLA_ROPE)
    ang = pos[:, None] * inv_freq[None, :]
    cos, sin = jnp.cos(ang), jnp.sin(ang)
    ones = jnp.ones((seq, MLA_NOPE), F32)
    tail = LANES - MLA_NOPE - MLA_ROPE
    cos_t = jnp.concatenate([ones, cos, cos, jnp.ones((seq, tail), F32)], axis=1)
    sin_t = jnp.concatenate([0 * ones, -sin, sin, jnp.zeros((seq, tail), F32)], axis=1)
    return dict(bias_swa=bias_swa, bias_win=bias_win, bias_cmp=bias_cmp, bias_sel=bias_sel,
                overlap=jnp.asarray(overlap, BF16), penalty=jnp.asarray(penalty, BF16), cos_t=cos_t, sin_t=sin_t)


def _pad_to(a, shape):
    return jnp.pad(a, [(0, s - d) for d, s in zip(a.shape, shape)])


def _pack_layer(w_in, swa_q_norm, swa_k_norm, nsa_q_norm, nsa_k_norm, mla_q_lat_norm, mla_w_q_up,
                mla_kv_lat_norm, mla_w_kv_up, mla_q_norm, mla_k_norm):
    kpe = w_in[:, 2636:2668]
    kpe_seg = jnp.concatenate([jnp.zeros((D_MODEL, MLA_NOPE), F32), kpe,
                               jnp.zeros((D_MODEL, LANES - MLA_QK), F32)], axis=1)
    spread = lambda cols: _pad_to(cols.reshape(D_MODEL, -1, HEAD_DIM), (D_MODEL, cols.shape[1] // HEAD_DIM, LANES)
                                  ).reshape(D_MODEL, -1)
    w = jnp.concatenate([
        w_in[:, :1920],
        spread(w_in[:, 1920:2048]),
        w_in[:, 2048:2304],
        _pad_to(w_in[:, 2304:2316], (D_MODEL, 128)),
        _pad_to(w_in[:, 2316:2508], (D_MODEL, 256)),
        w_in[:, 2508:2636],
        jnp.tile(kpe_seg, (1, MLA_HEADS)),
    ], axis=1).astype(BF16)
    tile4 = lambda g: jnp.tile(g, 4)
    g64 = _pad_to(jnp.stack([tile4(swa_q_norm), tile4(swa_k_norm), tile4(nsa_q_norm),
                             tile4(nsa_k_norm[1]), tile4(nsa_k_norm[2])]), (8, 256))
    glat = _pad_to(jnp.stack([_pad_to(mla_q_lat_norm, (256,)), _pad_to(mla_kv_lat_norm, (256,))]), (8, 256))
    wq = _pad_to(mla_w_q_up.reshape(MLA_Q_RANK, MLA_HEADS, MLA_QK), (256, MLA_HEADS, LANES))
    wq = wq.reshape(256, MLA_HEADS * LANES).astype(BF16)
    wkv = mla_w_kv_up.reshape(MLA_KV_RANK, MLA_HEADS, MLA_NOPE + MLA_V)
    wk = _pad_to(wkv[:, :, :MLA_NOPE], (MLA_KV_RANK, MLA_HEADS, LANES)).reshape(MLA_KV_RANK, MLA_HEADS * LANES)
    wv = _pad_to(wkv[:, :, MLA_NOPE:], (MLA_KV_RANK, MLA_HEADS, LANES)).reshape(MLA_KV_RANK, MLA_HEADS * LANES)
    wkv_p = jnp.concatenate([wk, wv], axis=1).astype(BF16)
    gmla = _pad_to(jnp.stack([jnp.tile(_pad_to(mla_q_norm, (LANES,)), MLA_HEADS),
                              jnp.tile(_pad_to(mla_k_norm, (LANES,)), MLA_HEADS)]), (8, 512))
    return w, g64, glat, wq, wkv_p, gmla


def _gate_expand():
    rows = np.arange(128)[None, :, None]
    cols = np.arange(GROUP_WIDTH)[None, None, :]
    branch = np.arange(3)[:, None, None]
    return jnp.asarray(rows == branch * 4 + cols // HEAD_DIM, BF16)


def _compress_rows(pf3, batch, seq):
    nb = seq // NSA_CMP_STRIDE
    kv = pf3[:, :, :256].reshape(batch, seq, 2, 2, HEAD_DIM).transpose(2, 0, 3, 1, 4)
    blocks = kv.reshape(2, batch * 2, nb, NSA_CMP_STRIDE * HEAD_DIM)
    nxt = jnp.concatenate([blocks[:, :, 1:], jnp.zeros_like(blocks[:, :, :1])], axis=2)
    return jnp.concatenate([blocks, nxt], axis=3)


def kernel(x, rel_bias, attn_norm, w_in, swa_q_norm, swa_k_norm, swa_sinks, nsa_q_norm, nsa_k_norm, nsa_cmp_pos, nsa_cmp_w1, nsa_cmp_w2, mla_q_lat_norm, mla_w_q_up, mla_kv_lat_norm, mla_w_kv_up, mla_q_norm, mla_k_norm, out_norm, w_out, ffn_norm, moe_w_group, moe_b_group, moe_w_expert, moe_b_expert, moe_w_gate, moe_w_up, moe_w_down):
    batch, seq, _ = x.shape
    depth = w_in.shape[0]
    tokens = batch * seq
    tm = 512
    assert seq % 2048 == 0 and tokens % MOE_TM == 0
    tabs = _position_tables(rel_bias, seq)
    gexp = _gate_expand()
    zero_sinks = jnp.zeros((4,), F32)
    x2 = x.reshape(tokens, D_MODEL)
    for l in range(depth):
        w, g64, glat, wq, wkv, gmla = _pack_layer(
            w_in[l], swa_q_norm[l], swa_k_norm[l], nsa_q_norm[l], nsa_k_norm[l], mla_q_lat_norm[l],
            mla_w_q_up[l], mla_kv_lat_norm[l], mla_w_kv_up[l], mla_q_norm[l], mla_k_norm[l])
        pb, pf = _prep(x2, attn_norm[l][None, :], w, g64, glat, wq, wkv, gmla, tabs["cos_t"], tabs["sin_t"],
                       seq, tm)
        pb3 = pb.reshape(batch, seq, PB_WIDTH)
        pf3 = pf.reshape(batch, seq, PF_WIDTH)
        o_a = _sb_attention(pb3, batch, seq)
        o_b = _banded_attention(pb3, swa_sinks[l], tabs["bias_swa"], batch, seq, SWA_WINDOW,
                                PB_SWAQ, PB_SWAK, PB_SWAV, True)
        rows = _compress_rows(pf3, batch, seq)
        kvc = _compress(rows, nsa_cmp_pos[l].reshape(2, 1, -1),
                        nsa_cmp_w1[l].reshape(2, -1, NSA_CMP_HIDDEN).astype(BF16), nsa_cmp_w2[l].astype(BF16),
                        nsa_k_norm[l][0][None, :], 128)
        o_c, q_aug = _cmp_attention(pb3, kvc, tabs["bias_cmp"], tabs["overlap"], batch, seq)
        o_s = _sel_attention(pb3, q_aug, tabs["penalty"], tabs["bias_sel"], batch, seq)
        o_w = _banded_attention(pb3, zero_sinks, tabs["bias_win"], batch, seq, NSA_WINDOW,
                                PB_NSAQ, PB_NSAKW, PB_NSAVW, False)
        o_d = _mla_attention(pb3, batch, seq)
        flat = lambda o: o.reshape(tokens, GROUP_WIDTH)
        x2 = _outproj(x2, flat(o_a), flat(o_b), flat(o_c), flat(o_s), flat(o_w), flat(o_d), pf, gexp,
                      out_norm[l][None, :], w_out[l].astype(BF16), tm)
        w_router = _pad_to(jnp.concatenate([moe_w_group[l], moe_w_expert[l]], axis=1), (D_MODEL, LANES))
        wr_hi = w_router.astype(BF16)
        wr_lo = (w_router - wr_hi.astype(F32)).astype(BF16)
        b_router = _pad_to(jnp.concatenate([moe_b_group[l], moe_b_expert[l]])[None, :], (1, LANES))
        quads = MOE_EXPERTS // MOE_QUAD
        by_quad = lambda w: w.astype(BF16).reshape(quads, MOE_QUAD, D_MODEL, MOE_HIDDEN).transpose(0, 2, 1, 3
                                                   ).reshape(quads, D_MODEL, MOE_QUAD * MOE_HIDDEN)
        x2 = _moe(x2, ffn_norm[l][None, :], wr_hi, wr_lo, b_router, by_quad(moe_w_gate[l]), by_quad(moe_w_up[l]),
                  moe_w_down[l].astype(BF16).reshape(quads, MOE_QUAD * MOE_HIDDEN, D_MODEL))
    return x2.reshape(batch, seq, D_MODEL)
```

```python
import functools
import math

import numpy as np
import jax
import jax.numpy as jnp
from jax import lax
from jax.experimental import pallas as pl
from jax.experimental.pallas import tpu as pltpu

F32 = jnp.float32
BF16 = jnp.bfloat16

D_MODEL = 1024
HEAD_DIM = 64
NEG = -1e30
EPS = 1e-6
FORCE_BONUS = 1000.0
SWA_WINDOW = 128
NSA_CMP_LEN = 32
NSA_CMP_STRIDE = 16
NSA_CMP_HIDDEN = 128
NSA_SEL_LEN = 64
NSA_TOPK = 16
NSA_WINDOW = 512
MLA_HEADS = 4
MLA_NOPE = 64
MLA_ROPE = 32
MLA_V = 64
MLA_Q_RANK = 192
MLA_KV_RANK = 128
MLA_QK = MLA_NOPE + MLA_ROPE
ROPE_THETA = 10000.0
T5_BUCKETS = 32
T5_MAX_DIST = 1024
MOE_GROUPS = 4
MOE_EPG = 8
MOE_EXPERTS = MOE_GROUPS * MOE_EPG
MOE_HIDDEN = 256
GROUP_WIDTH = 256
LANES = 128
VMEM_LIMIT = 48 * 1024 * 1024

PB_MLAQ, PB_MLAK, PB_MLAV = 0, 512, 1024
PB_SBQ, PB_SBK, PB_SBV = 1536, 1792, 2048
PB_SWAQ, PB_NSAQ, PB_NSAVS = 2304, 2560, 2816
PB_SWAK, PB_SWAV, PB_NSAKW, PB_NSAVW = 3072, 3200, 3328, 3456
PB_NSAKS = 3584
PB_WIDTH = 4096
PF_KC, PF_VC, PF_GATE = 0, 128, 256
PF_WIDTH = 384
W_SBQ, W_SBK, W_SBV, W_SWAQ, W_SWAK, W_SWAV, W_NSAQ = 0, 256, 512, 768, 1024, 1152, 1280
W_KC, W_VC, W_KS, W_VS, W_KW, W_VW, W_GATE = 1536, 1664, 1792, 1920, 2176, 2304, 2432
W_CQ, W_CKV, W_KPE = 2560, 2816, 2944
W_WIDTH = 3456

NT_DIMS = (((1,), (1,)), ((), ()))


def _dot(a, b):
    return jnp.dot(a, b, preferred_element_type=F32)


def _dot_nt(a, b):
    return lax.dot_general(a, b, NT_DIMS, preferred_element_type=F32)


def _split_bf16(x):
    hi = x.astype(BF16)
    lo = (x - hi.astype(F32)).astype(BF16)
    return hi, lo


def _dot_exact_rhs(x, m):
    hi, lo = _split_bf16(x)
    return _dot(hi, m) + _dot(lo, m)


def _block_diag_ones(width, seg):
    idx = np.arange(width) // seg
    return jnp.asarray(idx[:, None] == idx[None, :], BF16)


def _seg_rms(x, seg_ones, count):
    width = x.shape[1]
    ms = _dot_exact_rhs(x * x, seg_ones[:width, :width]) * (1.0 / count)
    return x * lax.rsqrt(ms + EPS)


def _row_rms(x, count):
    return x * lax.rsqrt(jnp.sum(x * x, axis=-1, keepdims=True) * (1.0 / count) + EPS)


def _prep_kernel(x_ref, an_ref, w_ref, g64_ref, glat_ref, wq_ref, wkv_ref, gmla_ref, cos_ref, sin_ref,
                 s64_ref, s128_ref, pb_ref, pf_ref):
    x = x_ref[...]
    s64 = s64_ref[...]
    s128 = s128_ref[...]
    h = _row_rms(x, D_MODEL) * an_ref[...]
    hb = h.astype(BF16)

    def proj(lo, hi):
        return _dot(hb, w_ref[:, lo:hi])

    def put(col, value):
        pb_ref[:, col:col + value.shape[1]] = value.astype(BF16)

    def ones_tail(width):
        lane = lax.broadcasted_iota(jnp.int32, (1, width), 1)
        return jnp.where((lane & (LANES - 1)) >= HEAD_DIM, 1.0, 0.0)

    scale = HEAD_DIM ** -0.5
    g64 = g64_ref[...]
    put(PB_SBQ, proj(W_SBQ, W_SBQ + 256) * scale)
    put(PB_SBK, proj(W_SBK, W_SBK + 256))
    put(PB_SBV, proj(W_SBV, W_SBV + 256))
    put(PB_SWAQ, _seg_rms(proj(W_SWAQ, W_SWAQ + 256), s64, 64) * g64[0:1, :] * scale)
    put(PB_SWAK, _seg_rms(proj(W_SWAK, W_SWAK + 128), s64, 64) * g64[1:2, :128])
    put(PB_SWAV, proj(W_SWAV, W_SWAV + 128))
    put(PB_NSAQ, _seg_rms(proj(W_NSAQ, W_NSAQ + 256), s64, 64) * g64[2:3, :] * scale)
    pf_ref[:, PF_KC:PF_KC + 128] = proj(W_KC, W_KC + 128)
    pf_ref[:, PF_VC:PF_VC + 128] = proj(W_VC, W_VC + 128)
    ks = _seg_rms(proj(W_KS, W_KS + 128), s64, 64) * g64[3:4, :128]
    ks_swapped = pltpu.roll(ks, HEAD_DIM, 1)
    low = lax.broadcasted_iota(jnp.int32, ks.shape, 1) < HEAD_DIM
    put(PB_NSAKS, jnp.where(low, ks, 0.0))
    put(PB_NSAKS + LANES, jnp.where(low, 0.0, ks_swapped))
    put(PB_NSAKS + 2 * LANES, jnp.where(low, ks_swapped, 0.0))
    put(PB_NSAKS + 3 * LANES, jnp.where(low, 0.0, ks))
    put(PB_NSAVS, proj(W_VS, W_VS + 256) + ones_tail(256))
    put(PB_NSAKW, _seg_rms(proj(W_KW, W_KW + 128), s64, 64) * g64[4:5, :128])
    put(PB_NSAVW, proj(W_VW, W_VW + 128))
    pf_ref[:, PF_GATE:PF_GATE + 128] = jax.nn.sigmoid(proj(W_GATE, W_GATE + 128))

    glat = glat_ref[...]
    cq = _row_rms(proj(W_CQ, W_CQ + 256), MLA_Q_RANK) * glat[0:1, :]
    q = _dot(cq.astype(BF16), wq_ref[...])
    ckv = _row_rms(proj(W_CKV, W_CKV + 128), MLA_KV_RANK) * glat[1:2, :128]
    kv = _dot(ckv.astype(BF16), wkv_ref[...])
    k = kv[:, :512] + proj(W_KPE, W_KPE + 512)
    gm = gmla_ref[...]
    q = _seg_rms(q, s128, MLA_QK) * gm[0:1, :]
    k = _seg_rms(k, s128, MLA_QK) * gm[1:2, :]
    cos = cos_ref[...]
    sin = sin_ref[...]
    lane = lax.broadcasted_iota(jnp.int32, (x.shape[0], LANES), 1)
    first_half = lane < MLA_NOPE + MLA_ROPE // 2

    def rope(t):
        partner = jnp.where(first_half, pltpu.roll(t, LANES - MLA_ROPE // 2, 1), pltpu.roll(t, MLA_ROPE // 2, 1))
        return t * cos + partner * sin

    qscale = MLA_QK ** -0.5
    for hd in range(MLA_HEADS):
        sl = slice(hd * LANES, (hd + 1) * LANES)
        put(PB_MLAQ + hd * LANES, rope(q[:, sl]) * qscale)
        put(PB_MLAK + hd * LANES, rope(k[:, sl]))
    put(PB_MLAV, kv[:, 512:1024] + ones_tail(512))


def _prep(x2, an, w, g64, glat, wq, wkv, gmla, cos_t, sin_t, seq, tm):
    tokens = x2.shape[0]
    n_pos = seq // tm
    const = lambda t: (0, 0)
    return pl.pallas_call(
        _prep_kernel,
        out_shape=(jax.ShapeDtypeStruct((tokens, PB_WIDTH), BF16), jax.ShapeDtypeStruct((tokens, PF_WIDTH), F32)),
        grid=(tokens // tm,),
        in_specs=[
            pl.BlockSpec((tm, D_MODEL), lambda t: (t, 0)),
            pl.BlockSpec((1, D_MODEL), const),
            pl.BlockSpec((D_MODEL, W_WIDTH), const),
            pl.BlockSpec((8, 256), const),
            pl.BlockSpec((8, 256), const),
            pl.BlockSpec((256, 512), const),
            pl.BlockSpec((128, 1024), const),
            pl.BlockSpec((8, 512), const),
            pl.BlockSpec((tm, LANES), lambda t: (t % n_pos, 0)),
            pl.BlockSpec((tm, LANES), lambda t: (t % n_pos, 0)),
            pl.BlockSpec((256, 256), const),
            pl.BlockSpec((512, 512), const),
        ],
        out_specs=(pl.BlockSpec((tm, PB_WIDTH), lambda t: (t, 0)), pl.BlockSpec((tm, PF_WIDTH), lambda t: (t, 0))),
        compiler_params=pltpu.CompilerParams(dimension_semantics=("parallel",), vmem_limit_bytes=VMEM_LIMIT),
        name="prep",
    )(x2, an, w, g64, glat, wq, wkv, gmla, cos_t, sin_t, _block_diag_ones(256, HEAD_DIM),
      _block_diag_ones(512, LANES))


SB_TQ, SB_KB, SB_SUB = 256, 2048, 128


SB_UNDERFLOW = 110.0


def _sb_kernel(qi_ref, kj_ref, kmax_ref, q_ref, k_ref, v_ref, sums_ref, o_ref, carry_ref, acc_ref, zb_ref,
               dead_ref):
    b = pl.program_id(0)
    p = pl.program_id(1)
    qi = qi_ref[p]
    kj = kj_ref[p]
    q_start = qi * SB_TQ
    first = kj == (q_start + SB_TQ - 1) // SB_KB

    @pl.when(first)
    def _():
        carry_ref[...] = jnp.zeros_like(carry_ref)
        acc_ref[...] = jnp.zeros_like(acc_ref)
        dead_ref[0] = 0
        ones = jnp.ones((HEAD_DIM, SB_SUB), BF16)
        for hd in range(4):
            q = q_ref[:, hd * HEAD_DIM:(hd + 1) * HEAD_DIM].astype(F32)
            zb_ref[hd * SB_TQ:(hd + 1) * SB_TQ, :] = (jnp.sqrt(_dot_exact_rhs(q * q, ones))
                                                      * (kmax_ref[b * 4 + hd] * 1.01))

    rel = (lax.broadcasted_iota(jnp.int32, (SB_TQ, SB_SUB), 1)
           - lax.broadcasted_iota(jnp.int32, (SB_TQ, SB_SUB), 0))

    def sub_tile(u, k_start):
        mask = jnp.tile(rel < q_start - k_start, (4, 1))
        rows = slice(u * SB_SUB, (u + 1) * SB_SUB)
        head = lambda hd: slice(hd * HEAD_DIM, (hd + 1) * HEAD_DIM)
        z = jnp.concatenate([_dot_nt(q_ref[:, head(hd)], k_ref[rows, head(hd)]) for hd in range(4)], axis=0)
        log_keep = jnp.where(mask, -(jnp.maximum(z, 0.0) + jnp.log(1.0 + jnp.exp(-jnp.abs(z)))), 0.0)
        hi, lo = _split_bf16(log_keep)
        sums = _dot(jnp.concatenate([hi, lo], axis=1), sums_ref[...])
        carry = carry_ref[...]
        a = jnp.where(mask, jnp.exp(z + sums[:, :SB_SUB] + carry), 0.0).astype(BF16)
        for hd in range(4):
            acc_ref[hd] += _dot(a[hd * SB_TQ:(hd + 1) * SB_TQ], v_ref[rows, head(hd)])
        carry_ref[...] = carry + sums[:, SB_SUB:]

    for u in reversed(range(SB_KB // SB_SUB)):
        k_start = kj * SB_KB + u * SB_SUB

        @pl.when((k_start < q_start + SB_TQ) & (dead_ref[0] == 0))
        def _(u=u, k_start=k_start):
            live = jnp.max(carry_ref[...] + zb_ref[...]) > -SB_UNDERFLOW

            @pl.when(live)
            def _():
                sub_tile(u, k_start)

            @pl.when(jnp.logical_not(live))
            def _():
                dead_ref[0] = 1

    @pl.when(kj == 0)
    def _():
        for hd in range(4):
            o_ref[:, hd * HEAD_DIM:(hd + 1) * HEAD_DIM] = acc_ref[hd]


def _sb_attention(pb, batch, seq):
    nq = seq // SB_TQ
    qi, kj = [], []
    for i in range(nq):
        for j in reversed(range((i * SB_TQ + SB_TQ - 1) // SB_KB + 1)):
            qi.append(i)
            kj.append(j)
    qi = jnp.asarray(np.array(qi, np.int32))
    kj = jnp.asarray(np.array(kj, np.int32))
    keys = pb[:, :, PB_SBK:PB_SBK + 256].astype(F32).reshape(batch, seq, 4, HEAD_DIM)
    kmax = jnp.sqrt(jnp.max(jnp.sum(keys * keys, axis=-1), axis=1)).reshape(batch * 4)
    j = np.arange(2 * SB_SUB)[:, None] % SB_SUB
    s = np.arange(2 * SB_SUB)[None, :]
    sums = jnp.asarray((s >= SB_SUB) | (j >= s), BF16)
    grid_spec = pltpu.PrefetchScalarGridSpec(
        num_scalar_prefetch=2,
        grid=(batch, int(qi.shape[0])),
        in_specs=[
            pl.BlockSpec(memory_space=pltpu.SMEM),
            pl.BlockSpec((None, SB_TQ, 256), lambda b, p, qi, kj: (b, qi[p], PB_SBQ // 256)),
            pl.BlockSpec((None, SB_KB, 256), lambda b, p, qi, kj: (b, kj[p], PB_SBK // 256)),
            pl.BlockSpec((None, SB_KB, 256), lambda b, p, qi, kj: (b, kj[p], PB_SBV // 256)),
            pl.BlockSpec((2 * SB_SUB, 2 * SB_SUB), lambda b, p, qi, kj: (0, 0)),
        ],
        out_specs=pl.BlockSpec((None, SB_TQ, 256), lambda b, p, qi, kj: (b, qi[p], 0)),
        scratch_shapes=[pltpu.VMEM((4 * SB_TQ, SB_SUB), F32), pltpu.VMEM((4, SB_TQ, HEAD_DIM), F32),
                        pltpu.VMEM((4 * SB_TQ, SB_SUB), F32), pltpu.SMEM((1,), jnp.int32)],
    )
    return pl.pallas_call(
        _sb_kernel,
        out_shape=jax.ShapeDtypeStruct((batch, seq, 256), F32),
        grid_spec=grid_spec,
        compiler_params=pltpu.CompilerParams(dimension_semantics=("parallel", "arbitrary"),
                                             vmem_limit_bytes=VMEM_LIMIT),
        name="stick_breaking",
    )(qi, kj, kmax, pb, pb, pb, sums)


def _banded_kernel(sink_ref, q_ref, kp_ref, kc_ref, vp_ref, vc_ref, bias_ref, o_ref, *, tq, pad, window, use_sink):
    i = pl.program_id(1)
    dist_prev = (lax.broadcasted_iota(jnp.int32, (tq, pad), 0) + pad
                 - lax.broadcasted_iota(jnp.int32, (tq, pad), 1))
    dist_cur = lax.broadcasted_iota(jnp.int32, (tq, tq), 0) - lax.broadcasted_iota(jnp.int32, (tq, tq), 1)
    mask_prev = (dist_prev < window) & (i > 0)
    mask_cur = (dist_cur >= 0) & (dist_cur < window)
    def scores(hd):
        kcols = slice((hd // 2) * HEAD_DIM, (hd // 2 + 1) * HEAD_DIM)
        q = q_ref[:, hd * HEAD_DIM:(hd + 1) * HEAD_DIM]
        return (jnp.where(mask_prev, _dot_nt(q, kp_ref[:, kcols]) + bias_ref[hd, :, :pad], NEG),
                jnp.where(mask_cur, _dot_nt(q, kc_ref[:, kcols]) + bias_ref[hd, :, pad:], NEG))

    ahead = 2
    pending = [scores(hd) for hd in range(ahead)]
    for hd in range(4):
        if hd + ahead < 4:
            pending.append(scores(hd + ahead))
        s_prev, s_cur = pending.pop(0)
        cols = slice(hd * HEAD_DIM, (hd + 1) * HEAD_DIM)
        kcols = slice((hd // 2) * HEAD_DIM, (hd // 2 + 1) * HEAD_DIM)
        m = jnp.maximum(jnp.max(s_prev, axis=-1, keepdims=True), jnp.max(s_cur, axis=-1, keepdims=True))
        if use_sink:
            sink = sink_ref[hd]
            m = jnp.maximum(m, sink)
        p_prev = jnp.where(mask_prev, jnp.exp(s_prev - m), 0.0)
        p_cur = jnp.where(mask_cur, jnp.exp(s_cur - m), 0.0)
        denom = jnp.sum(p_prev, axis=-1, keepdims=True) + jnp.sum(p_cur, axis=-1, keepdims=True)
        if use_sink:
            denom = denom + jnp.exp(sink - m)
        o = _dot(p_prev.astype(BF16), vp_ref[:, kcols]) + _dot(p_cur.astype(BF16), vc_ref[:, kcols])
        o_ref[:, cols] = o / jnp.maximum(denom, 1e-30)


BAND_TQ = 256


def _band_tiles(window):
    pad = -(-(window - 1) // LANES) * LANES
    return pad, max(pad, BAND_TQ)


def _banded_attention(pb, sinks, bias, batch, seq, window, q_col, k_col, v_col, use_sink):
    pad, tq = _band_tiles(window)
    per = tq // pad
    prev = lambda i: jnp.maximum(i * per - 1, 0)
    grid_spec = pltpu.PrefetchScalarGridSpec(
        num_scalar_prefetch=1,
        grid=(batch, seq // tq),
        in_specs=[
            pl.BlockSpec((None, tq, 256), lambda b, i, s: (b, i, q_col // 256)),
            pl.BlockSpec((None, pad, 128), lambda b, i, s: (b, prev(i), k_col // 128)),
            pl.BlockSpec((None, tq, 128), lambda b, i, s: (b, i, k_col // 128)),
            pl.BlockSpec((None, pad, 128), lambda b, i, s: (b, prev(i), v_col // 128)),
            pl.BlockSpec((None, tq, 128), lambda b, i, s: (b, i, v_col // 128)),
            pl.BlockSpec((4, tq, pad + tq), lambda b, i, s: (0, 0, 0)),
        ],
        out_specs=pl.BlockSpec((None, tq, 256), lambda b, i, s: (b, i, 0)),
    )
    return pl.pallas_call(
        functools.partial(_banded_kernel, tq=tq, pad=pad, window=window, use_sink=use_sink),
        out_shape=jax.ShapeDtypeStruct((batch, seq, 256), F32),
        grid_spec=grid_spec,
        compiler_params=pltpu.CompilerParams(dimension_semantics=("parallel", "arbitrary"),
                                             vmem_limit_bytes=VMEM_LIMIT),
        name="banded_w%d" % window,
    )(sinks, pb, pb, pb, pb, pb, bias)


def _compress_kernel(rows_ref, pos_ref, w1_ref, w2_ref, g_ref, o_ref):
    win = rows_ref[...] + pos_ref[...]
    hid = jax.nn.gelu(_dot(win.astype(BF16), w1_ref[...]), approximate=True)
    out = _dot(hid.astype(BF16), w2_ref[...])
    normed = _row_rms(out, HEAD_DIM) * g_ref[...]
    o_ref[...] = jnp.where(pl.program_id(0) == 0, normed, out).astype(BF16)


def _compress(rows, pos, w1, w2, gain, tn):
    _, bh, ncp, width = rows.shape
    return pl.pallas_call(
        _compress_kernel,
        out_shape=jax.ShapeDtypeStruct((2, bh, ncp, HEAD_DIM), BF16),
        grid=(2, bh, ncp // tn),
        in_specs=[
            pl.BlockSpec((None, None, tn, width), lambda c, r, n: (c, r, n, 0)),
            pl.BlockSpec((None, 1, width), lambda c, r, n: (c, 0, 0)),
            pl.BlockSpec((None, width, NSA_CMP_HIDDEN), lambda c, r, n: (c, 0, 0)),
            pl.BlockSpec((None, NSA_CMP_HIDDEN, HEAD_DIM), lambda c, r, n: (c, 0, 0)),
            pl.BlockSpec((1, HEAD_DIM), lambda c, r, n: (0, 0)),
        ],
        out_specs=pl.BlockSpec((None, None, tn, HEAD_DIM), lambda c, r, n: (c, r, n, 0)),
        compiler_params=pltpu.CompilerParams(dimension_semantics=("parallel", "parallel", "parallel"),
                                             vmem_limit_bytes=VMEM_LIMIT),
        name="nsa_compress",
    )(rows, pos, w1, w2, gain)


CMP_TQ = 1024


def _cmp_kernel(q_ref, kc_ref, vc_ref, bias_ref, ov_ref, o_ref, qa_ref, *, ncp, n_sel_pad, topk):
    i = pl.program_id(2)
    q_pos = i * CMP_TQ + lax.broadcasted_iota(jnp.int32, (CMP_TQ, ncp), 0)
    cmp_end = lax.broadcasted_iota(jnp.int32, (CMP_TQ, ncp), 1) * NSA_CMP_STRIDE + (NSA_CMP_LEN - 1)
    mask = cmp_end <= q_pos
    kc = kc_ref[...]
    vc = vc_ref[...]
    p_sum = jnp.zeros((CMP_TQ, ncp), F32)
    for g in range(2):
        cols = slice(g * HEAD_DIM, (g + 1) * HEAD_DIM)
        s = jnp.where(mask, _dot_nt(q_ref[:, cols], kc) + bias_ref[g], NEG)
        m = jnp.max(s, axis=-1, keepdims=True)
        p = jnp.where(mask, jnp.exp(s - m), 0.0)
        p = p / jnp.maximum(jnp.sum(p, axis=-1, keepdims=True), 1e-30)
        o_ref[:, cols] = _dot(p.astype(BF16), vc)
        p_sum = p_sum + p
    imp = _dot_exact_rhs(p_sum, ov_ref[...])

    row_pos = i * CMP_TQ + lax.broadcasted_iota(jnp.int32, (CMP_TQ, n_sel_pad), 0)
    blk = lax.broadcasted_iota(jnp.int32, (CMP_TQ, n_sel_pad), 1)
    cur = row_pos >> int(math.log2(NSA_SEL_LEN))
    forced = (blk == 0) | (blk == cur) | (blk == cur - 1)
    valid = blk * NSA_SEL_LEN <= row_pos
    score = jnp.where(valid, imp + jnp.where(forced, FORCE_BONUS, 0.0), NEG)
    blk_f = blk.astype(F32)
    dropped = jnp.ones((CMP_TQ, n_sel_pad), F32)
    for _ in range(topk):
        best = jnp.max(score, axis=-1, keepdims=True)
        first = jnp.min(jnp.where(score == best, blk_f, float(n_sel_pad)), axis=-1, keepdims=True)
        hit = blk_f == first
        dropped = jnp.where(hit, 0.0, dropped)
        score = jnp.where(hit, -jnp.inf, score)
    dropped = jnp.where(valid, dropped, 1.0).astype(BF16)
    q = q_ref[...]
    low = lax.broadcasted_iota(jnp.int32, q.shape, 1) < HEAD_DIM
    zero = jnp.zeros_like(q)
    width = LANES + n_sel_pad
    qa_ref[:, 0:LANES] = jnp.where(low, q, zero)
    qa_ref[:, LANES:width] = dropped
    qa_ref[:, width:width + LANES] = jnp.where(low, zero, q)
    qa_ref[:, width + LANES:2 * width] = dropped


def _cmp_attention(pb, kvc, bias_c, overlap, batch, seq):
    ncp = kvc.shape[2]
    n_sel_pad = overlap.shape[1]
    topk = min(NSA_TOPK, seq // NSA_SEL_LEN)
    pair = 2 * (LANES + n_sel_pad)
    return pl.pallas_call(
        functools.partial(_cmp_kernel, ncp=ncp, n_sel_pad=n_sel_pad, topk=topk),
        out_shape=(jax.ShapeDtypeStruct((batch, seq, 256), F32),
                   jax.ShapeDtypeStruct((batch, seq, 2 * pair), BF16)),
        grid=(batch, 2, seq // CMP_TQ),
        in_specs=[
            pl.BlockSpec((None, CMP_TQ, 128), lambda b, h, i: (b, i, PB_NSAQ // 128 + h)),
            pl.BlockSpec((None, None, ncp, HEAD_DIM), lambda b, h, i: (0, b * 2 + h, 0, 0)),
            pl.BlockSpec((None, None, ncp, HEAD_DIM), lambda b, h, i: (1, b * 2 + h, 0, 0)),
            pl.BlockSpec((2, CMP_TQ, ncp), lambda b, h, i: (h, i, 0)),
            pl.BlockSpec((ncp, n_sel_pad), lambda b, h, i: (0, 0)),
        ],
        out_specs=(pl.BlockSpec((None, CMP_TQ, 128), lambda b, h, i: (b, i, h)),
                   pl.BlockSpec((None, CMP_TQ, pair), lambda b, h, i: (b, i, h))),
        compiler_params=pltpu.CompilerParams(dimension_semantics=("parallel", "parallel", "arbitrary"),
                                             vmem_limit_bytes=VMEM_LIMIT),
        name="nsa_cmp_select",
    )(pb, kvc, kvc, bias_c, overlap)


SEL_T = 512


def _flash_init(m_ref, acc_ref):
    m_ref[...] = jnp.full_like(m_ref, NEG)
    acc_ref[...] = jnp.zeros_like(acc_ref)


def _flash_update(s, v_ones, m_ref, acc_ref, hd):
    m_old = m_ref[hd]
    m_new = jnp.maximum(m_old, jnp.max(s, axis=-1, keepdims=True))
    alpha = jnp.exp(m_old - m_new)
    pr = jnp.exp(s - jnp.tile(m_new, (1, s.shape[1] // LANES)))
    acc_ref[hd] = alpha * acc_ref[hd] + _dot(pr.astype(BF16), v_ones)
    m_ref[hd] = m_new


def _flash_finish(acc_ref, o_ref, heads):
    for hd in range(heads):
        acc = acc_ref[hd]
        row_sum = pltpu.roll(acc, HEAD_DIM, 1)
        o_ref[:, hd * HEAD_DIM:(hd + 1) * HEAD_DIM] = (acc / jnp.maximum(row_sum, 1e-30))[:, :HEAD_DIM]


def _sel_kernel(qi_ref, kj_ref, q_ref, k_ref, v_ref, pen_ref, bias_ref, o_ref, m_ref, acc_ref):
    p = pl.program_id(1)
    qi = qi_ref[p]
    kj = kj_ref[p]
    width = q_ref.shape[1] // 4

    @pl.when(kj == 0)
    def _():
        _flash_init(m_ref, acc_ref)

    def step(diagonal):
        pen = pen_ref[...]

        def scores(hd):
            k_pen = jnp.concatenate([k_ref[:, hd * LANES:(hd + 1) * LANES], pen], axis=1)
            s = _dot_nt(q_ref[:, hd * width:(hd + 1) * width], k_pen) + bias_ref[hd]
            if diagonal:
                row = lax.broadcasted_iota(jnp.int32, (SEL_T, SEL_T), 0)
                col = lax.broadcasted_iota(jnp.int32, (SEL_T, SEL_T), 1)
                s = jnp.where(col <= row, s, NEG)
            return s

        ahead = 3
        pending = [scores(hd) for hd in range(ahead)]
        for hd in range(4):
            if hd + ahead < 4:
                pending.append(scores(hd + ahead))
            _flash_update(pending.pop(0), v_ref[:, (hd // 2) * LANES:(hd // 2 + 1) * LANES], m_ref, acc_ref, hd)

    @pl.when(kj < qi)
    def _():
        step(False)

    @pl.when(kj == qi)
    def _():
        step(True)
        _flash_finish(acc_ref, o_ref, 4)


def _causal_pairs(n):
    qi, kj = [], []
    for i in range(n):
        for j in range(i + 1):
            qi.append(i)
            kj.append(j)
    return jnp.asarray(np.array(qi, np.int32)), jnp.asarray(np.array(kj, np.int32))


def _sel_attention(pb, q_aug, penalty, bias_s, batch, seq):
    n_sel_pad = penalty.shape[1]
    n_delta = bias_s.shape[1]
    qi, kj = _causal_pairs(seq // SEL_T)
    grid_spec = pltpu.PrefetchScalarGridSpec(
        num_scalar_prefetch=2,
        grid=(batch, int(qi.shape[0])),
        in_specs=[
            pl.BlockSpec((None, SEL_T, q_aug.shape[2]), lambda b, p, qi, kj: (b, qi[p], 0)),
            pl.BlockSpec((None, SEL_T, 512), lambda b, p, qi, kj: (b, kj[p], PB_NSAKS // 512)),
            pl.BlockSpec((None, SEL_T, 256), lambda b, p, qi, kj: (b, kj[p], PB_NSAVS // 256)),
            pl.BlockSpec((SEL_T, n_sel_pad), lambda b, p, qi, kj: (kj[p], 0)),
            pl.BlockSpec((4, None, SEL_T, SEL_T),
                         lambda b, p, qi, kj: (0, jnp.minimum(qi[p] - kj[p], n_delta - 1), 0, 0)),
        ],
        out_specs=pl.BlockSpec((None, SEL_T, 256), lambda b, p, qi, kj: (b, qi[p], 0)),
        scratch_shapes=[pltpu.VMEM((4, SEL_T, LANES), F32), pltpu.VMEM((4, SEL_T, LANES), F32)],
    )
    return pl.pallas_call(
        _sel_kernel,
        out_shape=jax.ShapeDtypeStruct((batch, seq, 256), F32),
        grid_spec=grid_spec,
        compiler_params=pltpu.CompilerParams(dimension_semantics=("parallel", "arbitrary"),
                                             vmem_limit_bytes=VMEM_LIMIT),
        name="nsa_selected",
    )(qi, kj, q_aug, pb, pb, penalty, bias_s)


MLA_T = 1024


def _mla_kernel(qi_ref, kj_ref, q_ref, k_ref, v_ref, o_ref, m_ref, acc_ref):
    p = pl.program_id(1)
    qi = qi_ref[p]
    kj = kj_ref[p]

    @pl.when(kj == 0)
    def _():
        _flash_init(m_ref, acc_ref)

    def step(diagonal):
        def scores(hd):
            cols = slice(hd * LANES, (hd + 1) * LANES)
            s = _dot_nt(q_ref[:, cols], k_ref[:, cols])
            if diagonal:
                row = lax.broadcasted_iota(jnp.int32, (MLA_T, MLA_T), 0)
                col = lax.broadcasted_iota(jnp.int32, (MLA_T, MLA_T), 1)
                s = jnp.where(col <= row, s, NEG)
            return s

        ahead = 3
        pending = [scores(hd) for hd in range(ahead)]
        for hd in range(MLA_HEADS):
            if hd + ahead < MLA_HEADS:
                pending.append(scores(hd + ahead))
            _flash_update(pending.pop(0), v_ref[:, hd * LANES:(hd + 1) * LANES], m_ref, acc_ref, hd)

    @pl.when(kj < qi)
    def _():
        step(False)

    @pl.when(kj == qi)
    def _():
        step(True)
        _flash_finish(acc_ref, o_ref, MLA_HEADS)


def _mla_attention(pb, batch, seq):
    qi, kj = _causal_pairs(seq // MLA_T)
    grid_spec = pltpu.PrefetchScalarGridSpec(
        num_scalar_prefetch=2,
        grid=(batch, int(qi.shape[0])),
        in_specs=[
            pl.BlockSpec((None, MLA_T, 512), lambda b, p, qi, kj: (b, qi[p], PB_MLAQ // 512)),
            pl.BlockSpec((None, MLA_T, 512), lambda b, p, qi, kj: (b, kj[p], PB_MLAK // 512)),
            pl.BlockSpec((None, MLA_T, 512), lambda b, p, qi, kj: (b, kj[p], PB_MLAV // 512)),
        ],
        out_specs=pl.BlockSpec((None, MLA_T, 256), lambda b, p, qi, kj: (b, qi[p], 0)),
        scratch_shapes=[pltpu.VMEM((4, MLA_T, LANES), F32), pltpu.VMEM((4, MLA_T, LANES), F32)],
    )
    return pl.pallas_call(
        _mla_kernel,
        out_shape=jax.ShapeDtypeStruct((batch, seq, 256), F32),
        grid_spec=grid_spec,
        compiler_params=pltpu.CompilerParams(dimension_semantics=("parallel", "arbitrary"),
                                             vmem_limit_bytes=VMEM_LIMIT),
        name="mla_causal",
    )(qi, kj, pb, pb, pb)


def _outproj_kernel(x_ref, oa_ref, ob_ref, oc_ref, os_ref, ow_ref, od_ref, gate_ref, gexp_ref, gn_ref, w_ref,
                    o_ref):
    gates = gate_ref[...]
    g_hi, g_lo = _split_bf16(gates)

    def gate(branch):
        e = gexp_ref[branch]
        return _dot(g_hi, e) + _dot(g_lo, e)

    o_nsa = gate(0) * oc_ref[...] + gate(1) * os_ref[...] + gate(2) * ow_ref[...]
    gn = gn_ref[...]
    y = x_ref[...]
    for grp, o in enumerate((oa_ref[...], ob_ref[...], o_nsa, od_ref[...])):
        cols = slice(grp * GROUP_WIDTH, (grp + 1) * GROUP_WIDTH)
        normed = _row_rms(o, GROUP_WIDTH) * gn[:, cols]
        y = y + _dot(normed.astype(BF16), w_ref[cols, :])
    o_ref[...] = y


def _outproj(x2, oa, ob, oc, osel, ow, od, pf, gexp, gn, w, tm):
    tokens = x2.shape[0]
    row = lambda t: (t, 0)
    o_spec = pl.BlockSpec((tm, GROUP_WIDTH), row)
    return pl.pallas_call(
        _outproj_kernel,
        out_shape=jax.ShapeDtypeStruct((tokens, D_MODEL), F32),
        grid=(tokens // tm,),
        in_specs=[
            pl.BlockSpec((tm, D_MODEL), row), o_spec, o_spec, o_spec, o_spec, o_spec, o_spec,
            pl.BlockSpec((tm, 128), lambda t: (t, PF_GATE // 128)),
            pl.BlockSpec((3, 128, GROUP_WIDTH), lambda t: (0, 0, 0)),
            pl.BlockSpec((1, D_MODEL), lambda t: (0, 0)),
            pl.BlockSpec((D_MODEL, D_MODEL), lambda t: (0, 0)),
        ],
        out_specs=pl.BlockSpec((tm, D_MODEL), row),
        compiler_params=pltpu.CompilerParams(dimension_semantics=("parallel",), vmem_limit_bytes=VMEM_LIMIT),
        name="out_proj",
    )(x2, oa, ob, oc, osel, ow, od, pf, gexp, gn, w)


MOE_TM = 1024
MOE_QUAD = 4
MOE_CHUNK = 320
MOE_CHUNK_ALIGN = 64
MOE_SLOTS = MOE_TM + MOE_GROUPS * MOE_CHUNK
MOE_VMEM_LIMIT = 60 * 1024 * 1024


def _moe_kernel(x_ref, fn_ref, wr_hi_ref, wr_lo_ref, br_ref, cexp_ref, wg_ref, wu_ref, wd_ref, o_ref, hs_ref,
                cs_ref, ys_ref, slot_ref, start_ref):
    step = pl.program_id(1)
    lane = lax.broadcasted_iota(jnp.int32, (MOE_TM, LANES), 1)

    @pl.when(step == 0)
    def _():
        h = _row_rms(x_ref[...], D_MODEL) * fn_ref[...]
        h_hi, h_lo = _split_bf16(h)
        logits = (_dot(h_hi, wr_hi_ref[...]) + _dot(h_lo, wr_hi_ref[...]) + _dot(h_hi, wr_lo_ref[...])
                  + br_ref[...])
        lane_f = lane.astype(F32)
        no_lane = float(LANES)
        is_group = lane < MOE_GROUPS
        g_max = jnp.max(jnp.where(is_group, logits, -jnp.inf), axis=-1, keepdims=True)
        g_star = jnp.min(jnp.where(is_group & (logits == g_max), lane_f, no_lane), axis=-1, keepdims=True)
        g_den = jnp.sum(jnp.where(is_group, jnp.exp(logits - g_max), 0.0), axis=-1, keepdims=True)
        g_w = 1.0 / g_den
        group_of_lane = ((lane - MOE_GROUPS) >> int(math.log2(MOE_EPG))).astype(F32)
        in_group = (lane >= MOE_GROUPS) & (lane < MOE_GROUPS + MOE_EXPERTS) & (group_of_lane == g_star)
        e_l = jnp.where(in_group, logits, -jnp.inf)
        top1 = jnp.max(e_l, axis=-1, keepdims=True)
        i1 = jnp.min(jnp.where(e_l == top1, lane_f, no_lane), axis=-1, keepdims=True)
        e_l2 = jnp.where(lane_f == i1, -jnp.inf, e_l)
        top2 = jnp.max(e_l2, axis=-1, keepdims=True)
        i2 = jnp.min(jnp.where(e_l2 == top2, lane_f, no_lane), axis=-1, keepdims=True)
        r = jnp.exp(top2 - top1)
        w1 = g_w / (1.0 + r)
        w2 = g_w * r / (1.0 + r)
        comb = jnp.where(lane_f == i1, w1, jnp.where(lane_f == i2, w2, 0.0))

        onehot = jnp.where(lane_f == g_star, 1.0, 0.0)
        row = lax.broadcasted_iota(jnp.int32, (MOE_TM, LANES), 0)
        incl = onehot
        shift = 1
        while shift < MOE_TM:
            incl = incl + jnp.where(row >= shift, pltpu.roll(incl, shift, 0), 0.0)
            shift *= 2
        counts = incl[MOE_TM - 1:MOE_TM, :]
        padded = jnp.floor((counts + (MOE_CHUNK - 0.5)) * (1.0 / MOE_CHUNK)) * MOE_CHUNK
        before = (lax.broadcasted_iota(jnp.int32, (LANES, LANES), 0)
                  < lax.broadcasted_iota(jnp.int32, (LANES, LANES), 1))
        starts = _dot(jnp.broadcast_to(padded, (8, LANES)).astype(BF16),
                      jnp.where(before, 1.0, 0.0).astype(BF16))[0:1, :]
        for g in range(MOE_GROUPS + 1):
            start_ref[g] = jnp.sum(jnp.where(lane[0:1, :] == g, starts, 0.0)).astype(jnp.int32)
        slot = jnp.sum(onehot * (starts + incl - onehot), axis=-1, keepdims=True)
        slot_ref[...] = jnp.broadcast_to(slot, (MOE_TM, LANES))
        slot_row = slot_ref[...].T[0:1, :]
        n_slots = hs_ref.shape[0]
        place = jnp.where(lax.broadcasted_iota(jnp.int32, (n_slots, MOE_TM), 0).astype(F32) == slot_row,
                          1.0, 0.0).astype(BF16)
        hs_ref[...] = _dot(place, h.astype(BF16)).astype(BF16)
        c_hi, c_lo = _split_bf16(comb)
        cs_ref[:, :LANES] = _dot(place, c_hi).astype(BF16)
        cs_ref[:, LANES:] = _dot(place, c_lo).astype(BF16)
        ys_ref[...] = jnp.zeros_like(ys_ref)

    group = step // (MOE_EPG // MOE_QUAD)
    first_slot = start_ref[group]
    half = MOE_QUAD * MOE_HIDDEN // 2

    def chunk(c, carry):
        rows = pl.ds(pl.multiple_of(first_slot + c * MOE_CHUNK, MOE_CHUNK_ALIGN), MOE_CHUNK)
        hb = hs_ref[rows, :]
        weight = _dot(cs_ref[rows, :], cexp_ref[...])
        y = None
        for s in range(2):
            cols = slice(s * half, (s + 1) * half)
            a = jax.nn.silu(_dot(hb, wg_ref[:, cols])) * _dot(hb, wu_ref[:, cols]) * weight[:, cols]
            part = _dot(a.astype(BF16), wd_ref[cols, :])
            y = part if y is None else y + part
        ys_ref[rows, :] += y
        return carry

    lax.fori_loop(0, (start_ref[group + 1] - first_slot) // MOE_CHUNK, chunk, 0)

    @pl.when(step == MOE_EXPERTS // MOE_QUAD - 1)
    def _():
        n_slots = hs_ref.shape[0]
        back = jnp.where(lax.broadcasted_iota(jnp.int32, (MOE_TM, n_slots), 1).astype(F32) == slot_ref[:, 0:1],
                         1.0, 0.0).astype(BF16)
        y_hi, y_lo = _split_bf16(ys_ref[...])
        o_ref[...] = x_ref[...] + _dot(back, y_hi) + _dot(back, y_lo)


def _moe(x2, fn, wr_hi, wr_lo, br, wg, wu, wd):
    tokens = x2.shape[0]
    quads = MOE_EXPERTS // MOE_QUAD
    width = MOE_QUAD * MOE_HIDDEN
    const = lambda t, e: (0, 0)
    lane_of_col = MOE_GROUPS + np.arange(quads)[:, None, None] * MOE_QUAD + np.arange(width)[None, None, :] // MOE_HIDDEN
    cexp = jnp.asarray((np.arange(2 * LANES)[None, :, None] % LANES) == lane_of_col, BF16)
    return pl.pallas_call(
        _moe_kernel,
        out_shape=jax.ShapeDtypeStruct((tokens, D_MODEL), F32),
        grid=(tokens // MOE_TM, quads),
        in_specs=[
            pl.BlockSpec((MOE_TM, D_MODEL), lambda t, e: (t, 0)),
            pl.BlockSpec((1, D_MODEL), const),
            pl.BlockSpec((D_MODEL, LANES), const),
            pl.BlockSpec((D_MODEL, LANES), const),
            pl.BlockSpec((1, LANES), const),
            pl.BlockSpec((None, 2 * LANES, width), lambda t, e: (e, 0, 0)),
            pl.BlockSpec((None, D_MODEL, width), lambda t, e: (e, 0, 0)),
            pl.BlockSpec((None, D_MODEL, width), lambda t, e: (e, 0, 0)),
            pl.BlockSpec((None, width, D_MODEL), lambda t, e: (e, 0, 0)),
        ],
        out_specs=pl.BlockSpec((MOE_TM, D_MODEL), lambda t, e: (t, 0)),
        scratch_shapes=[pltpu.VMEM((MOE_SLOTS, D_MODEL), BF16), pltpu.VMEM((MOE_SLOTS, 2 * LANES), BF16),
                        pltpu.VMEM((MOE_SLOTS, D_MODEL), F32), pltpu.VMEM((MOE_TM, LANES), F32),
                        pltpu.SMEM((8,), jnp.int32)],
        compiler_params=pltpu.CompilerParams(dimension_semantics=("parallel", "arbitrary"),
                                             vmem_limit_bytes=MOE_VMEM_LIMIT),
        name="hier_moe",
    )(x2, fn, wr_hi, wr_lo, br, cexp, wg, wu, wd)


def _t5_bucket(dist):
    n = jnp.maximum(dist, 0)
    max_exact = T5_BUCKETS // 2
    nf = jnp.maximum(n, 1).astype(F32)
    large = max_exact + (jnp.log(nf / max_exact) / math.log(T5_MAX_DIST / max_exact)
                         * (T5_BUCKETS - max_exact)).astype(jnp.int32)
    large = jnp.minimum(large, T5_BUCKETS - 1)
    return jnp.where(n < max_exact, n, large)


def _sel_delta_cap():
    max_exact = T5_BUCKETS // 2
    span = T5_BUCKETS - max_exact
    last_bucket_from = max_exact * (T5_MAX_DIST / max_exact) ** ((span - 1) / span)
    cap = 1
    while (cap - 1) * SEL_T + 1 < 1.25 * last_bucket_from:
        cap += 1
    return cap


def _position_tables(rel_bias, seq):
    buckets = _t5_bucket(jnp.arange(seq))
    first = jnp.sum(buckets[None, :] < jnp.arange(T5_BUCKETS)[:, None], axis=1)
    tbl = rel_bias.T

    def toeplitz(heads, dist):
        shape = (tbl[heads].shape[0],) + (1,) * dist.ndim
        out = jnp.broadcast_to(tbl[heads][:, 0].reshape(shape), shape[:1] + dist.shape)
        for b in range(1, T5_BUCKETS):
            out = jnp.where((dist >= first[b])[None], tbl[heads][:, b].reshape(shape), out)
        return out

    swa_h, nsa_h = slice(0, 4), slice(4, 8)
    def band(window):
        pad, tq = _band_tiles(window)
        return jnp.arange(tq)[:, None] + pad - jnp.arange(pad + tq)[None, :]

    bias_swa = toeplitz(swa_h, band(SWA_WINDOW))
    bias_win = toeplitz(nsa_h, band(NSA_WINDOW))
    ncp = seq // NSA_CMP_STRIDE
    cmp_end = jnp.arange(ncp) * NSA_CMP_STRIDE + NSA_CMP_LEN - 1
    bias_cmp = toeplitz(nsa_h, jnp.arange(seq)[:, None] - cmp_end[None, :])
    nd = min(_sel_delta_cap() + 1, seq // SEL_T)
    dist_s = (jnp.arange(nd)[:, None, None] * SEL_T + jnp.arange(SEL_T)[None, :, None]
              - jnp.arange(SEL_T)[None, None, :])
    bias_sel = toeplitz(nsa_h, dist_s)

    n_sel_pad = -(-(seq // NSA_SEL_LEN) // LANES) * LANES
    sel_start = np.arange(n_sel_pad) * NSA_SEL_LEN
    c_start = np.arange(ncp) * NSA_CMP_STRIDE
    c_end = c_start + NSA_CMP_LEN - 1
    real = (np.arange(ncp) < ncp - NSA_CMP_LEN // NSA_CMP_STRIDE + 1)[:, None] & (sel_start < seq)[None, :]
    overlap = ((c_start[:, None] < sel_start[None, :] + NSA_SEL_LEN) & (c_end[:, None] >= sel_start[None, :]) & real)
    own_block = (np.arange(seq) // NSA_SEL_LEN)[:, None] == np.arange(n_sel_pad)[None, :]
    penalty = np.where(own_block, -2.0 ** 100, 0.0)

    pos = jnp.arange(seq, dtype=F32)
    inv_freq = ROPE_THETA ** (-jnp.arange(0, MLA_ROPE, 2, dtype=F32) / MLA_ROPE)
    ang = pos[:, None] * inv_freq[None, :]
    cos, sin = jnp.cos(ang), jnp.sin(ang)
    ones = jnp.ones((seq, MLA_NOPE), F32)
    tail = LANES - MLA_NOPE - MLA_ROPE
    cos_t = jnp.concatenate([ones, cos, cos, jnp.ones((seq, tail), F32)], axis=1)
    sin_t = jnp.concatenate([0 * ones, -sin, sin, jnp.zeros((seq, tail), F32)], axis=1)
    return dict(bias_swa=bias_swa, bias_win=bias_win, bias_cmp=bias_cmp, bias_sel=bias_sel,
                overlap=jnp.asarray(overlap, BF16), penalty=jnp.asarray(penalty, BF16), cos_t=cos_t, sin_t=sin_t)


def _pad_to(a, shape):
    return jnp.pad(a, [(0, s - d) for d, s in zip(a.shape, shape)])


def _pack_layer(w_in, swa_q_norm, swa_k_norm, nsa_q_norm, nsa_k_norm, mla_q_lat_norm, mla_w_q_up,
                mla_kv_lat_norm, mla_w_kv_up, mla_q_norm, mla_k_norm):
    kpe = w_in[:, 2636:2668]
    kpe_seg = jnp.concatenate([jnp.zeros((D_MODEL, MLA_NOPE), F32), kpe,
                               jnp.zeros((D_MODEL, LANES - MLA_QK), F32)], axis=1)
    spread = lambda cols: _pad_to(cols.reshape(D_MODEL, -1, HEAD_DIM), (D_MODEL, cols.shape[1] // HEAD_DIM, LANES)
                                  ).reshape(D_MODEL, -1)
    w = jnp.concatenate([
        w_in[:, :1920],
        spread(w_in[:, 1920:2048]),
        w_in[:, 2048:2304],
        _pad_to(w_in[:, 2304:2316], (D_MODEL, 128)),
        _pad_to(w_in[:, 2316:2508], (D_MODEL, 256)),
        w_in[:, 2508:2636],
        jnp.tile(kpe_seg, (1, MLA_HEADS)),
    ], axis=1).astype(BF16)
    tile4 = lambda g: jnp.tile(g, 4)
    g64 = _pad_to(jnp.stack([tile4(swa_q_norm), tile4(swa_k_norm), tile4(nsa_q_norm),
                             tile4(nsa_k_norm[1]), tile4(nsa_k_norm[2])]), (8, 256))
    glat = _pad_to(jnp.stack([_pad_to(mla_q_lat_norm, (256,)), _pad_to(mla_kv_lat_norm, (256,))]), (8, 256))
    wq = _pad_to(mla_w_q_up.reshape(MLA_Q_RANK, MLA_HEADS, MLA_QK), (256, MLA_HEADS, LANES))
    wq = wq.reshape(256, MLA_HEADS * LANES).astype(BF16)
    wkv = mla_w_kv_up.reshape(MLA_KV_RANK, MLA_HEADS, MLA_NOPE + MLA_V)
    wk = _pad_to(wkv[:, :, :MLA_NOPE], (MLA_KV_RANK, MLA_HEADS, LANES)).reshape(MLA_KV_RANK, MLA_HEADS * LANES)
    wv = _pad_to(wkv[:, :, MLA_NOPE:], (MLA_KV_RANK, MLA_HEADS, LANES)).reshape(MLA_KV_RANK, MLA_HEADS * LANES)
    wkv_p = jnp.concatenate([wk, wv], axis=1).astype(BF16)
    gmla = _pad_to(jnp.stack([jnp.tile(_pad_to(mla_q_norm, (LANES,)), MLA_HEADS),
                              jnp.tile(_pad_to(mla_k_norm, (LANES,)), MLA_HEADS)]), (8, 512))
    return w, g64, glat, wq, wkv_p, gmla


def _gate_expand():
    rows = np.arange(128)[None, :, None]
    cols = np.arange(GROUP_WIDTH)[None, None, :]
    branch = np.arange(3)[:, None, None]
    return jnp.asarray(rows == branch * 4 + cols // HEAD_DIM, BF16)


def _compress_rows(pf3, batch, seq):
    nb = seq // NSA_CMP_STRIDE
    kv = pf3[:, :, :256].reshape(batch, seq, 2, 2, HEAD_DIM).transpose(2, 0, 3, 1, 4)
    blocks = kv.reshape(2, batch * 2, nb, NSA_CMP_STRIDE * HEAD_DIM)
    nxt = jnp.concatenate([blocks[:, :, 1:], jnp.zeros_like(blocks[:, :, :1])], axis=2)
    return jnp.concatenate([blocks, nxt], axis=3)


def kernel(x, rel_bias, attn_norm, w_in, swa_q_norm, swa_k_norm, swa_sinks, nsa_q_norm, nsa_k_norm, nsa_cmp_pos, nsa_cmp_w1, nsa_cmp_w2, mla_q_lat_norm, mla_w_q_up, mla_kv_lat_norm, mla_w_kv_up, mla_q_norm, mla_k_norm, out_norm, w_out, ffn_norm, moe_w_group, moe_b_group, moe_w_expert, moe_b_expert, moe_w_gate, moe_w_up, moe_w_down):
    batch, seq, _ = x.shape
    depth = w_in.shape[0]
    tokens = batch * seq
    tm = 512
    assert seq % 2048 == 0 and tokens % MOE_TM == 0
    tabs = _position_tables(rel_bias, seq)
    gexp = _gate_expand()
    zero_sinks = jnp.zeros((4,), F32)
    x2 = x.reshape(tokens, D_MODEL)
    for l in range(depth):
        w, g64, glat, wq, wkv, gmla = _pack_layer(
            w_in[l], swa_q_norm[l], swa_k_norm[l], nsa_q_norm[l], nsa_k_norm[l], mla_q_lat_norm[l],
            mla_w_q_up[l], mla_kv_lat_norm[l], mla_w_kv_up[l], mla_q_norm[l], mla_k_norm[l])
        pb, pf = _prep(x2, attn_norm[l][None, :], w, g64, glat, wq, wkv, gmla, tabs["cos_t"], tabs["sin_t"],
                       seq, tm)
        pb3 = pb.reshape(batch, seq, PB_WIDTH)
        pf3 = pf.reshape(batch, seq, PF_WIDTH)
        o_a = _sb_attention(pb3, batch, seq)
        o_b = _banded_attention(pb3, swa_sinks[l], tabs["bias_swa"], batch, seq, SWA_WINDOW,
                                PB_SWAQ, PB_SWAK, PB_SWAV, True)
        rows = _compress_rows(pf3, batch, seq)
        kvc = _compress(rows, nsa_cmp_pos[l].reshape(2, 1, -1),
                        nsa_cmp_w1[l].reshape(2, -1, NSA_CMP_HIDDEN).astype(BF16), nsa_cmp_w2[l].astype(BF16),
                        nsa_k_norm[l][0][None, :], 128)
        o_c, q_aug = _cmp_attention(pb3, kvc, tabs["bias_cmp"], tabs["overlap"], batch, seq)
        o_s = _sel_attention(pb3, q_aug, tabs["penalty"], tabs["bias_sel"], batch, seq)
        o_w = _banded_attention(pb3, zero_sinks, tabs["bias_win"], batch, seq, NSA_WINDOW,
                                PB_NSAQ, PB_NSAKW, PB_NSAVW, False)
        o_d = _mla_attention(pb3, batch, seq)
        flat = lambda o: o.reshape(tokens, GROUP_WIDTH)
        x2 = _outproj(x2, flat(o_a), flat(o_b), flat(o_c), flat(o_s), flat(o_w), flat(o_d), pf, gexp,
                      out_norm[l][None, :], w_out[l].astype(BF16), tm)
        w_router = _pad_to(jnp.concatenate([moe_w_group[l], moe_w_expert[l]], axis=1), (D_MODEL, LANES))
        wr_hi = w_router.astype(BF16)
        wr_lo = (w_router - wr_hi.astype(F32)).astype(BF16)
        b_router = _pad_to(jnp.concatenate([moe_b_group[l], moe_b_expert[l]])[None, :], (1, LANES))
        quads = MOE_EXPERTS // MOE_QUAD
        by_quad = lambda w: w.astype(BF16).reshape(quads, MOE_QUAD, D_MODEL, MOE_HIDDEN).transpose(0, 2, 1, 3
                                                   ).reshape(quads, D_MODEL, MOE_QUAD * MOE_HIDDEN)
        x2 = _moe(x2, ffn_norm[l][None, :], wr_hi, wr_lo, b_router, by_quad(moe_w_gate[l]), by_quad(moe_w_up[l]),
                  moe_w_down[l].astype(BF16).reshape(quads, MOE_QUAD * MOE_HIDDEN, D_MODEL))
    return x2.reshape(batch, seq, D_MODEL)
```

```python
import functools
import math

import numpy as np
import jax
import jax.numpy as jnp
from jax import lax
from jax.experimental import pallas as pl
from jax.experimental.pallas import tpu as pltpu

F32 = jnp.float32
BF16 = jnp.bfloat16

D_MODEL = 1024
HEAD_DIM = 64
NEG = -1e30
EPS = 1e-6
FORCE_BONUS = 1000.0
SWA_WINDOW = 128
NSA_CMP_LEN = 32
NSA_CMP_STRIDE = 16
NSA_CMP_HIDDEN = 128
NSA_SEL_LEN = 64
NSA_TOPK = 16
NSA_WINDOW = 512
MLA_HEADS = 4
MLA_NOPE = 64
MLA_ROPE = 32
MLA_V = 64
MLA_Q_RANK = 192
MLA_KV_RANK = 128
MLA_QK = MLA_NOPE + MLA_ROPE
ROPE_THETA = 10000.0
T5_BUCKETS = 32
T5_MAX_DIST = 1024
MOE_GROUPS = 4
MOE_EPG = 8
MOE_EXPERTS = MOE_GROUPS * MOE_EPG
MOE_HIDDEN = 256
GROUP_WIDTH = 256
LANES = 128
VMEM_LIMIT = 48 * 1024 * 1024

PB_MLAQ, PB_MLAK, PB_MLAV = 0, 512, 1024
PB_SBQ, PB_SBK, PB_SBV = 1536, 1792, 2048
PB_SWAQ, PB_NSAQ, PB_NSAVS = 2304, 2560, 2816
PB_SWAK, PB_SWAV, PB_NSAKW, PB_NSAVW = 3072, 3200, 3328, 3456
PB_NSAKS = 3584
PB_WIDTH = 4096
PF_KC, PF_VC, PF_GATE = 0, 128, 256
PF_WIDTH = 384
W_SBQ, W_SBK, W_SBV, W_SWAQ, W_SWAK, W_SWAV, W_NSAQ = 0, 256, 512, 768, 1024, 1152, 1280
W_KC, W_VC, W_KS, W_VS, W_KW, W_VW, W_GATE = 1536, 1664, 1792, 1920, 2176, 2304, 2432
W_CQ, W_CKV, W_KPE = 2560, 2816, 2944
W_WIDTH = 3456

NT_DIMS = (((1,), (1,)), ((), ()))


def _dot(a, b):
    return jnp.dot(a, b, preferred_element_type=F32)


def _dot_nt(a, b):
    return lax.dot_general(a, b, NT_DIMS, preferred_element_type=F32)


def _split_bf16(x):
    hi = x.astype(BF16)
    lo = (x - hi.astype(F32)).astype(BF16)
    return hi, lo


def _dot_exact_rhs(x, m):
    hi, lo = _split_bf16(x)
    return _dot(hi, m) + _dot(lo, m)


def _block_diag_ones(width, seg):
    idx = np.arange(width) // seg
    return jnp.asarray(idx[:, None] == idx[None, :], BF16)


def _seg_rms(x, seg_ones, count):
    width = x.shape[1]
    ms = _dot_exact_rhs(x * x, seg_ones[:width, :width]) * (1.0 / count)
    return x * lax.rsqrt(ms + EPS)


def _row_rms(x, count):
    return x * lax.rsqrt(jnp.sum(x * x, axis=-1, keepdims=True) * (1.0 / count) + EPS)


def _prep_kernel(x_ref, an_ref, w_ref, g64_ref, glat_ref, wq_ref, wkv_ref, gmla_ref, cos_ref, sin_ref,
                 s64_ref, s128_ref, pb_ref, pf_ref):
    x = x_ref[...]
    s64 = s64_ref[...]
    s128 = s128_ref[...]
    h = _row_rms(x, D_MODEL) * an_ref[...]
    hb = h.astype(BF16)

    def proj(lo, hi):
        return _dot(hb, w_ref[:, lo:hi])

    def put(col, value):
        pb_ref[:, col:col + value.shape[1]] = value.astype(BF16)

    def ones_tail(width):
        lane = lax.broadcasted_iota(jnp.int32, (1, width), 1)
        return jnp.where((lane & (LANES - 1)) >= HEAD_DIM, 1.0, 0.0)

    scale = HEAD_DIM ** -0.5
    g64 = g64_ref[...]
    put(PB_SBQ, proj(W_SBQ, W_SBQ + 256) * scale)
    put(PB_SBK, proj(W_SBK, W_SBK + 256))
    put(PB_SBV, proj(W_SBV, W_SBV + 256))
    put(PB_SWAQ, _seg_rms(proj(W_SWAQ, W_SWAQ + 256), s64, 64) * g64[0:1, :] * scale)
    put(PB_SWAK, _seg_rms(proj(W_SWAK, W_SWAK + 128), s64, 64) * g64[1:2, :128])
    put(PB_SWAV, proj(W_SWAV, W_SWAV + 128))
    put(PB_NSAQ, _seg_rms(proj(W_NSAQ, W_NSAQ + 256), s64, 64) * g64[2:3, :] * scale)
    pf_ref[:, PF_KC:PF_KC + 128] = proj(W_KC, W_KC + 128)
    pf_ref[:, PF_VC:PF_VC + 128] = proj(W_VC, W_VC + 128)
    ks = _seg_rms(proj(W_KS, W_KS + 128), s64, 64) * g64[3:4, :128]
    ks_swapped = pltpu.roll(ks, HEAD_DIM, 1)
    low = lax.broadcasted_iota(jnp.int32, ks.shape, 1) < HEAD_DIM
    put(PB_NSAKS, jnp.where(low, ks, 0.0))
    put(PB_NSAKS + LANES, jnp.where(low, 0.0, ks_swapped))
    put(PB_NSAKS + 2 * LANES, jnp.where(low, ks_swapped, 0.0))
    put(PB_NSAKS + 3 * LANES, jnp.where(low, 0.0, ks))
    put(PB_NSAVS, proj(W_VS, W_VS + 256) + ones_tail(256))
    put(PB_NSAKW, _seg_rms(proj(W_KW, W_KW + 128), s64, 64) * g64[4:5, :128])
    put(PB_NSAVW, proj(W_VW, W_VW + 128))
    pf_ref[:, PF_GATE:PF_GATE + 128] = jax.nn.sigmoid(proj(W_GATE, W_GATE + 128))

    glat = glat_ref[...]
    cq = _row_rms(proj(W_CQ, W_CQ + 256), MLA_Q_RANK) * glat[0:1, :]
    q = _dot(cq.astype(BF16), wq_ref[...])
    ckv = _row_rms(proj(W_CKV, W_CKV + 128), MLA_KV_RANK) * glat[1:2, :128]
    kv = _dot(ckv.astype(BF16), wkv_ref[...])
    k = kv[:, :512] + proj(W_KPE, W_KPE + 512)
    gm = gmla_ref[...]
    q = _seg_rms(q, s128, MLA_QK) * gm[0:1, :]
    k = _seg_rms(k, s128, MLA_QK) * gm[1:2, :]
    cos = cos_ref[...]
    sin = sin_ref[...]
    lane = lax.broadcasted_iota(jnp.int32, (x.shape[0], LANES), 1)
    first_half = lane < MLA_NOPE + MLA_ROPE // 2

    def rope(t):
        partner = jnp.where(first_half, pltpu.roll(t, LANES - MLA_ROPE // 2, 1), pltpu.roll(t, MLA_ROPE // 2, 1))
        return t * cos + partner * sin

    qscale = MLA_QK ** -0.5
    for hd in range(MLA_HEADS):
        sl = slice(hd * LANES, (hd + 1) * LANES)
        put(PB_MLAQ + hd * LANES, rope(q[:, sl]) * qscale)
        put(PB_MLAK + hd * LANES, rope(k[:, sl]))
    put(PB_MLAV, kv[:, 512:1024] + ones_tail(512))


def _prep(x2, an, w, g64, glat, wq, wkv, gmla, cos_t, sin_t, seq, tm):
    tokens = x2.shape[0]
    n_pos = seq // tm
    const = lambda t: (0, 0)
    return pl.pallas_call(
        _prep_kernel,
        out_shape=(jax.ShapeDtypeStruct((tokens, PB_WIDTH), BF16), jax.ShapeDtypeStruct((tokens, PF_WIDTH), F32)),
        grid=(tokens // tm,),
        in_specs=[
            pl.BlockSpec((tm, D_MODEL), lambda t: (t, 0)),
            pl.BlockSpec((1, D_MODEL), const),
            pl.BlockSpec((D_MODEL, W_WIDTH), const),
            pl.BlockSpec((8, 256), const),
            pl.BlockSpec((8, 256), const),
            pl.BlockSpec((256, 512), const),
            pl.BlockSpec((128, 1024), const),
            pl.BlockSpec((8, 512), const),
            pl.BlockSpec((tm, LANES), lambda t: (t % n_pos, 0)),
            pl.BlockSpec((tm, LANES), lambda t: (t % n_pos, 0)),
            pl.BlockSpec((256, 256), const),
            pl.BlockSpec((512, 512), const),
        ],
        out_specs=(pl.BlockSpec((tm, PB_WIDTH), lambda t: (t, 0)), pl.BlockSpec((tm, PF_WIDTH), lambda t: (t, 0))),
        compiler_params=pltpu.CompilerParams(dimension_semantics=("parallel",), vmem_limit_bytes=VMEM_LIMIT),
        name="prep",
    )(x2, an, w, g64, glat, wq, wkv, gmla, cos_t, sin_t, _block_diag_ones(256, HEAD_DIM),
      _block_diag_ones(512, LANES))


SB_TQ, SB_KB, SB_SUB = 256, 2048, 128


SB_UNDERFLOW = 110.0


def _sb_kernel(qi_ref, kj_ref, first_ref, last_ref, kmax_ref, q_ref, k_ref, v_ref, sums_ref, *refs, resume):
    if resume:
        acc_in_ref, carry_in_ref = refs[:2]
        refs = refs[2:]
        o_ref, carry_out_ref, carry_ref, acc_ref, zb_ref, dead_ref = refs
        slack_ref = None
    else:
        o_ref, carry_out_ref, slack_ref, carry_ref, acc_ref, zb_ref, dead_ref = refs
    b = pl.program_id(0)
    p = pl.program_id(1)
    qi = qi_ref[p]
    kj = kj_ref[p]
    q_start = qi * SB_TQ

    @pl.when(first_ref[p] == 1)
    def _():
        if resume:
            for hd in range(4):
                carry_ref[hd * SB_TQ:(hd + 1) * SB_TQ, :] = carry_in_ref[:, hd * SB_SUB:(hd + 1) * SB_SUB]
                acc_ref[hd] = acc_in_ref[:, hd * HEAD_DIM:(hd + 1) * HEAD_DIM]
        else:
            carry_ref[...] = jnp.zeros_like(carry_ref)
            acc_ref[...] = jnp.zeros_like(acc_ref)
        dead_ref[0] = 0
        ones = jnp.ones((HEAD_DIM, SB_SUB), BF16)
        for hd in range(4):
            q = q_ref[:, hd * HEAD_DIM:(hd + 1) * HEAD_DIM].astype(F32)
            zb_ref[hd * SB_TQ:(hd + 1) * SB_TQ, :] = (jnp.sqrt(_dot_exact_rhs(q * q, ones))
                                                      * (kmax_ref[b * 4 + hd] * 1.01))

    rel = (lax.broadcasted_iota(jnp.int32, (SB_TQ, SB_SUB), 1)
           - lax.broadcasted_iota(jnp.int32, (SB_TQ, SB_SUB), 0))

    def sub_tile(u, k_start):
        mask = jnp.tile(rel < q_start - k_start, (4, 1))
        rows = slice(u * SB_SUB, (u + 1) * SB_SUB)
        head = lambda hd: slice(hd * HEAD_DIM, (hd + 1) * HEAD_DIM)
        z = jnp.concatenate([_dot_nt(q_ref[:, head(hd)], k_ref[rows, head(hd)]) for hd in range(4)], axis=0)
        log_keep = jnp.where(mask, -(jnp.maximum(z, 0.0) + jnp.log(1.0 + jnp.exp(-jnp.abs(z)))), 0.0)
        hi, lo = _split_bf16(log_keep)
        sums = _dot(jnp.concatenate([hi, lo], axis=1), sums_ref[...])
        carry = carry_ref[...]
        a = jnp.where(mask, jnp.exp(z + sums[:, :SB_SUB] + carry), 0.0).astype(BF16)
        for hd in range(4):
            acc_ref[hd] += _dot(a[hd * SB_TQ:(hd + 1) * SB_TQ], v_ref[rows, head(hd)])
        carry_ref[...] = carry + sums[:, SB_SUB:]

    for u in reversed(range(SB_KB // SB_SUB)):
        k_start = kj * SB_KB + u * SB_SUB

        @pl.when((k_start < q_start + SB_TQ) & (dead_ref[0] == 0))
        def _(u=u, k_start=k_start):
            live = jnp.max(carry_ref[...] + zb_ref[...]) > -SB_UNDERFLOW

            @pl.when(live)
            def _():
                sub_tile(u, k_start)

            @pl.when(jnp.logical_not(live))
            def _():
                dead_ref[0] = 1

    @pl.when(last_ref[p] == 1)
    def _():
        for hd in range(4):
            o_ref[:, hd * HEAD_DIM:(hd + 1) * HEAD_DIM] = acc_ref[hd]
            carry_out_ref[:, hd * SB_SUB:(hd + 1) * SB_SUB] = carry_ref[hd * SB_TQ:(hd + 1) * SB_TQ, :]
        if slack_ref is not None:
            slack_ref[...] = jnp.full(slack_ref.shape, jnp.max(carry_ref[...] + zb_ref[...]), F32)


def _sb_call(pairs, pb, kmax, sums, batch, seq, state):
    qi = [i for i, blocks in pairs for _ in blocks]
    kj = [j for _, blocks in pairs for j in blocks]
    first = [int(n == 0) for _, blocks in pairs for n in range(len(blocks))]
    last = [int(n == len(blocks) - 1) for _, blocks in pairs for n in range(len(blocks))]
    prefetch = [jnp.asarray(np.array(a, np.int32)) for a in (qi, kj, first, last)]
    tile = lambda width: pl.BlockSpec((None, SB_TQ, width), lambda b, p, qi, kj, fi, la: (b, qi[p], 0))
    in_specs = [
        pl.BlockSpec(memory_space=pltpu.SMEM),
        pl.BlockSpec((None, SB_TQ, 256), lambda b, p, qi, kj, fi, la: (b, qi[p], PB_SBQ // 256)),
        pl.BlockSpec((None, SB_KB, 256), lambda b, p, qi, kj, fi, la: (b, kj[p], PB_SBK // 256)),
        pl.BlockSpec((None, SB_KB, 256), lambda b, p, qi, kj, fi, la: (b, kj[p], PB_SBV // 256)),
        pl.BlockSpec((2 * SB_SUB, 2 * SB_SUB), lambda b, p, qi, kj, fi, la: (0, 0)),
    ]
    resume = state is not None
    out_specs = [tile(256), tile(4 * SB_SUB)]
    out_shape = [jax.ShapeDtypeStruct((batch, seq, 256), F32), jax.ShapeDtypeStruct((batch, seq, 4 * SB_SUB), F32)]
    if resume:
        in_specs += [tile(256), tile(4 * SB_SUB)]
    else:
        out_specs.append(pl.BlockSpec((None, None, 8, LANES), lambda b, p, qi, kj, fi, la: (b, qi[p], 0, 0)))
        out_shape.append(jax.ShapeDtypeStruct((batch, seq // SB_TQ, 8, LANES), F32))
    grid_spec = pltpu.PrefetchScalarGridSpec(
        num_scalar_prefetch=4,
        grid=(batch, len(qi)),
        in_specs=in_specs,
        out_specs=tuple(out_specs),
        scratch_shapes=[pltpu.VMEM((4 * SB_TQ, SB_SUB), F32), pltpu.VMEM((4, SB_TQ, HEAD_DIM), F32),
                        pltpu.VMEM((4 * SB_TQ, SB_SUB), F32), pltpu.SMEM((1,), jnp.int32)],
    )
    return pl.pallas_call(
        functools.partial(_sb_kernel, resume=resume),
        out_shape=tuple(out_shape),
        grid_spec=grid_spec,
        input_output_aliases={len(prefetch) + 5: 0, len(prefetch) + 6: 1} if resume else {},
        compiler_params=pltpu.CompilerParams(dimension_semantics=("parallel", "arbitrary"),
                                             vmem_limit_bytes=VMEM_LIMIT),
        name="stick_breaking_far" if resume else "stick_breaking",
    )(*prefetch, kmax, pb, pb, pb, sums, *(state or ()))


def _sb_attention(pb, batch, seq):
    keys = pb[:, :, PB_SBK:PB_SBK + 256].astype(F32).reshape(batch, seq, 4, HEAD_DIM)
    kmax = jnp.sqrt(jnp.max(jnp.sum(keys * keys, axis=-1), axis=1)).reshape(batch * 4)
    j = np.arange(2 * SB_SUB)[:, None] % SB_SUB
    s = np.arange(2 * SB_SUB)[None, :]
    sums = jnp.asarray((s >= SB_SUB) | (j >= s), BF16)
    near, far = [], []
    for i in range(seq // SB_TQ):
        blocks = list(reversed(range((i * SB_TQ + SB_TQ - 1) // SB_KB + 1)))
        n_near = 2 if (i * SB_TQ) % SB_KB == 0 else 1
        near.append((i, blocks[:n_near]))
        if blocks[n_near:]:
            far.append((i, blocks[n_near:]))
    values, carry, slack = _sb_call(near, pb, kmax, sums, batch, seq, None)
    if not far:
        return values
    return lax.cond(jnp.max(slack) > -SB_UNDERFLOW,
                    lambda: _sb_call(far, pb, kmax, sums, batch, seq, (values, carry))[0],
                    lambda: values)


def _banded_kernel(sink_ref, q_ref, kp_ref, kc_ref, vp_ref, vc_ref, bias_ref, o_ref, *, tq, pad, window, use_sink):
    i = pl.program_id(1)
    dist_prev = (lax.broadcasted_iota(jnp.int32, (tq, pad), 0) + pad
                 - lax.broadcasted_iota(jnp.int32, (tq, pad), 1))
    dist_cur = lax.broadcasted_iota(jnp.int32, (tq, tq), 0) - lax.broadcasted_iota(jnp.int32, (tq, tq), 1)
    mask_prev = (dist_prev < window) & (i > 0)
    mask_cur = (dist_cur >= 0) & (dist_cur < window)
    def scores(hd):
        kcols = slice((hd // 2) * HEAD_DIM, (hd // 2 + 1) * HEAD_DIM)
        q = q_ref[:, hd * HEAD_DIM:(hd + 1) * HEAD_DIM]
        return (jnp.where(mask_prev, _dot_nt(q, kp_ref[:, kcols]) + bias_ref[hd, :, :pad], NEG),
                jnp.where(mask_cur, _dot_nt(q, kc_ref[:, kcols]) + bias_ref[hd, :, pad:], NEG))

    ahead = 2
    pending = [scores(hd) for hd in range(ahead)]
    for hd in range(4):
        if hd + ahead < 4:
            pending.append(scores(hd + ahead))
        s_prev, s_cur = pending.pop(0)
        cols = slice(hd * HEAD_DIM, (hd + 1) * HEAD_DIM)
        kcols = slice((hd // 2) * HEAD_DIM, (hd // 2 + 1) * HEAD_DIM)
        m = jnp.maximum(jnp.max(s_prev, axis=-1, keepdims=True), jnp.max(s_cur, axis=-1, keepdims=True))
        if use_sink:
            sink = sink_ref[hd]
            m = jnp.maximum(m, sink)
        p_prev = jnp.where(mask_prev, jnp.exp(s_prev - m), 0.0)
        p_cur = jnp.where(mask_cur, jnp.exp(s_cur - m), 0.0)
        denom = jnp.sum(p_prev, axis=-1, keepdims=True) + jnp.sum(p_cur, axis=-1, keepdims=True)
        if use_sink:
            denom = denom + jnp.exp(sink - m)
        o = _dot(p_prev.astype(BF16), vp_ref[:, kcols]) + _dot(p_cur.astype(BF16), vc_ref[:, kcols])
        o_ref[:, cols] = o / jnp.maximum(denom, 1e-30)


BAND_TQ = 256


def _band_tiles(window):
    pad = -(-(window - 1) // LANES) * LANES
    return pad, max(pad, BAND_TQ)


def _banded_attention(pb, sinks, bias, batch, seq, window, q_col, k_col, v_col, use_sink):
    pad, tq = _band_tiles(window)
    per = tq // pad
    prev = lambda i: jnp.maximum(i * per - 1, 0)
    grid_spec = pltpu.PrefetchScalarGridSpec(
        num_scalar_prefetch=1,
        grid=(batch, seq // tq),
        in_specs=[
            pl.BlockSpec((None, tq, 256), lambda b, i, s: (b, i, q_col // 256)),
            pl.BlockSpec((None, pad, 128), lambda b, i, s: (b, prev(i), k_col // 128)),
            pl.BlockSpec((None, tq, 128), lambda b, i, s: (b, i, k_col // 128)),
            pl.BlockSpec((None, pad, 128), lambda b, i, s: (b, prev(i), v_col // 128)),
            pl.BlockSpec((None, tq, 128), lambda b, i, s: (b, i, v_col // 128)),
            pl.BlockSpec((4, tq, pad + tq), lambda b, i, s: (0, 0, 0)),
        ],
        out_specs=pl.BlockSpec((None, tq, 256), lambda b, i, s: (b, i, 0)),
    )
    return pl.pallas_call(
        functools.partial(_banded_kernel, tq=tq, pad=pad, window=window, use_sink=use_sink),
        out_shape=jax.ShapeDtypeStruct((batch, seq, 256), F32),
        grid_spec=grid_spec,
        compiler_params=pltpu.CompilerParams(dimension_semantics=("parallel", "arbitrary"),
                                             vmem_limit_bytes=VMEM_LIMIT),
        name="banded_w%d" % window,
    )(sinks, pb, pb, pb, pb, pb, bias)


def _compress_kernel(rows_ref, pos_ref, w1_ref, w2_ref, g_ref, o_ref):
    win = rows_ref[...] + pos_ref[...]
    hid = jax.nn.gelu(_dot(win.astype(BF16), w1_ref[...]), approximate=True)
    out = _dot(hid.astype(BF16), w2_ref[...])
    normed = _row_rms(out, HEAD_DIM) * g_ref[...]
    o_ref[...] = jnp.where(pl.program_id(0) == 0, normed, out).astype(BF16)


def _compress(rows, pos, w1, w2, gain, tn):
    _, bh, ncp, width = rows.shape
    return pl.pallas_call(
        _compress_kernel,
        out_shape=jax.ShapeDtypeStruct((2, bh, ncp, HEAD_DIM), BF16),
        grid=(2, bh, ncp // tn),
        in_specs=[
            pl.BlockSpec((None, None, tn, width), lambda c, r, n: (c, r, n, 0)),
            pl.BlockSpec((None, 1, width), lambda c, r, n: (c, 0, 0)),
            pl.BlockSpec((None, width, NSA_CMP_HIDDEN), lambda c, r, n: (c, 0, 0)),
            pl.BlockSpec((None, NSA_CMP_HIDDEN, HEAD_DIM), lambda c, r, n: (c, 0, 0)),
            pl.BlockSpec((1, HEAD_DIM), lambda c, r, n: (0, 0)),
        ],
        out_specs=pl.BlockSpec((None, None, tn, HEAD_DIM), lambda c, r, n: (c, r, n, 0)),
        compiler_params=pltpu.CompilerParams(dimension_semantics=("parallel", "parallel", "parallel"),
                                             vmem_limit_bytes=VMEM_LIMIT),
        name="nsa_compress",
    )(rows, pos, w1, w2, gain)


CMP_TQ = 1024


def _cmp_kernel(q_ref, kc_ref, vc_ref, bias_ref, ov_ref, o_ref, qa_ref, *, ncp, n_sel_pad, topk):
    i = pl.program_id(2)
    q_pos = i * CMP_TQ + lax.broadcasted_iota(jnp.int32, (CMP_TQ, ncp), 0)
    cmp_end = lax.broadcasted_iota(jnp.int32, (CMP_TQ, ncp), 1) * NSA_CMP_STRIDE + (NSA_CMP_LEN - 1)
    mask = cmp_end <= q_pos
    kc = kc_ref[...]
    vc = vc_ref[...]
    p_sum = jnp.zeros((CMP_TQ, ncp), F32)
    for g in range(2):
        cols = slice(g * HEAD_DIM, (g + 1) * HEAD_DIM)
        s = jnp.where(mask, _dot_nt(q_ref[:, cols], kc) + bias_ref[g], NEG)
        m = jnp.max(s, axis=-1, keepdims=True)
        p = jnp.where(mask, jnp.exp(s - m), 0.0)
        p = p / jnp.maximum(jnp.sum(p, axis=-1, keepdims=True), 1e-30)
        o_ref[:, cols] = _dot(p.astype(BF16), vc)
        p_sum = p_sum + p
    imp = _dot_exact_rhs(p_sum, ov_ref[...])

    row_pos = i * CMP_TQ + lax.broadcasted_iota(jnp.int32, (CMP_TQ, n_sel_pad), 0)
    blk = lax.broadcasted_iota(jnp.int32, (CMP_TQ, n_sel_pad), 1)
    cur = row_pos >> int(math.log2(NSA_SEL_LEN))
    forced = (blk == 0) | (blk == cur) | (blk == cur - 1)
    valid = blk * NSA_SEL_LEN <= row_pos
    score = jnp.where(valid, imp + jnp.where(forced, FORCE_BONUS, 0.0), NEG)
    blk_f = blk.astype(F32)
    dropped = jnp.ones((CMP_TQ, n_sel_pad), F32)
    for _ in range(topk):
        best = jnp.max(score, axis=-1, keepdims=True)
        first = jnp.min(jnp.where(score == best, blk_f, float(n_sel_pad)), axis=-1, keepdims=True)
        hit = blk_f == first
        dropped = jnp.where(hit, 0.0, dropped)
        score = jnp.where(hit, -jnp.inf, score)
    dropped = jnp.where(valid, dropped, 1.0).astype(BF16)
    q = q_ref[...]
    low = lax.broadcasted_iota(jnp.int32, q.shape, 1) < HEAD_DIM
    zero = jnp.zeros_like(q)
    width = LANES + n_sel_pad
    qa_ref[:, 0:LANES] = jnp.where(low, q, zero)
    qa_ref[:, LANES:width] = dropped
    qa_ref[:, width:width + LANES] = jnp.where(low, zero, q)
    qa_ref[:, width + LANES:2 * width] = dropped


def _cmp_attention(pb, kvc, bias_c, overlap, batch, seq):
    ncp = kvc.shape[2]
    n_sel_pad = overlap.shape[1]
    topk = min(NSA_TOPK, seq // NSA_SEL_LEN)
    pair = 2 * (LANES + n_sel_pad)
    return pl.pallas_call(
        functools.partial(_cmp_kernel, ncp=ncp, n_sel_pad=n_sel_pad, topk=topk),
        out_shape=(jax.ShapeDtypeStruct((batch, seq, 256), F32),
                   jax.ShapeDtypeStruct((batch, seq, 2 * pair), BF16)),
        grid=(batch, 2, seq // CMP_TQ),
        in_specs=[
            pl.BlockSpec((None, CMP_TQ, 128), lambda b, h, i: (b, i, PB_NSAQ // 128 + h)),
            pl.BlockSpec((None, None, ncp, HEAD_DIM), lambda b, h, i: (0, b * 2 + h, 0, 0)),
            pl.BlockSpec((None, None, ncp, HEAD_DIM), lambda b, h, i: (1, b * 2 + h, 0, 0)),
            pl.BlockSpec((2, CMP_TQ, ncp), lambda b, h, i: (h, i, 0)),
            pl.BlockSpec((ncp, n_sel_pad), lambda b, h, i: (0, 0)),
        ],
        out_specs=(pl.BlockSpec((None, CMP_TQ, 128), lambda b, h, i: (b, i, h)),
                   pl.BlockSpec((None, CMP_TQ, pair), lambda b, h, i: (b, i, h))),
        compiler_params=pltpu.CompilerParams(dimension_semantics=("parallel", "parallel", "arbitrary"),
                                             vmem_limit_bytes=VMEM_LIMIT),
        name="nsa_cmp_select",
    )(pb, kvc, kvc, bias_c, overlap)


SEL_T = 512


def _flash_init(m_ref, acc_ref):
    m_ref[...] = jnp.full_like(m_ref, NEG)
    acc_ref[...] = jnp.zeros_like(acc_ref)


def _flash_update(s, v_ones, m_ref, acc_ref, hd):
    m_old = m_ref[hd]
    m_new = jnp.maximum(m_old, jnp.max(s, axis=-1, keepdims=True))
    alpha = jnp.exp(m_old - m_new)
    pr = jnp.exp(s - jnp.tile(m_new, (1, s.shape[1] // LANES)))
    acc_ref[hd] = alpha * acc_ref[hd] + _dot(pr.astype(BF16), v_ones)
    m_ref[hd] = m_new


def _flash_finish(acc_ref, o_ref, heads):
    for hd in range(heads):
        acc = acc_ref[hd]
        row_sum = pltpu.roll(acc, HEAD_DIM, 1)
        o_ref[:, hd * HEAD_DIM:(hd + 1) * HEAD_DIM] = (acc / jnp.maximum(row_sum, 1e-30))[:, :HEAD_DIM]


def _sel_kernel(qi_ref, kj_ref, q_ref, k_ref, v_ref, pen_ref, bias_ref, o_ref, m_ref, acc_ref):
    p = pl.program_id(1)
    qi = qi_ref[p]
    kj = kj_ref[p]
    width = q_ref.shape[1] // 4

    @pl.when(kj == 0)
    def _():
        _flash_init(m_ref, acc_ref)

    def step(diagonal):
        pen = pen_ref[...]

        def scores(hd):
            k_pen = jnp.concatenate([k_ref[:, hd * LANES:(hd + 1) * LANES], pen], axis=1)
            s = _dot_nt(q_ref[:, hd * width:(hd + 1) * width], k_pen) + bias_ref[hd]
            if diagonal:
                row = lax.broadcasted_iota(jnp.int32, (SEL_T, SEL_T), 0)
                col = lax.broadcasted_iota(jnp.int32, (SEL_T, SEL_T), 1)
                s = jnp.where(col <= row, s, NEG)
            return s

        ahead = 3
        pending = [scores(hd) for hd in range(ahead)]
        for hd in range(4):
            if hd + ahead < 4:
                pending.append(scores(hd + ahead))
            _flash_update(pending.pop(0), v_ref[:, (hd // 2) * LANES:(hd // 2 + 1) * LANES], m_ref, acc_ref, hd)

    @pl.when(kj < qi)
    def _():
        step(False)

    @pl.when(kj == qi)
    def _():
        step(True)
        _flash_finish(acc_ref, o_ref, 4)


def _causal_pairs(n):
    qi, kj = [], []
    for i in range(n):
        for j in range(i + 1):
            qi.append(i)
            kj.append(j)
    return jnp.asarray(np.array(qi, np.int32)), jnp.asarray(np.array(kj, np.int32))


def _sel_attention(pb, q_aug, penalty, bias_s, batch, seq):
    n_sel_pad = penalty.shape[1]
    n_delta = bias_s.shape[1]
    qi, kj = _causal_pairs(seq // SEL_T)
    grid_spec = pltpu.PrefetchScalarGridSpec(
        num_scalar_prefetch=2,
        grid=(batch, int(qi.shape[0])),
        in_specs=[
            pl.BlockSpec((None, SEL_T, q_aug.shape[2]), lambda b, p, qi, kj: (b, qi[p], 0)),
            pl.BlockSpec((None, SEL_T, 512), lambda b, p, qi, kj: (b, kj[p], PB_NSAKS // 512)),
            pl.BlockSpec((None, SEL_T, 256), lambda b, p, qi, kj: (b, kj[p], PB_NSAVS // 256)),
            pl.BlockSpec((SEL_T, n_sel_pad), lambda b, p, qi, kj: (kj[p], 0)),
            pl.BlockSpec((4, None, SEL_T, SEL_T),
                         lambda b, p, qi, kj: (0, jnp.minimum(qi[p] - kj[p], n_delta - 1), 0, 0)),
        ],
        out_specs=pl.BlockSpec((None, SEL_T, 256), lambda b, p, qi, kj: (b, qi[p], 0)),
        scratch_shapes=[pltpu.VMEM((4, SEL_T, LANES), F32), pltpu.VMEM((4, SEL_T, LANES), F32)],
    )
    return pl.pallas_call(
        _sel_kernel,
        out_shape=jax.ShapeDtypeStruct((batch, seq, 256), F32),
        grid_spec=grid_spec,
        compiler_params=pltpu.CompilerParams(dimension_semantics=("parallel", "arbitrary"),
                                             vmem_limit_bytes=VMEM_LIMIT),
        name="nsa_selected",
    )(qi, kj, q_aug, pb, pb, penalty, bias_s)


MLA_T = 1024


def _mla_kernel(qi_ref, kj_ref, q_ref, k_ref, v_ref, o_ref, m_ref, acc_ref):
    p = pl.program_id(1)
    qi = qi_ref[p]
    kj = kj_ref[p]

    @pl.when(kj == 0)
    def _():
        _flash_init(m_ref, acc_ref)

    def step(diagonal):
        def scores(hd):
            cols = slice(hd * LANES, (hd + 1) * LANES)
            s = _dot_nt(q_ref[:, cols], k_ref[:, cols])
            if diagonal:
                row = lax.broadcasted_iota(jnp.int32, (MLA_T, MLA_T), 0)
                col = lax.broadcasted_iota(jnp.int32, (MLA_T, MLA_T), 1)
                s = jnp.where(col <= row, s, NEG)
            return s

        ahead = 3
        pending = [scores(hd) for hd in range(ahead)]
        for hd in range(MLA_HEADS):
            if hd + ahead < MLA_HEADS:
                pending.append(scores(hd + ahead))
            _flash_update(pending.pop(0), v_ref[:, hd * LANES:(hd + 1) * LANES], m_ref, acc_ref, hd)

    @pl.when(kj < qi)
    def _():
        step(False)

    @pl.when(kj == qi)
    def _():
        step(True)
        _flash_finish(acc_ref, o_ref, MLA_HEADS)


def _mla_attention(pb, batch, seq):
    qi, kj = _causal_pairs(seq // MLA_T)
    grid_spec = pltpu.PrefetchScalarGridSpec(
        num_scalar_prefetch=2,
        grid=(batch, int(qi.shape[0])),
        in_specs=[
            pl.BlockSpec((None, MLA_T, 512), lambda b, p, qi, kj: (b, qi[p], PB_MLAQ // 512)),
            pl.BlockSpec((None, MLA_T, 512), lambda b, p, qi, kj: (b, kj[p], PB_MLAK // 512)),
            pl.BlockSpec((None, MLA_T, 512), lambda b, p, qi, kj: (b, kj[p], PB_MLAV // 512)),
        ],
        out_specs=pl.BlockSpec((None, MLA_T, 256), lambda b, p, qi, kj: (b, qi[p], 0)),
        scratch_shapes=[pltpu.VMEM((4, MLA_T, LANES), F32), pltpu.VMEM((4, MLA_T, LANES), F32)],
    )
    return pl.pallas_call(
        _mla_kernel,
        out_shape=jax.ShapeDtypeStruct((batch, seq, 256), F32),
        grid_spec=grid_spec,
        compiler_params=pltpu.CompilerParams(dimension_semantics=("parallel", "arbitrary"),
                                             vmem_limit_bytes=VMEM_LIMIT),
        name="mla_causal",
    )(qi, kj, pb, pb, pb)


def _outproj_kernel(x_ref, oa_ref, ob_ref, oc_ref, os_ref, ow_ref, od_ref, gate_ref, gexp_ref, gn_ref, w_ref,
                    o_ref):
    gates = gate_ref[...]
    g_hi, g_lo = _split_bf16(gates)

    def gate(branch):
        e = gexp_ref[branch]
        return _dot(g_hi, e) + _dot(g_lo, e)

    o_nsa = gate(0) * oc_ref[...] + gate(1) * os_ref[...] + gate(2) * ow_ref[...]
    gn = gn_ref[...]
    y = x_ref[...]
    for grp, o in enumerate((oa_ref[...], ob_ref[...], o_nsa, od_ref[...])):
        cols = slice(grp * GROUP_WIDTH, (grp + 1) * GROUP_WIDTH)
        normed = _row_rms(o, GROUP_WIDTH) * gn[:, cols]
        y = y + _dot(normed.astype(BF16), w_ref[cols, :])
    o_ref[...] = y


def _outproj(x2, oa, ob, oc, osel, ow, od, pf, gexp, gn, w, tm):
    tokens = x2.shape[0]
    row = lambda t: (t, 0)
    o_spec = pl.BlockSpec((tm, GROUP_WIDTH), row)
    return pl.pallas_call(
        _outproj_kernel,
        out_shape=jax.ShapeDtypeStruct((tokens, D_MODEL), F32),
        grid=(tokens // tm,),
        in_specs=[
            pl.BlockSpec((tm, D_MODEL), row), o_spec, o_spec, o_spec, o_spec, o_spec, o_spec,
            pl.BlockSpec((tm, 128), lambda t: (t, PF_GATE // 128)),
            pl.BlockSpec((3, 128, GROUP_WIDTH), lambda t: (0, 0, 0)),
            pl.BlockSpec((1, D_MODEL), lambda t: (0, 0)),
            pl.BlockSpec((D_MODEL, D_MODEL), lambda t: (0, 0)),
        ],
        out_specs=pl.BlockSpec((tm, D_MODEL), row),
        compiler_params=pltpu.CompilerParams(dimension_semantics=("parallel",), vmem_limit_bytes=VMEM_LIMIT),
        name="out_proj",
    )(x2, oa, ob, oc, osel, ow, od, pf, gexp, gn, w)


MOE_TM = 1024
MOE_QUAD = 4
MOE_CHUNK = 256
MOE_CHUNK_ALIGN = 256
MOE_SLOTS = MOE_TM + MOE_GROUPS * MOE_CHUNK
MOE_VMEM_LIMIT = 60 * 1024 * 1024


def _moe_kernel(x_ref, fn_ref, wr_hi_ref, wr_lo_ref, br_ref, cexp_ref, wg_ref, wu_ref, wd_ref, o_ref, hs_ref,
                cs_ref, ys_ref, slot_ref, start_ref):
    step = pl.program_id(1)
    lane = lax.broadcasted_iota(jnp.int32, (MOE_TM, LANES), 1)

    @pl.when(step == 0)
    def _():
        h = _row_rms(x_ref[...], D_MODEL) * fn_ref[...]
        h_hi, h_lo = _split_bf16(h)
        logits = (_dot(h_hi, wr_hi_ref[...]) + _dot(h_lo, wr_hi_ref[...]) + _dot(h_hi, wr_lo_ref[...])
                  + br_ref[...])
        lane_f = lane.astype(F32)
        no_lane = float(LANES)
        is_group = lane < MOE_GROUPS
        g_max = jnp.max(jnp.where(is_group, logits, -jnp.inf), axis=-1, keepdims=True)
        g_star = jnp.min(jnp.where(is_group & (logits == g_max), lane_f, no_lane), axis=-1, keepdims=True)
        g_den = jnp.sum(jnp.where(is_group, jnp.exp(logits - g_max), 0.0), axis=-1, keepdims=True)
        g_w = 1.0 / g_den
        group_of_lane = ((lane - MOE_GROUPS) >> int(math.log2(MOE_EPG))).astype(F32)
        in_group = (lane >= MOE_GROUPS) & (lane < MOE_GROUPS + MOE_EXPERTS) & (group_of_lane == g_star)
        e_l = jnp.where(in_group, logits, -jnp.inf)
        top1 = jnp.max(e_l, axis=-1, keepdims=True)
        i1 = jnp.min(jnp.where(e_l == top1, lane_f, no_lane), axis=-1, keepdims=True)
        e_l2 = jnp.where(lane_f == i1, -jnp.inf, e_l)
        top2 = jnp.max(e_l2, axis=-1, keepdims=True)
        i2 = jnp.min(jnp.where(e_l2 == top2, lane_f, no_lane), axis=-1, keepdims=True)
        r = jnp.exp(top2 - top1)
        w1 = g_w / (1.0 + r)
        w2 = g_w * r / (1.0 + r)
        comb = jnp.where(lane_f == i1, w1, jnp.where(lane_f == i2, w2, 0.0))

        onehot = jnp.where(lane_f == g_star, 1.0, 0.0)
        row = lax.broadcasted_iota(jnp.int32, (MOE_TM, LANES), 0)
        incl = onehot
        shift = 1
        while shift < MOE_TM:
            incl = incl + jnp.where(row >= shift, pltpu.roll(incl, shift, 0), 0.0)
            shift *= 2
        counts = incl[MOE_TM - 1:MOE_TM, :]
        padded = jnp.floor((counts + (MOE_CHUNK - 0.5)) * (1.0 / MOE_CHUNK)) * MOE_CHUNK
        before = (lax.broadcasted_iota(jnp.int32, (LANES, LANES), 0)
                  < lax.broadcasted_iota(jnp.int32, (LANES, LANES), 1))
        starts = _dot(jnp.broadcast_to(padded, (8, LANES)).astype(BF16),
                      jnp.where(before, 1.0, 0.0).astype(BF16))[0:1, :]
        for g in range(MOE_GROUPS + 1):
            start_ref[g] = jnp.sum(jnp.where(lane[0:1, :] == g, starts, 0.0)).astype(jnp.int32)
        slot = jnp.sum(onehot * (starts + incl - onehot), axis=-1, keepdims=True)
        slot_ref[...] = jnp.broadcast_to(slot, (MOE_TM, LANES))
        slot_row = slot_ref[...].T[0:1, :]
        n_slots = hs_ref.shape[0]
        place = jnp.where(lax.broadcasted_iota(jnp.int32, (n_slots, MOE_TM), 0).astype(F32) == slot_row,
                          1.0, 0.0).astype(BF16)
        hs_ref[...] = _dot(place, h.astype(BF16)).astype(BF16)
        c_hi, c_lo = _split_bf16(comb)
        cs_ref[:, :LANES] = _dot(place, c_hi).astype(BF16)
        cs_ref[:, LANES:] = _dot(place, c_lo).astype(BF16)
        ys_ref[...] = jnp.zeros_like(ys_ref)

    group = step // (MOE_EPG // MOE_QUAD)
    first_slot = start_ref[group]
    half = MOE_QUAD * MOE_HIDDEN // 2

    def chunk(c, carry):
        rows = pl.ds(pl.multiple_of(first_slot + c * MOE_CHUNK, MOE_CHUNK_ALIGN), MOE_CHUNK)
        hb = hs_ref[rows, :]
        weight = _dot(cs_ref[rows, :], cexp_ref[...])
        y = None
        for s in range(2):
            cols = slice(s * half, (s + 1) * half)
            a = jax.nn.silu(_dot(hb, wg_ref[:, cols])) * _dot(hb, wu_ref[:, cols]) * weight[:, cols]
            part = _dot(a.astype(BF16), wd_ref[cols, :])
            y = part if y is None else y + part
        ys_ref[rows, :] += y
        return carry

    lax.fori_loop(0, (start_ref[group + 1] - first_slot) // MOE_CHUNK, chunk, 0)

    @pl.when(step == MOE_EXPERTS // MOE_QUAD - 1)
    def _():
        n_slots = hs_ref.shape[0]
        back = jnp.where(lax.broadcasted_iota(jnp.int32, (MOE_TM, n_slots), 1).astype(F32) == slot_ref[:, 0:1],
                         1.0, 0.0).astype(BF16)
        y_hi, y_lo = _split_bf16(ys_ref[...])
        o_ref[...] = x_ref[...] + _dot(back, y_hi) + _dot(back, y_lo)


def _moe(x2, fn, wr_hi, wr_lo, br, wg, wu, wd):
    tokens = x2.shape[0]
    quads = MOE_EXPERTS // MOE_QUAD
    width = MOE_QUAD * MOE_HIDDEN
    const = lambda t, e: (0, 0)
    lane_of_col = MOE_GROUPS + np.arange(quads)[:, None, None] * MOE_QUAD + np.arange(width)[None, None, :] // MOE_HIDDEN
    cexp = jnp.asarray((np.arange(2 * LANES)[None, :, None] % LANES) == lane_of_col, BF16)
    return pl.pallas_call(
        _moe_kernel,
        out_shape=jax.ShapeDtypeStruct((tokens, D_MODEL), F32),
        grid=(tokens // MOE_TM, quads),
        in_specs=[
            pl.BlockSpec((MOE_TM, D_MODEL), lambda t, e: (t, 0)),
            pl.BlockSpec((1, D_MODEL), const),
            pl.BlockSpec((D_MODEL, LANES), const),
            pl.BlockSpec((D_MODEL, LANES), const),
            pl.BlockSpec((1, LANES), const),
            pl.BlockSpec((None, 2 * LANES, width), lambda t, e: (e, 0, 0)),
            pl.BlockSpec((None, D_MODEL, width), lambda t, e: (e, 0, 0)),
            pl.BlockSpec((None, D_MODEL, width), lambda t, e: (e, 0, 0)),
            pl.BlockSpec((None, width, D_MODEL), lambda t, e: (e, 0, 0)),
        ],
        out_specs=pl.BlockSpec((MOE_TM, D_MODEL), lambda t, e: (t, 0)),
        scratch_shapes=[pltpu.VMEM((MOE_SLOTS, D_MODEL), BF16), pltpu.VMEM((MOE_SLOTS, 2 * LANES), BF16),
                        pltpu.VMEM((MOE_SLOTS, D_MODEL), F32), pltpu.VMEM((MOE_TM, LANES), F32),
                        pltpu.SMEM((8,), jnp.int32)],
        compiler_params=pltpu.CompilerParams(dimension_semantics=("parallel", "arbitrary"),
                                             vmem_limit_bytes=MOE_VMEM_LIMIT),
        name="hier_moe",
    )(x2, fn, wr_hi, wr_lo, br, cexp, wg, wu, wd)


def _t5_bucket(dist):
    n = jnp.maximum(dist, 0)
    max_exact = T5_BUCKETS // 2
    nf = jnp.maximum(n, 1).astype(F32)
    large = max_exact + (jnp.log(nf / max_exact) / math.log(T5_MAX_DIST / max_exact)
                         * (T5_BUCKETS - max_exact)).astype(jnp.int32)
    large = jnp.minimum(large, T5_BUCKETS - 1)
    return jnp.where(n < max_exact, n, large)


def _sel_delta_cap():
    max_exact = T5_BUCKETS // 2
    span = T5_BUCKETS - max_exact
    last_bucket_from = max_exact * (T5_MAX_DIST / max_exact) ** ((span - 1) / span)
    cap = 1
    while (cap - 1) * SEL_T + 1 < 1.25 * last_bucket_from:
        cap += 1
    return cap


def _position_tables(rel_bias, seq):
    buckets = _t5_bucket(jnp.arange(seq))
    first = jnp.sum(buckets[None, :] < jnp.arange(T5_BUCKETS)[:, None], axis=1)
    tbl = rel_bias.T

    def toeplitz(heads, dist):
        shape = (tbl[heads].shape[0],) + (1,) * dist.ndim
        out = jnp.broadcast_to(tbl[heads][:, 0].reshape(shape), shape[:1] + dist.shape)
        for b in range(1, T5_BUCKETS):
            out = jnp.where((dist >= first[b])[None], tbl[heads][:, b].reshape(shape), out)
        return out

    swa_h, nsa_h = slice(0, 4), slice(4, 8)
    def band(window):
        pad, tq = _band_tiles(window)
        return jnp.arange(tq)[:, None] + pad - jnp.arange(pad + tq)[None, :]

    bias_swa = toeplitz(swa_h, band(SWA_WINDOW))
    bias_win = toeplitz(nsa_h, band(NSA_WINDOW))
    ncp = seq // NSA_CMP_STRIDE
    cmp_end = jnp.arange(ncp) * NSA_CMP_STRIDE + NSA_CMP_LEN - 1
    bias_cmp = toeplitz(nsa_h, jnp.arange(seq)[:, None] - cmp_end[None, :])
    nd = min(_sel_delta_cap() + 1, seq // SEL_T)
    dist_s = (jnp.arange(nd)[:, None, None] * SEL_T + jnp.arange(SEL_T)[None, :, None]
              - jnp.arange(SEL_T)[None, None, :])
    bias_sel = toeplitz(nsa_h, dist_s)

    n_sel_pad = -(-(seq // NSA_SEL_LEN) // LANES) * LANES
    sel_start = np.arange(n_sel_pad) * NSA_SEL_LEN
    c_start = np.arange(ncp) * NSA_CMP_STRIDE
    c_end = c_start + NSA_CMP_LEN - 1
    real = (np.arange(ncp) < ncp - NSA_CMP_LEN // NSA_CMP_STRIDE + 1)[:, None] & (sel_start < seq)[None, :]
    overlap = ((c_start[:, None] < sel_start[None, :] + NSA_SEL_LEN) & (c_end[:, None] >= sel_start[None, :]) & real)
    own_block = (np.arange(seq) // NSA_SEL_LEN)[:, None] == np.arange(n_sel_pad)[None, :]
    penalty = np.where(own_block, -2.0 ** 100, 0.0)

    pos = jnp.arange(seq, dtype=F32)
    inv_freq = ROPE_THETA ** (-jnp.arange(0, MLA_ROPE, 2, dtype=F32) / MLA_ROPE)
    ang = pos[:, None] * inv_freq[None, :]
    cos, sin = jnp.cos(ang), jnp.sin(ang)
    ones = jnp.ones((seq, MLA_NOPE), F32)
    tail = LANES - MLA_NOPE - MLA_ROPE
    cos_t = jnp.concatenate([ones, cos, cos, jnp.ones((seq, tail), F32)], axis=1)
    sin_t = jnp.concatenate([0 * ones, -sin, sin, jnp.zeros((seq, tail), F32)], axis=1)
    return dict(bias_swa=bias_swa, bias_win=bias_win, bias_cmp=bias_cmp, bias_sel=bias_sel,
                overlap=jnp.asarray(overlap, BF16), penalty=jnp.asarray(penalty, BF16), cos_t=cos_t, sin_t=sin_t)


def _pad_to(a, shape):
    return jnp.pad(a, [(0, s - d) for d, s in zip(a.shape, shape)])


def _pack_layer(w_in, swa_q_norm, swa_k_norm, nsa_q_norm, nsa_k_norm, mla_q_lat_norm, mla_w_q_up,
                mla_kv_lat_norm, mla_w_kv_up, mla_q_norm, mla_k_norm):
    kpe = w_in[:, 2636:2668]
    kpe_seg = jnp.concatenate([jnp.zeros((D_MODEL, MLA_NOPE), F32), kpe,
                               jnp.zeros((D_MODEL, LANES - MLA_QK), F32)], axis=1)
    spread = lambda cols: _pad_to(cols.reshape(D_MODEL, -1, HEAD_DIM), (D_MODEL, cols.shape[1] // HEAD_DIM, LANES)
                                  ).reshape(D_MODEL, -1)
    w = jnp.concatenate([
        w_in[:, :1920],
        spread(w_in[:, 1920:2048]),
        w_in[:, 2048:2304],
        _pad_to(w_in[:, 2304:2316], (D_MODEL, 128)),
        _pad_to(w_in[:, 2316:2508], (D_MODEL, 256)),
        w_in[:, 2508:2636],
        jnp.tile(kpe_seg, (1, MLA_HEADS)),
    ], axis=1).astype(BF16)
    tile4 = lambda g: jnp.tile(g, 4)
    g64 = _pad_to(jnp.stack([tile4(swa_q_norm), tile4(swa_k_norm), tile4(nsa_q_norm),
                             tile4(nsa_k_norm[1]), tile4(nsa_k_norm[2])]), (8, 256))
    glat = _pad_to(jnp.stack([_pad_to(mla_q_lat_norm, (256,)), _pad_to(mla_kv_lat_norm, (256,))]), (8, 256))
    wq = _pad_to(mla_w_q_up.reshape(MLA_Q_RANK, MLA_HEADS, MLA_QK), (256, MLA_HEADS, LANES))
    wq = wq.reshape(256, MLA_HEADS * LANES).astype(BF16)
    wkv = mla_w_kv_up.reshape(MLA_KV_RANK, MLA_HEADS, MLA_NOPE + MLA_V)
    wk = _pad_to(wkv[:, :, :MLA_NOPE], (MLA_KV_RANK, MLA_HEADS, LANES)).reshape(MLA_KV_RANK, MLA_HEADS * LANES)
    wv = _pad_to(wkv[:, :, MLA_NOPE:], (MLA_KV_RANK, MLA_HEADS, LANES)).reshape(MLA_KV_RANK, MLA_HEADS * LANES)
    wkv_p = jnp.concatenate([wk, wv], axis=1).astype(BF16)
    gmla = _pad_to(jnp.stack([jnp.tile(_pad_to(mla_q_norm, (LANES,)), MLA_HEADS),
                              jnp.tile(_pad_to(mla_k_norm, (LANES,)), MLA_HEADS)]), (8, 512))
    return w, g64, glat, wq, wkv_p, gmla


def _gate_expand():
    rows = np.arange(128)[None, :, None]
    cols = np.arange(GROUP_WIDTH)[None, None, :]
    branch = np.arange(3)[:, None, None]
    return jnp.asarray(rows == branch * 4 + cols // HEAD_DIM, BF16)


def _compress_rows(pf3, batch, seq):
    nb = seq // NSA_CMP_STRIDE
    kv = pf3[:, :, :256].reshape(batch, seq, 2, 2, HEAD_DIM).transpose(2, 0, 3, 1, 4)
    blocks = kv.reshape(2, batch * 2, nb, NSA_CMP_STRIDE * HEAD_DIM)
    nxt = jnp.concatenate([blocks[:, :, 1:], jnp.zeros_like(blocks[:, :, :1])], axis=2)
    return jnp.concatenate([blocks, nxt], axis=3)


def kernel(x, rel_bias, attn_norm, w_in, swa_q_norm, swa_k_norm, swa_sinks, nsa_q_norm, nsa_k_norm, nsa_cmp_pos, nsa_cmp_w1, nsa_cmp_w2, mla_q_lat_norm, mla_w_q_up, mla_kv_lat_norm, mla_w_kv_up, mla_q_norm, mla_k_norm, out_norm, w_out, ffn_norm, moe_w_group, moe_b_group, moe_w_expert, moe_b_expert, moe_w_gate, moe_w_up, moe_w_down):
    batch, seq, _ = x.shape
    depth = w_in.shape[0]
    tokens = batch * seq
    tm = 512
    assert seq % 2048 == 0 and tokens % MOE_TM == 0
    tabs = _position_tables(rel_bias, seq)
    gexp = _gate_expand()
    zero_sinks = jnp.zeros((4,), F32)
    x2 = x.reshape(tokens, D_MODEL)
    for l in range(depth):
        w, g64, glat, wq, wkv, gmla = _pack_layer(
            w_in[l], swa_q_norm[l], swa_k_norm[l], nsa_q_norm[l], nsa_k_norm[l], mla_q_lat_norm[l],
            mla_w_q_up[l], mla_kv_lat_norm[l], mla_w_kv_up[l], mla_q_norm[l], mla_k_norm[l])
        pb, pf = _prep(x2, attn_norm[l][None, :], w, g64, glat, wq, wkv, gmla, tabs["cos_t"], tabs["sin_t"],
                       seq, tm)
        pb3 = pb.reshape(batch, seq, PB_WIDTH)
        pf3 = pf.reshape(batch, seq, PF_WIDTH)
        o_a = _sb_attention(pb3, batch, seq)
        o_b = _banded_attention(pb3, swa_sinks[l], tabs["bias_swa"], batch, seq, SWA_WINDOW,
                                PB_SWAQ, PB_SWAK, PB_SWAV, True)
        rows = _compress_rows(pf3, batch, seq)
        kvc = _compress(rows, nsa_cmp_pos[l].reshape(2, 1, -1),
                        nsa_cmp_w1[l].reshape(2, -1, NSA_CMP_HIDDEN).astype(BF16), nsa_cmp_w2[l].astype(BF16),
                        nsa_k_norm[l][0][None, :], 128)
        o_c, q_aug = _cmp_attention(pb3, kvc, tabs["bias_cmp"], tabs["overlap"], batch, seq)
        o_s = _sel_attention(pb3, q_aug, tabs["penalty"], tabs["bias_sel"], batch, seq)
        o_w = _banded_attention(pb3, zero_sinks, tabs["bias_win"], batch, seq, NSA_WINDOW,
                                PB_NSAQ, PB_NSAKW, PB_NSAVW, False)
        o_d = _mla_attention(pb3, batch, seq)
        flat = lambda o: o.reshape(tokens, GROUP_WIDTH)
        x2 = _outproj(x2, flat(o_a), flat(o_b), flat(o_c), flat(o_s), flat(o_w), flat(o_d), pf, gexp,
                      out_norm[l][None, :], w_out[l].astype(BF16), tm)
        w_router = _pad_to(jnp.concatenate([moe_w_group[l], moe_w_expert[l]], axis=1), (D_MODEL, LANES))
        wr_hi = w_router.astype(BF16)
        wr_lo = (w_router - wr_hi.astype(F32)).astype(BF16)
        b_router = _pad_to(jnp.concatenate([moe_b_group[l], moe_b_expert[l]])[None, :], (1, LANES))
        quads = MOE_EXPERTS // MOE_QUAD
        by_quad = lambda w: w.astype(BF16).reshape(quads, MOE_QUAD, D_MODEL, MOE_HIDDEN).transpose(0, 2, 1, 3
                                                   ).reshape(quads, D_MODEL, MOE_QUAD * MOE_HIDDEN)
        x2 = _moe(x2, ffn_norm[l][None, :], wr_hi, wr_lo, b_router, by_quad(moe_w_gate[l]), by_quad(moe_w_up[l]),
                  moe_w_down[l].astype(BF16).reshape(quads, MOE_QUAD * MOE_HIDDEN, D_MODEL))
    return x2.reshape(batch, seq, D_MODEL)
```

```python
import functools
import math

import numpy as np
import jax
import jax.numpy as jnp
from jax import lax
from jax.experimental import pallas as pl
from jax.experimental.pallas import tpu as pltpu

F32 = jnp.float32
BF16 = jnp.bfloat16

D_MODEL = 1024
HEAD_DIM = 64
NEG = -1e30
EPS = 1e-6
FORCE_BONUS = 1000.0
SWA_WINDOW = 128
NSA_CMP_LEN = 32
NSA_CMP_STRIDE = 16
NSA_CMP_HIDDEN = 128
NSA_SEL_LEN = 64
NSA_TOPK = 16
NSA_WINDOW = 512
MLA_HEADS = 4
MLA_NOPE = 64
MLA_ROPE = 32
MLA_V = 64
MLA_Q_RANK = 192
MLA_KV_RANK = 128
MLA_QK = MLA_NOPE + MLA_ROPE
ROPE_THETA = 10000.0
T5_BUCKETS = 32
T5_MAX_DIST = 1024
MOE_GROUPS = 4
MOE_EPG = 8
MOE_EXPERTS = MOE_GROUPS * MOE_EPG
MOE_HIDDEN = 256
GROUP_WIDTH = 256
LANES = 128
VMEM_LIMIT = 48 * 1024 * 1024

PB_MLAQ, PB_MLAK, PB_MLAV = 0, 512, 1024
PB_SBQ, PB_SBK, PB_SBV = 1536, 1792, 2048
PB_SWAQ, PB_NSAQ, PB_NSAVS = 2304, 2560, 2816
PB_SWAK, PB_SWAV, PB_NSAKW, PB_NSAVW = 3072, 3200, 3328, 3456
PB_NSAKS = 3584
PB_WIDTH = 4096
PF_KC, PF_VC, PF_GATE = 0, 128, 256
PF_WIDTH = 384
W_SBQ, W_SBK, W_SBV, W_SWAQ, W_SWAK, W_SWAV, W_NSAQ = 0, 256, 512, 768, 1024, 1152, 1280
W_KC, W_VC, W_KS, W_VS, W_KW, W_VW, W_GATE = 1536, 1664, 1792, 1920, 2176, 2304, 2432
W_CQ, W_CKV, W_KPE = 2560, 2816, 2944
W_WIDTH = 3456

NT_DIMS = (((1,), (1,)), ((), ()))


def _dot(a, b):
    return jnp.dot(a, b, preferred_element_type=F32)


def _dot_nt(a, b):
    return lax.dot_general(a, b, NT_DIMS, preferred_element_type=F32)


def _split_bf16(x):
    hi = x.astype(BF16)
    lo = (x - hi.astype(F32)).astype(BF16)
    return hi, lo


def _dot_exact_rhs(x, m):
    hi, lo = _split_bf16(x)
    return _dot(hi, m) + _dot(lo, m)


def _block_diag_ones(width, seg):
    idx = np.arange(width) // seg
    return jnp.asarray(idx[:, None] == idx[None, :], BF16)


def _seg_rms(x, seg_ones, count):
    width = x.shape[1]
    ms = _dot_exact_rhs(x * x, seg_ones[:width, :width]) * (1.0 / count)
    return x * lax.rsqrt(ms + EPS)


def _row_rms(x, count):
    return x * lax.rsqrt(jnp.sum(x * x, axis=-1, keepdims=True) * (1.0 / count) + EPS)


def _prep_kernel(x_ref, an_ref, w_ref, g64_ref, glat_ref, wq_ref, wkv_ref, gmla_ref, cos_ref, sin_ref,
                 s64_ref, s128_ref, pb_ref, pf_ref):
    x = x_ref[...]
    s64 = s64_ref[...]
    s128 = s128_ref[...]
    h = _row_rms(x, D_MODEL) * an_ref[...]
    hb = h.astype(BF16)

    def proj(lo, hi):
        return _dot(hb, w_ref[:, lo:hi])

    def put(col, value):
        pb_ref[:, col:col + value.shape[1]] = value.astype(BF16)

    def ones_tail(width):
        lane = lax.broadcasted_iota(jnp.int32, (1, width), 1)
        return jnp.where((lane & (LANES - 1)) >= HEAD_DIM, 1.0, 0.0)

    scale = HEAD_DIM ** -0.5
    g64 = g64_ref[...]
    put(PB_SBQ, proj(W_SBQ, W_SBQ + 256) * scale)
    put(PB_SBK, proj(W_SBK, W_SBK + 256))
    put(PB_SBV, proj(W_SBV, W_SBV + 256))
    put(PB_SWAQ, _seg_rms(proj(W_SWAQ, W_SWAQ + 256), s64, 64) * g64[0:1, :] * scale)
    put(PB_SWAK, _seg_rms(proj(W_SWAK, W_SWAK + 128), s64, 64) * g64[1:2, :128])
    put(PB_SWAV, proj(W_SWAV, W_SWAV + 128))
    put(PB_NSAQ, _seg_rms(proj(W_NSAQ, W_NSAQ + 256), s64, 64) * g64[2:3, :] * scale)
    pf_ref[:, PF_KC:PF_KC + 128] = proj(W_KC, W_KC + 128)
    pf_ref[:, PF_VC:PF_VC + 128] = proj(W_VC, W_VC + 128)
    ks = _seg_rms(proj(W_KS, W_KS + 128), s64, 64) * g64[3:4, :128]
    ks_swapped = pltpu.roll(ks, HEAD_DIM, 1)
    low = lax.broadcasted_iota(jnp.int32, ks.shape, 1) < HEAD_DIM
    put(PB_NSAKS, jnp.where(low, ks, 0.0))
    put(PB_NSAKS + LANES, jnp.where(low, 0.0, ks_swapped))
    put(PB_NSAKS + 2 * LANES, jnp.where(low, ks_swapped, 0.0))
    put(PB_NSAKS + 3 * LANES, jnp.where(low, 0.0, ks))
    put(PB_NSAVS, proj(W_VS, W_VS + 256) + ones_tail(256))
    put(PB_NSAKW, _seg_rms(proj(W_KW, W_KW + 128), s64, 64) * g64[4:5, :128])
    put(PB_NSAVW, proj(W_VW, W_VW + 128))
    pf_ref[:, PF_GATE:PF_GATE + 128] = jax.nn.sigmoid(proj(W_GATE, W_GATE + 128))

    glat = glat_ref[...]
    cq = _row_rms(proj(W_CQ, W_CQ + 256), MLA_Q_RANK) * glat[0:1, :]
    q = _dot(cq.astype(BF16), wq_ref[...])
    ckv = _row_rms(proj(W_CKV, W_CKV + 128), MLA_KV_RANK) * glat[1:2, :128]
    kv = _dot(ckv.astype(BF16), wkv_ref[...])
    k = kv[:, :512] + proj(W_KPE, W_KPE + 512)
    gm = gmla_ref[...]
    q = _seg_rms(q, s128, MLA_QK) * gm[0:1, :]
    k = _seg_rms(k, s128, MLA_QK) * gm[1:2, :]
    cos = cos_ref[...]
    sin = sin_ref[...]
    lane = lax.broadcasted_iota(jnp.int32, (x.shape[0], LANES), 1)
    first_half = lane < MLA_NOPE + MLA_ROPE // 2

    def rope(t):
        partner = jnp.where(first_half, pltpu.roll(t, LANES - MLA_ROPE // 2, 1), pltpu.roll(t, MLA_ROPE // 2, 1))
        return t * cos + partner * sin

    qscale = MLA_QK ** -0.5
    for hd in range(MLA_HEADS):
        sl = slice(hd * LANES, (hd + 1) * LANES)
        put(PB_MLAQ + hd * LANES, rope(q[:, sl]) * qscale)
        put(PB_MLAK + hd * LANES, rope(k[:, sl]))
    put(PB_MLAV, kv[:, 512:1024] + ones_tail(512))


def _prep(x2, an, w, g64, glat, wq, wkv, gmla, cos_t, sin_t, seq, tm):
    tokens = x2.shape[0]
    n_pos = seq // tm
    const = lambda t: (0, 0)
    return pl.pallas_call(
        _prep_kernel,
        out_shape=(jax.ShapeDtypeStruct((tokens, PB_WIDTH), BF16), jax.ShapeDtypeStruct((tokens, PF_WIDTH), F32)),
        grid=(tokens // tm,),
        in_specs=[
            pl.BlockSpec((tm, D_MODEL), lambda t: (t, 0)),
            pl.BlockSpec((1, D_MODEL), const),
            pl.BlockSpec((D_MODEL, W_WIDTH), const),
            pl.BlockSpec((8, 256), const),
            pl.BlockSpec((8, 256), const),
            pl.BlockSpec((256, 512), const),
            pl.BlockSpec((128, 1024), const),
            pl.BlockSpec((8, 512), const),
            pl.BlockSpec((tm, LANES), lambda t: (t % n_pos, 0)),
            pl.BlockSpec((tm, LANES), lambda t: (t % n_pos, 0)),
            pl.BlockSpec((256, 256), const),
            pl.BlockSpec((512, 512), const),
        ],
        out_specs=(pl.BlockSpec((tm, PB_WIDTH), lambda t: (t, 0)), pl.BlockSpec((tm, PF_WIDTH), lambda t: (t, 0))),
        compiler_params=pltpu.CompilerParams(dimension_semantics=("parallel",), vmem_limit_bytes=VMEM_LIMIT),
        name="prep",
    )(x2, an, w, g64, glat, wq, wkv, gmla, cos_t, sin_t, _block_diag_ones(256, HEAD_DIM),
      _block_diag_ones(512, LANES))


SB_TQ, SB_KB, SB_SUB = 256, 2048, 128


SB_UNDERFLOW = 110.0


def _sb_kernel(qi_ref, kj_ref, first_ref, last_ref, kmax_ref, q_ref, k_ref, v_ref, sums_ref, *refs, resume):
    if resume:
        acc_in_ref, carry_in_ref = refs[:2]
        refs = refs[2:]
        o_ref, carry_out_ref, carry_ref, acc_ref, zb_ref, dead_ref = refs
        slack_ref = None
    else:
        o_ref, carry_out_ref, slack_ref, carry_ref, acc_ref, zb_ref, dead_ref = refs
    b = pl.program_id(0)
    p = pl.program_id(1)
    qi = qi_ref[p]
    kj = kj_ref[p]
    q_start = qi * SB_TQ

    @pl.when(first_ref[p] == 1)
    def _():
        if resume:
            for hd in range(4):
                carry_ref[hd * SB_TQ:(hd + 1) * SB_TQ, :] = carry_in_ref[:, hd * SB_SUB:(hd + 1) * SB_SUB]
                acc_ref[hd] = acc_in_ref[:, hd * HEAD_DIM:(hd + 1) * HEAD_DIM]
        else:
            carry_ref[...] = jnp.zeros_like(carry_ref)
            acc_ref[...] = jnp.zeros_like(acc_ref)
        dead_ref[0] = 0
        ones = jnp.ones((HEAD_DIM, SB_SUB), BF16)
        for hd in range(4):
            q = q_ref[:, hd * HEAD_DIM:(hd + 1) * HEAD_DIM].astype(F32)
            zb_ref[hd * SB_TQ:(hd + 1) * SB_TQ, :] = (jnp.sqrt(_dot_exact_rhs(q * q, ones))
                                                      * (kmax_ref[b * 4 + hd] * 1.01))

    rel = (lax.broadcasted_iota(jnp.int32, (SB_TQ, SB_SUB), 1)
           - lax.broadcasted_iota(jnp.int32, (SB_TQ, SB_SUB), 0))

    def sub_tile(u, k_start):
        mask = jnp.tile(rel < q_start - k_start, (4, 1))
        rows = slice(u * SB_SUB, (u + 1) * SB_SUB)
        head = lambda hd: slice(hd * HEAD_DIM, (hd + 1) * HEAD_DIM)
        z = jnp.concatenate([_dot_nt(q_ref[:, head(hd)], k_ref[rows, head(hd)]) for hd in range(4)], axis=0)
        log_keep = jnp.where(mask, -(jnp.maximum(z, 0.0) + jnp.log(1.0 + jnp.exp(-jnp.abs(z)))), 0.0)
        hi, lo = _split_bf16(log_keep)
        sums = _dot(jnp.concatenate([hi, lo], axis=1), sums_ref[...])
        carry = carry_ref[...]
        a = jnp.where(mask, jnp.exp(z + sums[:, :SB_SUB] + carry), 0.0).astype(BF16)
        for hd in range(4):
            acc_ref[hd] += _dot(a[hd * SB_TQ:(hd + 1) * SB_TQ], v_ref[rows, head(hd)])
        carry_ref[...] = carry + sums[:, SB_SUB:]

    for u in reversed(range(SB_KB // SB_SUB)):
        k_start = kj * SB_KB + u * SB_SUB

        @pl.when((k_start < q_start + SB_TQ) & (dead_ref[0] == 0))
        def _(u=u, k_start=k_start):
            live = jnp.max(carry_ref[...] + zb_ref[...]) > -SB_UNDERFLOW

            @pl.when(live)
            def _():
                sub_tile(u, k_start)

            @pl.when(jnp.logical_not(live))
            def _():
                dead_ref[0] = 1

    @pl.when(last_ref[p] == 1)
    def _():
        for hd in range(4):
            o_ref[:, hd * HEAD_DIM:(hd + 1) * HEAD_DIM] = acc_ref[hd]
            carry_out_ref[:, hd * SB_SUB:(hd + 1) * SB_SUB] = carry_ref[hd * SB_TQ:(hd + 1) * SB_TQ, :]
        if slack_ref is not None:
            slack_ref[...] = jnp.full(slack_ref.shape, jnp.max(carry_ref[...] + zb_ref[...]), F32)


def _sb_call(pairs, pb, kmax, sums, batch, seq, state):
    qi = [i for i, blocks in pairs for _ in blocks]
    kj = [j for _, blocks in pairs for j in blocks]
    first = [int(n == 0) for _, blocks in pairs for n in range(len(blocks))]
    last = [int(n == len(blocks) - 1) for _, blocks in pairs for n in range(len(blocks))]
    prefetch = [jnp.asarray(np.array(a, np.int32)) for a in (qi, kj, first, last)]
    tile = lambda width: pl.BlockSpec((None, SB_TQ, width), lambda b, p, qi, kj, fi, la: (b, qi[p], 0))
    in_specs = [
        pl.BlockSpec(memory_space=pltpu.SMEM),
        pl.BlockSpec((None, SB_TQ, 256), lambda b, p, qi, kj, fi, la: (b, qi[p], PB_SBQ // 256)),
        pl.BlockSpec((None, SB_KB, 256), lambda b, p, qi, kj, fi, la: (b, kj[p], PB_SBK // 256)),
        pl.BlockSpec((None, SB_KB, 256), lambda b, p, qi, kj, fi, la: (b, kj[p], PB_SBV // 256)),
        pl.BlockSpec((2 * SB_SUB, 2 * SB_SUB), lambda b, p, qi, kj, fi, la: (0, 0)),
    ]
    resume = state is not None
    out_specs = [tile(256), tile(4 * SB_SUB)]
    out_shape = [jax.ShapeDtypeStruct((batch, seq, 256), F32), jax.ShapeDtypeStruct((batch, seq, 4 * SB_SUB), F32)]
    if resume:
        in_specs += [tile(256), tile(4 * SB_SUB)]
    else:
        out_specs.append(pl.BlockSpec((None, None, 8, LANES), lambda b, p, qi, kj, fi, la: (b, qi[p], 0, 0)))
        out_shape.append(jax.ShapeDtypeStruct((batch, seq // SB_TQ, 8, LANES), F32))
    grid_spec = pltpu.PrefetchScalarGridSpec(
        num_scalar_prefetch=4,
        grid=(batch, len(qi)),
        in_specs=in_specs,
        out_specs=tuple(out_specs),
        scratch_shapes=[pltpu.VMEM((4 * SB_TQ, SB_SUB), F32), pltpu.VMEM((4, SB_TQ, HEAD_DIM), F32),
                        pltpu.VMEM((4 * SB_TQ, SB_SUB), F32), pltpu.SMEM((1,), jnp.int32)],
    )
    return pl.pallas_call(
        functools.partial(_sb_kernel, resume=resume),
        out_shape=tuple(out_shape),
        grid_spec=grid_spec,
        input_output_aliases={len(prefetch) + 5: 0, len(prefetch) + 6: 1} if resume else {},
        compiler_params=pltpu.CompilerParams(dimension_semantics=("parallel", "arbitrary"),
                                             vmem_limit_bytes=VMEM_LIMIT),
        name="stick_breaking_far" if resume else "stick_breaking",
    )(*prefetch, kmax, pb, pb, pb, sums, *(state or ()))


def _sb_attention(pb, batch, seq):
    keys = pb[:, :, PB_SBK:PB_SBK + 256].astype(F32).reshape(batch, seq, 4, HEAD_DIM)
    kmax = jnp.sqrt(jnp.max(jnp.sum(keys * keys, axis=-1), axis=1)).reshape(batch * 4)
    j = np.arange(2 * SB_SUB)[:, None] % SB_SUB
    s = np.arange(2 * SB_SUB)[None, :]
    sums = jnp.asarray((s >= SB_SUB) | (j >= s), BF16)
    near, far = [], []
    for i in range(seq // SB_TQ):
        blocks = list(reversed(range((i * SB_TQ + SB_TQ - 1) // SB_KB + 1)))
        n_near = 2 if (i * SB_TQ) % SB_KB == 0 else 1
        near.append((i, blocks[:n_near]))
        if blocks[n_near:]:
            far.append((i, blocks[n_near:]))
    values, carry, slack = _sb_call(near, pb, kmax, sums, batch, seq, None)
    if not far:
        return values
    has_far = jnp.asarray(np.array([i for i, _ in far], np.int32))
    return lax.cond(jnp.max(slack[:, has_far]) > -SB_UNDERFLOW,
                    lambda: _sb_call(far, pb, kmax, sums, batch, seq, (values, carry))[0],
                    lambda: values)


def _banded_kernel(sink_ref, q_ref, kp_ref, kc_ref, vp_ref, vc_ref, bias_ref, o_ref, *, tq, pad, window, use_sink):
    i = pl.program_id(1)
    dist_prev = (lax.broadcasted_iota(jnp.int32, (tq, pad), 0) + pad
                 - lax.broadcasted_iota(jnp.int32, (tq, pad), 1))
    dist_cur = lax.broadcasted_iota(jnp.int32, (tq, tq), 0) - lax.broadcasted_iota(jnp.int32, (tq, tq), 1)
    mask_prev = (dist_prev < window) & (i > 0)
    mask_cur = (dist_cur >= 0) & (dist_cur < window)
    def scores(hd):
        kcols = slice((hd // 2) * HEAD_DIM, (hd // 2 + 1) * HEAD_DIM)
        q = q_ref[:, hd * HEAD_DIM:(hd + 1) * HEAD_DIM]
        return (jnp.where(mask_prev, _dot_nt(q, kp_ref[:, kcols]) + bias_ref[hd, :, :pad], NEG),
                jnp.where(mask_cur, _dot_nt(q, kc_ref[:, kcols]) + bias_ref[hd, :, pad:], NEG))

    ahead = 2
    pending = [scores(hd) for hd in range(ahead)]
    for hd in range(4):
        if hd + ahead < 4:
            pending.append(scores(hd + ahead))
        s_prev, s_cur = pending.pop(0)
        cols = slice(hd * HEAD_DIM, (hd + 1) * HEAD_DIM)
        kcols = slice((hd // 2) * HEAD_DIM, (hd // 2 + 1) * HEAD_DIM)
        m = jnp.maximum(jnp.max(s_prev, axis=-1, keepdims=True), jnp.max(s_cur, axis=-1, keepdims=True))
        if use_sink:
            sink = sink_ref[hd]
            m = jnp.maximum(m, sink)
        p_prev = jnp.where(mask_prev, jnp.exp(s_prev - m), 0.0)
        p_cur = jnp.where(mask_cur, jnp.exp(s_cur - m), 0.0)
        denom = jnp.sum(p_prev, axis=-1, keepdims=True) + jnp.sum(p_cur, axis=-1, keepdims=True)
        if use_sink:
            denom = denom + jnp.exp(sink - m)
        o = _dot(p_prev.astype(BF16), vp_ref[:, kcols]) + _dot(p_cur.astype(BF16), vc_ref[:, kcols])
        o_ref[:, cols] = o / jnp.maximum(denom, 1e-30)


BAND_TQ = 256


def _band_tiles(window):
    pad = -(-(window - 1) // LANES) * LANES
    return pad, max(pad, BAND_TQ)


def _banded_attention(pb, sinks, bias, batch, seq, window, q_col, k_col, v_col, use_sink):
    pad, tq = _band_tiles(window)
    per = tq // pad
    prev = lambda i: jnp.maximum(i * per - 1, 0)
    grid_spec = pltpu.PrefetchScalarGridSpec(
        num_scalar_prefetch=1,
        grid=(batch, seq // tq),
        in_specs=[
            pl.BlockSpec((None, tq, 256), lambda b, i, s: (b, i, q_col // 256)),
            pl.BlockSpec((None, pad, 128), lambda b, i, s: (b, prev(i), k_col // 128)),
            pl.BlockSpec((None, tq, 128), lambda b, i, s: (b, i, k_col // 128)),
            pl.BlockSpec((None, pad, 128), lambda b, i, s: (b, prev(i), v_col // 128)),
            pl.BlockSpec((None, tq, 128), lambda b, i, s: (b, i, v_col // 128)),
            pl.BlockSpec((4, tq, pad + tq), lambda b, i, s: (0, 0, 0)),
        ],
        out_specs=pl.BlockSpec((None, tq, 256), lambda b, i, s: (b, i, 0)),
    )
    return pl.pallas_call(
        functools.partial(_banded_kernel, tq=tq, pad=pad, window=window, use_sink=use_sink),
        out_shape=jax.ShapeDtypeStruct((batch, seq, 256), F32),
        grid_spec=grid_spec,
        compiler_params=pltpu.CompilerParams(dimension_semantics=("parallel", "arbitrary"),
                                             vmem_limit_bytes=VMEM_LIMIT),
        name="banded_w%d" % window,
    )(sinks, pb, pb, pb, pb, pb, bias)


def _compress_kernel(rows_ref, pos_ref, w1_ref, w2_ref, g_ref, o_ref):
    win = rows_ref[...] + pos_ref[...]
    hid = jax.nn.gelu(_dot(win.astype(BF16), w1_ref[...]), approximate=True)
    out = _dot(hid.astype(BF16), w2_ref[...])
    normed = _row_rms(out, HEAD_DIM) * g_ref[...]
    o_ref[...] = jnp.where(pl.program_id(0) == 0, normed, out).astype(BF16)


def _compress(rows, pos, w1, w2, gain, tn):
    _, bh, ncp, width = rows.shape
    return pl.pallas_call(
        _compress_kernel,
        out_shape=jax.ShapeDtypeStruct((2, bh, ncp, HEAD_DIM), BF16),
        grid=(2, bh, ncp // tn),
        in_specs=[
            pl.BlockSpec((None, None, tn, width), lambda c, r, n: (c, r, n, 0)),
            pl.BlockSpec((None, 1, width), lambda c, r, n: (c, 0, 0)),
            pl.BlockSpec((None, width, NSA_CMP_HIDDEN), lambda c, r, n: (c, 0, 0)),
            pl.BlockSpec((None, NSA_CMP_HIDDEN, HEAD_DIM), lambda c, r, n: (c, 0, 0)),
            pl.BlockSpec((1, HEAD_DIM), lambda c, r, n: (0, 0)),
        ],
        out_specs=pl.BlockSpec((None, None, tn, HEAD_DIM), lambda c, r, n: (c, r, n, 0)),
        compiler_params=pltpu.CompilerParams(dimension_semantics=("parallel", "parallel", "parallel"),
                                             vmem_limit_bytes=VMEM_LIMIT),
        name="nsa_compress",
    )(rows, pos, w1, w2, gain)


CMP_TQ = 1024


def _cmp_kernel(q_ref, kc_ref, vc_ref, bias_ref, ov_ref, o_ref, qa_ref, *, ncp, n_sel_pad, topk):
    i = pl.program_id(2)
    q_pos = i * CMP_TQ + lax.broadcasted_iota(jnp.int32, (CMP_TQ, ncp), 0)
    cmp_end = lax.broadcasted_iota(jnp.int32, (CMP_TQ, ncp), 1) * NSA_CMP_STRIDE + (NSA_CMP_LEN - 1)
    mask = cmp_end <= q_pos
    kc = kc_ref[...]
    vc = vc_ref[...]
    p_sum = jnp.zeros((CMP_TQ, ncp), F32)
    for g in range(2):
        cols = slice(g * HEAD_DIM, (g + 1) * HEAD_DIM)
        s = jnp.where(mask, _dot_nt(q_ref[:, cols], kc) + bias_ref[g], NEG)
        m = jnp.max(s, axis=-1, keepdims=True)
        p = jnp.where(mask, jnp.exp(s - m), 0.0)
        p = p / jnp.maximum(jnp.sum(p, axis=-1, keepdims=True), 1e-30)
        o_ref[:, cols] = _dot(p.astype(BF16), vc)
        p_sum = p_sum + p
    imp = _dot_exact_rhs(p_sum, ov_ref[...])

    row_pos = i * CMP_TQ + lax.broadcasted_iota(jnp.int32, (CMP_TQ, n_sel_pad), 0)
    blk = lax.broadcasted_iota(jnp.int32, (CMP_TQ, n_sel_pad), 1)
    cur = row_pos >> int(math.log2(NSA_SEL_LEN))
    forced = (blk == 0) | (blk == cur) | (blk == cur - 1)
    valid = blk * NSA_SEL_LEN <= row_pos
    score = jnp.where(valid, imp + jnp.where(forced, FORCE_BONUS, 0.0), NEG)
    blk_f = blk.astype(F32)
    dropped = jnp.ones((CMP_TQ, n_sel_pad), F32)
    for _ in range(topk):
        best = jnp.max(score, axis=-1, keepdims=True)
        first = jnp.min(jnp.where(score == best, blk_f, float(n_sel_pad)), axis=-1, keepdims=True)
        hit = blk_f == first
        dropped = jnp.where(hit, 0.0, dropped)
        score = jnp.where(hit, -jnp.inf, score)
    dropped = jnp.where(valid, dropped, 1.0).astype(BF16)
    q = q_ref[...]
    low = lax.broadcasted_iota(jnp.int32, q.shape, 1) < HEAD_DIM
    zero = jnp.zeros_like(q)
    width = LANES + n_sel_pad
    qa_ref[:, 0:LANES] = jnp.where(low, q, zero)
    qa_ref[:, LANES:width] = dropped
    qa_ref[:, width:width + LANES] = jnp.where(low, zero, q)
    qa_ref[:, width + LANES:2 * width] = dropped


def _cmp_attention(pb, kvc, bias_c, overlap, batch, seq):
    ncp = kvc.shape[2]
    n_sel_pad = overlap.shape[1]
    topk = min(NSA_TOPK, seq // NSA_SEL_LEN)
    pair = 2 * (LANES + n_sel_pad)
    return pl.pallas_call(
        functools.partial(_cmp_kernel, ncp=ncp, n_sel_pad=n_sel_pad, topk=topk),
        out_shape=(jax.ShapeDtypeStruct((batch, seq, 256), F32),
                   jax.ShapeDtypeStruct((batch, seq, 2 * pair), BF16)),
        grid=(batch, 2, seq // CMP_TQ),
        in_specs=[
            pl.BlockSpec((None, CMP_TQ, 128), lambda b, h, i: (b, i, PB_NSAQ // 128 + h)),
            pl.BlockSpec((None, None, ncp, HEAD_DIM), lambda b, h, i: (0, b * 2 + h, 0, 0)),
            pl.BlockSpec((None, None, ncp, HEAD_DIM), lambda b, h, i: (1, b * 2 + h, 0, 0)),
            pl.BlockSpec((2, CMP_TQ, ncp), lambda b, h, i: (h, i, 0)),
            pl.BlockSpec((ncp, n_sel_pad), lambda b, h, i: (0, 0)),
        ],
        out_specs=(pl.BlockSpec((None, CMP_TQ, 128), lambda b, h, i: (b, i, h)),
                   pl.BlockSpec((None, CMP_TQ, pair), lambda b, h, i: (b, i, h))),
        compiler_params=pltpu.CompilerParams(dimension_semantics=("parallel", "parallel", "arbitrary"),
                                             vmem_limit_bytes=VMEM_LIMIT),
        name="nsa_cmp_select",
    )(pb, kvc, kvc, bias_c, overlap)


SEL_T = 512


def _flash_init(m_ref, acc_ref):
    m_ref[...] = jnp.full_like(m_ref, NEG)
    acc_ref[...] = jnp.zeros_like(acc_ref)


def _flash_update(s, v_ones, m_ref, acc_ref, hd):
    m_old = m_ref[hd]
    m_new = jnp.maximum(m_old, jnp.max(s, axis=-1, keepdims=True))
    alpha = jnp.exp(m_old - m_new)
    pr = jnp.exp(s - jnp.tile(m_new, (1, s.shape[1] // LANES)))
    acc_ref[hd] = alpha * acc_ref[hd] + _dot(pr.astype(BF16), v_ones)
    m_ref[hd] = m_new


def _flash_finish(acc_ref, o_ref, heads):
    for hd in range(heads):
        acc = acc_ref[hd]
        row_sum = pltpu.roll(acc, HEAD_DIM, 1)
        o_ref[:, hd * HEAD_DIM:(hd + 1) * HEAD_DIM] = (acc / jnp.maximum(row_sum, 1e-30))[:, :HEAD_DIM]


def _sel_kernel(qi_ref, kj_ref, q_ref, k_ref, v_ref, pen_ref, bias_ref, o_ref, m_ref, acc_ref):
    p = pl.program_id(1)
    qi = qi_ref[p]
    kj = kj_ref[p]
    width = q_ref.shape[1] // 4

    @pl.when(kj == 0)
    def _():
        _flash_init(m_ref, acc_ref)

    def step(diagonal):
        pen = pen_ref[...]

        def scores(hd):
            k_pen = jnp.concatenate([k_ref[:, hd * LANES:(hd + 1) * LANES], pen], axis=1)
            s = _dot_nt(q_ref[:, hd * width:(hd + 1) * width], k_pen) + bias_ref[hd]
            if diagonal:
                row = lax.broadcasted_iota(jnp.int32, (SEL_T, SEL_T), 0)
                col = lax.broadcasted_iota(jnp.int32, (SEL_T, SEL_T), 1)
                s = jnp.where(col <= row, s, NEG)
            return s

        ahead = 3
        pending = [scores(hd) for hd in range(ahead)]
        for hd in range(4):
            if hd + ahead < 4:
                pending.append(scores(hd + ahead))
            _flash_update(pending.pop(0), v_ref[:, (hd // 2) * LANES:(hd // 2 + 1) * LANES], m_ref, acc_ref, hd)

    @pl.when(kj < qi)
    def _():
        step(False)

    @pl.when(kj == qi)
    def _():
        step(True)
        _flash_finish(acc_ref, o_ref, 4)


def _causal_pairs(n):
    qi, kj = [], []
    for i in range(n):
        for j in range(i + 1):
            qi.append(i)
            kj.append(j)
    return jnp.asarray(np.array(qi, np.int32)), jnp.asarray(np.array(kj, np.int32))


def _sel_attention(pb, q_aug, penalty, bias_s, batch, seq):
    n_sel_pad = penalty.shape[1]
    n_delta = bias_s.shape[1]
    qi, kj = _causal_pairs(seq // SEL_T)
    grid_spec = pltpu.PrefetchScalarGridSpec(
        num_scalar_prefetch=2,
        grid=(batch, int(qi.shape[0])),
        in_specs=[
            pl.BlockSpec((None, SEL_T, q_aug.shape[2]), lambda b, p, qi, kj: (b, qi[p], 0)),
            pl.BlockSpec((None, SEL_T, 512), lambda b, p, qi, kj: (b, kj[p], PB_NSAKS // 512)),
            pl.BlockSpec((None, SEL_T, 256), lambda b, p, qi, kj: (b, kj[p], PB_NSAVS // 256)),
            pl.BlockSpec((SEL_T, n_sel_pad), lambda b, p, qi, kj: (kj[p], 0)),
            pl.BlockSpec((4, None, SEL_T, SEL_T),
                         lambda b, p, qi, kj: (0, jnp.minimum(qi[p] - kj[p], n_delta - 1), 0, 0)),
        ],
        out_specs=pl.BlockSpec((None, SEL_T, 256), lambda b, p, qi, kj: (b, qi[p], 0)),
        scratch_shapes=[pltpu.VMEM((4, SEL_T, LANES), F32), pltpu.VMEM((4, SEL_T, LANES), F32)],
    )
    return pl.pallas_call(
        _sel_kernel,
        out_shape=jax.ShapeDtypeStruct((batch, seq, 256), F32),
        grid_spec=grid_spec,
        compiler_params=pltpu.CompilerParams(dimension_semantics=("parallel", "arbitrary"),
                                             vmem_limit_bytes=VMEM_LIMIT),
        name="nsa_selected",
    )(qi, kj, q_aug, pb, pb, penalty, bias_s)


MLA_T = 1024


def _mla_kernel(qi_ref, kj_ref, q_ref, k_ref, v_ref, o_ref, m_ref, acc_ref):
    p = pl.program_id(1)
    qi = qi_ref[p]
    kj = kj_ref[p]

    @pl.when(kj == 0)
    def _():
        _flash_init(m_ref, acc_ref)

    def step(diagonal):
        def scores(hd):
            cols = slice(hd * LANES, (hd + 1) * LANES)
            s = _dot_nt(q_ref[:, cols], k_ref[:, cols])
            if diagonal:
                row = lax.broadcasted_iota(jnp.int32, (MLA_T, MLA_T), 0)
                col = lax.broadcasted_iota(jnp.int32, (MLA_T, MLA_T), 1)
                s = jnp.where(col <= row, s, NEG)
            return s

        ahead = 3
        pending = [scores(hd) for hd in range(ahead)]
        for hd in range(MLA_HEADS):
            if hd + ahead < MLA_HEADS:
                pending.append(scores(hd + ahead))
            _flash_update(pending.pop(0), v_ref[:, hd * LANES:(hd + 1) * LANES], m_ref, acc_ref, hd)

    @pl.when(kj < qi)
    def _():
        step(False)

    @pl.when(kj == qi)
    def _():
        step(True)
        _flash_finish(acc_ref, o_ref, MLA_HEADS)


def _mla_attention(pb, batch, seq):
    qi, kj = _causal_pairs(seq // MLA_T)
    grid_spec = pltpu.PrefetchScalarGridSpec(
        num_scalar_prefetch=2,
        grid=(batch, int(qi.shape[0])),
        in_specs=[
            pl.BlockSpec((None, MLA_T, 512), lambda b, p, qi, kj: (b, qi[p], PB_MLAQ // 512)),
            pl.BlockSpec((None, MLA_T, 512), lambda b, p, qi, kj: (b, kj[p], PB_MLAK // 512)),
            pl.BlockSpec((None, MLA_T, 512), lambda b, p, qi, kj: (b, kj[p], PB_MLAV // 512)),
        ],
        out_specs=pl.BlockSpec((None, MLA_T, 256), lambda b, p, qi, kj: (b, qi[p], 0)),
        scratch_shapes=[pltpu.VMEM((4, MLA_T, LANES), F32), pltpu.VMEM((4, MLA_T, LANES), F32)],
    )
    return pl.pallas_call(
        _mla_kernel,
        out_shape=jax.ShapeDtypeStruct((batch, seq, 256), F32),
        grid_spec=grid_spec,
        compiler_params=pltpu.CompilerParams(dimension_semantics=("parallel", "arbitrary"),
                                             vmem_limit_bytes=VMEM_LIMIT),
        name="mla_causal",
    )(qi, kj, pb, pb, pb)


def _outproj_kernel(x_ref, oa_ref, ob_ref, oc_ref, os_ref, ow_ref, od_ref, gate_ref, gexp_ref, gn_ref, w_ref,
                    o_ref):
    gates = gate_ref[...]
    g_hi, g_lo = _split_bf16(gates)

    def gate(branch):
        e = gexp_ref[branch]
        return _dot(g_hi, e) + _dot(g_lo, e)

    o_nsa = gate(0) * oc_ref[...] + gate(1) * os_ref[...] + gate(2) * ow_ref[...]
    gn = gn_ref[...]
    y = x_ref[...]
    for grp, o in enumerate((oa_ref[...], ob_ref[...], o_nsa, od_ref[...])):
        cols = slice(grp * GROUP_WIDTH, (grp + 1) * GROUP_WIDTH)
        normed = _row_rms(o, GROUP_WIDTH) * gn[:, cols]
        y = y + _dot(normed.astype(BF16), w_ref[cols, :])
    o_ref[...] = y


def _outproj(x2, oa, ob, oc, osel, ow, od, pf, gexp, gn, w, tm):
    tokens = x2.shape[0]
    row = lambda t: (t, 0)
    o_spec = pl.BlockSpec((tm, GROUP_WIDTH), row)
    return pl.pallas_call(
        _outproj_kernel,
        out_shape=jax.ShapeDtypeStruct((tokens, D_MODEL), F32),
        grid=(tokens // tm,),
        in_specs=[
            pl.BlockSpec((tm, D_MODEL), row), o_spec, o_spec, o_spec, o_spec, o_spec, o_spec,
            pl.BlockSpec((tm, 128), lambda t: (t, PF_GATE // 128)),
            pl.BlockSpec((3, 128, GROUP_WIDTH), lambda t: (0, 0, 0)),
            pl.BlockSpec((1, D_MODEL), lambda t: (0, 0)),
            pl.BlockSpec((D_MODEL, D_MODEL), lambda t: (0, 0)),
        ],
        out_specs=pl.BlockSpec((tm, D_MODEL), row),
        compiler_params=pltpu.CompilerParams(dimension_semantics=("parallel",), vmem_limit_bytes=VMEM_LIMIT),
        name="out_proj",
    )(x2, oa, ob, oc, osel, ow, od, pf, gexp, gn, w)


MOE_TM = 1024
MOE_QUAD = 4
MOE_CHUNK = 256
MOE_CHUNK_ALIGN = 256
MOE_SLOTS = MOE_TM + MOE_GROUPS * MOE_CHUNK
MOE_VMEM_LIMIT = 60 * 1024 * 1024


def _moe_kernel(x_ref, fn_ref, wr_hi_ref, wr_lo_ref, br_ref, cexp_ref, wg_ref, wu_ref, wd_ref, o_ref, hs_ref,
                cs_ref, ys_ref, slot_ref, start_ref):
    step = pl.program_id(1)
    lane = lax.broadcasted_iota(jnp.int32, (MOE_TM, LANES), 1)

    @pl.when(step == 0)
    def _():
        h = _row_rms(x_ref[...], D_MODEL) * fn_ref[...]
        h_hi, h_lo = _split_bf16(h)
        logits = (_dot(h_hi, wr_hi_ref[...]) + _dot(h_lo, wr_hi_ref[...]) + _dot(h_hi, wr_lo_ref[...])
                  + br_ref[...])
        lane_f = lane.astype(F32)
        no_lane = float(LANES)
        is_group = lane < MOE_GROUPS
        g_max = jnp.max(jnp.where(is_group, logits, -jnp.inf), axis=-1, keepdims=True)
        g_star = jnp.min(jnp.where(is_group & (logits == g_max), lane_f, no_lane), axis=-1, keepdims=True)
        g_den = jnp.sum(jnp.where(is_group, jnp.exp(logits - g_max), 0.0), axis=-1, keepdims=True)
        g_w = 1.0 / g_den
        group_of_lane = ((lane - MOE_GROUPS) >> int(math.log2(MOE_EPG))).astype(F32)
        in_group = (lane >= MOE_GROUPS) & (lane < MOE_GROUPS + MOE_EXPERTS) & (group_of_lane == g_star)
        e_l = jnp.where(in_group, logits, -jnp.inf)
        top1 = jnp.max(e_l, axis=-1, keepdims=True)
        i1 = jnp.min(jnp.where(e_l == top1, lane_f, no_lane), axis=-1, keepdims=True)
        e_l2 = jnp.where(lane_f == i1, -jnp.inf, e_l)
        top2 = jnp.max(e_l2, axis=-1, keepdims=True)
        i2 = jnp.min(jnp.where(e_l2 == top2, lane_f, no_lane), axis=-1, keepdims=True)
        r = jnp.exp(top2 - top1)
        w1 = g_w / (1.0 + r)
        w2 = g_w * r / (1.0 + r)
        comb = jnp.where(lane_f == i1, w1, jnp.where(lane_f == i2, w2, 0.0))

        onehot = jnp.where(lane_f == g_star, 1.0, 0.0)
        row = lax.broadcasted_iota(jnp.int32, (MOE_TM, LANES), 0)
        incl = onehot
        shift = 1
        while shift < MOE_TM:
            incl = incl + jnp.where(row >= shift, pltpu.roll(incl, shift, 0), 0.0)
            shift *= 2
        counts = incl[MOE_TM - 1:MOE_TM, :]
        padded = jnp.floor((counts + (MOE_CHUNK - 0.5)) * (1.0 / MOE_CHUNK)) * MOE_CHUNK
        before = (lax.broadcasted_iota(jnp.int32, (LANES, LANES), 0)
                  < lax.broadcasted_iota(jnp.int32, (LANES, LANES), 1))
        starts = _dot(jnp.broadcast_to(padded, (8, LANES)).astype(BF16),
                      jnp.where(before, 1.0, 0.0).astype(BF16))[0:1, :]
        for g in range(MOE_GROUPS + 1):
            start_ref[g] = jnp.sum(jnp.where(lane[0:1, :] == g, starts, 0.0)).astype(jnp.int32)
        slot = jnp.sum(onehot * (starts + incl - onehot), axis=-1, keepdims=True)
        slot_ref[...] = jnp.broadcast_to(slot, (MOE_TM, LANES))
        slot_row = slot_ref[...].T[0:1, :]
        n_slots = hs_ref.shape[0]
        place = jnp.where(lax.broadcasted_iota(jnp.int32, (n_slots, MOE_TM), 0).astype(F32) == slot_row,
                          1.0, 0.0).astype(BF16)
        hs_ref[...] = _dot(place, h.astype(BF16)).astype(BF16)
        c_hi, c_lo = _split_bf16(comb)
        cs_ref[:, :LANES] = _dot(place, c_hi).astype(BF16)
        cs_ref[:, LANES:] = _dot(place, c_lo).astype(BF16)
        ys_ref[...] = jnp.zeros_like(ys_ref)

    group = step // (MOE_EPG // MOE_QUAD)
    first_slot = start_ref[group]
    half = MOE_QUAD * MOE_HIDDEN // 2

    def chunk(c, carry):
        rows = pl.ds(pl.multiple_of(first_slot + c * MOE_CHUNK, MOE_CHUNK_ALIGN), MOE_CHUNK)
        hb = hs_ref[rows, :]
        weight = _dot(cs_ref[rows, :], cexp_ref[...])
        y = None
        for s in range(2):
            cols = slice(s * half, (s + 1) * half)
            a = jax.nn.silu(_dot(hb, wg_ref[:, cols])) * _dot(hb, wu_ref[:, cols]) * weight[:, cols]
            part = _dot(a.astype(BF16), wd_ref[cols, :])
            y = part if y is None else y + part
        ys_ref[rows, :] += y
        return carry

    lax.fori_loop(0, (start_ref[group + 1] - first_slot) // MOE_CHUNK, chunk, 0)

    @pl.when(step == MOE_EXPERTS // MOE_QUAD - 1)
    def _():
        n_slots = hs_ref.shape[0]
        back = jnp.where(lax.broadcasted_iota(jnp.int32, (MOE_TM, n_slots), 1).astype(F32) == slot_ref[:, 0:1],
                         1.0, 0.0).astype(BF16)
        y_hi, y_lo = _split_bf16(ys_ref[...])
        o_ref[...] = x_ref[...] + _dot(back, y_hi) + _dot(back, y_lo)


def _moe(x2, fn, wr_hi, wr_lo, br, wg, wu, wd):
    tokens = x2.shape[0]
    quads = MOE_EXPERTS // MOE_QUAD
    width = MOE_QUAD * MOE_HIDDEN
    const = lambda t, e: (0, 0)
    lane_of_col = MOE_GROUPS + np.arange(quads)[:, None, None] * MOE_QUAD + np.arange(width)[None, None, :] // MOE_HIDDEN
    cexp = jnp.asarray((np.arange(2 * LANES)[None, :, None] % LANES) == lane_of_col, BF16)
    return pl.pallas_call(
        _moe_kernel,
        out_shape=jax.ShapeDtypeStruct((tokens, D_MODEL), F32),
        grid=(tokens // MOE_TM, quads),
        in_specs=[
            pl.BlockSpec((MOE_TM, D_MODEL), lambda t, e: (t, 0)),
            pl.BlockSpec((1, D_MODEL), const),
            pl.BlockSpec((D_MODEL, LANES), const),
            pl.BlockSpec((D_MODEL, LANES), const),
            pl.BlockSpec((1, LANES), const),
            pl.BlockSpec((None, 2 * LANES, width), lambda t, e: (e, 0, 0)),
            pl.BlockSpec((None, D_MODEL, width), lambda t, e: (e, 0, 0)),
            pl.BlockSpec((None, D_MODEL, width), lambda t, e: (e, 0, 0)),
            pl.BlockSpec((None, width, D_MODEL), lambda t, e: (e, 0, 0)),
        ],
        out_specs=pl.BlockSpec((MOE_TM, D_MODEL), lambda t, e: (t, 0)),
        scratch_shapes=[pltpu.VMEM((MOE_SLOTS, D_MODEL), BF16), pltpu.VMEM((MOE_SLOTS, 2 * LANES), BF16),
                        pltpu.VMEM((MOE_SLOTS, D_MODEL), F32), pltpu.VMEM((MOE_TM, LANES), F32),
                        pltpu.SMEM((8,), jnp.int32)],
        compiler_params=pltpu.CompilerParams(dimension_semantics=("parallel", "arbitrary"),
                                             vmem_limit_bytes=MOE_VMEM_LIMIT),
        name="hier_moe",
    )(x2, fn, wr_hi, wr_lo, br, cexp, wg, wu, wd)


def _t5_bucket(dist):
    n = jnp.maximum(dist, 0)
    max_exact = T5_BUCKETS // 2
    nf = jnp.maximum(n, 1).astype(F32)
    large = max_exact + (jnp.log(nf / max_exact) / math.log(T5_MAX_DIST / max_exact)
                         * (T5_BUCKETS - max_exact)).astype(jnp.int32)
    large = jnp.minimum(large, T5_BUCKETS - 1)
    return jnp.where(n < max_exact, n, large)


def _sel_delta_cap():
    max_exact = T5_BUCKETS // 2
    span = T5_BUCKETS - max_exact
    last_bucket_from = max_exact * (T5_MAX_DIST / max_exact) ** ((span - 1) / span)
    cap = 1
    while (cap - 1) * SEL_T + 1 < 1.25 * last_bucket_from:
        cap += 1
    return cap


def _position_tables(rel_bias, seq):
    buckets = _t5_bucket(jnp.arange(seq))
    first = jnp.sum(buckets[None, :] < jnp.arange(T5_BUCKETS)[:, None], axis=1)
    tbl = rel_bias.T

    def toeplitz(heads, dist):
        shape = (tbl[heads].shape[0],) + (1,) * dist.ndim
        out = jnp.broadcast_to(tbl[heads][:, 0].reshape(shape), shape[:1] + dist.shape)
        for b in range(1, T5_BUCKETS):
            out = jnp.where((dist >= first[b])[None], tbl[heads][:, b].reshape(shape), out)
        return out

    swa_h, nsa_h = slice(0, 4), slice(4, 8)
    def band(window):
        pad, tq = _band_tiles(window)
        return jnp.arange(tq)[:, None] + pad - jnp.arange(pad + tq)[None, :]

    bias_swa = toeplitz(swa_h, band(SWA_WINDOW))
    bias_win = toeplitz(nsa_h, band(NSA_WINDOW))
    ncp = seq // NSA_CMP_STRIDE
    cmp_end = jnp.arange(ncp) * NSA_CMP_STRIDE + NSA_CMP_LEN - 1
    bias_cmp = toeplitz(nsa_h, jnp.arange(seq)[:, None] - cmp_end[None, :])
    nd = min(_sel_delta_cap() + 1, seq // SEL_T)
    dist_s = (jnp.arange(nd)[:, None, None] * SEL_T + jnp.arange(SEL_T)[None, :, None]
              - jnp.arange(SEL_T)[None, None, :])
    bias_sel = toeplitz(nsa_h, dist_s)

    n_sel_pad = -(-(seq // NSA_SEL_LEN) // LANES) * LANES
    sel_start = np.arange(n_sel_pad) * NSA_SEL_LEN
    c_start = np.arange(ncp) * NSA_CMP_STRIDE
    c_end = c_start + NSA_CMP_LEN - 1
    real = (np.arange(ncp) < ncp - NSA_CMP_LEN // NSA_CMP_STRIDE + 1)[:, None] & (sel_start < seq)[None, :]
    overlap = ((c_start[:, None] < sel_start[None, :] + NSA_SEL_LEN) & (c_end[:, None] >= sel_start[None, :]) & real)
    own_block = (np.arange(seq) // NSA_SEL_LEN)[:, None] == np.arange(n_sel_pad)[None, :]
    penalty = np.where(own_block, -2.0 ** 100, 0.0)

    pos = jnp.arange(seq, dtype=F32)
    inv_freq = ROPE_THETA ** (-jnp.arange(0, MLA_ROPE, 2, dtype=F32) / MLA_ROPE)
    ang = pos[:, None] * inv_freq[None, :]
    cos, sin = jnp.cos(ang), jnp.sin(ang)
    ones = jnp.ones((seq, MLA_NOPE), F32)
    tail = LANES - MLA_NOPE - MLA_ROPE
    cos_t = jnp.concatenate([ones, cos, cos, jnp.ones((seq, tail), F32)], axis=1)
    sin_t = jnp.concatenate([0 * ones, -sin, sin, jnp.zeros((seq, tail), F32)], axis=1)
    return dict(bias_swa=bias_swa, bias_win=bias_win, bias_cmp=bias_cmp, bias_sel=bias_sel,
                overlap=jnp.asarray(overlap, BF16), penalty=jnp.asarray(penalty, BF16), cos_t=cos_t, sin_t=sin_t)


def _pad_to(a, shape):
    return jnp.pad(a, [(0, s - d) for d, s in zip(a.shape, shape)])


def _pack_layer(w_in, swa_q_norm, swa_k_norm, nsa_q_norm, nsa_k_norm, mla_q_lat_norm, mla_w_q_up,
                mla_kv_lat_norm, mla_w_kv_up, mla_q_norm, mla_k_norm):
    kpe = w_in[:, 2636:2668]
    kpe_seg = jnp.concatenate([jnp.zeros((D_MODEL, MLA_NOPE), F32), kpe,
                               jnp.zeros((D_MODEL, LANES - MLA_QK), F32)], axis=1)
    spread = lambda cols: _pad_to(cols.reshape(D_MODEL, -1, HEAD_DIM), (D_MODEL, cols.shape[1] // HEAD_DIM, LANES)
                                  ).reshape(D_MODEL, -1)
    w = jnp.concatenate([
        w_in[:, :1920],
        spread(w_in[:, 1920:2048]),
        w_in[:, 2048:2304],
        _pad_to(w_in[:, 2304:2316], (D_MODEL, 128)),
        _pad_to(w_in[:, 2316:2508], (D_MODEL, 256)),
        w_in[:, 2508:2636],
        jnp.tile(kpe_seg, (1, MLA_HEADS)),
    ], axis=1).astype(BF16)
    tile4 = lambda g: jnp.tile(g, 4)
    g64 = _pad_to(jnp.stack([tile4(swa_q_norm), tile4(swa_k_norm), tile4(nsa_q_norm),
                             tile4(nsa_k_norm[1]), tile4(nsa_k_norm[2])]), (8, 256))
    glat = _pad_to(jnp.stack([_pad_to(mla_q_lat_norm, (256,)), _pad_to(mla_kv_lat_norm, (256,))]), (8, 256))
    wq = _pad_to(mla_w_q_up.reshape(MLA_Q_RANK, MLA_HEADS, MLA_QK), (256, MLA_HEADS, LANES))
    wq = wq.reshape(256, MLA_HEADS * LANES).astype(BF16)
    wkv = mla_w_kv_up.reshape(MLA_KV_RANK, MLA_HEADS, MLA_NOPE + MLA_V)
    wk = _pad_to(wkv[:, :, :MLA_NOPE], (MLA_KV_RANK, MLA_HEADS, LANES)).reshape(MLA_KV_RANK, MLA_HEADS * LANES)
    wv = _pad_to(wkv[:, :, MLA_NOPE:], (MLA_KV_RANK, MLA_HEADS, LANES)).reshape(MLA_KV_RANK, MLA_HEADS * LANES)
    wkv_p = jnp.concatenate([wk, wv], axis=1).astype(BF16)
    gmla = _pad_to(jnp.stack([jnp.tile(_pad_to(mla_q_norm, (LANES,)), MLA_HEADS),
                              jnp.tile(_pad_to(mla_k_norm, (LANES,)), MLA_HEADS)]), (8, 512))
    return w, g64, glat, wq, wkv_p, gmla


def _gate_expand():
    rows = np.arange(128)[None, :, None]
    cols = np.arange(GROUP_WIDTH)[None, None, :]
    branch = np.arange(3)[:, None, None]
    return jnp.asarray(rows == branch * 4 + cols // HEAD_DIM, BF16)


def _compress_rows(pf3, batch, seq):
    nb = seq // NSA_CMP_STRIDE
    kv = pf3[:, :, :256].reshape(batch, seq, 2, 2, HEAD_DIM).transpose(2, 0, 3, 1, 4)
    blocks = kv.reshape(2, batch * 2, nb, NSA_CMP_STRIDE * HEAD_DIM)
    nxt = jnp.concatenate([blocks[:, :, 1:], jnp.zeros_like(blocks[:, :, :1])], axis=2)
    return jnp.concatenate([blocks, nxt], axis=3)


def kernel(x, rel_bias, attn_norm, w_in, swa_q_norm, swa_k_norm, swa_sinks, nsa_q_norm, nsa_k_norm, nsa_cmp_pos, nsa_cmp_w1, nsa_cmp_w2, mla_q_lat_norm, mla_w_q_up, mla_kv_lat_norm, mla_w_kv_up, mla_q_norm, mla_k_norm, out_norm, w_out, ffn_norm, moe_w_group, moe_b_group, moe_w_expert, moe_b_expert, moe_w_gate, moe_w_up, moe_w_down):
    batch, seq, _ = x.shape
    depth = w_in.shape[0]
    tokens = batch * seq
    tm = 512
    assert seq % 2048 == 0 and tokens % MOE_TM == 0
    tabs = _position_tables(rel_bias, seq)
    gexp = _gate_expand()
    zero_sinks = jnp.zeros((4,), F32)
    x2 = x.reshape(tokens, D_MODEL)
    for l in range(depth):
        w, g64, glat, wq, wkv, gmla = _pack_layer(
            w_in[l], swa_q_norm[l], swa_k_norm[l], nsa_q_norm[l], nsa_k_norm[l], mla_q_lat_norm[l],
            mla_w_q_up[l], mla_kv_lat_norm[l], mla_w_kv_up[l], mla_q_norm[l], mla_k_norm[l])
        pb, pf = _prep(x2, attn_norm[l][None, :], w, g64, glat, wq, wkv, gmla, tabs["cos_t"], tabs["sin_t"],
                       seq, tm)
        pb3 = pb.reshape(batch, seq, PB_WIDTH)
        pf3 = pf.reshape(batch, seq, PF_WIDTH)
        o_a = _sb_attention(pb3, batch, seq)
        o_b = _banded_attention(pb3, swa_sinks[l], tabs["bias_swa"], batch, seq, SWA_WINDOW,
                                PB_SWAQ, PB_SWAK, PB_SWAV, True)
        rows = _compress_rows(pf3, batch, seq)
        kvc = _compress(rows, nsa_cmp_pos[l].reshape(2, 1, -1),
                        nsa_cmp_w1[l].reshape(2, -1, NSA_CMP_HIDDEN).astype(BF16), nsa_cmp_w2[l].astype(BF16),
                        nsa_k_norm[l][0][None, :], 128)
        o_c, q_aug = _cmp_attention(pb3, kvc, tabs["bias_cmp"], tabs["overlap"], batch, seq)
        o_s = _sel_attention(pb3, q_aug, tabs["penalty"], tabs["bias_sel"], batch, seq)
        o_w = _banded_attention(pb3, zero_sinks, tabs["bias_win"], batch, seq, NSA_WINDOW,
                                PB_NSAQ, PB_NSAKW, PB_NSAVW, False)
        o_d = _mla_attention(pb3, batch, seq)
        flat = lambda o: o.reshape(tokens, GROUP_WIDTH)
        x2 = _outproj(x2, flat(o_a), flat(o_b), flat(o_c), flat(o_s), flat(o_w), flat(o_d), pf, gexp,
                      out_norm[l][None, :], w_out[l].astype(BF16), tm)
        w_router = _pad_to(jnp.concatenate([moe_w_group[l], moe_w_expert[l]], axis=1), (D_MODEL, LANES))
        wr_hi = w_router.astype(BF16)
        wr_lo = (w_router - wr_hi.astype(F32)).astype(BF16)
        b_router = _pad_to(jnp.concatenate([moe_b_group[l], moe_b_expert[l]])[None, :], (1, LANES))
        quads = MOE_EXPERTS // MOE_QUAD
        by_quad = lambda w: w.astype(BF16).reshape(quads, MOE_QUAD, D_MODEL, MOE_HIDDEN).transpose(0, 2, 1, 3
                                                   ).reshape(quads, D_MODEL, MOE_QUAD * MOE_HIDDEN)
        x2 = _moe(x2, ffn_norm[l][None, :], wr_hi, wr_lo, b_router, by_quad(moe_w_gate[l]), by_quad(moe_w_up[l]),
                  moe_w_down[l].astype(BF16).reshape(quads, MOE_QUAD * MOE_HIDDEN, D_MODEL))
    return x2.reshape(batch, seq, D_MODEL)
```

```python
import functools
import math

import numpy as np
import jax
import jax.numpy as jnp
from jax import lax
from jax.experimental import pallas as pl
from jax.experimental.pallas import tpu as pltpu

F32 = jnp.float32
BF16 = jnp.bfloat16

D_MODEL = 1024
HEAD_DIM = 64
NEG = -1e30
EPS = 1e-6
FORCE_BONUS = 1000.0
SWA_WINDOW = 128
NSA_CMP_LEN = 32
NSA_CMP_STRIDE = 16
NSA_CMP_HIDDEN = 128
NSA_SEL_LEN = 64
NSA_TOPK = 16
NSA_WINDOW = 512
MLA_HEADS = 4
MLA_NOPE = 64
MLA_ROPE = 32
MLA_V = 64
MLA_Q_RANK = 192
MLA_KV_RANK = 128
MLA_QK = MLA_NOPE + MLA_ROPE
ROPE_THETA = 10000.0
T5_BUCKETS = 32
T5_MAX_DIST = 1024
MOE_GROUPS = 4
MOE_EPG = 8
MOE_EXPERTS = MOE_GROUPS * MOE_EPG
MOE_HIDDEN = 256
GROUP_WIDTH = 256
LANES = 128
VMEM_LIMIT = 48 * 1024 * 1024

PB_MLAQ, PB_MLAK, PB_MLAV = 0, 512, 1024
PB_SBQ, PB_SBK, PB_SBV = 1536, 1792, 2048
PB_SWAQ, PB_NSAQ, PB_NSAVS = 2304, 2560, 2816
PB_SWAK, PB_SWAV, PB_NSAKW, PB_NSAVW = 3072, 3200, 3328, 3456
PB_NSAKS = 3584
PB_WIDTH = 4096
PF_KC, PF_VC, PF_GATE = 0, 128, 256
PF_WIDTH = 384
W_SBQ, W_SBK, W_SBV, W_SWAQ, W_SWAK, W_SWAV, W_NSAQ = 0, 256, 512, 768, 1024, 1152, 1280
W_KC, W_VC, W_KS, W_VS, W_KW, W_VW, W_GATE = 1536, 1664, 1792, 1920, 2176, 2304, 2432
W_CQ, W_CKV, W_KPE = 2560, 2816, 2944
W_WIDTH = 3456

NT_DIMS = (((1,), (1,)), ((), ()))


def _dot(a, b):
    return jnp.dot(a, b, preferred_element_type=F32)


def _dot_nt(a, b):
    return lax.dot_general(a, b, NT_DIMS, preferred_element_type=F32)


def _split_bf16(x):
    hi = x.astype(BF16)
    lo = (x - hi.astype(F32)).astype(BF16)
    return hi, lo


def _dot_exact_rhs(x, m):
    hi, lo = _split_bf16(x)
    return _dot(hi, m) + _dot(lo, m)


def _block_diag_ones(width, seg):
    idx = np.arange(width) // seg
    return jnp.asarray(idx[:, None] == idx[None, :], BF16)


def _seg_rms(x, seg_ones, count):
    width = x.shape[1]
    ms = _dot_exact_rhs(x * x, seg_ones[:width, :width]) * (1.0 / count)
    return x * lax.rsqrt(ms + EPS)


def _row_rms(x, count):
    return x * lax.rsqrt(jnp.sum(x * x, axis=-1, keepdims=True) * (1.0 / count) + EPS)


def _prep_kernel(x_ref, an_ref, w_ref, g64_ref, glat_ref, wq_ref, wkv_ref, gmla_ref, cos_ref, sin_ref,
                 s64_ref, s128_ref, pb_ref, pf_ref):
    x = x_ref[...]
    s64 = s64_ref[...]
    s128 = s128_ref[...]
    h = _row_rms(x, D_MODEL) * an_ref[...]
    hb = h.astype(BF16)

    def proj(lo, hi):
        return _dot(hb, w_ref[:, lo:hi])

    def put(col, value):
        pb_ref[:, col:col + value.shape[1]] = value.astype(BF16)

    def ones_tail(width):
        lane = lax.broadcasted_iota(jnp.int32, (1, width), 1)
        return jnp.where((lane & (LANES - 1)) >= HEAD_DIM, 1.0, 0.0)

    scale = HEAD_DIM ** -0.5
    g64 = g64_ref[...]
    put(PB_SBQ, proj(W_SBQ, W_SBQ + 256) * scale)
    put(PB_SBK, proj(W_SBK, W_SBK + 256))
    put(PB_SBV, proj(W_SBV, W_SBV + 256))
    put(PB_SWAQ, _seg_rms(proj(W_SWAQ, W_SWAQ + 256), s64, 64) * g64[0:1, :] * scale)
    put(PB_SWAK, _seg_rms(proj(W_SWAK, W_SWAK + 128), s64, 64) * g64[1:2, :128])
    put(PB_SWAV, proj(W_SWAV, W_SWAV + 128))
    put(PB_NSAQ, _seg_rms(proj(W_NSAQ, W_NSAQ + 256), s64, 64) * g64[2:3, :] * scale)
    pf_ref[:, PF_KC:PF_KC + 128] = proj(W_KC, W_KC + 128)
    pf_ref[:, PF_VC:PF_VC + 128] = proj(W_VC, W_VC + 128)
    ks = _seg_rms(proj(W_KS, W_KS + 128), s64, 64) * g64[3:4, :128]
    ks_swapped = pltpu.roll(ks, HEAD_DIM, 1)
    low = lax.broadcasted_iota(jnp.int32, ks.shape, 1) < HEAD_DIM
    put(PB_NSAKS, jnp.where(low, ks, 0.0))
    put(PB_NSAKS + LANES, jnp.where(low, 0.0, ks_swapped))
    put(PB_NSAKS + 2 * LANES, jnp.where(low, ks_swapped, 0.0))
    put(PB_NSAKS + 3 * LANES, jnp.where(low, 0.0, ks))
    put(PB_NSAVS, proj(W_VS, W_VS + 256) + ones_tail(256))
    put(PB_NSAKW, _seg_rms(proj(W_KW, W_KW + 128), s64, 64) * g64[4:5, :128])
    put(PB_NSAVW, proj(W_VW, W_VW + 128))
    pf_ref[:, PF_GATE:PF_GATE + 128] = jax.nn.sigmoid(proj(W_GATE, W_GATE + 128))

    glat = glat_ref[...]
    cq = _row_rms(proj(W_CQ, W_CQ + 256), MLA_Q_RANK) * glat[0:1, :]
    q = _dot(cq.astype(BF16), wq_ref[...])
    ckv = _row_rms(proj(W_CKV, W_CKV + 128), MLA_KV_RANK) * glat[1:2, :128]
    kv = _dot(ckv.astype(BF16), wkv_ref[...])
    k = kv[:, :512] + proj(W_KPE, W_KPE + 512)
    gm = gmla_ref[...]
    q = _seg_rms(q, s128, MLA_QK) * gm[0:1, :]
    k = _seg_rms(k, s128, MLA_QK) * gm[1:2, :]
    cos = cos_ref[...]
    sin = sin_ref[...]
    lane = lax.broadcasted_iota(jnp.int32, (x.shape[0], LANES), 1)
    first_half = lane < MLA_NOPE + MLA_ROPE // 2

    def rope(t):
        partner = jnp.where(first_half, pltpu.roll(t, LANES - MLA_ROPE // 2, 1), pltpu.roll(t, MLA_ROPE // 2, 1))
        return t * cos + partner * sin

    qscale = MLA_QK ** -0.5
    for hd in range(MLA_HEADS):
        sl = slice(hd * LANES, (hd + 1) * LANES)
        put(PB_MLAQ + hd * LANES, rope(q[:, sl]) * qscale)
        put(PB_MLAK + hd * LANES, rope(k[:, sl]))
    put(PB_MLAV, kv[:, 512:1024] + ones_tail(512))


def _prep(x2, an, w, g64, glat, wq, wkv, gmla, cos_t, sin_t, seq, tm):
    tokens = x2.shape[0]
    n_pos = seq // tm
    const = lambda t: (0, 0)
    return pl.pallas_call(
        _prep_kernel,
        out_shape=(jax.ShapeDtypeStruct((tokens, PB_WIDTH), BF16), jax.ShapeDtypeStruct((tokens, PF_WIDTH), F32)),
        grid=(tokens // tm,),
        in_specs=[
            pl.BlockSpec((tm, D_MODEL), lambda t: (t, 0)),
            pl.BlockSpec((1, D_MODEL), const),
            pl.BlockSpec((D_MODEL, W_WIDTH), const),
            pl.BlockSpec((8, 256), const),
            pl.BlockSpec((8, 256), const),
            pl.BlockSpec((256, 512), const),
            pl.BlockSpec((128, 1024), const),
            pl.BlockSpec((8, 512), const),
            pl.BlockSpec((tm, LANES), lambda t: (t % n_pos, 0)),
            pl.BlockSpec((tm, LANES), lambda t: (t % n_pos, 0)),
            pl.BlockSpec((256, 256), const),
            pl.BlockSpec((512, 512), const),
        ],
        out_specs=(pl.BlockSpec((tm, PB_WIDTH), lambda t: (t, 0)), pl.BlockSpec((tm, PF_WIDTH), lambda t: (t, 0))),
        compiler_params=pltpu.CompilerParams(dimension_semantics=("parallel",), vmem_limit_bytes=VMEM_LIMIT),
        name="prep",
    )(x2, an, w, g64, glat, wq, wkv, gmla, cos_t, sin_t, _block_diag_ones(256, HEAD_DIM),
      _block_diag_ones(512, LANES))


SB_TQ, SB_KB, SB_SUB = 256, 2048, 128


SB_UNDERFLOW = 110.0


def _sb_kernel(qi_ref, kj_ref, first_ref, last_ref, kmax_ref, q_ref, k_ref, v_ref, sums_ref, *refs, resume):
    if resume:
        acc_in_ref, carry_in_ref = refs[:2]
        refs = refs[2:]
        o_ref, carry_out_ref, carry_ref, acc_ref, zb_ref, dead_ref = refs
        slack_ref = None
    else:
        o_ref, carry_out_ref, slack_ref, carry_ref, acc_ref, zb_ref, dead_ref = refs
    b = pl.program_id(0)
    p = pl.program_id(1)
    qi = qi_ref[p]
    kj = kj_ref[p]
    q_start = qi * SB_TQ

    @pl.when(first_ref[p] == 1)
    def _():
        if resume:
            for hd in range(4):
                carry_ref[hd * SB_TQ:(hd + 1) * SB_TQ, :] = carry_in_ref[:, hd * SB_SUB:(hd + 1) * SB_SUB]
                acc_ref[hd] = acc_in_ref[:, hd * HEAD_DIM:(hd + 1) * HEAD_DIM]
        else:
            carry_ref[...] = jnp.zeros_like(carry_ref)
            acc_ref[...] = jnp.zeros_like(acc_ref)
        dead_ref[0] = 0
        ones = jnp.ones((HEAD_DIM, SB_SUB), BF16)
        for hd in range(4):
            q = q_ref[:, hd * HEAD_DIM:(hd + 1) * HEAD_DIM].astype(F32)
            zb_ref[hd * SB_TQ:(hd + 1) * SB_TQ, :] = (jnp.sqrt(_dot_exact_rhs(q * q, ones))
                                                      * (kmax_ref[b * 4 + hd] * 1.01))

    rel = (lax.broadcasted_iota(jnp.int32, (SB_TQ, SB_SUB), 1)
           - lax.broadcasted_iota(jnp.int32, (SB_TQ, SB_SUB), 0))

    def sub_tile(u, k_start):
        mask = jnp.tile(rel < q_start - k_start, (4, 1))
        rows = slice(u * SB_SUB, (u + 1) * SB_SUB)
        head = lambda hd: slice(hd * HEAD_DIM, (hd + 1) * HEAD_DIM)
        z = jnp.concatenate([_dot_nt(q_ref[:, head(hd)], k_ref[rows, head(hd)]) for hd in range(4)], axis=0)
        log_keep = jnp.where(mask, -(jnp.maximum(z, 0.0) + jnp.log(1.0 + jnp.exp(-jnp.abs(z)))), 0.0)
        hi, lo = _split_bf16(log_keep)
        sums = _dot(jnp.concatenate([hi, lo], axis=1), sums_ref[...])
        carry = carry_ref[...]
        a = jnp.where(mask, jnp.exp(z + sums[:, :SB_SUB] + carry), 0.0).astype(BF16)
        for hd in range(4):
            acc_ref[hd] += _dot(a[hd * SB_TQ:(hd + 1) * SB_TQ], v_ref[rows, head(hd)])
        carry_ref[...] = carry + sums[:, SB_SUB:]

    for u in reversed(range(SB_KB // SB_SUB)):
        k_start = kj * SB_KB + u * SB_SUB

        @pl.when((k_start < q_start + SB_TQ) & (dead_ref[0] == 0))
        def _(u=u, k_start=k_start):
            live = jnp.max(carry_ref[...] + zb_ref[...]) > -SB_UNDERFLOW

            @pl.when(live)
            def _():
                sub_tile(u, k_start)

            @pl.when(jnp.logical_not(live))
            def _():
                dead_ref[0] = 1

    @pl.when(last_ref[p] == 1)
    def _():
        for hd in range(4):
            o_ref[:, hd * HEAD_DIM:(hd + 1) * HEAD_DIM] = acc_ref[hd]
            carry_out_ref[:, hd * SB_SUB:(hd + 1) * SB_SUB] = carry_ref[hd * SB_TQ:(hd + 1) * SB_TQ, :]
        if slack_ref is not None:
            slack_ref[...] = jnp.full(slack_ref.shape, jnp.max(carry_ref[...] + zb_ref[...]), F32)


def _sb_call(pairs, pb, kmax, sums, batch, seq, state):
    qi = [i for i, blocks in pairs for _ in blocks]
    kj = [j for _, blocks in pairs for j in blocks]
    first = [int(n == 0) for _, blocks in pairs for n in range(len(blocks))]
    last = [int(n == len(blocks) - 1) for _, blocks in pairs for n in range(len(blocks))]
    prefetch = [jnp.asarray(np.array(a, np.int32)) for a in (qi, kj, first, last)]
    tile = lambda width: pl.BlockSpec((None, SB_TQ, width), lambda b, p, qi, kj, fi, la: (b, qi[p], 0))
    in_specs = [
        pl.BlockSpec(memory_space=pltpu.SMEM),
        pl.BlockSpec((None, SB_TQ, 256), lambda b, p, qi, kj, fi, la: (b, qi[p], PB_SBQ // 256)),
        pl.BlockSpec((None, SB_KB, 256), lambda b, p, qi, kj, fi, la: (b, kj[p], PB_SBK // 256)),
        pl.BlockSpec((None, SB_KB, 256), lambda b, p, qi, kj, fi, la: (b, kj[p], PB_SBV // 256)),
        pl.BlockSpec((2 * SB_SUB, 2 * SB_SUB), lambda b, p, qi, kj, fi, la: (0, 0)),
    ]
    resume = state is not None
    out_specs = [tile(256), tile(4 * SB_SUB)]
    out_shape = [jax.ShapeDtypeStruct((batch, seq, 256), F32), jax.ShapeDtypeStruct((batch, seq, 4 * SB_SUB), F32)]
    if resume:
        in_specs += [tile(256), tile(4 * SB_SUB)]
    else:
        out_specs.append(pl.BlockSpec((None, None, 8, LANES), lambda b, p, qi, kj, fi, la: (b, qi[p], 0, 0)))
        out_shape.append(jax.ShapeDtypeStruct((batch, seq // SB_TQ, 8, LANES), F32))
    grid_spec = pltpu.PrefetchScalarGridSpec(
        num_scalar_prefetch=4,
        grid=(batch, len(qi)),
        in_specs=in_specs,
        out_specs=tuple(out_specs),
        scratch_shapes=[pltpu.VMEM((4 * SB_TQ, SB_SUB), F32), pltpu.VMEM((4, SB_TQ, HEAD_DIM), F32),
                        pltpu.VMEM((4 * SB_TQ, SB_SUB), F32), pltpu.SMEM((1,), jnp.int32)],
    )
    return pl.pallas_call(
        functools.partial(_sb_kernel, resume=resume),
        out_shape=tuple(out_shape),
        grid_spec=grid_spec,
        input_output_aliases={len(prefetch) + 5: 0, len(prefetch) + 6: 1} if resume else {},
        compiler_params=pltpu.CompilerParams(dimension_semantics=("parallel", "arbitrary"),
                                             vmem_limit_bytes=VMEM_LIMIT),
        name="stick_breaking_far" if resume else "stick_breaking",
    )(*prefetch, kmax, pb, pb, pb, sums, *(state or ()))


def _sb_attention(pb, batch, seq):
    keys = pb[:, :, PB_SBK:PB_SBK + 256].astype(F32).reshape(batch, seq, 4, HEAD_DIM)
    kmax = jnp.sqrt(jnp.max(jnp.sum(keys * keys, axis=-1), axis=1)).reshape(batch * 4)
    j = np.arange(2 * SB_SUB)[:, None] % SB_SUB
    s = np.arange(2 * SB_SUB)[None, :]
    sums = jnp.asarray((s >= SB_SUB) | (j >= s), BF16)
    near, far = [], []
    for i in range(seq // SB_TQ):
        blocks = list(reversed(range((i * SB_TQ + SB_TQ - 1) // SB_KB + 1)))
        n_near = 2 if (i * SB_TQ) % SB_KB < 2 * SB_TQ else 1
        near.append((i, blocks[:n_near]))
        if blocks[n_near:]:
            far.append((i, blocks[n_near:]))
    values, carry, slack = _sb_call(near, pb, kmax, sums, batch, seq, None)
    if not far:
        return values
    has_far = jnp.asarray(np.array([i for i, _ in far], np.int32))
    return lax.cond(jnp.max(slack[:, has_far]) > -SB_UNDERFLOW,
                    lambda: _sb_call(far, pb, kmax, sums, batch, seq, (values, carry))[0],
                    lambda: values)


def _banded_kernel(sink_ref, q_ref, kp_ref, kc_ref, vp_ref, vc_ref, bias_ref, o_ref, *, tq, pad, window, use_sink):
    i = pl.program_id(1)
    dist_prev = (lax.broadcasted_iota(jnp.int32, (tq, pad), 0) + pad
                 - lax.broadcasted_iota(jnp.int32, (tq, pad), 1))
    dist_cur = lax.broadcasted_iota(jnp.int32, (tq, tq), 0) - lax.broadcasted_iota(jnp.int32, (tq, tq), 1)
    mask_prev = (dist_prev < window) & (i > 0)
    mask_cur = (dist_cur >= 0) & (dist_cur < window)
    def scores(hd):
        kcols = slice((hd // 2) * HEAD_DIM, (hd // 2 + 1) * HEAD_DIM)
        q = q_ref[:, hd * HEAD_DIM:(hd + 1) * HEAD_DIM]
        return (jnp.where(mask_prev, _dot_nt(q, kp_ref[:, kcols]) + bias_ref[hd, :, :pad], NEG),
                jnp.where(mask_cur, _dot_nt(q, kc_ref[:, kcols]) + bias_ref[hd, :, pad:], NEG))

    ahead = 2
    pending = [scores(hd) for hd in range(ahead)]
    for hd in range(4):
        if hd + ahead < 4:
            pending.append(scores(hd + ahead))
        s_prev, s_cur = pending.pop(0)
        cols = slice(hd * HEAD_DIM, (hd + 1) * HEAD_DIM)
        kcols = slice((hd // 2) * HEAD_DIM, (hd // 2 + 1) * HEAD_DIM)
        m = jnp.maximum(jnp.max(s_prev, axis=-1, keepdims=True), jnp.max(s_cur, axis=-1, keepdims=True))
        if use_sink:
            sink = sink_ref[hd]
            m = jnp.maximum(m, sink)
        p_prev = jnp.where(mask_prev, jnp.exp(s_prev - m), 0.0)
        p_cur = jnp.where(mask_cur, jnp.exp(s_cur - m), 0.0)
        denom = jnp.sum(p_prev, axis=-1, keepdims=True) + jnp.sum(p_cur, axis=-1, keepdims=True)
        if use_sink:
            denom = denom + jnp.exp(sink - m)
        o = _dot(p_prev.astype(BF16), vp_ref[:, kcols]) + _dot(p_cur.astype(BF16), vc_ref[:, kcols])
        o_ref[:, cols] = o / jnp.maximum(denom, 1e-30)


BAND_TQ = 256


def _band_tiles(window):
    pad = -(-(window - 1) // LANES) * LANES
    return pad, max(pad, BAND_TQ)


def _banded_attention(pb, sinks, bias, batch, seq, window, q_col, k_col, v_col, use_sink):
    pad, tq = _band_tiles(window)
    per = tq // pad
    prev = lambda i: jnp.maximum(i * per - 1, 0)
    grid_spec = pltpu.PrefetchScalarGridSpec(
        num_scalar_prefetch=1,
        grid=(batch, seq // tq),
        in_specs=[
            pl.BlockSpec((None, tq, 256), lambda b, i, s: (b, i, q_col // 256)),
            pl.BlockSpec((None, pad, 128), lambda b, i, s: (b, prev(i), k_col // 128)),
            pl.BlockSpec((None, tq, 128), lambda b, i, s: (b, i, k_col // 128)),
            pl.BlockSpec((None, pad, 128), lambda b, i, s: (b, prev(i), v_col // 128)),
            pl.BlockSpec((None, tq, 128), lambda b, i, s: (b, i, v_col // 128)),
            pl.BlockSpec((4, tq, pad + tq), lambda b, i, s: (0, 0, 0)),
        ],
        out_specs=pl.BlockSpec((None, tq, 256), lambda b, i, s: (b, i, 0)),
    )
    return pl.pallas_call(
        functools.partial(_banded_kernel, tq=tq, pad=pad, window=window, use_sink=use_sink),
        out_shape=jax.ShapeDtypeStruct((batch, seq, 256), F32),
        grid_spec=grid_spec,
        compiler_params=pltpu.CompilerParams(dimension_semantics=("parallel", "arbitrary"),
                                             vmem_limit_bytes=VMEM_LIMIT),
        name="banded_w%d" % window,
    )(sinks, pb, pb, pb, pb, pb, bias)


def _compress_kernel(rows_ref, pos_ref, w1_ref, w2_ref, g_ref, o_ref):
    win = rows_ref[...] + pos_ref[...]
    hid = jax.nn.gelu(_dot(win.astype(BF16), w1_ref[...]), approximate=True)
    out = _dot(hid.astype(BF16), w2_ref[...])
    normed = _row_rms(out, HEAD_DIM) * g_ref[...]
    o_ref[...] = jnp.where(pl.program_id(0) == 0, normed, out).astype(BF16)


def _compress(rows, pos, w1, w2, gain, tn):
    _, bh, ncp, width = rows.shape
    return pl.pallas_call(
        _compress_kernel,
        out_shape=jax.ShapeDtypeStruct((2, bh, ncp, HEAD_DIM), BF16),
        grid=(2, bh, ncp // tn),
        in_specs=[
            pl.BlockSpec((None, None, tn, width), lambda c, r, n: (c, r, n, 0)),
            pl.BlockSpec((None, 1, width), lambda c, r, n: (c, 0, 0)),
            pl.BlockSpec((None, width, NSA_CMP_HIDDEN), lambda c, r, n: (c, 0, 0)),
            pl.BlockSpec((None, NSA_CMP_HIDDEN, HEAD_DIM), lambda c, r, n: (c, 0, 0)),
            pl.BlockSpec((1, HEAD_DIM), lambda c, r, n: (0, 0)),
        ],
        out_specs=pl.BlockSpec((None, None, tn, HEAD_DIM), lambda c, r, n: (c, r, n, 0)),
        compiler_params=pltpu.CompilerParams(dimension_semantics=("parallel", "parallel", "parallel"),
                                             vmem_limit_bytes=VMEM_LIMIT),
        name="nsa_compress",
    )(rows, pos, w1, w2, gain)


CMP_TQ = 1024


def _cmp_kernel(q_ref, kc_ref, vc_ref, bias_ref, ov_ref, o_ref, qa_ref, *, ncp, n_sel_pad, topk):
    i = pl.program_id(2)
    q_pos = i * CMP_TQ + lax.broadcasted_iota(jnp.int32, (CMP_TQ, ncp), 0)
    cmp_end = lax.broadcasted_iota(jnp.int32, (CMP_TQ, ncp), 1) * NSA_CMP_STRIDE + (NSA_CMP_LEN - 1)
    mask = cmp_end <= q_pos
    kc = kc_ref[...]
    vc = vc_ref[...]
    p_sum = jnp.zeros((CMP_TQ, ncp), F32)
    for g in range(2):
        cols = slice(g * HEAD_DIM, (g + 1) * HEAD_DIM)
        s = jnp.where(mask, _dot_nt(q_ref[:, cols], kc) + bias_ref[g], NEG)
        m = jnp.max(s, axis=-1, keepdims=True)
        p = jnp.where(mask, jnp.exp(s - m), 0.0)
        p = p / jnp.maximum(jnp.sum(p, axis=-1, keepdims=True), 1e-30)
        o_ref[:, cols] = _dot(p.astype(BF16), vc)
        p_sum = p_sum + p
    imp = _dot_exact_rhs(p_sum, ov_ref[...])

    row_pos = i * CMP_TQ + lax.broadcasted_iota(jnp.int32, (CMP_TQ, n_sel_pad), 0)
    blk = lax.broadcasted_iota(jnp.int32, (CMP_TQ, n_sel_pad), 1)
    cur = row_pos >> int(math.log2(NSA_SEL_LEN))
    forced = (blk == 0) | (blk == cur) | (blk == cur - 1)
    valid = blk * NSA_SEL_LEN <= row_pos
    score = jnp.where(valid, imp + jnp.where(forced, FORCE_BONUS, 0.0), NEG)
    blk_f = blk.astype(F32)
    dropped = jnp.ones((CMP_TQ, n_sel_pad), F32)
    for _ in range(topk):
        best = jnp.max(score, axis=-1, keepdims=True)
        first = jnp.min(jnp.where(score == best, blk_f, float(n_sel_pad)), axis=-1, keepdims=True)
        hit = blk_f == first
        dropped = jnp.where(hit, 0.0, dropped)
        score = jnp.where(hit, -jnp.inf, score)
    dropped = jnp.where(valid, dropped, 1.0).astype(BF16)
    q = q_ref[...]
    low = lax.broadcasted_iota(jnp.int32, q.shape, 1) < HEAD_DIM
    zero = jnp.zeros_like(q)
    width = LANES + n_sel_pad
    qa_ref[:, 0:LANES] = jnp.where(low, q, zero)
    qa_ref[:, LANES:width] = dropped
    qa_ref[:, width:width + LANES] = jnp.where(low, zero, q)
    qa_ref[:, width + LANES:2 * width] = dropped


def _cmp_attention(pb, kvc, bias_c, overlap, batch, seq):
    ncp = kvc.shape[2]
    n_sel_pad = overlap.shape[1]
    topk = min(NSA_TOPK, seq // NSA_SEL_LEN)
    pair = 2 * (LANES + n_sel_pad)
    return pl.pallas_call(
        functools.partial(_cmp_kernel, ncp=ncp, n_sel_pad=n_sel_pad, topk=topk),
        out_shape=(jax.ShapeDtypeStruct((batch, seq, 256), F32),
                   jax.ShapeDtypeStruct((batch, seq, 2 * pair), BF16)),
        grid=(batch, 2, seq // CMP_TQ),
        in_specs=[
            pl.BlockSpec((None, CMP_TQ, 128), lambda b, h, i: (b, i, PB_NSAQ // 128 + h)),
            pl.BlockSpec((None, None, ncp, HEAD_DIM), lambda b, h, i: (0, b * 2 + h, 0, 0)),
            pl.BlockSpec((None, None, ncp, HEAD_DIM), lambda b, h, i: (1, b * 2 + h, 0, 0)),
            pl.BlockSpec((2, CMP_TQ, ncp), lambda b, h, i: (h, i, 0)),
            pl.BlockSpec((ncp, n_sel_pad), lambda b, h, i: (0, 0)),
        ],
        out_specs=(pl.BlockSpec((None, CMP_TQ, 128), lambda b, h, i: (b, i, h)),
                   pl.BlockSpec((None, CMP_TQ, pair), lambda b, h, i: (b, i, h))),
        compiler_params=pltpu.CompilerParams(dimension_semantics=("parallel", "parallel", "arbitrary"),
                                             vmem_limit_bytes=VMEM_LIMIT),
        name="nsa_cmp_select",
    )(pb, kvc, kvc, bias_c, overlap)


SEL_T = 512


def _flash_init(m_ref, acc_ref):
    m_ref[...] = jnp.full_like(m_ref, NEG)
    acc_ref[...] = jnp.zeros_like(acc_ref)


def _flash_update(s, v_ones, m_ref, acc_ref, hd):
    m_old = m_ref[hd]
    m_new = jnp.maximum(m_old, jnp.max(s, axis=-1, keepdims=True))
    alpha = jnp.exp(m_old - m_new)
    pr = jnp.exp(s - jnp.tile(m_new, (1, s.shape[1] // LANES)))
    acc_ref[hd] = alpha * acc_ref[hd] + _dot(pr.astype(BF16), v_ones)
    m_ref[hd] = m_new


def _flash_finish(acc_ref, o_ref, heads):
    for hd in range(heads):
        acc = acc_ref[hd]
        row_sum = pltpu.roll(acc, HEAD_DIM, 1)
        o_ref[:, hd * HEAD_DIM:(hd + 1) * HEAD_DIM] = (acc / jnp.maximum(row_sum, 1e-30))[:, :HEAD_DIM]


def _sel_kernel(qi_ref, kj_ref, q_ref, k_ref, v_ref, pen_ref, bias_ref, o_ref, m_ref, acc_ref):
    p = pl.program_id(1)
    qi = qi_ref[p]
    kj = kj_ref[p]
    width = q_ref.shape[1] // 4

    @pl.when(kj == 0)
    def _():
        _flash_init(m_ref, acc_ref)

    def step(diagonal):
        pen = pen_ref[...]

        def scores(hd):
            k_pen = jnp.concatenate([k_ref[:, hd * LANES:(hd + 1) * LANES], pen], axis=1)
            s = _dot_nt(q_ref[:, hd * width:(hd + 1) * width], k_pen) + bias_ref[hd]
            if diagonal:
                row = lax.broadcasted_iota(jnp.int32, (SEL_T, SEL_T), 0)
                col = lax.broadcasted_iota(jnp.int32, (SEL_T, SEL_T), 1)
                s = jnp.where(col <= row, s, NEG)
            return s

        ahead = 3
        pending = [scores(hd) for hd in range(ahead)]
        for hd in range(4):
            if hd + ahead < 4:
                pending.append(scores(hd + ahead))
            _flash_update(pending.pop(0), v_ref[:, (hd // 2) * LANES:(hd // 2 + 1) * LANES], m_ref, acc_ref, hd)

    @pl.when(kj < qi)
    def _():
        step(False)

    @pl.when(kj == qi)
    def _():
        step(True)
        _flash_finish(acc_ref, o_ref, 4)


def _causal_pairs(n):
    qi, kj = [], []
    for i in range(n):
        for j in range(i + 1):
            qi.append(i)
            kj.append(j)
    return jnp.asarray(np.array(qi, np.int32)), jnp.asarray(np.array(kj, np.int32))


def _sel_attention(pb, q_aug, penalty, bias_s, batch, seq):
    n_sel_pad = penalty.shape[1]
    n_delta = bias_s.shape[1]
    qi, kj = _causal_pairs(seq // SEL_T)
    grid_spec = pltpu.PrefetchScalarGridSpec(
        num_scalar_prefetch=2,
        grid=(batch, int(qi.shape[0])),
        in_specs=[
            pl.BlockSpec((None, SEL_T, q_aug.shape[2]), lambda b, p, qi, kj: (b, qi[p], 0)),
            pl.BlockSpec((None, SEL_T, 512), lambda b, p, qi, kj: (b, kj[p], PB_NSAKS // 512)),
            pl.BlockSpec((None, SEL_T, 256), lambda b, p, qi, kj: (b, kj[p], PB_NSAVS // 256)),
            pl.BlockSpec((SEL_T, n_sel_pad), lambda b, p, qi, kj: (kj[p], 0)),
            pl.BlockSpec((4, None, SEL_T, SEL_T),
                         lambda b, p, qi, kj: (0, jnp.minimum(qi[p] - kj[p], n_delta - 1), 0, 0)),
        ],
        out_specs=pl.BlockSpec((None, SEL_T, 256), lambda b, p, qi, kj: (b, qi[p], 0)),
        scratch_shapes=[pltpu.VMEM((4, SEL_T, LANES), F32), pltpu.VMEM((4, SEL_T, LANES), F32)],
    )
    return pl.pallas_call(
        _sel_kernel,
        out_shape=jax.ShapeDtypeStruct((batch, seq, 256), F32),
        grid_spec=grid_spec,
        compiler_params=pltpu.CompilerParams(dimension_semantics=("parallel", "arbitrary"),
                                             vmem_limit_bytes=VMEM_LIMIT),
        name="nsa_selected",
    )(qi, kj, q_aug, pb, pb, penalty, bias_s)


MLA_T = 1024


def _mla_kernel(qi_ref, kj_ref, q_ref, k_ref, v_ref, o_ref, m_ref, acc_ref):
    p = pl.program_id(1)
    qi = qi_ref[p]
    kj = kj_ref[p]

    @pl.when(kj == 0)
    def _():
        _flash_init(m_ref, acc_ref)

    def step(diagonal):
        def scores(hd):
            cols = slice(hd * LANES, (hd + 1) * LANES)
            s = _dot_nt(q_ref[:, cols], k_ref[:, cols])
            if diagonal:
                row = lax.broadcasted_iota(jnp.int32, (MLA_T, MLA_T), 0)
                col = lax.broadcasted_iota(jnp.int32, (MLA_T, MLA_T), 1)
                s = jnp.where(col <= row, s, NEG)
            return s

        ahead = 3
        pending = [scores(hd) for hd in range(ahead)]
        for hd in range(MLA_HEADS):
            if hd + ahead < MLA_HEADS:
                pending.append(scores(hd + ahead))
            _flash_update(pending.pop(0), v_ref[:, hd * LANES:(hd + 1) * LANES], m_ref, acc_ref, hd)

    @pl.when(kj < qi)
    def _():
        step(False)

    @pl.when(kj == qi)
    def _():
        step(True)
        _flash_finish(acc_ref, o_ref, MLA_HEADS)


def _mla_attention(pb, batch, seq):
    qi, kj = _causal_pairs(seq // MLA_T)
    grid_spec = pltpu.PrefetchScalarGridSpec(
        num_scalar_prefetch=2,
        grid=(batch, int(qi.shape[0])),
        in_specs=[
            pl.BlockSpec((None, MLA_T, 512), lambda b, p, qi, kj: (b, qi[p], PB_MLAQ // 512)),
            pl.BlockSpec((None, MLA_T, 512), lambda b, p, qi, kj: (b, kj[p], PB_MLAK // 512)),
            pl.BlockSpec((None, MLA_T, 512), lambda b, p, qi, kj: (b, kj[p], PB_MLAV // 512)),
        ],
        out_specs=pl.BlockSpec((None, MLA_T, 256), lambda b, p, qi, kj: (b, qi[p], 0)),
        scratch_shapes=[pltpu.VMEM((4, MLA_T, LANES), F32), pltpu.VMEM((4, MLA_T, LANES), F32)],
    )
    return pl.pallas_call(
        _mla_kernel,
        out_shape=jax.ShapeDtypeStruct((batch, seq, 256), F32),
        grid_spec=grid_spec,
        compiler_params=pltpu.CompilerParams(dimension_semantics=("parallel", "arbitrary"),
                                             vmem_limit_bytes=VMEM_LIMIT),
        name="mla_causal",
    )(qi, kj, pb, pb, pb)


def _outproj_kernel(x_ref, oa_ref, ob_ref, oc_ref, os_ref, ow_ref, od_ref, gate_ref, gexp_ref, gn_ref, w_ref,
                    o_ref):
    gates = gate_ref[...]
    g_hi, g_lo = _split_bf16(gates)

    def gate(branch):
        e = gexp_ref[branch]
        return _dot(g_hi, e) + _dot(g_lo, e)

    o_nsa = gate(0) * oc_ref[...] + gate(1) * os_ref[...] + gate(2) * ow_ref[...]
    gn = gn_ref[...]
    y = x_ref[...]
    for grp, o in enumerate((oa_ref[...], ob_ref[...], o_nsa, od_ref[...])):
        cols = slice(grp * GROUP_WIDTH, (grp + 1) * GROUP_WIDTH)
        normed = _row_rms(o, GROUP_WIDTH) * gn[:, cols]
        y = y + _dot(normed.astype(BF16), w_ref[cols, :])
    o_ref[...] = y


def _outproj(x2, oa, ob, oc, osel, ow, od, pf, gexp, gn, w, tm):
    tokens = x2.shape[0]
    row = lambda t: (t, 0)
    o_spec = pl.BlockSpec((tm, GROUP_WIDTH), row)
    return pl.pallas_call(
        _outproj_kernel,
        out_shape=jax.ShapeDtypeStruct((tokens, D_MODEL), F32),
        grid=(tokens // tm,),
        in_specs=[
            pl.BlockSpec((tm, D_MODEL), row), o_spec, o_spec, o_spec, o_spec, o_spec, o_spec,
            pl.BlockSpec((tm, 128), lambda t: (t, PF_GATE // 128)),
            pl.BlockSpec((3, 128, GROUP_WIDTH), lambda t: (0, 0, 0)),
            pl.BlockSpec((1, D_MODEL), lambda t: (0, 0)),
            pl.BlockSpec((D_MODEL, D_MODEL), lambda t: (0, 0)),
        ],
        out_specs=pl.BlockSpec((tm, D_MODEL), row),
        compiler_params=pltpu.CompilerParams(dimension_semantics=("parallel",), vmem_limit_bytes=VMEM_LIMIT),
        name="out_proj",
    )(x2, oa, ob, oc, osel, ow, od, pf, gexp, gn, w)


MOE_TM = 1024
MOE_QUAD = 4
MOE_CHUNK = 256
MOE_CHUNK_ALIGN = 256
MOE_SLOTS = MOE_TM + MOE_GROUPS * MOE_CHUNK
MOE_VMEM_LIMIT = 60 * 1024 * 1024


def _moe_kernel(x_ref, fn_ref, wr_hi_ref, wr_lo_ref, br_ref, cexp_ref, wg_ref, wu_ref, wd_ref, o_ref, hs_ref,
                cs_ref, ys_ref, slot_ref, start_ref):
    step = pl.program_id(1)
    lane = lax.broadcasted_iota(jnp.int32, (MOE_TM, LANES), 1)

    @pl.when(step == 0)
    def _():
        h = _row_rms(x_ref[...], D_MODEL) * fn_ref[...]
        h_hi, h_lo = _split_bf16(h)
        logits = (_dot(h_hi, wr_hi_ref[...]) + _dot(h_lo, wr_hi_ref[...]) + _dot(h_hi, wr_lo_ref[...])
                  + br_ref[...])
        lane_f = lane.astype(F32)
        no_lane = float(LANES)
        is_group = lane < MOE_GROUPS
        g_max = jnp.max(jnp.where(is_group, logits, -jnp.inf), axis=-1, keepdims=True)
        g_star = jnp.min(jnp.where(is_group & (logits == g_max), lane_f, no_lane), axis=-1, keepdims=True)
        g_den = jnp.sum(jnp.where(is_group, jnp.exp(logits - g_max), 0.0), axis=-1, keepdims=True)
        g_w = 1.0 / g_den
        group_of_lane = ((lane - MOE_GROUPS) >> int(math.log2(MOE_EPG))).astype(F32)
        in_group = (lane >= MOE_GROUPS) & (lane < MOE_GROUPS + MOE_EXPERTS) & (group_of_lane == g_star)
        e_l = jnp.where(in_group, logits, -jnp.inf)
        top1 = jnp.max(e_l, axis=-1, keepdims=True)
        i1 = jnp.min(jnp.where(e_l == top1, lane_f, no_lane), axis=-1, keepdims=True)
        e_l2 = jnp.where(lane_f == i1, -jnp.inf, e_l)
        top2 = jnp.max(e_l2, axis=-1, keepdims=True)
        i2 = jnp.min(jnp.where(e_l2 == top2, lane_f, no_lane), axis=-1, keepdims=True)
        r = jnp.exp(top2 - top1)
        w1 = g_w / (1.0 + r)
        w2 = g_w * r / (1.0 + r)
        comb = jnp.where(lane_f == i1, w1, jnp.where(lane_f == i2, w2, 0.0))

        onehot = jnp.where(lane_f == g_star, 1.0, 0.0)
        row = lax.broadcasted_iota(jnp.int32, (MOE_TM, LANES), 0)
        incl = onehot
        shift = 1
        while shift < MOE_TM:
            incl = incl + jnp.where(row >= shift, pltpu.roll(incl, shift, 0), 0.0)
            shift *= 2
        counts = incl[MOE_TM - 1:MOE_TM, :]
        padded = jnp.floor((counts + (MOE_CHUNK - 0.5)) * (1.0 / MOE_CHUNK)) * MOE_CHUNK
        before = (lax.broadcasted_iota(jnp.int32, (LANES, LANES), 0)
                  < lax.broadcasted_iota(jnp.int32, (LANES, LANES), 1))
        starts = _dot(jnp.broadcast_to(padded, (8, LANES)).astype(BF16),
                      jnp.where(before, 1.0, 0.0).astype(BF16))[0:1, :]
        for g in range(MOE_GROUPS + 1):
            start_ref[g] = jnp.sum(jnp.where(lane[0:1, :] == g, starts, 0.0)).astype(jnp.int32)
        slot = jnp.sum(onehot * (starts + incl - onehot), axis=-1, keepdims=True)
        slot_ref[...] = jnp.broadcast_to(slot, (MOE_TM, LANES))
        slot_row = slot_ref[...].T[0:1, :]
        n_slots = hs_ref.shape[0]
        place = jnp.where(lax.broadcasted_iota(jnp.int32, (n_slots, MOE_TM), 0).astype(F32) == slot_row,
                          1.0, 0.0).astype(BF16)
        hs_ref[...] = _dot(place, h.astype(BF16)).astype(BF16)
        c_hi, c_lo = _split_bf16(comb)
        cs_ref[:, :LANES] = _dot(place, c_hi).astype(BF16)
        cs_ref[:, LANES:] = _dot(place, c_lo).astype(BF16)
        ys_ref[...] = jnp.zeros_like(ys_ref)

    group = step // (MOE_EPG // MOE_QUAD)
    first_slot = start_ref[group]
    half = MOE_QUAD * MOE_HIDDEN // 2

    def chunk(c, carry):
        rows = pl.ds(pl.multiple_of(first_slot + c * MOE_CHUNK, MOE_CHUNK_ALIGN), MOE_CHUNK)
        hb = hs_ref[rows, :]
        weight = _dot(cs_ref[rows, :], cexp_ref[...])
        y = None
        for s in range(2):
            cols = slice(s * half, (s + 1) * half)
            a = jax.nn.silu(_dot(hb, wg_ref[:, cols])) * _dot(hb, wu_ref[:, cols]) * weight[:, cols]
            part = _dot(a.astype(BF16), wd_ref[cols, :])
            y = part if y is None else y + part
        ys_ref[rows, :] += y
        return carry

    lax.fori_loop(0, (start_ref[group + 1] - first_slot) // MOE_CHUNK, chunk, 0)

    @pl.when(step == MOE_EXPERTS // MOE_QUAD - 1)
    def _():
        n_slots = hs_ref.shape[0]
        back = jnp.where(lax.broadcasted_iota(jnp.int32, (MOE_TM, n_slots), 1).astype(F32) == slot_ref[:, 0:1],
                         1.0, 0.0).astype(BF16)
        y_hi, y_lo = _split_bf16(ys_ref[...])
        o_ref[...] = x_ref[...] + _dot(back, y_hi) + _dot(back, y_lo)


def _moe(x2, fn, wr_hi, wr_lo, br, wg, wu, wd):
    tokens = x2.shape[0]
    quads = MOE_EXPERTS // MOE_QUAD
    width = MOE_QUAD * MOE_HIDDEN
    const = lambda t, e: (0, 0)
    lane_of_col = MOE_GROUPS + np.arange(quads)[:, None, None] * MOE_QUAD + np.arange(width)[None, None, :] // MOE_HIDDEN
    cexp = jnp.asarray((np.arange(2 * LANES)[None, :, None] % LANES) == lane_of_col, BF16)
    return pl.pallas_call(
        _moe_kernel,
        out_shape=jax.ShapeDtypeStruct((tokens, D_MODEL), F32),
        grid=(tokens // MOE_TM, quads),
        in_specs=[
            pl.BlockSpec((MOE_TM, D_MODEL), lambda t, e: (t, 0)),
            pl.BlockSpec((1, D_MODEL), const),
            pl.BlockSpec((D_MODEL, LANES), const),
            pl.BlockSpec((D_MODEL, LANES), const),
            pl.BlockSpec((1, LANES), const),
            pl.BlockSpec((None, 2 * LANES, width), lambda t, e: (e, 0, 0)),
            pl.BlockSpec((None, D_MODEL, width), lambda t, e: (e, 0, 0)),
            pl.BlockSpec((None, D_MODEL, width), lambda t, e: (e, 0, 0)),
            pl.BlockSpec((None, width, D_MODEL), lambda t, e: (e, 0, 0)),
        ],
        out_specs=pl.BlockSpec((MOE_TM, D_MODEL), lambda t, e: (t, 0)),
        scratch_shapes=[pltpu.VMEM((MOE_SLOTS, D_MODEL), BF16), pltpu.VMEM((MOE_SLOTS, 2 * LANES), BF16),
                        pltpu.VMEM((MOE_SLOTS, D_MODEL), F32), pltpu.VMEM((MOE_TM, LANES), F32),
                        pltpu.SMEM((8,), jnp.int32)],
        compiler_params=pltpu.CompilerParams(dimension_semantics=("parallel", "arbitrary"),
                                             vmem_limit_bytes=MOE_VMEM_LIMIT),
        name="hier_moe",
    )(x2, fn, wr_hi, wr_lo, br, cexp, wg, wu, wd)


def _t5_bucket(dist):
    n = jnp.maximum(dist, 0)
    max_exact = T5_BUCKETS // 2
    nf = jnp.maximum(n, 1).astype(F32)
    large = max_exact + (jnp.log(nf / max_exact) / math.log(T5_MAX_DIST / max_exact)
                         * (T5_BUCKETS - max_exact)).astype(jnp.int32)
    large = jnp.minimum(large, T5_BUCKETS - 1)
    return jnp.where(n < max_exact, n, large)


def _sel_delta_cap():
    max_exact = T5_BUCKETS // 2
    span = T5_BUCKETS - max_exact
    last_bucket_from = max_exact * (T5_MAX_DIST / max_exact) ** ((span - 1) / span)
    cap = 1
    while (cap - 1) * SEL_T + 1 < 1.25 * last_bucket_from:
        cap += 1
    return cap


def _position_tables(rel_bias, seq):
    buckets = _t5_bucket(jnp.arange(seq))
    first = jnp.sum(buckets[None, :] < jnp.arange(T5_BUCKETS)[:, None], axis=1)
    tbl = rel_bias.T

    def toeplitz(heads, dist):
        shape = (tbl[heads].shape[0],) + (1,) * dist.ndim
        out = jnp.broadcast_to(tbl[heads][:, 0].reshape(shape), shape[:1] + dist.shape)
        for b in range(1, T5_BUCKETS):
            out = jnp.where((dist >= first[b])[None], tbl[heads][:, b].reshape(shape), out)
        return out

    swa_h, nsa_h = slice(0, 4), slice(4, 8)
    def band(window):
        pad, tq = _band_tiles(window)
        return jnp.arange(tq)[:, None] + pad - jnp.arange(pad + tq)[None, :]

    bias_swa = toeplitz(swa_h, band(SWA_WINDOW))
    bias_win = toeplitz(nsa_h, band(NSA_WINDOW))
    ncp = seq // NSA_CMP_STRIDE
    cmp_end = jnp.arange(ncp) * NSA_CMP_STRIDE + NSA_CMP_LEN - 1
    bias_cmp = toeplitz(nsa_h, jnp.arange(seq)[:, None] - cmp_end[None, :])
    nd = min(_sel_delta_cap() + 1, seq // SEL_T)
    dist_s = (jnp.arange(nd)[:, None, None] * SEL_T + jnp.arange(SEL_T)[None, :, None]
              - jnp.arange(SEL_T)[None, None, :])
    bias_sel = toeplitz(nsa_h, dist_s)

    n_sel_pad = -(-(seq // NSA_SEL_LEN) // LANES) * LANES
    sel_start = np.arange(n_sel_pad) * NSA_SEL_LEN
    c_start = np.arange(ncp) * NSA_CMP_STRIDE
    c_end = c_start + NSA_CMP_LEN - 1
    real = (np.arange(ncp) < ncp - NSA_CMP_LEN // NSA_CMP_STRIDE + 1)[:, None] & (sel_start < seq)[None, :]
    overlap = ((c_start[:, None] < sel_start[None, :] + NSA_SEL_LEN) & (c_end[:, None] >= sel_start[None, :]) & real)
    own_block = (np.arange(seq) // NSA_SEL_LEN)[:, None] == np.arange(n_sel_pad)[None, :]
    penalty = np.where(own_block, -2.0 ** 100, 0.0)

    pos = jnp.arange(seq, dtype=F32)
    inv_freq = ROPE_THETA ** (-jnp.arange(0, MLA_ROPE, 2, dtype=F32) / MLA_ROPE)
    ang = pos[:, None] * inv_freq[None, :]
    cos, sin = jnp.cos(ang), jnp.sin(ang)
    ones = jnp.ones((seq, MLA_NOPE), F32)
    tail = LANES - MLA_NOPE - MLA_ROPE
    cos_t = jnp.concatenate([ones, cos, cos, jnp.ones((seq, tail), F32)], axis=1)
    sin_t = jnp.concatenate([0 * ones, -sin, sin, jnp.zeros((seq, tail), F32)], axis=1)
    return dict(bias_swa=bias_swa, bias_win=bias_win, bias_cmp=bias_cmp, bias_sel=bias_sel,
                overlap=jnp.asarray(overlap, BF16), penalty=jnp.asarray(penalty, BF16), cos_t=cos_t, sin_t=sin_t)


def _pad_to(a, shape):
    return jnp.pad(a, [(0, s - d) for d, s in zip(a.shape, shape)])


def _pack_layer(w_in, swa_q_norm, swa_k_norm, nsa_q_norm, nsa_k_norm, mla_q_lat_norm, mla_w_q_up,
                mla_kv_lat_norm, mla_w_kv_up, mla_q_norm, mla_k_norm):
    kpe = w_in[:, 2636:2668]
    kpe_seg = jnp.concatenate([jnp.zeros((D_MODEL, MLA_NOPE), F32), kpe,
                               jnp.zeros((D_MODEL, LANES - MLA_QK), F32)], axis=1)
    spread = lambda cols: _pad_to(cols.reshape(D_MODEL, -1, HEAD_DIM), (D_MODEL, cols.shape[1] // HEAD_DIM, LANES)
                                  ).reshape(D_MODEL, -1)
    w = jnp.concatenate([
        w_in[:, :1920],
        spread(w_in[:, 1920:2048]),
        w_in[:, 2048:2304],
        _pad_to(w_in[:, 2304:2316], (D_MODEL, 128)),
        _pad_to(w_in[:, 2316:2508], (D_MODEL, 256)),
        w_in[:, 2508:2636],
        jnp.tile(kpe_seg, (1, MLA_HEADS)),
    ], axis=1).astype(BF16)
    tile4 = lambda g: jnp.tile(g, 4)
    g64 = _pad_to(jnp.stack([tile4(swa_q_norm), tile4(swa_k_norm), tile4(nsa_q_norm),
                             tile4(nsa_k_norm[1]), tile4(nsa_k_norm[2])]), (8, 256))
    glat = _pad_to(jnp.stack([_pad_to(mla_q_lat_norm, (256,)), _pad_to(mla_kv_lat_norm, (256,))]), (8, 256))
    wq = _pad_to(mla_w_q_up.reshape(MLA_Q_RANK, MLA_HEADS, MLA_QK), (256, MLA_HEADS, LANES))
    wq = wq.reshape(256, MLA_HEADS * LANES).astype(BF16)
    wkv = mla_w_kv_up.reshape(MLA_KV_RANK, MLA_HEADS, MLA_NOPE + MLA_V)
    wk = _pad_to(wkv[:, :, :MLA_NOPE], (MLA_KV_RANK, MLA_HEADS, LANES)).reshape(MLA_KV_RANK, MLA_HEADS * LANES)
    wv = _pad_to(wkv[:, :, MLA_NOPE:], (MLA_KV_RANK, MLA_HEADS, LANES)).reshape(MLA_KV_RANK, MLA_HEADS * LANES)
    wkv_p = jnp.concatenate([wk, wv], axis=1).astype(BF16)
    gmla = _pad_to(jnp.stack([jnp.tile(_pad_to(mla_q_norm, (LANES,)), MLA_HEADS),
                              jnp.tile(_pad_to(mla_k_norm, (LANES,)), MLA_HEADS)]), (8, 512))
    return w, g64, glat, wq, wkv_p, gmla


def _gate_expand():
    rows = np.arange(128)[None, :, None]
    cols = np.arange(GROUP_WIDTH)[None, None, :]
    branch = np.arange(3)[:, None, None]
    return jnp.asarray(rows == branch * 4 + cols // HEAD_DIM, BF16)


def _compress_rows(pf3, batch, seq):
    nb = seq // NSA_CMP_STRIDE
    kv = pf3[:, :, :256].reshape(batch, seq, 2, 2, HEAD_DIM).transpose(2, 0, 3, 1, 4)
    blocks = kv.reshape(2, batch * 2, nb, NSA_CMP_STRIDE * HEAD_DIM)
    nxt = jnp.concatenate([blocks[:, :, 1:], jnp.zeros_like(blocks[:, :, :1])], axis=2)
    return jnp.concatenate([blocks, nxt], axis=3)


def kernel(x, rel_bias, attn_norm, w_in, swa_q_norm, swa_k_norm, swa_sinks, nsa_q_norm, nsa_k_norm, nsa_cmp_pos, nsa_cmp_w1, nsa_cmp_w2, mla_q_lat_norm, mla_w_q_up, mla_kv_lat_norm, mla_w_kv_up, mla_q_norm, mla_k_norm, out_norm, w_out, ffn_norm, moe_w_group, moe_b_group, moe_w_expert, moe_b_expert, moe_w_gate, moe_w_up, moe_w_down):
    batch, seq, _ = x.shape
    depth = w_in.shape[0]
    tokens = batch * seq
    tm = 512
    assert seq % 2048 == 0 and tokens % MOE_TM == 0
    tabs = _position_tables(rel_bias, seq)
    gexp = _gate_expand()
    zero_sinks = jnp.zeros((4,), F32)
    x2 = x.reshape(tokens, D_MODEL)
    for l in range(depth):
        w, g64, glat, wq, wkv, gmla = _pack_layer(
            w_in[l], swa_q_norm[l], swa_k_norm[l], nsa_q_norm[l], nsa_k_norm[l], mla_q_lat_norm[l],
            mla_w_q_up[l], mla_kv_lat_norm[l], mla_w_kv_up[l], mla_q_norm[l], mla_k_norm[l])
        pb, pf = _prep(x2, attn_norm[l][None, :], w, g64, glat, wq, wkv, gmla, tabs["cos_t"], tabs["sin_t"],
                       seq, tm)
        pb3 = pb.reshape(batch, seq, PB_WIDTH)
        pf3 = pf.reshape(batch, seq, PF_WIDTH)
        o_a = _sb_attention(pb3, batch, seq)
        o_b = _banded_attention(pb3, swa_sinks[l], tabs["bias_swa"], batch, seq, SWA_WINDOW,
                                PB_SWAQ, PB_SWAK, PB_SWAV, True)
        rows = _compress_rows(pf3, batch, seq)
        kvc = _compress(rows, nsa_cmp_pos[l].reshape(2, 1, -1),
                        nsa_cmp_w1[l].reshape(2, -1, NSA_CMP_HIDDEN).astype(BF16), nsa_cmp_w2[l].astype(BF16),
                        nsa_k_norm[l][0][None, :], 128)
        o_c, q_aug = _cmp_attention(pb3, kvc, tabs["bias_cmp"], tabs["overlap"], batch, seq)
        o_s = _sel_attention(pb3, q_aug, tabs["penalty"], tabs["bias_sel"], batch, seq)
        o_w = _banded_attention(pb3, zero_sinks, tabs["bias_win"], batch, seq, NSA_WINDOW,
                                PB_NSAQ, PB_NSAKW, PB_NSAVW, False)
        o_d = _mla_attention(pb3, batch, seq)
        flat = lambda o: o.reshape(tokens, GROUP_WIDTH)
        x2 = _outproj(x2, flat(o_a), flat(o_b), flat(o_c), flat(o_s), flat(o_w), flat(o_d), pf, gexp,
                      out_norm[l][None, :], w_out[l].astype(BF16), tm)
        w_router = _pad_to(jnp.concatenate([moe_w_group[l], moe_w_expert[l]], axis=1), (D_MODEL, LANES))
        wr_hi = w_router.astype(BF16)
        wr_lo = (w_router - wr_hi.astype(F32)).astype(BF16)
        b_router = _pad_to(jnp.concatenate([moe_b_group[l], moe_b_expert[l]])[None, :], (1, LANES))
        quads = MOE_EXPERTS // MOE_QUAD
        by_quad = lambda w: w.astype(BF16).reshape(quads, MOE_QUAD, D_MODEL, MOE_HIDDEN).transpose(0, 2, 1, 3
                                                   ).reshape(quads, D_MODEL, MOE_QUAD * MOE_HIDDEN)
        x2 = _moe(x2, ffn_norm[l][None, :], wr_hi, wr_lo, b_router, by_quad(moe_w_gate[l]), by_quad(moe_w_up[l]),
                  moe_w_down[l].astype(BF16).reshape(quads, MOE_QUAD * MOE_HIDDEN, D_MODEL))
    return x2.reshape(batch, seq, D_MODEL)
```

```python
import functools
import math

import numpy as np
import jax
import jax.numpy as jnp
from jax import lax
from jax.experimental import pallas as pl
from jax.experimental.pallas import tpu as pltpu

F32 = jnp.float32
BF16 = jnp.bfloat16

D_MODEL = 1024
HEAD_DIM = 64
NEG = -1e30
EPS = 1e-6
FORCE_BONUS = 1000.0
SWA_WINDOW = 128
NSA_CMP_LEN = 32
NSA_CMP_STRIDE = 16
NSA_CMP_HIDDEN = 128
NSA_SEL_LEN = 64
NSA_TOPK = 16
NSA_WINDOW = 512
MLA_HEADS = 4
MLA_NOPE = 64
MLA_ROPE = 32
MLA_V = 64
MLA_Q_RANK = 192
MLA_KV_RANK = 128
MLA_QK = MLA_NOPE + MLA_ROPE
ROPE_THETA = 10000.0
T5_BUCKETS = 32
T5_MAX_DIST = 1024
MOE_GROUPS = 4
MOE_EPG = 8
MOE_EXPERTS = MOE_GROUPS * MOE_EPG
MOE_HIDDEN = 256
GROUP_WIDTH = 256
LANES = 128
VMEM_LIMIT = 48 * 1024 * 1024

PB_MLAQ, PB_MLAK, PB_MLAV = 0, 512, 1024
PB_SBQ, PB_SBK, PB_SBV = 1536, 1792, 2048
PB_SWAQ, PB_NSAQ, PB_NSAVS = 2304, 2560, 2816
PB_SWAK, PB_SWAV, PB_NSAKW, PB_NSAVW = 3072, 3200, 3328, 3456
PB_NSAKS = 3584
PB_WIDTH = 4096
PF_KC, PF_VC, PF_GATE = 0, 128, 256
PF_WIDTH = 384
W_SBQ, W_SBK, W_SBV, W_SWAQ, W_SWAK, W_SWAV, W_NSAQ = 0, 256, 512, 768, 1024, 1152, 1280
W_KC, W_VC, W_KS, W_VS, W_KW, W_VW, W_GATE = 1536, 1664, 1792, 1920, 2176, 2304, 2432
W_CQ, W_CKV, W_KPE = 2560, 2816, 2944
W_WIDTH = 3456

NT_DIMS = (((1,), (1,)), ((), ()))


def _dot(a, b):
    return jnp.dot(a, b, preferred_element_type=F32)


def _dot_nt(a, b):
    return lax.dot_general(a, b, NT_DIMS, preferred_element_type=F32)


def _split_bf16(x):
    hi = x.astype(BF16)
    lo = (x - hi.astype(F32)).astype(BF16)
    return hi, lo


def _dot_exact_rhs(x, m):
    hi, lo = _split_bf16(x)
    return _dot(hi, m) + _dot(lo, m)


def _block_diag_ones(width, seg):
    idx = np.arange(width) // seg
    return jnp.asarray(idx[:, None] == idx[None, :], BF16)


def _seg_rms(x, seg_ones, count):
    width = x.shape[1]
    ms = _dot_exact_rhs(x * x, seg_ones[:width, :width]) * (1.0 / count)
    return x * lax.rsqrt(ms + EPS)


def _row_rms(x, count):
    return x * lax.rsqrt(jnp.sum(x * x, axis=-1, keepdims=True) * (1.0 / count) + EPS)


def _prep_kernel(x_ref, an_ref, w_ref, g64_ref, glat_ref, wq_ref, wkv_ref, gmla_ref, cos_ref, sin_ref,
                 s64_ref, s128_ref, pb_ref, pf_ref):
    x = x_ref[...]
    s64 = s64_ref[...]
    s128 = s128_ref[...]
    h = _row_rms(x, D_MODEL) * an_ref[...]
    hb = h.astype(BF16)

    def proj(lo, hi):
        return _dot(hb, w_ref[:, lo:hi])

    def put(col, value):
        pb_ref[:, col:col + value.shape[1]] = value.astype(BF16)

    def ones_tail(width):
        lane = lax.broadcasted_iota(jnp.int32, (1, width), 1)
        return jnp.where((lane & (LANES - 1)) >= HEAD_DIM, 1.0, 0.0)

    scale = HEAD_DIM ** -0.5
    g64 = g64_ref[...]
    put(PB_SBQ, proj(W_SBQ, W_SBQ + 256) * scale)
    put(PB_SBK, proj(W_SBK, W_SBK + 256))
    put(PB_SBV, proj(W_SBV, W_SBV + 256))
    put(PB_SWAQ, _seg_rms(proj(W_SWAQ, W_SWAQ + 256), s64, 64) * g64[0:1, :] * scale)
    put(PB_SWAK, _seg_rms(proj(W_SWAK, W_SWAK + 128), s64, 64) * g64[1:2, :128])
    put(PB_SWAV, proj(W_SWAV, W_SWAV + 128))
    put(PB_NSAQ, _seg_rms(proj(W_NSAQ, W_NSAQ + 256), s64, 64) * g64[2:3, :] * scale)
    pf_ref[:, PF_KC:PF_KC + 128] = proj(W_KC, W_KC + 128)
    pf_ref[:, PF_VC:PF_VC + 128] = proj(W_VC, W_VC + 128)
    ks = _seg_rms(proj(W_KS, W_KS + 128), s64, 64) * g64[3:4, :128]
    ks_swapped = pltpu.roll(ks, HEAD_DIM, 1)
    low = lax.broadcasted_iota(jnp.int32, ks.shape, 1) < HEAD_DIM
    put(PB_NSAKS, jnp.where(low, ks, 0.0))
    put(PB_NSAKS + LANES, jnp.where(low, 0.0, ks_swapped))
    put(PB_NSAKS + 2 * LANES, jnp.where(low, ks_swapped, 0.0))
    put(PB_NSAKS + 3 * LANES, jnp.where(low, 0.0, ks))
    put(PB_NSAVS, proj(W_VS, W_VS + 256) + ones_tail(256))
    put(PB_NSAKW, _seg_rms(proj(W_KW, W_KW + 128), s64, 64) * g64[4:5, :128])
    put(PB_NSAVW, proj(W_VW, W_VW + 128))
    pf_ref[:, PF_GATE:PF_GATE + 128] = jax.nn.sigmoid(proj(W_GATE, W_GATE + 128))

    glat = glat_ref[...]
    cq = _row_rms(proj(W_CQ, W_CQ + 256), MLA_Q_RANK) * glat[0:1, :]
    q = _dot(cq.astype(BF16), wq_ref[...])
    ckv = _row_rms(proj(W_CKV, W_CKV + 128), MLA_KV_RANK) * glat[1:2, :128]
    kv = _dot(ckv.astype(BF16), wkv_ref[...])
    k = kv[:, :512] + proj(W_KPE, W_KPE + 512)
    gm = gmla_ref[...]
    q = _seg_rms(q, s128, MLA_QK) * gm[0:1, :]
    k = _seg_rms(k, s128, MLA_QK) * gm[1:2, :]
    cos = cos_ref[...]
    sin = sin_ref[...]
    lane = lax.broadcasted_iota(jnp.int32, (x.shape[0], LANES), 1)
    first_half = lane < MLA_NOPE + MLA_ROPE // 2

    def rope(t):
        partner = jnp.where(first_half, pltpu.roll(t, LANES - MLA_ROPE // 2, 1), pltpu.roll(t, MLA_ROPE // 2, 1))
        return t * cos + partner * sin

    qscale = MLA_QK ** -0.5
    for hd in range(MLA_HEADS):
        sl = slice(hd * LANES, (hd + 1) * LANES)
        put(PB_MLAQ + hd * LANES, rope(q[:, sl]) * qscale)
        put(PB_MLAK + hd * LANES, rope(k[:, sl]))
    put(PB_MLAV, kv[:, 512:1024] + ones_tail(512))


def _prep(x2, an, w, g64, glat, wq, wkv, gmla, cos_t, sin_t, seq, tm):
    tokens = x2.shape[0]
    n_pos = seq // tm
    const = lambda t: (0, 0)
    return pl.pallas_call(
        _prep_kernel,
        out_shape=(jax.ShapeDtypeStruct((tokens, PB_WIDTH), BF16), jax.ShapeDtypeStruct((tokens, PF_WIDTH), F32)),
        grid=(tokens // tm,),
        in_specs=[
            pl.BlockSpec((tm, D_MODEL), lambda t: (t, 0)),
            pl.BlockSpec((1, D_MODEL), const),
            pl.BlockSpec((D_MODEL, W_WIDTH), const),
            pl.BlockSpec((8, 256), const),
            pl.BlockSpec((8, 256), const),
            pl.BlockSpec((256, 512), const),
            pl.BlockSpec((128, 1024), const),
            pl.BlockSpec((8, 512), const),
            pl.BlockSpec((tm, LANES), lambda t: (t % n_pos, 0)),
            pl.BlockSpec((tm, LANES), lambda t: (t % n_pos, 0)),
            pl.BlockSpec((256, 256), const),
            pl.BlockSpec((512, 512), const),
        ],
        out_specs=(pl.BlockSpec((tm, PB_WIDTH), lambda t: (t, 0)), pl.BlockSpec((tm, PF_WIDTH), lambda t: (t, 0))),
        compiler_params=pltpu.CompilerParams(dimension_semantics=("parallel",), vmem_limit_bytes=VMEM_LIMIT),
        name="prep",
    )(x2, an, w, g64, glat, wq, wkv, gmla, cos_t, sin_t, _block_diag_ones(256, HEAD_DIM),
      _block_diag_ones(512, LANES))


SB_TQ, SB_KB, SB_SUB = 256, 2048, 128


SB_UNDERFLOW = 110.0


def _sb_kernel(qi_ref, kj_ref, first_ref, last_ref, kmax_ref, q_ref, k_ref, v_ref, sums_ref, *refs, resume):
    if resume:
        acc_in_ref, carry_in_ref = refs[:2]
        refs = refs[2:]
        o_ref, carry_out_ref, carry_ref, acc_ref, zb_ref, dead_ref = refs
        slack_ref = None
    else:
        o_ref, carry_out_ref, slack_ref, carry_ref, acc_ref, zb_ref, dead_ref = refs
    b = pl.program_id(0)
    p = pl.program_id(1)
    qi = qi_ref[p]
    kj = kj_ref[p]
    q_start = qi * SB_TQ

    @pl.when(first_ref[p] == 1)
    def _():
        if resume:
            for hd in range(4):
                carry_ref[hd * SB_TQ:(hd + 1) * SB_TQ, :] = carry_in_ref[:, hd * SB_SUB:(hd + 1) * SB_SUB]
                acc_ref[hd] = acc_in_ref[:, hd * HEAD_DIM:(hd + 1) * HEAD_DIM]
        else:
            carry_ref[...] = jnp.zeros_like(carry_ref)
            acc_ref[...] = jnp.zeros_like(acc_ref)
        dead_ref[0] = 0
        ones = jnp.ones((HEAD_DIM, SB_SUB), BF16)
        for hd in range(4):
            q = q_ref[:, hd * HEAD_DIM:(hd + 1) * HEAD_DIM].astype(F32)
            zb_ref[hd * SB_TQ:(hd + 1) * SB_TQ, :] = (jnp.sqrt(_dot_exact_rhs(q * q, ones))
                                                      * (kmax_ref[b * 4 + hd] * 1.01))

    rel = (lax.broadcasted_iota(jnp.int32, (SB_TQ, SB_SUB), 1)
           - lax.broadcasted_iota(jnp.int32, (SB_TQ, SB_SUB), 0))

    def sub_tile(u, k_start):
        mask = jnp.tile(rel < q_start - k_start, (4, 1))
        rows = slice(u * SB_SUB, (u + 1) * SB_SUB)
        head = lambda hd: slice(hd * HEAD_DIM, (hd + 1) * HEAD_DIM)
        z = jnp.concatenate([_dot_nt(q_ref[:, head(hd)], k_ref[rows, head(hd)]) for hd in range(4)], axis=0)
        log_keep = jnp.where(mask, -(jnp.maximum(z, 0.0) + jnp.log(1.0 + jnp.exp(-jnp.abs(z)))), 0.0)
        hi, lo = _split_bf16(log_keep)
        sums = _dot(jnp.concatenate([hi, lo], axis=1), sums_ref[...])
        carry = carry_ref[...]
        a = jnp.where(mask, jnp.exp(z + sums[:, :SB_SUB] + carry), 0.0).astype(BF16)
        for hd in range(4):
            acc_ref[hd] += _dot(a[hd * SB_TQ:(hd + 1) * SB_TQ], v_ref[rows, head(hd)])
        carry_ref[...] = carry + sums[:, SB_SUB:]

    for u in reversed(range(SB_KB // SB_SUB)):
        k_start = kj * SB_KB + u * SB_SUB

        @pl.when((k_start < q_start + SB_TQ) & (dead_ref[0] == 0))
        def _(u=u, k_start=k_start):
            live = jnp.max(carry_ref[...] + zb_ref[...]) > -SB_UNDERFLOW

            @pl.when(live)
            def _():
                sub_tile(u, k_start)

            @pl.when(jnp.logical_not(live))
            def _():
                dead_ref[0] = 1

    @pl.when(last_ref[p] == 1)
    def _():
        for hd in range(4):
            o_ref[:, hd * HEAD_DIM:(hd + 1) * HEAD_DIM] = acc_ref[hd]
            carry_out_ref[:, hd * SB_SUB:(hd + 1) * SB_SUB] = carry_ref[hd * SB_TQ:(hd + 1) * SB_TQ, :]
        if slack_ref is not None:
            slack_ref[...] = jnp.full(slack_ref.shape, jnp.max(carry_ref[...] + zb_ref[...]), F32)


def _sb_call(pairs, pb, kmax, sums, batch, seq, state):
    qi = [i for i, blocks in pairs for _ in blocks]
    kj = [j for _, blocks in pairs for j in blocks]
    first = [int(n == 0) for _, blocks in pairs for n in range(len(blocks))]
    last = [int(n == len(blocks) - 1) for _, blocks in pairs for n in range(len(blocks))]
    prefetch = [jnp.asarray(np.array(a, np.int32)) for a in (qi, kj, first, last)]
    tile = lambda width: pl.BlockSpec((None, SB_TQ, width), lambda b, p, qi, kj, fi, la: (b, qi[p], 0))
    in_specs = [
        pl.BlockSpec(memory_space=pltpu.SMEM),
        pl.BlockSpec((None, SB_TQ, 256), lambda b, p, qi, kj, fi, la: (b, qi[p], PB_SBQ // 256)),
        pl.BlockSpec((None, SB_KB, 256), lambda b, p, qi, kj, fi, la: (b, kj[p], PB_SBK // 256)),
        pl.BlockSpec((None, SB_KB, 256), lambda b, p, qi, kj, fi, la: (b, kj[p], PB_SBV // 256)),
        pl.BlockSpec((2 * SB_SUB, 2 * SB_SUB), lambda b, p, qi, kj, fi, la: (0, 0)),
    ]
    resume = state is not None
    out_specs = [tile(256), tile(4 * SB_SUB)]
    out_shape = [jax.ShapeDtypeStruct((batch, seq, 256), F32), jax.ShapeDtypeStruct((batch, seq, 4 * SB_SUB), F32)]
    if resume:
        in_specs += [tile(256), tile(4 * SB_SUB)]
    else:
        out_specs.append(pl.BlockSpec((None, None, 8, LANES), lambda b, p, qi, kj, fi, la: (b, qi[p], 0, 0)))
        out_shape.append(jax.ShapeDtypeStruct((batch, seq // SB_TQ, 8, LANES), F32))
    grid_spec = pltpu.PrefetchScalarGridSpec(
        num_scalar_prefetch=4,
        grid=(batch, len(qi)),
        in_specs=in_specs,
        out_specs=tuple(out_specs),
        scratch_shapes=[pltpu.VMEM((4 * SB_TQ, SB_SUB), F32), pltpu.VMEM((4, SB_TQ, HEAD_DIM), F32),
                        pltpu.VMEM((4 * SB_TQ, SB_SUB), F32), pltpu.SMEM((1,), jnp.int32)],
    )
    return pl.pallas_call(
        functools.partial(_sb_kernel, resume=resume),
        out_shape=tuple(out_shape),
        grid_spec=grid_spec,
        input_output_aliases={len(prefetch) + 5: 0, len(prefetch) + 6: 1} if resume else {},
        compiler_params=pltpu.CompilerParams(dimension_semantics=("parallel", "arbitrary"),
                                             vmem_limit_bytes=VMEM_LIMIT),
        name="stick_breaking_far" if resume else "stick_breaking",
    )(*prefetch, kmax, pb, pb, pb, sums, *(state or ()))


def _sb_attention(pb, batch, seq):
    keys = pb[:, :, PB_SBK:PB_SBK + 256].astype(F32).reshape(batch, seq, 4, HEAD_DIM)
    kmax = jnp.sqrt(jnp.max(jnp.sum(keys * keys, axis=-1), axis=1)).reshape(batch * 4)
    j = np.arange(2 * SB_SUB)[:, None] % SB_SUB
    s = np.arange(2 * SB_SUB)[None, :]
    sums = jnp.asarray((s >= SB_SUB) | (j >= s), BF16)
    near, far = [], []
    for i in range(seq // SB_TQ):
        blocks = list(reversed(range((i * SB_TQ + SB_TQ - 1) // SB_KB + 1)))
        n_near = 2 if (i * SB_TQ) % SB_KB < 2 * SB_TQ else 1
        near.append((i, blocks[:n_near]))
        if blocks[n_near:]:
            far.append((i, blocks[n_near:]))
    values, carry, slack = _sb_call(near, pb, kmax, sums, batch, seq, None)
    if not far:
        return values
    has_far = jnp.asarray(np.array([i for i, _ in far], np.int32))
    return lax.cond(jnp.max(slack[:, has_far]) > -SB_UNDERFLOW,
                    lambda: _sb_call(far, pb, kmax, sums, batch, seq, (values, carry))[0],
                    lambda: values)


def _banded_kernel(sink_ref, q_ref, kp_ref, kc_ref, vp_ref, vc_ref, bias_ref, o_ref, *, tq, pad, window, use_sink):
    i = pl.program_id(1)
    dist_prev = (lax.broadcasted_iota(jnp.int32, (tq, pad), 0) + pad
                 - lax.broadcasted_iota(jnp.int32, (tq, pad), 1))
    dist_cur = lax.broadcasted_iota(jnp.int32, (tq, tq), 0) - lax.broadcasted_iota(jnp.int32, (tq, tq), 1)
    mask_prev = (dist_prev < window) & (i > 0)
    mask_cur = (dist_cur >= 0) & (dist_cur < window)
    def scores(hd):
        kcols = slice((hd // 2) * HEAD_DIM, (hd // 2 + 1) * HEAD_DIM)
        q = q_ref[:, hd * HEAD_DIM:(hd + 1) * HEAD_DIM]
        return (jnp.where(mask_prev, _dot_nt(q, kp_ref[:, kcols]) + bias_ref[hd, :, :pad], NEG),
                jnp.where(mask_cur, _dot_nt(q, kc_ref[:, kcols]) + bias_ref[hd, :, pad:], NEG))

    ahead = 2
    pending = [scores(hd) for hd in range(ahead)]
    for hd in range(4):
        if hd + ahead < 4:
            pending.append(scores(hd + ahead))
        s_prev, s_cur = pending.pop(0)
        cols = slice(hd * HEAD_DIM, (hd + 1) * HEAD_DIM)
        kcols = slice((hd // 2) * HEAD_DIM, (hd // 2 + 1) * HEAD_DIM)
        m = jnp.maximum(jnp.max(s_prev, axis=-1, keepdims=True), jnp.max(s_cur, axis=-1, keepdims=True))
        if use_sink:
            sink = sink_ref[hd]
            m = jnp.maximum(m, sink)
        p_prev = jnp.where(mask_prev, jnp.exp(s_prev - m), 0.0)
        p_cur = jnp.where(mask_cur, jnp.exp(s_cur - m), 0.0)
        denom = jnp.sum(p_prev, axis=-1, keepdims=True) + jnp.sum(p_cur, axis=-1, keepdims=True)
        if use_sink:
            denom = denom + jnp.exp(sink - m)
        o = _dot(p_prev.astype(BF16), vp_ref[:, kcols]) + _dot(p_cur.astype(BF16), vc_ref[:, kcols])
        o_ref[:, cols] = o / jnp.maximum(denom, 1e-30)


BAND_TQ = 256


def _band_tiles(window):
    pad = -(-(window - 1) // LANES) * LANES
    return pad, max(pad, BAND_TQ)


def _banded_attention(pb, sinks, bias, batch, seq, window, q_col, k_col, v_col, use_sink):
    pad, tq = _band_tiles(window)
    per = tq // pad
    prev = lambda i: jnp.maximum(i * per - 1, 0)
    grid_spec = pltpu.PrefetchScalarGridSpec(
        num_scalar_prefetch=1,
        grid=(batch, seq // tq),
        in_specs=[
            pl.BlockSpec((None, tq, 256), lambda b, i, s: (b, i, q_col // 256)),
            pl.BlockSpec((None, pad, 128), lambda b, i, s: (b, prev(i), k_col // 128)),
            pl.BlockSpec((None, tq, 128), lambda b, i, s: (b, i, k_col // 128)),
            pl.BlockSpec((None, pad, 128), lambda b, i, s: (b, prev(i), v_col // 128)),
            pl.BlockSpec((None, tq, 128), lambda b, i, s: (b, i, v_col // 128)),
            pl.BlockSpec((4, tq, pad + tq), lambda b, i, s: (0, 0, 0)),
        ],
        out_specs=pl.BlockSpec((None, tq, 256), lambda b, i, s: (b, i, 0)),
    )
    return pl.pallas_call(
        functools.partial(_banded_kernel, tq=tq, pad=pad, window=window, use_sink=use_sink),
        out_shape=jax.ShapeDtypeStruct((batch, seq, 256), F32),
        grid_spec=grid_spec,
        compiler_params=pltpu.CompilerParams(dimension_semantics=("parallel", "arbitrary"),
                                             vmem_limit_bytes=VMEM_LIMIT),
        name="banded_w%d" % window,
    )(sinks, pb, pb, pb, pb, pb, bias)


def _compress_kernel(rows_ref, pos_ref, w1_ref, w2_ref, g_ref, o_ref):
    win = rows_ref[...] + pos_ref[...]
    hid = jax.nn.gelu(_dot(win.astype(BF16), w1_ref[...]), approximate=True)
    out = _dot(hid.astype(BF16), w2_ref[...])
    normed = _row_rms(out, HEAD_DIM) * g_ref[...]
    o_ref[...] = jnp.where(pl.program_id(0) == 0, normed, out).astype(BF16)


def _compress(rows, pos, w1, w2, gain, tn):
    _, bh, ncp, width = rows.shape
    return pl.pallas_call(
        _compress_kernel,
        out_shape=jax.ShapeDtypeStruct((2, bh, ncp, HEAD_DIM), BF16),
        grid=(2, bh, ncp // tn),
        in_specs=[
            pl.BlockSpec((None, None, tn, width), lambda c, r, n: (c, r, n, 0)),
            pl.BlockSpec((None, 1, width), lambda c, r, n: (c, 0, 0)),
            pl.BlockSpec((None, width, NSA_CMP_HIDDEN), lambda c, r, n: (c, 0, 0)),
            pl.BlockSpec((None, NSA_CMP_HIDDEN, HEAD_DIM), lambda c, r, n: (c, 0, 0)),
            pl.BlockSpec((1, HEAD_DIM), lambda c, r, n: (0, 0)),
        ],
        out_specs=pl.BlockSpec((None, None, tn, HEAD_DIM), lambda c, r, n: (c, r, n, 0)),
        compiler_params=pltpu.CompilerParams(dimension_semantics=("parallel", "parallel", "parallel"),
                                             vmem_limit_bytes=VMEM_LIMIT),
        name="nsa_compress",
    )(rows, pos, w1, w2, gain)


CMP_TQ = 1024


def _cmp_kernel(q_ref, kc_ref, vc_ref, bias_ref, ov_ref, o_ref, qa_ref, *, ncp, n_sel_pad, topk):
    i = pl.program_id(2)
    q_pos = i * CMP_TQ + lax.broadcasted_iota(jnp.int32, (CMP_TQ, ncp), 0)
    cmp_end = lax.broadcasted_iota(jnp.int32, (CMP_TQ, ncp), 1) * NSA_CMP_STRIDE + (NSA_CMP_LEN - 1)
    mask = cmp_end <= q_pos
    kc = kc_ref[...]
    vc = vc_ref[...]
    p_sum = jnp.zeros((CMP_TQ, ncp), F32)
    for g in range(2):
        cols = slice(g * HEAD_DIM, (g + 1) * HEAD_DIM)
        s = jnp.where(mask, _dot_nt(q_ref[:, cols], kc) + bias_ref[g], NEG)
        m = jnp.max(s, axis=-1, keepdims=True)
        p = jnp.where(mask, jnp.exp(s - m), 0.0)
        p = p / jnp.maximum(jnp.sum(p, axis=-1, keepdims=True), 1e-30)
        o_ref[:, cols] = _dot(p.astype(BF16), vc)
        p_sum = p_sum + p
    imp = _dot_exact_rhs(p_sum, ov_ref[...])

    row_pos = i * CMP_TQ + lax.broadcasted_iota(jnp.int32, (CMP_TQ, n_sel_pad), 0)
    blk = lax.broadcasted_iota(jnp.int32, (CMP_TQ, n_sel_pad), 1)
    cur = row_pos >> int(math.log2(NSA_SEL_LEN))
    forced = (blk == 0) | (blk == cur) | (blk == cur - 1)
    valid = blk * NSA_SEL_LEN <= row_pos
    score = jnp.where(valid, imp + jnp.where(forced, FORCE_BONUS, 0.0), NEG)
    blk_f = blk.astype(F32)
    dropped = jnp.ones((CMP_TQ, n_sel_pad), F32)
    for _ in range(topk):
        best = jnp.max(score, axis=-1, keepdims=True)
        first = jnp.min(jnp.where(score == best, blk_f, float(n_sel_pad)), axis=-1, keepdims=True)
        hit = blk_f == first
        dropped = jnp.where(hit, 0.0, dropped)
        score = jnp.where(hit, -jnp.inf, score)
    dropped = jnp.where(valid, dropped, 1.0).astype(BF16)
    q = q_ref[...]
    low = lax.broadcasted_iota(jnp.int32, q.shape, 1) < HEAD_DIM
    zero = jnp.zeros_like(q)
    width = LANES + n_sel_pad
    qa_ref[:, 0:LANES] = jnp.where(low, q, zero)
    qa_ref[:, LANES:width] = dropped
    qa_ref[:, width:width + LANES] = jnp.where(low, zero, q)
    qa_ref[:, width + LANES:2 * width] = dropped


def _cmp_attention(pb, kvc, bias_c, overlap, batch, seq):
    ncp = kvc.shape[2]
    n_sel_pad = overlap.shape[1]
    topk = min(NSA_TOPK, seq // NSA_SEL_LEN)
    pair = 2 * (LANES + n_sel_pad)
    return pl.pallas_call(
        functools.partial(_cmp_kernel, ncp=ncp, n_sel_pad=n_sel_pad, topk=topk),
        out_shape=(jax.ShapeDtypeStruct((batch, seq, 256), F32),
                   jax.ShapeDtypeStruct((batch, seq, 2 * pair), BF16)),
        grid=(batch, 2, seq // CMP_TQ),
        in_specs=[
            pl.BlockSpec((None, CMP_TQ, 128), lambda b, h, i: (b, i, PB_NSAQ // 128 + h)),
            pl.BlockSpec((None, None, ncp, HEAD_DIM), lambda b, h, i: (0, b * 2 + h, 0, 0)),
            pl.BlockSpec((None, None, ncp, HEAD_DIM), lambda b, h, i: (1, b * 2 + h, 0, 0)),
            pl.BlockSpec((2, CMP_TQ, ncp), lambda b, h, i: (h, i, 0)),
            pl.BlockSpec((ncp, n_sel_pad), lambda b, h, i: (0, 0)),
        ],
        out_specs=(pl.BlockSpec((None, CMP_TQ, 128), lambda b, h, i: (b, i, h)),
                   pl.BlockSpec((None, CMP_TQ, pair), lambda b, h, i: (b, i, h))),
        compiler_params=pltpu.CompilerParams(dimension_semantics=("parallel", "parallel", "arbitrary"),
                                             vmem_limit_bytes=VMEM_LIMIT),
        name="nsa_cmp_select",
    )(pb, kvc, kvc, bias_c, overlap)


SEL_T = 512


def _flash_init(m_ref, acc_ref):
    m_ref[...] = jnp.full_like(m_ref, NEG)
    acc_ref[...] = jnp.zeros_like(acc_ref)


def _flash_update(s, v_ones, m_ref, acc_ref, hd):
    m_old = m_ref[hd]
    m_new = jnp.maximum(m_old, jnp.max(s, axis=-1, keepdims=True))
    alpha = jnp.exp(m_old - m_new)
    pr = jnp.exp(s - jnp.tile(m_new, (1, s.shape[1] // LANES)))
    acc_ref[hd] = alpha * acc_ref[hd] + _dot(pr.astype(BF16), v_ones)
    m_ref[hd] = m_new


def _flash_finish(acc_ref, o_ref, heads):
    for hd in range(heads):
        acc = acc_ref[hd]
        row_sum = pltpu.roll(acc, HEAD_DIM, 1)
        o_ref[:, hd * HEAD_DIM:(hd + 1) * HEAD_DIM] = (acc / jnp.maximum(row_sum, 1e-30))[:, :HEAD_DIM]


def _sel_kernel(qi_ref, kj_ref, q_ref, k_ref, v_ref, pen_ref, bias_ref, o_ref, m_ref, acc_ref):
    p = pl.program_id(1)
    qi = qi_ref[p]
    kj = kj_ref[p]
    width = q_ref.shape[1] // 4

    @pl.when(kj == 0)
    def _():
        _flash_init(m_ref, acc_ref)

    def step(diagonal):
        pen = pen_ref[...]

        def scores(hd):
            k_pen = jnp.concatenate([k_ref[:, hd * LANES:(hd + 1) * LANES], pen], axis=1)
            s = _dot_nt(q_ref[:, hd * width:(hd + 1) * width], k_pen) + bias_ref[hd]
            if diagonal:
                row = lax.broadcasted_iota(jnp.int32, (SEL_T, SEL_T), 0)
                col = lax.broadcasted_iota(jnp.int32, (SEL_T, SEL_T), 1)
                s = jnp.where(col <= row, s, NEG)
            return s

        ahead = 3
        pending = [scores(hd) for hd in range(ahead)]
        for hd in range(4):
            if hd + ahead < 4:
                pending.append(scores(hd + ahead))
            _flash_update(pending.pop(0), v_ref[:, (hd // 2) * LANES:(hd // 2 + 1) * LANES], m_ref, acc_ref, hd)

    @pl.when(kj < qi)
    def _():
        step(False)

    @pl.when(kj == qi)
    def _():
        step(True)
        _flash_finish(acc_ref, o_ref, 4)


def _causal_pairs(n):
    qi, kj = [], []
    for i in range(n):
        for j in range(i + 1):
            qi.append(i)
            kj.append(j)
    return jnp.asarray(np.array(qi, np.int32)), jnp.asarray(np.array(kj, np.int32))


def _sel_attention(pb, q_aug, penalty, bias_s, batch, seq):
    n_sel_pad = penalty.shape[1]
    n_delta = bias_s.shape[1]
    qi, kj = _causal_pairs(seq // SEL_T)
    grid_spec = pltpu.PrefetchScalarGridSpec(
        num_scalar_prefetch=2,
        grid=(batch, int(qi.shape[0])),
        in_specs=[
            pl.BlockSpec((None, SEL_T, q_aug.shape[2]), lambda b, p, qi, kj: (b, qi[p], 0)),
            pl.BlockSpec((None, SEL_T, 512), lambda b, p, qi, kj: (b, kj[p], PB_NSAKS // 512)),
            pl.BlockSpec((None, SEL_T, 256), lambda b, p, qi, kj: (b, kj[p], PB_NSAVS // 256)),
            pl.BlockSpec((SEL_T, n_sel_pad), lambda b, p, qi, kj: (kj[p], 0)),
            pl.BlockSpec((4, None, SEL_T, SEL_T),
                         lambda b, p, qi, kj: (0, jnp.minimum(qi[p] - kj[p], n_delta - 1), 0, 0)),
        ],
        out_specs=pl.BlockSpec((None, SEL_T, 256), lambda b, p, qi, kj: (b, qi[p], 0)),
        scratch_shapes=[pltpu.VMEM((4, SEL_T, LANES), F32), pltpu.VMEM((4, SEL_T, LANES), F32)],
    )
    return pl.pallas_call(
        _sel_kernel,
        out_shape=jax.ShapeDtypeStruct((batch, seq, 256), F32),
        grid_spec=grid_spec,
        compiler_params=pltpu.CompilerParams(dimension_semantics=("parallel", "arbitrary"),
                                             vmem_limit_bytes=VMEM_LIMIT),
        name="nsa_selected",
    )(qi, kj, q_aug, pb, pb, penalty, bias_s)


MLA_T = 1024


def _mla_kernel(qi_ref, kj_ref, q_ref, k_ref, v_ref, o_ref, m_ref, acc_ref):
    p = pl.program_id(1)
    qi = qi_ref[p]
    kj = kj_ref[p]

    @pl.when(kj == 0)
    def _():
        _flash_init(m_ref, acc_ref)

    def step(diagonal):
        def scores(hd):
            cols = slice(hd * LANES, (hd + 1) * LANES)
            s = _dot_nt(q_ref[:, cols], k_ref[:, cols])
            if diagonal:
                row = lax.broadcasted_iota(jnp.int32, (MLA_T, MLA_T), 0)
                col = lax.broadcasted_iota(jnp.int32, (MLA_T, MLA_T), 1)
                s = jnp.where(col <= row, s, NEG)
            return s

        ahead = 3
        pending = [scores(hd) for hd in range(ahead)]
        for hd in range(MLA_HEADS):
            if hd + ahead < MLA_HEADS:
                pending.append(scores(hd + ahead))
            _flash_update(pending.pop(0), v_ref[:, hd * LANES:(hd + 1) * LANES], m_ref, acc_ref, hd)

    @pl.when(kj < qi)
    def _():
        step(False)

    @pl.when(kj == qi)
    def _():
        step(True)
        _flash_finish(acc_ref, o_ref, MLA_HEADS)


def _mla_attention(pb, batch, seq):
    qi, kj = _causal_pairs(seq // MLA_T)
    grid_spec = pltpu.PrefetchScalarGridSpec(
        num_scalar_prefetch=2,
        grid=(batch, int(qi.shape[0])),
        in_specs=[
            pl.BlockSpec((None, MLA_T, 512), lambda b, p, qi, kj: (b, qi[p], PB_MLAQ // 512)),
            pl.BlockSpec((None, MLA_T, 512), lambda b, p, qi, kj: (b, kj[p], PB_MLAK // 512)),
            pl.BlockSpec((None, MLA_T, 512), lambda b, p, qi, kj: (b, kj[p], PB_MLAV // 512)),
        ],
        out_specs=pl.BlockSpec((None, MLA_T, 256), lambda b, p, qi, kj: (b, qi[p], 0)),
        scratch_shapes=[pltpu.VMEM((4, MLA_T, LANES), F32), pltpu.VMEM((4, MLA_T, LANES), F32)],
    )
    return pl.pallas_call(
        _mla_kernel,
        out_shape=jax.ShapeDtypeStruct((batch, seq, 256), F32),
        grid_spec=grid_spec,
        compiler_params=pltpu.CompilerParams(dimension_semantics=("parallel", "arbitrary"),
                                             vmem_limit_bytes=VMEM_LIMIT),
        name="mla_causal",
    )(qi, kj, pb, pb, pb)


def _outproj_kernel(x_ref, oa_ref, ob_ref, oc_ref, os_ref, ow_ref, od_ref, gate_ref, gexp_ref, gn_ref, w_ref,
                    o_ref):
    gates = gate_ref[...]
    g_hi, g_lo = _split_bf16(gates)

    def gate(branch):
        e = gexp_ref[branch]
        return _dot(g_hi, e) + _dot(g_lo, e)

    o_nsa = gate(0) * oc_ref[...] + gate(1) * os_ref[...] + gate(2) * ow_ref[...]
    gn = gn_ref[...]
    y = x_ref[...]
    for grp, o in enumerate((oa_ref[...], ob_ref[...], o_nsa, od_ref[...])):
        cols = slice(grp * GROUP_WIDTH, (grp + 1) * GROUP_WIDTH)
        normed = _row_rms(o, GROUP_WIDTH) * gn[:, cols]
        y = y + _dot(normed.astype(BF16), w_ref[cols, :])
    o_ref[...] = y


def _outproj(x2, oa, ob, oc, osel, ow, od, pf, gexp, gn, w, tm):
    tokens = x2.shape[0]
    row = lambda t: (t, 0)
    o_spec = pl.BlockSpec((tm, GROUP_WIDTH), row)
    return pl.pallas_call(
        _outproj_kernel,
        out_shape=jax.ShapeDtypeStruct((tokens, D_MODEL), F32),
        grid=(tokens // tm,),
        in_specs=[
            pl.BlockSpec((tm, D_MODEL), row), o_spec, o_spec, o_spec, o_spec, o_spec, o_spec,
            pl.BlockSpec((tm, 128), lambda t: (t, PF_GATE // 128)),
            pl.BlockSpec((3, 128, GROUP_WIDTH), lambda t: (0, 0, 0)),
            pl.BlockSpec((1, D_MODEL), lambda t: (0, 0)),
            pl.BlockSpec((D_MODEL, D_MODEL), lambda t: (0, 0)),
        ],
        out_specs=pl.BlockSpec((tm, D_MODEL), row),
        compiler_params=pltpu.CompilerParams(dimension_semantics=("parallel",), vmem_limit_bytes=VMEM_LIMIT),
        name="out_proj",
    )(x2, oa, ob, oc, osel, ow, od, pf, gexp, gn, w)


MOE_TM = 1024
MOE_QUAD = 4
MOE_CHUNK = 128
MOE_CHUNK_ALIGN = 128
MOE_SLOTS = MOE_TM + MOE_GROUPS * MOE_CHUNK
MOE_VMEM_LIMIT = 60 * 1024 * 1024


def _moe_kernel(x_ref, fn_ref, wr_hi_ref, wr_lo_ref, br_ref, cexp_ref, wg_ref, wu_ref, wd_ref, o_ref, hs_ref,
                cs_ref, ys_ref, slot_ref, start_ref):
    step = pl.program_id(1)
    lane = lax.broadcasted_iota(jnp.int32, (MOE_TM, LANES), 1)

    @pl.when(step == 0)
    def _():
        h = _row_rms(x_ref[...], D_MODEL) * fn_ref[...]
        h_hi, h_lo = _split_bf16(h)
        logits = (_dot(h_hi, wr_hi_ref[...]) + _dot(h_lo, wr_hi_ref[...]) + _dot(h_hi, wr_lo_ref[...])
                  + br_ref[...])
        lane_f = lane.astype(F32)
        no_lane = float(LANES)
        is_group = lane < MOE_GROUPS
        g_max = jnp.max(jnp.where(is_group, logits, -jnp.inf), axis=-1, keepdims=True)
        g_star = jnp.min(jnp.where(is_group & (logits == g_max), lane_f, no_lane), axis=-1, keepdims=True)
        g_den = jnp.sum(jnp.where(is_group, jnp.exp(logits - g_max), 0.0), axis=-1, keepdims=True)
        g_w = 1.0 / g_den
        group_of_lane = ((lane - MOE_GROUPS) >> int(math.log2(MOE_EPG))).astype(F32)
        in_group = (lane >= MOE_GROUPS) & (lane < MOE_GROUPS + MOE_EXPERTS) & (group_of_lane == g_star)
        e_l = jnp.where(in_group, logits, -jnp.inf)
        top1 = jnp.max(e_l, axis=-1, keepdims=True)
        i1 = jnp.min(jnp.where(e_l == top1, lane_f, no_lane), axis=-1, keepdims=True)
        e_l2 = jnp.where(lane_f == i1, -jnp.inf, e_l)
        top2 = jnp.max(e_l2, axis=-1, keepdims=True)
        i2 = jnp.min(jnp.where(e_l2 == top2, lane_f, no_lane), axis=-1, keepdims=True)
        r = jnp.exp(top2 - top1)
        w1 = g_w / (1.0 + r)
        w2 = g_w * r / (1.0 + r)
        comb = jnp.where(lane_f == i1, w1, jnp.where(lane_f == i2, w2, 0.0))

        onehot = jnp.where(lane_f == g_star, 1.0, 0.0)
        row = lax.broadcasted_iota(jnp.int32, (MOE_TM, LANES), 0)
        incl = onehot
        shift = 1
        while shift < MOE_TM:
            incl = incl + jnp.where(row >= shift, pltpu.roll(incl, shift, 0), 0.0)
            shift *= 2
        counts = incl[MOE_TM - 1:MOE_TM, :]
        padded = jnp.floor((counts + (MOE_CHUNK - 0.5)) * (1.0 / MOE_CHUNK)) * MOE_CHUNK
        before = (lax.broadcasted_iota(jnp.int32, (LANES, LANES), 0)
                  < lax.broadcasted_iota(jnp.int32, (LANES, LANES), 1))
        starts = _dot(jnp.broadcast_to(padded, (8, LANES)).astype(BF16),
                      jnp.where(before, 1.0, 0.0).astype(BF16))[0:1, :]
        for g in range(MOE_GROUPS + 1):
            start_ref[g] = jnp.sum(jnp.where(lane[0:1, :] == g, starts, 0.0)).astype(jnp.int32)
        slot = jnp.sum(onehot * (starts + incl - onehot), axis=-1, keepdims=True)
        slot_ref[...] = jnp.broadcast_to(slot, (MOE_TM, LANES))
        slot_row = slot_ref[...].T[0:1, :]
        n_slots = hs_ref.shape[0]
        place = jnp.where(lax.broadcasted_iota(jnp.int32, (n_slots, MOE_TM), 0).astype(F32) == slot_row,
                          1.0, 0.0).astype(BF16)
        hs_ref[...] = _dot(place, h.astype(BF16)).astype(BF16)
        c_hi, c_lo = _split_bf16(comb)
        cs_ref[:, :LANES] = _dot(place, c_hi).astype(BF16)
        cs_ref[:, LANES:] = _dot(place, c_lo).astype(BF16)
        ys_ref[...] = jnp.zeros_like(ys_ref)

    group = step // (MOE_EPG // MOE_QUAD)
    first_slot = start_ref[group]
    half = MOE_QUAD * MOE_HIDDEN // 2

    def chunk(c, carry):
        rows = pl.ds(pl.multiple_of(first_slot + c * MOE_CHUNK, MOE_CHUNK_ALIGN), MOE_CHUNK)
        hb = hs_ref[rows, :]
        weight = _dot(cs_ref[rows, :], cexp_ref[...])
        y = None
        for s in range(2):
            cols = slice(s * half, (s + 1) * half)
            a = jax.nn.silu(_dot(hb, wg_ref[:, cols])) * _dot(hb, wu_ref[:, cols]) * weight[:, cols]
            part = _dot(a.astype(BF16), wd_ref[cols, :])
            y = part if y is None else y + part
        ys_ref[rows, :] += y
        return carry

    lax.fori_loop(0, (start_ref[group + 1] - first_slot) // MOE_CHUNK, chunk, 0)

    @pl.when(step == MOE_EXPERTS // MOE_QUAD - 1)
    def _():
        n_slots = hs_ref.shape[0]
        back = jnp.where(lax.broadcasted_iota(jnp.int32, (MOE_TM, n_slots), 1).astype(F32) == slot_ref[:, 0:1],
                         1.0, 0.0).astype(BF16)
        y_hi, y_lo = _split_bf16(ys_ref[...])
        o_ref[...] = x_ref[...] + _dot(back, y_hi) + _dot(back, y_lo)


def _moe(x2, fn, wr_hi, wr_lo, br, wg, wu, wd):
    tokens = x2.shape[0]
    quads = MOE_EXPERTS // MOE_QUAD
    width = MOE_QUAD * MOE_HIDDEN
    const = lambda t, e: (0, 0)
    lane_of_col = MOE_GROUPS + np.arange(quads)[:, None, None] * MOE_QUAD + np.arange(width)[None, None, :] // MOE_HIDDEN
    cexp = jnp.asarray((np.arange(2 * LANES)[None, :, None] % LANES) == lane_of_col, BF16)
    return pl.pallas_call(
        _moe_kernel,
        out_shape=jax.ShapeDtypeStruct((tokens, D_MODEL), F32),
        grid=(tokens // MOE_TM, quads),
        in_specs=[
            pl.BlockSpec((MOE_TM, D_MODEL), lambda t, e: (t, 0)),
            pl.BlockSpec((1, D_MODEL), const),
            pl.BlockSpec((D_MODEL, LANES), const),
            pl.BlockSpec((D_MODEL, LANES), const),
            pl.BlockSpec((1, LANES), const),
            pl.BlockSpec((None, 2 * LANES, width), lambda t, e: (e, 0, 0)),
            pl.BlockSpec((None, D_MODEL, width), lambda t, e: (e, 0, 0)),
            pl.BlockSpec((None, D_MODEL, width), lambda t, e: (e, 0, 0)),
            pl.BlockSpec((None, width, D_MODEL), lambda t, e: (e, 0, 0)),
        ],
        out_specs=pl.BlockSpec((MOE_TM, D_MODEL), lambda t, e: (t, 0)),
        scratch_shapes=[pltpu.VMEM((MOE_SLOTS, D_MODEL), BF16), pltpu.VMEM((MOE_SLOTS, 2 * LANES), BF16),
                        pltpu.VMEM((MOE_SLOTS, D_MODEL), F32), pltpu.VMEM((MOE_TM, LANES), F32),
                        pltpu.SMEM((8,), jnp.int32)],
        compiler_params=pltpu.CompilerParams(dimension_semantics=("parallel", "arbitrary"),
                                             vmem_limit_bytes=MOE_VMEM_LIMIT),
        name="hier_moe",
    )(x2, fn, wr_hi, wr_lo, br, cexp, wg, wu, wd)


def _t5_bucket(dist):
    n = jnp.maximum(dist, 0)
    max_exact = T5_BUCKETS // 2
    nf = jnp.maximum(n, 1).astype(F32)
    large = max_exact + (jnp.log(nf / max_exact) / math.log(T5_MAX_DIST / max_exact)
                         * (T5_BUCKETS - max_exact)).astype(jnp.int32)
    large = jnp.minimum(large, T5_BUCKETS - 1)
    return jnp.where(n < max_exact, n, large)


def _sel_delta_cap():
    max_exact = T5_BUCKETS // 2
    span = T5_BUCKETS - max_exact
    last_bucket_from = max_exact * (T5_MAX_DIST / max_exact) ** ((span - 1) / span)
    cap = 1
    while (cap - 1) * SEL_T + 1 < 1.25 * last_bucket_from:
        cap += 1
    return cap


def _position_tables(rel_bias, seq):
    buckets = _t5_bucket(jnp.arange(seq))
    first = jnp.sum(buckets[None, :] < jnp.arange(T5_BUCKETS)[:, None], axis=1)
    tbl = rel_bias.T

    def toeplitz(heads, dist):
        shape = (tbl[heads].shape[0],) + (1,) * dist.ndim
        out = jnp.broadcast_to(tbl[heads][:, 0].reshape(shape), shape[:1] + dist.shape)
        for b in range(1, T5_BUCKETS):
            out = jnp.where((dist >= first[b])[None], tbl[heads][:, b].reshape(shape), out)
        return out

    swa_h, nsa_h = slice(0, 4), slice(4, 8)
    def band(window):
        pad, tq = _band_tiles(window)
        return jnp.arange(tq)[:, None] + pad - jnp.arange(pad + tq)[None, :]

    bias_swa = toeplitz(swa_h, band(SWA_WINDOW))
    bias_win = toeplitz(nsa_h, band(NSA_WINDOW))
    ncp = seq // NSA_CMP_STRIDE
    cmp_end = jnp.arange(ncp) * NSA_CMP_STRIDE + NSA_CMP_LEN - 1
    bias_cmp = toeplitz(nsa_h, jnp.arange(seq)[:, None] - cmp_end[None, :])
    nd = min(_sel_delta_cap() + 1, seq // SEL_T)
    dist_s = (jnp.arange(nd)[:, None, None] * SEL_T + jnp.arange(SEL_T)[None, :, None]
              - jnp.arange(SEL_T)[None, None, :])
    bias_sel = toeplitz(nsa_h, dist_s)

    n_sel_pad = -(-(seq // NSA_SEL_LEN) // LANES) * LANES
    sel_start = np.arange(n_sel_pad) * NSA_SEL_LEN
    c_start = np.arange(ncp) * NSA_CMP_STRIDE
    c_end = c_start + NSA_CMP_LEN - 1
    real = (np.arange(ncp) < ncp - NSA_CMP_LEN // NSA_CMP_STRIDE + 1)[:, None] & (sel_start < seq)[None, :]
    overlap = ((c_start[:, None] < sel_start[None, :] + NSA_SEL_LEN) & (c_end[:, None] >= sel_start[None, :]) & real)
    own_block = (np.arange(seq) // NSA_SEL_LEN)[:, None] == np.arange(n_sel_pad)[None, :]
    penalty = np.where(own_block, -2.0 ** 100, 0.0)

    pos = jnp.arange(seq, dtype=F32)
    inv_freq = ROPE_THETA ** (-jnp.arange(0, MLA_ROPE, 2, dtype=F32) / MLA_ROPE)
    ang = pos[:, None] * inv_freq[None, :]
    cos, sin = jnp.cos(ang), jnp.sin(ang)
    ones = jnp.ones((seq, MLA_NOPE), F32)
    tail = LANES - MLA_NOPE - MLA_ROPE
    cos_t = jnp.concatenate([ones, cos, cos, jnp.ones((seq, tail), F32)], axis=1)
    sin_t = jnp.concatenate([0 * ones, -sin, sin, jnp.zeros((seq, tail), F32)], axis=1)
    return dict(bias_swa=bias_swa, bias_win=bias_win, bias_cmp=bias_cmp, bias_sel=bias_sel,
                overlap=jnp.asarray(overlap, BF16), penalty=jnp.asarray(penalty, BF16), cos_t=cos_t, sin_t=sin_t)


def _pad_to(a, shape):
    return jnp.pad(a, [(0, s - d) for d, s in zip(a.shape, shape)])


def _pack_layer(w_in, swa_q_norm, swa_k_norm, nsa_q_norm, nsa_k_norm, mla_q_lat_norm, mla_w_q_up,
                mla_kv_lat_norm, mla_w_kv_up, mla_q_norm, mla_k_norm):
    kpe = w_in[:, 2636:2668]
    kpe_seg = jnp.concatenate([jnp.zeros((D_MODEL, MLA_NOPE), F32), kpe,
                               jnp.zeros((D_MODEL, LANES - MLA_QK), F32)], axis=1)
    spread = lambda cols: _pad_to(cols.reshape(D_MODEL, -1, HEAD_DIM), (D_MODEL, cols.shape[1] // HEAD_DIM, LANES)
                                  ).reshape(D_MODEL, -1)
    w = jnp.concatenate([
        w_in[:, :1920],
        spread(w_in[:, 1920:2048]),
        w_in[:, 2048:2304],
        _pad_to(w_in[:, 2304:2316], (D_MODEL, 128)),
        _pad_to(w_in[:, 2316:2508], (D_MODEL, 256)),
        w_in[:, 2508:2636],
        jnp.tile(kpe_seg, (1, MLA_HEADS)),
    ], axis=1).astype(BF16)
    tile4 = lambda g: jnp.tile(g, 4)
    g64 = _pad_to(jnp.stack([tile4(swa_q_norm), tile4(swa_k_norm), tile4(nsa_q_norm),
                             tile4(nsa_k_norm[1]), tile4(nsa_k_norm[2])]), (8, 256))
    glat = _pad_to(jnp.stack([_pad_to(mla_q_lat_norm, (256,)), _pad_to(mla_kv_lat_norm, (256,))]), (8, 256))
    wq = _pad_to(mla_w_q_up.reshape(MLA_Q_RANK, MLA_HEADS, MLA_QK), (256, MLA_HEADS, LANES))
    wq = wq.reshape(256, MLA_HEADS * LANES).astype(BF16)
    wkv = mla_w_kv_up.reshape(MLA_KV_RANK, MLA_HEADS, MLA_NOPE + MLA_V)
    wk = _pad_to(wkv[:, :, :MLA_NOPE], (MLA_KV_RANK, MLA_HEADS, LANES)).reshape(MLA_KV_RANK, MLA_HEADS * LANES)
    wv = _pad_to(wkv[:, :, MLA_NOPE:], (MLA_KV_RANK, MLA_HEADS, LANES)).reshape(MLA_KV_RANK, MLA_HEADS * LANES)
    wkv_p = jnp.concatenate([wk, wv], axis=1).astype(BF16)
    gmla = _pad_to(jnp.stack([jnp.tile(_pad_to(mla_q_norm, (LANES,)), MLA_HEADS),
                              jnp.tile(_pad_to(mla_k_norm, (LANES,)), MLA_HEADS)]), (8, 512))
    return w, g64, glat, wq, wkv_p, gmla


def _gate_expand():
    rows = np.arange(128)[None, :, None]
    cols = np.arange(GROUP_WIDTH)[None, None, :]
    branch = np.arange(3)[:, None, None]
    return jnp.asarray(rows == branch * 4 + cols // HEAD_DIM, BF16)


def _compress_rows(pf3, batch, seq):
    nb = seq // NSA_CMP_STRIDE
    kv = pf3[:, :, :256].reshape(batch, seq, 2, 2, HEAD_DIM).transpose(2, 0, 3, 1, 4)
    blocks = kv.reshape(2, batch * 2, nb, NSA_CMP_STRIDE * HEAD_DIM)
    nxt = jnp.concatenate([blocks[:, :, 1:], jnp.zeros_like(blocks[:, :, :1])], axis=2)
    return jnp.concatenate([blocks, nxt], axis=3)


def kernel(x, rel_bias, attn_norm, w_in, swa_q_norm, swa_k_norm, swa_sinks, nsa_q_norm, nsa_k_norm, nsa_cmp_pos, nsa_cmp_w1, nsa_cmp_w2, mla_q_lat_norm, mla_w_q_up, mla_kv_lat_norm, mla_w_kv_up, mla_q_norm, mla_k_norm, out_norm, w_out, ffn_norm, moe_w_group, moe_b_group, moe_w_expert, moe_b_expert, moe_w_gate, moe_w_up, moe_w_down):
    batch, seq, _ = x.shape
    depth = w_in.shape[0]
    tokens = batch * seq
    tm = 512
    assert seq % 2048 == 0 and tokens % MOE_TM == 0
    tabs = _position_tables(rel_bias, seq)
    gexp = _gate_expand()
    zero_sinks = jnp.zeros((4,), F32)
    x2 = x.reshape(tokens, D_MODEL)
    for l in range(depth):
        w, g64, glat, wq, wkv, gmla = _pack_layer(
            w_in[l], swa_q_norm[l], swa_k_norm[l], nsa_q_norm[l], nsa_k_norm[l], mla_q_lat_norm[l],
            mla_w_q_up[l], mla_kv_lat_norm[l], mla_w_kv_up[l], mla_q_norm[l], mla_k_norm[l])
        pb, pf = _prep(x2, attn_norm[l][None, :], w, g64, glat, wq, wkv, gmla, tabs["cos_t"], tabs["sin_t"],
                       seq, tm)
        pb3 = pb.reshape(batch, seq, PB_WIDTH)
        pf3 = pf.reshape(batch, seq, PF_WIDTH)
        o_a = _sb_attention(pb3, batch, seq)
        o_b = _banded_attention(pb3, swa_sinks[l], tabs["bias_swa"], batch, seq, SWA_WINDOW,
                                PB_SWAQ, PB_SWAK, PB_SWAV, True)
        rows = _compress_rows(pf3, batch, seq)
        kvc = _compress(rows, nsa_cmp_pos[l].reshape(2, 1, -1),
                        nsa_cmp_w1[l].reshape(2, -1, NSA_CMP_HIDDEN).astype(BF16), nsa_cmp_w2[l].astype(BF16),
                        nsa_k_norm[l][0][None, :], 128)
        o_c, q_aug = _cmp_attention(pb3, kvc, tabs["bias_cmp"], tabs["overlap"], batch, seq)
        o_s = _sel_attention(pb3, q_aug, tabs["penalty"], tabs["bias_sel"], batch, seq)
        o_w = _banded_attention(pb3, zero_sinks, tabs["bias_win"], batch, seq, NSA_WINDOW,
                                PB_NSAQ, PB_NSAKW, PB_NSAVW, False)
        o_d = _mla_attention(pb3, batch, seq)
        flat = lambda o: o.reshape(tokens, GROUP_WIDTH)
        x2 = _outproj(x2, flat(o_a), flat(o_b), flat(o_c), flat(o_s), flat(o_w), flat(o_d), pf, gexp,
                      out_norm[l][None, :], w_out[l].astype(BF16), tm)
        w_router = _pad_to(jnp.concatenate([moe_w_group[l], moe_w_expert[l]], axis=1), (D_MODEL, LANES))
        wr_hi = w_router.astype(BF16)
        wr_lo = (w_router - wr_hi.astype(F32)).astype(BF16)
        b_router = _pad_to(jnp.concatenate([moe_b_group[l], moe_b_expert[l]])[None, :], (1, LANES))
        quads = MOE_EXPERTS // MOE_QUAD
        by_quad = lambda w: w.astype(BF16).reshape(quads, MOE_QUAD, D_MODEL, MOE_HIDDEN).transpose(0, 2, 1, 3
                                                   ).reshape(quads, D_MODEL, MOE_QUAD * MOE_HIDDEN)
        x2 = _moe(x2, ffn_norm[l][None, :], wr_hi, wr_lo, b_router, by_quad(moe_w_gate[l]), by_quad(moe_w_up[l]),
                  moe_w_down[l].astype(BF16).reshape(quads, MOE_QUAD * MOE_HIDDEN, D_MODEL))
    return x2.reshape(batch, seq, D_MODEL)
```
